```python
import jax, jax.numpy as jnp
from jax import lax
import numpy as np

D_MODEL = 2048
BATCH = 4
SEQ = 4096
DEPTH = 1

N_MEM = 256
D_MIX = D_MODEL
MLA_HEADS = 8
MLA_NOPE = 128
MLA_ROPE = 64
MLA_V = 128
Q_LORA = 512
KV_LORA = 512
GMLP_GROUPS = 4
GMLP_CH = 128
CHUNK = 128
MEM_HEADS = 4
MEM_HEAD_DIM = 128
MLA_WIDTH = MLA_HEADS * MLA_V
GMLP_WIDTH = GMLP_GROUPS * GMLP_CH
MEM_WIDTH = MEM_HEADS * MEM_HEAD_DIM
IN_COLS = Q_LORA + KV_LORA + MLA_ROPE + 2 * GMLP_WIDTH + MEM_WIDTH
N_EXPERTS = 32
TOP_K = 4
D_EXPERT = D_MODEL
SWIGLU_ALPHA = 1.702
SWIGLU_LIMIT = 7.0
MOE_BLOCK = 256
Q_BLOCK = 128
ROPE_BASE = 10000.0
EPS = 1e-6

kernel_name = 'hybrid_mla_gmlp_memxattn_moe'


def rms_norm(x, g):
    xf = x.astype(jnp.float32)
    y = xf * lax.rsqrt(jnp.mean(xf * xf, axis=-1, keepdims=True) + EPS)
    return (y * g.astype(jnp.float32)).astype(x.dtype)


def apply_rope(x, positions):
    half = x.shape[-1] // 2
    inv_freq = ROPE_BASE ** (-jnp.arange(half, dtype=jnp.float32) / half)
    ang = positions.astype(jnp.float32)[:, :, None, None] * inv_freq
    cos, sin = jnp.cos(ang), jnp.sin(ang)
    xf = x.astype(jnp.float32)
    x1, x2 = xf[..., :half], xf[..., half:]
    return jnp.concatenate([x1 * cos - x2 * sin, x1 * sin + x2 * cos], axis=-1).astype(x.dtype)


def causal_block_attention(q, k, v):
    B, S, H, Dk = q.shape
    Dv = v.shape[-1]
    nb = S // Q_BLOCK
    qb = q.reshape(B, nb, Q_BLOCK, H, Dk).transpose(1, 0, 2, 3, 4)
    scale = Dk ** -0.5
    kpos = jnp.arange(S)

    def one_block(args):
        qi, i = args
        s = jnp.einsum('bqhd,bkhd->bhqk', qi, k).astype(jnp.float32) * scale
        qpos = i * Q_BLOCK + jnp.arange(Q_BLOCK)
        s = jnp.where(kpos[None, :] <= qpos[:, None], s, -jnp.inf)
        p = jax.nn.softmax(s, axis=-1).astype(v.dtype)
        return jnp.einsum('bhqk,bkhd->bqhd', p, v)

    o = lax.map(one_block, (qb, jnp.arange(nb)))
    return o.transpose(1, 0, 2, 3, 4).reshape(B, S, H, Dv)


def token_mixers(h, mem, positions, w_in, q_a_norm, w_q_b, kv_a_norm, w_kv_b,
                 q_nope_norm, q_rope_norm, k_nope_norm, k_rope_norm,
                 gmlp_v_norm, w_spatial, b_spatial,
                 mem_norm, w_mem_kv, mem_q_norm, mem_k_norm,
                 mla_out_norm, gmlp_out_norm, mem_out_norm, w_o):
    B, S, _ = h.shape
    z = h @ w_in
    o1 = Q_LORA
    o2 = o1 + KV_LORA
    o3 = o2 + MLA_ROPE
    o4 = o3 + 2 * GMLP_WIDTH
    c_q, c_kv, k_rope, g_in, q_mem = z[..., :o1], z[..., o1:o2], z[..., o2:o3], z[..., o3:o4], z[..., o4:]

    q = (rms_norm(c_q, q_a_norm) @ w_q_b).reshape(B, S, MLA_HEADS, MLA_NOPE + MLA_ROPE)
    q_nope = rms_norm(q[..., :MLA_NOPE], q_nope_norm)
    q_pe = apply_rope(rms_norm(q[..., MLA_NOPE:], q_rope_norm), positions)
    kv = (rms_norm(c_kv, kv_a_norm) @ w_kv_b).reshape(B, S, MLA_HEADS, MLA_NOPE + MLA_V)
    k_nope = rms_norm(kv[..., :MLA_NOPE], k_nope_norm)
    v = kv[..., MLA_NOPE:]
    k_pe = apply_rope(rms_norm(k_rope, k_rope_norm)[:, :, None, :], positions)
    k = jnp.concatenate([k_nope, jnp.broadcast_to(k_pe, (B, S, MLA_HEADS, MLA_ROPE))], axis=-1)
    qf = jnp.concatenate([q_nope, q_pe], axis=-1)
    o_mla = causal_block_attention(qf, k, v).reshape(B, S, MLA_WIDTH)

    g = jax.nn.gelu(g_in, approximate=False)
    u, vg = g[..., :GMLP_WIDTH], g[..., GMLP_WIDTH:]
    vg = rms_norm(vg, gmlp_v_norm).reshape(B, S // CHUNK, CHUNK, GMLP_GROUPS, GMLP_CH)
    causal = jnp.tril(jnp.ones((CHUNK, CHUNK), dtype=bool))
    ws = jnp.where(causal[None], w_spatial, jnp.zeros_like(w_spatial))
    sp = jnp.einsum('gts,bnsgc->bntgc', ws, vg) + b_spatial.T[None, None, :, :, None]
    o_gmlp = u * sp.reshape(B, S, GMLP_WIDTH)

    M = mem.shape[1]
    mkv = (rms_norm(mem, mem_norm) @ w_mem_kv).reshape(B, M, 2, MEM_HEADS, MEM_HEAD_DIM)
    mk = rms_norm(mkv[:, :, 0], mem_k_norm)
    mv = mkv[:, :, 1]
    mq = rms_norm(q_mem.reshape(B, S, MEM_HEADS, MEM_HEAD_DIM), mem_q_norm)
    s = jnp.einsum('bshd,bmhd->bhsm', mq, mk).astype(jnp.float32) * (MEM_HEAD_DIM ** -0.5)
    p = jax.nn.softmax(s, axis=-1).astype(mv.dtype)
    o_mem = jnp.einsum('bhsm,bmhd->bshd', p, mv).reshape(B, S, MEM_WIDTH)

    o = jnp.concatenate([rms_norm(o_mla, mla_out_norm),
                         rms_norm(o_gmlp, gmlp_out_norm),
                         rms_norm(o_mem, mem_out_norm)], axis=-1)
    return o @ w_o


def moe_ffn(h, w_router, b_router, w_gate_up, b_gate_up, w_down, b_down):
    B, S, D = h.shape
    xt = h.reshape(-1, D)
    N = xt.shape[0]
    logits = (xt @ w_router + b_router).astype(jnp.float32)
    top_val, top_idx = lax.top_k(logits, TOP_K)
    gates = jax.nn.softmax(top_val, axis=-1)
    NK = N * TOP_K
    e_flat = top_idx.reshape(-1).astype(jnp.int32)
    g_flat = gates.reshape(-1).astype(h.dtype)
    tok_flat = jnp.arange(NK, dtype=jnp.int32) // TOP_K
    order = jnp.argsort(e_flat)
    e_sorted = e_flat[order]
    counts = jnp.bincount(e_flat, length=N_EXPERTS).astype(jnp.int32)
    padded = ((counts + MOE_BLOCK - 1) // MOE_BLOCK) * MOE_BLOCK
    start = jnp.cumsum(counts) - counts
    pend = jnp.cumsum(padded)
    pstart = pend - padded
    dest = pstart[e_sorted] + jnp.arange(NK, dtype=jnp.int32) - start[e_sorted]
    n_blocks = -(-(NK + N_EXPERTS * (MOE_BLOCK - 1)) // MOE_BLOCK)
    P = n_blocks * MOE_BLOCK
    row_tok = jnp.zeros((P,), jnp.int32).at[dest].set(tok_flat[order])
    row_gate = jnp.zeros((P,), h.dtype).at[dest].set(g_flat[order])
    block_exp = jnp.minimum(
        jnp.searchsorted(pend, jnp.arange(n_blocks, dtype=jnp.int32) * MOE_BLOCK, side='right'),
        N_EXPERTS - 1)

    def expert_block(args):
        toks, e = args
        xb = xt[toks]
        gu = xb @ w_gate_up[e] + b_gate_up[e]
        gate = jnp.minimum(gu[:, :D_EXPERT], SWIGLU_LIMIT)
        up = jnp.clip(gu[:, D_EXPERT:], -SWIGLU_LIMIT, SWIGLU_LIMIT)
        act = (up + 1.0) * (gate * jax.nn.sigmoid(SWIGLU_ALPHA * gate))
        return act @ w_down[e] + b_down[e]

    y = lax.map(expert_block, (row_tok.reshape(n_blocks, MOE_BLOCK), block_exp))
    y = y.reshape(P, D) * row_gate[:, None]
    out = jnp.zeros((N, D), y.dtype).at[row_tok].add(y)
    return out.reshape(B, S, D)


def setup_inputs(seed: int = 0) -> dict:
    key = jax.random.key(seed)
    ks = jax.random.split(key, 40)
    L = DEPTH
    f32 = jnp.float32

    def nrm(k, shape, scale):
        return jax.random.normal(k, shape, f32) * scale

    def gain(k, shape):
        return 1.0 + 0.02 * jax.random.normal(k, shape, f32)

    x = jax.random.normal(ks[0], (BATCH, SEQ, D_MODEL), f32)
    mem = jax.random.normal(ks[1], (BATCH, N_MEM, D_MODEL), f32)
    offsets = jax.random.randint(ks[2], (BATCH, 1), 0, 1024, dtype=jnp.int32)
    positions = offsets + jnp.arange(SEQ, dtype=jnp.int32)[None, :]
    return {
        'x': x,
        'mem': mem,
        'positions': positions,
        'attn_norm': gain(ks[3], (L, D_MODEL)),
        'w_in': nrm(ks[4], (L, D_MODEL, IN_COLS), D_MODEL ** -0.5),
        'q_a_norm': gain(ks[5], (L, Q_LORA)),
        'w_q_b': nrm(ks[6], (L, Q_LORA, MLA_HEADS * (MLA_NOPE + MLA_ROPE)), Q_LORA ** -0.5),
        'kv_a_norm': gain(ks[7], (L, KV_LORA)),
        'w_kv_b': nrm(ks[8], (L, KV_LORA, MLA_HEADS * (MLA_NOPE + MLA_V)), KV_LORA ** -0.5),
        'q_nope_norm': gain(ks[9], (L, MLA_NOPE)),
        'q_rope_norm': gain(ks[10], (L, MLA_ROPE)),
        'k_nope_norm': gain(ks[11], (L, MLA_NOPE)),
        'k_rope_norm': gain(ks[12], (L, MLA_ROPE)),
        'gmlp_v_norm': gain(ks[13], (L, GMLP_WIDTH)),
        'w_spatial': nrm(ks[14], (L, GMLP_GROUPS, CHUNK, CHUNK), 0.5 * CHUNK ** -0.5),
        'b_spatial': 1.0 + 0.01 * jax.random.normal(ks[15], (L, GMLP_GROUPS, CHUNK), f32),
        'mem_norm': gain(ks[16], (L, D_MODEL)),
        'w_mem_kv': nrm(ks[17], (L, D_MODEL, 2 * MEM_WIDTH), D_MODEL ** -0.5),
        'mem_q_norm': gain(ks[18], (L, MEM_HEAD_DIM)),
        'mem_k_norm': gain(ks[19], (L, MEM_HEAD_DIM)),
        'mla_out_norm': gain(ks[20], (L, MLA_WIDTH)),
        'gmlp_out_norm': gain(ks[21], (L, GMLP_WIDTH)),
        'mem_out_norm': gain(ks[22], (L, MEM_WIDTH)),
        'w_o': nrm(ks[23], (L, D_MIX, D_MODEL), D_MIX ** -0.5),
        'ffn_norm': gain(ks[24], (L, D_MODEL)),
        'w_router': nrm(ks[25], (L, D_MODEL, N_EXPERTS), D_MODEL ** -0.5),
        'b_router': nrm(ks[26], (L, N_EXPERTS), 0.01),
        'w_gate_up': nrm(ks[27], (L, N_EXPERTS, D_MODEL, 2 * D_EXPERT), D_MODEL ** -0.5),
        'b_gate_up': nrm(ks[28], (L, N_EXPERTS, 2 * D_EXPERT), 0.01),
        'w_down': nrm(ks[29], (L, N_EXPERTS, D_EXPERT, D_MODEL), D_EXPERT ** -0.5),
        'b_down': nrm(ks[30], (L, N_EXPERTS, D_MODEL), 0.01),
    }


def reference(x, mem, positions, attn_norm, w_in, q_a_norm, w_q_b, kv_a_norm, w_kv_b,
              q_nope_norm, q_rope_norm, k_nope_norm, k_rope_norm,
              gmlp_v_norm, w_spatial, b_spatial,
              mem_norm, w_mem_kv, mem_q_norm, mem_k_norm,
              mla_out_norm, gmlp_out_norm, mem_out_norm, w_o,
              ffn_norm, w_router, b_router, w_gate_up, b_gate_up, w_down, b_down):
    for l in range(DEPTH):
        h = rms_norm(x, attn_norm[l])
        x = x + token_mixers(h, mem, positions, w_in[l], q_a_norm[l], w_q_b[l], kv_a_norm[l], w_kv_b[l],
                             q_nope_norm[l], q_rope_norm[l], k_nope_norm[l], k_rope_norm[l],
                             gmlp_v_norm[l], w_spatial[l], b_spatial[l],
                             mem_norm[l], w_mem_kv[l], mem_q_norm[l], mem_k_norm[l],
                             mla_out_norm[l], gmlp_out_norm[l], mem_out_norm[l], w_o[l])
        h = rms_norm(x, ffn_norm[l])
        x = x + moe_ffn(h, w_router[l], b_router[l], w_gate_up[l], b_gate_up[l], w_down[l], b_down[l])
    return x
```

```python
import functools

import jax
import jax.numpy as jnp
import numpy as np
from jax import lax
from jax.experimental import pallas as pl
from jax.experimental.pallas import tpu as pltpu

F32 = jnp.float32
BF16 = jnp.bfloat16

EPS = 1e-6
LANES = 128
VMEM_LIMIT = 56 * 1024 * 1024

MLA_HEADS = 8
MLA_NOPE = 128
MLA_ROPE = 64
MLA_V = 128
Q_LORA = 512
KV_LORA = 512
GMLP_GROUPS = 4
GMLP_CH = 128
GMLP_WIDTH = GMLP_GROUPS * GMLP_CH
CHUNK = 128
MEM_HEADS = 4
MEM_HEAD_DIM = 128
MEM_WIDTH = MEM_HEADS * MEM_HEAD_DIM
N_EXPERTS = 32
TOP_K = 4
SWIGLU_ALPHA = 1.702
SWIGLU_LIMIT = 7.0
ROPE_BASE = 10000.0
HEAD_PAD = 2 * LANES

TOK_TILE = 512
ATT_TQ = 512
ATT_TK = 512
MOE_TM = 256
MOE_BPI = 4
MOE_TF = 256
CMB_TILE = 256


def _rms(x, g, n=None):
    n = x.shape[-1] if n is None else n
    ms = jnp.sum(x * x, axis=-1, keepdims=True) * (1.0 / n)
    return x * lax.rsqrt(ms + EPS) * g


def _gelu(x):
    return 0.5 * x * (1.0 + lax.erf(x * (2.0 ** -0.5)))


def _dot(a, b):
    return jnp.dot(a, b, preferred_element_type=F32)


def _dot_nt(a, b):
    return lax.dot_general(a, b, (((1,), (1,)), ((), ())), preferred_element_type=F32)


def _const_spec(shape):
    nd = len(shape)
    return pl.BlockSpec(shape, lambda *_: (0,) * nd)


def _mem_kv_kernel(mem_ref, g_ref, w_ref, kg_ref, mk_ref, mv_ref):
    m = mem_ref[0]
    hn = _rms(m, g_ref[...]).astype(BF16)
    kv = _dot(hn, w_ref[...])
    for h in range(MEM_HEADS):
        k = kv[:, h * MEM_HEAD_DIM:(h + 1) * MEM_HEAD_DIM]
        mk_ref[0, :, h * MEM_HEAD_DIM:(h + 1) * MEM_HEAD_DIM] = _rms(k, kg_ref[...]).astype(BF16)
    mv_ref[0] = kv[:, MEM_WIDTH:].astype(BF16)


def _mem_kv(mem, mem_norm, w_mem_kv, mem_k_norm):
    B, M, D = mem.shape
    return pl.pallas_call(
        _mem_kv_kernel,
        grid=(B,),
        in_specs=[pl.BlockSpec((1, M, D), lambda b: (b, 0, 0)),
                  _const_spec((1, D)),
                  _const_spec((D, 2 * MEM_WIDTH)),
                  _const_spec((1, MEM_HEAD_DIM))],
        out_specs=[pl.BlockSpec((1, M, MEM_WIDTH), lambda b: (b, 0, 0)),
                   pl.BlockSpec((1, M, MEM_WIDTH), lambda b: (b, 0, 0))],
        out_shape=[jax.ShapeDtypeStruct((B, M, MEM_WIDTH), BF16),
                   jax.ShapeDtypeStruct((B, M, MEM_WIDTH), BF16)],
        compiler_params=pltpu.CompilerParams(dimension_semantics=("arbitrary",),
                                             vmem_limit_bytes=VMEM_LIMIT),
        name="mem_kv",
    )(mem, mem_norm.reshape(1, D), w_mem_kv.astype(BF16), mem_k_norm.reshape(1, MEM_HEAD_DIM))


_O_CQ = 0
_O_CKV = _O_CQ + Q_LORA
_O_U = _O_CKV + KV_LORA
_O_VG = _O_U + GMLP_WIDTH
_O_QM = _O_VG + GMLP_WIDTH
_O_KR = _O_QM + MEM_WIDTH
_IN_COLS_PAD = _O_KR + LANES


def _rope_tile(t, cos, sin_signed):
    return t * cos + pltpu.roll(t, LANES // 2, axis=1) * sin_signed


def _inproj_kernel(x_ref, pos_ref, invf_ref, sgn_ref, an_ref, w_ref, qan_ref, kvan_ref, krn_ref,
                   gvn_ref, ws_ref, bs_ref, mk_ref, mv_ref, mqn_ref, gon_ref, mon_ref,
                   cq_ref, ckv_ref, kpe_ref, ogm_ref, cs_ref):
    tm = x_ref.shape[0]
    h = _rms(x_ref[...], an_ref[...]).astype(BF16)
    z = _dot(h, w_ref[...])

    cq_ref[...] = _rms(z[:, _O_CQ:_O_CQ + Q_LORA], qan_ref[...]).astype(BF16)
    ckv_ref[...] = _rms(z[:, _O_CKV:_O_CKV + KV_LORA], kvan_ref[...]).astype(BF16)

    ang = pos_ref[...].astype(F32) * invf_ref[...]
    cos = jnp.cos(ang)
    sin_signed = jnp.sin(ang) * sgn_ref[...]
    cs_ref[:, :LANES] = cos
    cs_ref[:, LANES:] = sin_signed
    kr = _rms(z[:, _O_KR:_O_KR + LANES], krn_ref[...], MLA_ROPE)
    kpe_ref[...] = _rope_tile(kr, cos, sin_signed).astype(BF16)

    u = _gelu(z[:, _O_U:_O_U + GMLP_WIDTH])
    vg = _gelu(z[:, _O_VG:_O_VG + GMLP_WIDTH])
    vg = _rms(vg, gvn_ref[...]).astype(BF16)
    row = lax.broadcasted_iota(jnp.int32, (CHUNK, CHUNK), 0)
    col = lax.broadcasted_iota(jnp.int32, (CHUNK, CHUNK), 1)
    sp_cols = []
    for g in range(GMLP_GROUPS):
        wsg = jnp.where(col <= row, ws_ref[g], 0.0).astype(BF16)
        sp_rows = [_dot(wsg, vg[c * CHUNK:(c + 1) * CHUNK, g * GMLP_CH:(g + 1) * GMLP_CH])
                   for c in range(tm // CHUNK)]
        sp_cols.append(jnp.concatenate(sp_rows, axis=0))
    sp = jnp.concatenate(sp_cols, axis=1) + jnp.concatenate([bs_ref[...]] * (tm // CHUNK), axis=0)
    ogm_ref[:, :GMLP_WIDTH] = _rms(u * sp, gon_ref[...]).astype(BF16)

    o_heads = []
    for hd in range(MEM_HEADS):
        sl = slice(hd * MEM_HEAD_DIM, (hd + 1) * MEM_HEAD_DIM)
        q = z[:, _O_QM + hd * MEM_HEAD_DIM:_O_QM + (hd + 1) * MEM_HEAD_DIM]
        qn = (_rms(q, mqn_ref[...]) * (MEM_HEAD_DIM ** -0.5)).astype(BF16)
        s = _dot_nt(qn, mk_ref[0, :, sl])
        s = s - jnp.max(s, axis=-1, keepdims=True)
        p = jnp.exp(s)
        p = p / jnp.sum(p, axis=-1, keepdims=True)
        o_heads.append(_dot(p.astype(BF16), mv_ref[0, :, sl]))
    o_mem = jnp.concatenate(o_heads, axis=1)
    ogm_ref[:, GMLP_WIDTH:] = _rms(o_mem, mon_ref[...]).astype(BF16)


def _inproj(x2, pos, tables, attn_norm, w_in_r, q_a_norm, kv_a_norm, krn_tile, gmlp_v_norm,
            w_spatial, bs_tile, mk, mv, mem_q_norm, gmlp_out_norm, mem_out_norm, seq):
    N, D = x2.shape
    tm = TOK_TILE
    tiles_per_seq = seq // tm
    invf_tile, sgn_tile = tables
    M = mk.shape[1]
    row = lambda w: pl.BlockSpec((tm, w), lambda i: (i, 0))
    batch_blk = pl.BlockSpec((1, M, MEM_WIDTH), lambda i: (i // tiles_per_seq, 0, 0))
    return pl.pallas_call(
        _inproj_kernel,
        grid=(N // tm,),
        in_specs=[row(D), row(1), _const_spec((1, LANES)), _const_spec((1, LANES)),
                  _const_spec((1, D)), _const_spec((D, _IN_COLS_PAD)),
                  _const_spec((1, Q_LORA)), _const_spec((1, KV_LORA)), _const_spec((1, LANES)),
                  _const_spec((1, GMLP_WIDTH)), _const_spec((GMLP_GROUPS, CHUNK, CHUNK)),
                  _const_spec((CHUNK, GMLP_WIDTH)), batch_blk, batch_blk,
                  _const_spec((1, MEM_HEAD_DIM)), _const_spec((1, GMLP_WIDTH)),
                  _const_spec((1, MEM_WIDTH))],
        out_specs=[row(Q_LORA), row(KV_LORA), row(LANES), row(GMLP_WIDTH + MEM_WIDTH),
                   row(2 * LANES)],
        out_shape=[jax.ShapeDtypeStruct((N, Q_LORA), BF16),
                   jax.ShapeDtypeStruct((N, KV_LORA), BF16),
                   jax.ShapeDtypeStruct((N, LANES), BF16),
                   jax.ShapeDtypeStruct((N, GMLP_WIDTH + MEM_WIDTH), BF16),
                   jax.ShapeDtypeStruct((N, 2 * LANES), F32)],
        compiler_params=pltpu.CompilerParams(dimension_semantics=("arbitrary",),
                                             vmem_limit_bytes=VMEM_LIMIT),
        name="inproj",
    )(x2, pos, invf_tile, sgn_tile, attn_norm.reshape(1, D), w_in_r,
      q_a_norm.reshape(1, -1), kv_a_norm.reshape(1, -1), krn_tile, gmlp_v_norm.reshape(1, -1),
      w_spatial, bs_tile, mk, mv, mem_q_norm.reshape(1, -1), gmlp_out_norm.reshape(1, -1),
      mem_out_norm.reshape(1, -1))


def _qkv_kernel(cq_ref, ckv_ref, kpe_ref, cs_ref, wq_ref, wkv_ref, qnn_ref, qrn_ref, knn_ref,
                q_ref, k_ref, v_ref):
    cos = cs_ref[:, :LANES]
    sin_signed = cs_ref[:, LANES:]
    scale = (MLA_NOPE + MLA_ROPE) ** -0.5
    qr = _dot(cq_ref[...], wq_ref[...])
    kvr = _dot(ckv_ref[...], wkv_ref[...])
    kpe = kpe_ref[...]
    for h in range(MLA_HEADS):
        o = h * HEAD_PAD
        qn = _rms(qr[:, o:o + MLA_NOPE], qnn_ref[...]) * scale
        qt = _rms(qr[:, o + MLA_NOPE:o + HEAD_PAD], qrn_ref[...], MLA_ROPE)
        qt = _rope_tile(qt, cos, sin_signed) * scale
        q_ref[:, o:o + MLA_NOPE] = qn.astype(BF16)
        q_ref[:, o + MLA_NOPE:o + HEAD_PAD] = qt.astype(BF16)
        ko = h * (MLA_NOPE + MLA_V)
        k_ref[:, o:o + MLA_NOPE] = _rms(kvr[:, ko:ko + MLA_NOPE], knn_ref[...]).astype(BF16)
        k_ref[:, o + MLA_NOPE:o + HEAD_PAD] = kpe
        v_ref[:, h * MLA_V:(h + 1) * MLA_V] = kvr[:, ko + MLA_NOPE:ko + MLA_NOPE + MLA_V].astype(BF16)


def _qkv(cq, ckv, kpe, cs, wq_pad, wkv, q_nope_norm, qrn_tile, k_nope_norm):
    N = cq.shape[0]
    tm = TOK_TILE
    row = lambda w: pl.BlockSpec((tm, w), lambda i: (i, 0))
    return pl.pallas_call(
        _qkv_kernel,
        grid=(N // tm,),
        in_specs=[row(Q_LORA), row(KV_LORA), row(LANES), row(2 * LANES),
                  _const_spec(wq_pad.shape), _const_spec(wkv.shape),
                  _const_spec((1, MLA_NOPE)), _const_spec((1, LANES)), _const_spec((1, MLA_NOPE))],
        out_specs=[row(MLA_HEADS * HEAD_PAD), row(MLA_HEADS * HEAD_PAD), row(MLA_HEADS * MLA_V)],
        out_shape=[jax.ShapeDtypeStruct((N, MLA_HEADS * HEAD_PAD), BF16),
                   jax.ShapeDtypeStruct((N, MLA_HEADS * HEAD_PAD), BF16),
                   jax.ShapeDtypeStruct((N, MLA_HEADS * MLA_V), BF16)],
        compiler_params=pltpu.CompilerParams(dimension_semantics=("arbitrary",),
                                             vmem_limit_bytes=VMEM_LIMIT),
        name="qkv",
    )(cq, ckv, kpe, cs, wq_pad, wkv, q_nope_norm.reshape(1, -1), qrn_tile,
      k_nope_norm.reshape(1, -1))


def _attn_kernel(q_ref, k_ref, v_ref, o_ref):
    tq, tk = ATT_TQ, ATT_TK
    qi = pl.program_id(2)
    q = q_ref[0]

    def step(kb, carry, masked):
        m, l, acc = carry
        k0 = pl.multiple_of(kb * tk, tk)
        s = _dot_nt(q, k_ref[0, pl.ds(k0, tk), :])
        if masked:
            qpos = qi * tq + lax.broadcasted_iota(jnp.int32, (tq, tk), 0)
            kpos = kb * tk + lax.broadcasted_iota(jnp.int32, (tq, tk), 1)
            s = jnp.where(kpos <= qpos, s, -jnp.inf)
        m_new = jnp.maximum(m, jnp.max(s, axis=-1, keepdims=True))
        alpha = jnp.exp(m - m_new)
        p = jnp.exp(s - m_new)
        l = alpha * l + jnp.sum(p, axis=-1, keepdims=True)
        acc = alpha * acc + _dot(p.astype(BF16), v_ref[0, pl.ds(k0, tk), :])
        return m_new, l, acc

    init = (jnp.full((tq, 1), -jnp.inf, F32), jnp.zeros((tq, 1), F32), jnp.zeros((tq, MLA_V), F32))
    n_full = (qi * tq) // tk
    carry = lax.fori_loop(0, n_full, lambda kb, c: step(kb, c, False), init)
    for d in range(tq // tk):
        carry = step(n_full + d, carry, True)
    _, l, acc = carry
    o_ref[0] = (acc / l).astype(BF16)


def _attention(q, k, v, batch, seq):
    q3 = q.reshape(batch, seq, MLA_HEADS * HEAD_PAD)
    k3 = k.reshape(batch, seq, MLA_HEADS * HEAD_PAD)
    v3 = v.reshape(batch, seq, MLA_HEADS * MLA_V)
    return pl.pallas_call(
        _attn_kernel,
        grid=(batch, MLA_HEADS, seq // ATT_TQ),
        in_specs=[pl.BlockSpec((1, ATT_TQ, HEAD_PAD), lambda b, h, i: (b, i, h)),
                  pl.BlockSpec((1, seq, HEAD_PAD), lambda b, h, i: (b, 0, h)),
                  pl.BlockSpec((1, seq, MLA_V), lambda b, h, i: (b, 0, h))],
        out_specs=pl.BlockSpec((1, ATT_TQ, MLA_V), lambda b, h, i: (b, i, h)),
        out_shape=jax.ShapeDtypeStruct((batch, seq, MLA_HEADS * MLA_V), BF16),
        compiler_params=pltpu.CompilerParams(
            dimension_semantics=("arbitrary", "arbitrary", "arbitrary"),
            vmem_limit_bytes=VMEM_LIMIT),
        name="attn",
    )(q3, k3, v3)


def _outproj_kernel(x_ref, oa_ref, ogm_ref, aon_ref, woa_ref, wob_ref, fn_ref, wr_ref, br_ref,
                    x1_ref, h2_ref, ti_ref, tg_ref):
    tm = x_ref.shape[0]
    oa = _rms(oa_ref[...].astype(F32), aon_ref[...]).astype(BF16)
    x1 = x_ref[...] + _dot(oa, woa_ref[...]) + _dot(ogm_ref[...], wob_ref[...])
    x1_ref[...] = x1
    h2 = _rms(x1, fn_ref[...])
    h2_ref[...] = h2
    h_hi = h2.astype(BF16)
    h_lo = (h2 - h_hi.astype(F32)).astype(BF16)
    wr = wr_ref[...]
    w_hi = wr.astype(BF16)
    w_lo = (wr - w_hi.astype(F32)).astype(BF16)
    logits = _dot(h_hi, w_hi) + (_dot(h_lo, w_hi) + _dot(h_hi, w_lo)) + br_ref[...]
    lane = lax.broadcasted_iota(jnp.int32, (tm, LANES), 1)
    lg = jnp.where(lane < N_EXPERTS, logits, -jnp.inf)
    vals, idxs = [], []
    for _ in range(TOP_K):
        m = jnp.max(lg, axis=-1, keepdims=True)
        am = jnp.min(jnp.where(lg == m, lane, LANES), axis=-1, keepdims=True)
        vals.append(m)
        idxs.append(am)
        lg = jnp.where(lane == am, -jnp.inf, lg)
    es = [jnp.exp(v - vals[0]) for v in vals]
    denom = es[0] + es[1] + es[2] + es[3]
    ti = jnp.zeros((tm, LANES), jnp.int32)
    tg = jnp.zeros((tm, LANES), F32)
    for kk in range(TOP_K):
        ti = jnp.where(lane == kk, idxs[kk], ti)
        tg = jnp.where(lane == kk, es[kk] / denom, tg)
    ti_ref[...] = ti
    tg_ref[...] = tg


def _outproj(x2, o_mla, ogm, mla_out_norm, wo_a, wo_b, ffn_norm, wr_pad, br_pad):
    N, D = x2.shape
    tm = TOK_TILE
    row = lambda w: pl.BlockSpec((tm, w), lambda i: (i, 0))
    wa = o_mla.shape[1]
    wb = ogm.shape[1]
    return pl.pallas_call(
        _outproj_kernel,
        grid=(N // tm,),
        in_specs=[row(D), row(wa), row(wb), _const_spec((1, wa)), _const_spec((wa, D)),
                  _const_spec((wb, D)), _const_spec((1, D)), _const_spec((D, LANES)),
                  _const_spec((1, LANES))],
        out_specs=[row(D), row(D), row(LANES), row(LANES)],
        out_shape=[jax.ShapeDtypeStruct((N, D), F32), jax.ShapeDtypeStruct((N, D), F32),
                   jax.ShapeDtypeStruct((N, LANES), jnp.int32),
                   jax.ShapeDtypeStruct((N, LANES), F32)],
        compiler_params=pltpu.CompilerParams(dimension_semantics=("arbitrary",),
                                             vmem_limit_bytes=VMEM_LIMIT),
        name="outproj",
    )(x2, o_mla, ogm, mla_out_norm.reshape(1, -1), wo_a, wo_b, ffn_norm.reshape(1, -1), wr_pad,
      br_pad)


def _moe_kernel(ie_ref, ist_ref, inb_ref, tail_ref, tok_ref,
                h2_hbm, wg_ref, wu_ref, wd_ref, bg_ref, bu_ref, bd_ref,
                ys_hbm,
                xs32, xsb, acc, wgb, wub, wdb, zblk, sem_g, sem_o, sem_z):
    tm = MOE_TM
    i = pl.program_id(0)
    j = pl.program_id(1)
    nj = pl.num_programs(1)
    nblk = inb_ref[i]
    start = ist_ref[i]
    n_blocks = ys_hbm.shape[0] // tm

    def zero_copy(b):
        d0 = pl.multiple_of(b * tm, tm)
        return pltpu.make_async_copy(zblk, ys_hbm.at[pl.ds(d0, tm), :], sem_z)

    @pl.when(jnp.logical_and(i == 0, j == 0))
    def _zero_tail_start():
        zblk[...] = jnp.zeros(zblk.shape, zblk.dtype)

        def issue(b, c):
            zero_copy(b).start()
            return c
        lax.fori_loop(tail_ref[0], n_blocks, issue, 0)

    @pl.when(jnp.logical_and(i == pl.num_programs(0) - 1, j == nj - 1))
    def _zero_tail_wait():
        def finish(b, c):
            zero_copy(b).wait()
            return c
        lax.fori_loop(tail_ref[0], n_blocks, finish, 0)

    def row_copy(r):
        t = tok_ref[start + r]
        return pltpu.make_async_copy(h2_hbm.at[pl.ds(t, 1), :], xs32.at[pl.ds(r, 1), :], sem_g)

    def block_wait_copy(m):
        r0 = pl.multiple_of(m * tm, tm)
        return pltpu.make_async_copy(h2_hbm.at[pl.ds(0, tm), :], xs32.at[pl.ds(r0, tm), :], sem_g)

    def out_copy(m):
        r0 = pl.multiple_of(m * tm, tm)
        d0 = pl.multiple_of(start + m * tm, tm)
        return pltpu.make_async_copy(acc.at[pl.ds(r0, tm), :], ys_hbm.at[pl.ds(d0, tm), :], sem_o)

    @pl.when(jnp.logical_and(j == 0, nblk > 0))
    def _gather():
        def issue(r, c):
            row_copy(r).start()
            return c
        lax.fori_loop(0, nblk * tm, issue, 0)

        def wait_block(m, c):
            block_wait_copy(m).wait()
            return c
        lax.fori_loop(0, nblk, wait_block, 0)

        def finish(m, c):
            r0 = pl.multiple_of(m * tm, tm)
            xsb[pl.ds(r0, tm), :] = xs32[pl.ds(r0, tm), :].astype(BF16)
            acc[pl.ds(r0, tm), :] = jnp.broadcast_to(bd_ref[0], (tm, acc.shape[1]))
            return c
        lax.fori_loop(0, nblk, finish, 0)

    @pl.when(nblk > 0)
    def _compute():
        wgb[...] = wg_ref[0].astype(BF16)
        wub[...] = wu_ref[0].astype(BF16)
        wdb[...] = wd_ref[0].astype(BF16)

        def body(m, c):
            r0 = pl.multiple_of(m * tm, tm)
            x = xsb[pl.ds(r0, tm), :]
            g = jnp.minimum(_dot(x, wgb[...]) + bg_ref[0], SWIGLU_LIMIT)
            u = jnp.clip(_dot(x, wub[...]) + bu_ref[0], -SWIGLU_LIMIT, SWIGLU_LIMIT)
            a = (u + 1.0) * (g * jax.nn.sigmoid(SWIGLU_ALPHA * g))
            acc[pl.ds(r0, tm), :] += _dot(a.astype(BF16), wdb[...])
            return c
        lax.fori_loop(0, nblk, body, 0)

    @pl.when(jnp.logical_and(j == nj - 1, nblk > 0))
    def _writeback():
        def issue(m, c):
            out_copy(m).start()
            return c
        lax.fori_loop(0, nblk, issue, 0)

        def finish(m, c):
            out_copy(m).wait()
            return c
        lax.fori_loop(0, nblk, finish, 0)


def _moe(h2, row_tok, item_e, item_start, item_nblk, tail_blk, w_gate_up, b_gate_up, w_down,
         b_down, n_rows):
    N, D = h2.shape
    E, _, F2 = w_gate_up.shape
    F = F2 // 2
    tf = MOE_TF
    nj = F // tf
    n_items = item_e.shape[0]
    rows = MOE_BPI * MOE_TM

    def jj(i, j, inb):
        return jnp.where(inb[i] > 0, j, nj - 1)

    def spec(shape, index):
        return pl.BlockSpec(shape, lambda i, j, ie, ist, inb, tail, tok: index(ie[i], jj(i, j, inb)))

    wg_spec = spec((1, D, tf), lambda e, c: (e, 0, c))
    wu_spec = spec((1, D, tf), lambda e, c: (e, 0, nj + c))
    wd_spec = spec((1, tf, D), lambda e, c: (e, c, 0))
    bg_spec = spec((1, 1, tf), lambda e, c: (e, 0, c))
    bu_spec = spec((1, 1, tf), lambda e, c: (e, 0, nj + c))
    bd_spec = spec((1, 1, D), lambda e, c: (e, 0, 0))
    any_spec = pl.BlockSpec(memory_space=pl.ANY)
    grid_spec = pltpu.PrefetchScalarGridSpec(
        num_scalar_prefetch=5,
        grid=(n_items, nj),
        in_specs=[any_spec, wg_spec, wu_spec, wd_spec, bg_spec, bu_spec, bd_spec],
        out_specs=any_spec,
        scratch_shapes=[pltpu.VMEM((rows, D), F32), pltpu.VMEM((rows, D), BF16),
                        pltpu.VMEM((rows, D), F32),
                        pltpu.VMEM((D, tf), BF16), pltpu.VMEM((D, tf), BF16),
                        pltpu.VMEM((tf, D), BF16), pltpu.VMEM((MOE_TM, D), F32),
                        pltpu.SemaphoreType.DMA(()), pltpu.SemaphoreType.DMA(()),
                        pltpu.SemaphoreType.DMA(())],
    )
    bgu = b_gate_up.reshape(E, 1, F2)
    return pl.pallas_call(
        _moe_kernel,
        grid_spec=grid_spec,
        out_shape=jax.ShapeDtypeStruct((n_rows, D), F32),
        compiler_params=pltpu.CompilerParams(dimension_semantics=("arbitrary", "arbitrary"),
                                             vmem_limit_bytes=VMEM_LIMIT),
        name="moe",
    )(item_e, item_start, item_nblk, tail_blk, row_tok, h2, w_gate_up, w_gate_up, w_down, bgu, bgu,
      b_down.reshape(E, 1, D))


def _combine_kernel(pos_ref, x1_ref, g_ref, ys_hbm, o_ref, buf, sem):
    tm = CMB_TILE
    i = pl.program_id(0)
    base = i * (tm * TOP_K)

    def issue(r, c):
        for kk in range(TOP_K):
            p = pos_ref[base + r * TOP_K + kk]
            pltpu.make_async_copy(ys_hbm.at[pl.ds(p, 1), :], buf.at[kk, pl.ds(r, 1), :], sem).start()
        return c
    lax.fori_loop(0, tm, issue, 0)
    out = x1_ref[...]
    for kk in range(TOP_K):
        pltpu.make_async_copy(ys_hbm.at[pl.ds(0, tm), :], buf.at[kk], sem).wait()
    for kk in range(TOP_K):
        out = out + g_ref[:, kk:kk + 1] * buf[kk]
    o_ref[...] = out


def _combine(x1, gates, ys, dest):
    N, D = x1.shape
    tm = CMB_TILE
    grid_spec = pltpu.PrefetchScalarGridSpec(
        num_scalar_prefetch=1,
        grid=(N // tm,),
        in_specs=[pl.BlockSpec((tm, D), lambda i, pos: (i, 0)),
                  pl.BlockSpec((tm, LANES), lambda i, pos: (i, 0)),
                  pl.BlockSpec(memory_space=pl.ANY)],
        out_specs=pl.BlockSpec((tm, D), lambda i, pos: (i, 0)),
        scratch_shapes=[pltpu.VMEM((TOP_K, tm, D), F32), pltpu.SemaphoreType.DMA(())],
    )
    return pl.pallas_call(
        _combine_kernel,
        grid_spec=grid_spec,
        out_shape=jax.ShapeDtypeStruct((N, D), F32),
        compiler_params=pltpu.CompilerParams(dimension_semantics=("arbitrary",),
                                             vmem_limit_bytes=VMEM_LIMIT),
        name="combine",
    )(dest, x1, gates, ys)


def _routing(top_idx, n_tok):
    tm, bpi = MOE_TM, MOE_BPI
    nk = n_tok * TOP_K
    e_flat = top_idx.reshape(-1)
    onehot = (e_flat[:, None] == jnp.arange(N_EXPERTS, dtype=jnp.int32)[None, :]).astype(jnp.int32)
    csum = jnp.cumsum(onehot, axis=0)
    rank = jnp.take_along_axis(csum, e_flat[:, None], axis=1)[:, 0] - 1
    counts = csum[-1]
    nb = (counts + tm - 1) // tm
    bend = jnp.cumsum(nb)
    bstart = bend - nb
    dest = (bstart[e_flat] * tm + rank).astype(jnp.int32)
    n_blocks = -(-(nk + N_EXPERTS * (tm - 1)) // tm)
    n_rows = n_blocks * tm
    row_tok = jnp.zeros((n_rows,), jnp.int32).at[dest].set(jnp.arange(nk, dtype=jnp.int32) // TOP_K)
    n_items = n_blocks // bpi + N_EXPERTS
    items_e = (nb + bpi - 1) // bpi
    iend = jnp.cumsum(items_e)
    istart = iend - items_e
    slot = jnp.arange(n_items, dtype=jnp.int32)
    valid = slot < iend[-1]
    exp_of = jnp.minimum(jnp.searchsorted(iend, slot, side="right"), N_EXPERTS - 1).astype(jnp.int32)
    last_e = jnp.max(jnp.where(nb > 0, jnp.arange(N_EXPERTS, dtype=jnp.int32), 0))
    local = slot - istart[exp_of]
    item_e = jnp.where(valid, exp_of, last_e).astype(jnp.int32)
    item_start = jnp.where(valid, (bstart[exp_of] + local * bpi) * tm, 0).astype(jnp.int32)
    item_nblk = jnp.where(valid, jnp.minimum(bpi, nb[exp_of] - local * bpi), 0).astype(jnp.int32)
    tail_blk = bend[-1:].astype(jnp.int32)
    return dest, row_tok, item_e, item_start, item_nblk, tail_blk, n_rows


def _rope_lane_tile(v):
    half = MLA_ROPE // 2
    z = jnp.zeros((half,), v.dtype)
    return jnp.concatenate([v[:half], z, v[half:], z]).reshape(1, LANES)


def _rope_cols(w):
    half = MLA_ROPE // 2
    z = jnp.zeros((w.shape[0], half), w.dtype)
    return jnp.concatenate([w[:, :half], z, w[:, half:], z], axis=1)


def _layer(x, mem, positions, attn_norm, w_in, q_a_norm, w_q_b, kv_a_norm, w_kv_b,
           q_nope_norm, q_rope_norm, k_nope_norm, k_rope_norm,
           gmlp_v_norm, w_spatial, b_spatial,
           mem_norm, w_mem_kv, mem_q_norm, mem_k_norm,
           mla_out_norm, gmlp_out_norm, mem_out_norm, w_o,
           ffn_norm, w_router, b_router, w_gate_up, b_gate_up, w_down, b_down):
    B, S, D = x.shape
    N = B * S
    x2 = x.reshape(N, D)

    o1 = Q_LORA
    o2 = o1 + KV_LORA
    o3 = o2 + MLA_ROPE
    o4 = o3 + 2 * GMLP_WIDTH
    w_in_r = jnp.concatenate([w_in[:, :o2], w_in[:, o3:o4], w_in[:, o4:], _rope_cols(w_in[:, o2:o3])],
                             axis=1).astype(BF16)
    wq = w_q_b.reshape(Q_LORA, MLA_HEADS, MLA_NOPE + MLA_ROPE)
    wq_rope = jax.vmap(_rope_cols, in_axes=1, out_axes=1)(wq[:, :, MLA_NOPE:])
    wq_pad = jnp.concatenate([wq[:, :, :MLA_NOPE], wq_rope], axis=2).reshape(Q_LORA, MLA_HEADS * HEAD_PAD)
    wq_pad = wq_pad.astype(BF16)
    half = MLA_ROPE // 2
    inv_freq = ROPE_BASE ** (-jnp.arange(half, dtype=F32) / half)
    zf = jnp.zeros((half,), F32)
    invf_tile = jnp.concatenate([inv_freq, zf, inv_freq, zf]).reshape(1, LANES)
    sgn_tile = jnp.concatenate([-jnp.ones((half,), F32), zf, jnp.ones((half,), F32), zf]).reshape(1, LANES)
    bs_tile = jnp.repeat(b_spatial.T, GMLP_CH, axis=1)

    mk, mv = _mem_kv(mem, mem_norm, w_mem_kv, mem_k_norm)
    cq, ckv, kpe, ogm, cs = _inproj(
        x2, positions.reshape(N, 1), (invf_tile, sgn_tile), attn_norm, w_in_r, q_a_norm, kv_a_norm,
        _rope_lane_tile(k_rope_norm), gmlp_v_norm, w_spatial, bs_tile, mk, mv, mem_q_norm,
        gmlp_out_norm, mem_out_norm, S)
    q, k, v = _qkv(cq, ckv, kpe, cs, wq_pad, w_kv_b.astype(BF16), q_nope_norm,
                   _rope_lane_tile(q_rope_norm), k_nope_norm)
    o_mla = _attention(q, k, v, B, S).reshape(N, MLA_HEADS * MLA_V)

    wo = w_o.astype(BF16)
    wr_pad = jnp.pad(w_router, ((0, 0), (0, LANES - N_EXPERTS)))
    br_pad = jnp.pad(b_router, (0, LANES - N_EXPERTS)).reshape(1, LANES)
    x1, h2, ti, tg = _outproj(x2, o_mla, ogm, mla_out_norm, wo[:MLA_HEADS * MLA_V],
                              wo[MLA_HEADS * MLA_V:], ffn_norm, wr_pad, br_pad)

    dest, row_tok, item_e, item_start, item_nblk, tail_blk, n_rows = _routing(ti[:, :TOP_K], N)
    ys = _moe(h2, row_tok, item_e, item_start, item_nblk, tail_blk, w_gate_up, b_gate_up, w_down,
              b_down, n_rows)
    out = _combine(x1, tg, ys, dest)
    return out.reshape(B, S, D)


def kernel(x, mem, positions, attn_norm, w_in, q_a_norm, w_q_b, kv_a_norm, w_kv_b, q_nope_norm, q_rope_norm, k_nope_norm, k_rope_norm, gmlp_v_norm, w_spatial, b_spatial, mem_norm, w_mem_kv, mem_q_norm, mem_k_norm, mla_out_norm, gmlp_out_norm, mem_out_norm, w_o, ffn_norm, w_router, b_router, w_gate_up, b_gate_up, w_down, b_down):
    depth = attn_norm.shape[0]
    for l in range(depth):
        x = _layer(x, mem, positions, attn_norm[l], w_in[l], q_a_norm[l], w_q_b[l], kv_a_norm[l],
                   w_kv_b[l], q_nope_norm[l], q_rope_norm[l], k_nope_norm[l], k_rope_norm[l],
                   gmlp_v_norm[l], w_spatial[l], b_spatial[l], mem_norm[l], w_mem_kv[l],
                   mem_q_norm[l], mem_k_norm[l], mla_out_norm[l], gmlp_out_norm[l],
                   mem_out_norm[l], w_o[l], ffn_norm[l], w_router[l], b_router[l], w_gate_up[l],
                   b_gate_up[l], w_down[l], b_down[l])
    return x
```

```python
import functools

import jax
import jax.numpy as jnp
import numpy as np
from jax import lax
from jax.experimental import pallas as pl
from jax.experimental.pallas import tpu as pltpu

F32 = jnp.float32
BF16 = jnp.bfloat16

EPS = 1e-6
LANES = 128
VMEM_LIMIT = 56 * 1024 * 1024

MLA_HEADS = 8
MLA_NOPE = 128
MLA_ROPE = 64
MLA_V = 128
Q_LORA = 512
KV_LORA = 512
GMLP_GROUPS = 4
GMLP_CH = 128
GMLP_WIDTH = GMLP_GROUPS * GMLP_CH
CHUNK = 128
MEM_HEADS = 4
MEM_HEAD_DIM = 128
MEM_WIDTH = MEM_HEADS * MEM_HEAD_DIM
N_EXPERTS = 32
TOP_K = 4
SWIGLU_ALPHA = 1.702
SWIGLU_LIMIT = 7.0
ROPE_BASE = 10000.0
HEAD_PAD = 2 * LANES

TOK_TILE = 512
ATT_TQ = 512
ATT_TK = 512
MOE_TM = 256
MOE_BPI = 4
MOE_TF = 256
CMB_TILE = 256


def _rms(x, g, n=None):
    n = x.shape[-1] if n is None else n
    ms = jnp.sum(x * x, axis=-1, keepdims=True) * (1.0 / n)
    return x * lax.rsqrt(ms + EPS) * g


def _gelu(x):
    return 0.5 * x * (1.0 + lax.erf(x * (2.0 ** -0.5)))


def _dot(a, b):
    return jnp.dot(a, b, preferred_element_type=F32)


def _dot_nt(a, b):
    return lax.dot_general(a, b, (((1,), (1,)), ((), ())), preferred_element_type=F32)


def _const_spec(shape):
    nd = len(shape)
    return pl.BlockSpec(shape, lambda *_: (0,) * nd)


def _mem_kv_kernel(mem_ref, g_ref, w_ref, kg_ref, mk_ref, mv_ref):
    m = mem_ref[0]
    hn = _rms(m, g_ref[...]).astype(BF16)
    kv = _dot(hn, w_ref[...])
    for h in range(MEM_HEADS):
        k = kv[:, h * MEM_HEAD_DIM:(h + 1) * MEM_HEAD_DIM]
        mk_ref[0, :, h * MEM_HEAD_DIM:(h + 1) * MEM_HEAD_DIM] = _rms(k, kg_ref[...]).astype(BF16)
    mv_ref[0] = kv[:, MEM_WIDTH:].astype(BF16)


def _mem_kv(mem, mem_norm, w_mem_kv, mem_k_norm):
    B, M, D = mem.shape
    return pl.pallas_call(
        _mem_kv_kernel,
        grid=(B,),
        in_specs=[pl.BlockSpec((1, M, D), lambda b: (b, 0, 0)),
                  _const_spec((1, D)),
                  _const_spec((D, 2 * MEM_WIDTH)),
                  _const_spec((1, MEM_HEAD_DIM))],
        out_specs=[pl.BlockSpec((1, M, MEM_WIDTH), lambda b: (b, 0, 0)),
                   pl.BlockSpec((1, M, MEM_WIDTH), lambda b: (b, 0, 0))],
        out_shape=[jax.ShapeDtypeStruct((B, M, MEM_WIDTH), BF16),
                   jax.ShapeDtypeStruct((B, M, MEM_WIDTH), BF16)],
        compiler_params=pltpu.CompilerParams(dimension_semantics=("arbitrary",),
                                             vmem_limit_bytes=VMEM_LIMIT),
        name="mem_kv",
    )(mem, mem_norm.reshape(1, D), w_mem_kv.astype(BF16), mem_k_norm.reshape(1, MEM_HEAD_DIM))


_O_CQ = 0
_O_CKV = _O_CQ + Q_LORA
_O_U = _O_CKV + KV_LORA
_O_VG = _O_U + GMLP_WIDTH
_O_QM = _O_VG + GMLP_WIDTH
_O_KR = _O_QM + MEM_WIDTH
_IN_COLS_PAD = _O_KR + LANES


def _rope_tile(t, cos, sin_signed):
    return t * cos + pltpu.roll(t, LANES // 2, axis=1) * sin_signed


def _inproj_kernel(x_ref, pos_ref, invf_ref, sgn_ref, an_ref, w_ref, qan_ref, kvan_ref, krn_ref,
                   gvn_ref, ws_ref, bs_ref, mk_ref, mv_ref, mqn_ref, gon_ref, mon_ref,
                   cq_ref, ckv_ref, kpe_ref, ogm_ref, cs_ref):
    tm = x_ref.shape[0]
    h = _rms(x_ref[...], an_ref[...]).astype(BF16)
    z = _dot(h, w_ref[...])

    cq_ref[...] = _rms(z[:, _O_CQ:_O_CQ + Q_LORA], qan_ref[...]).astype(BF16)
    ckv_ref[...] = _rms(z[:, _O_CKV:_O_CKV + KV_LORA], kvan_ref[...]).astype(BF16)

    ang = pos_ref[...].astype(F32) * invf_ref[...]
    cos = jnp.cos(ang)
    sin_signed = jnp.sin(ang) * sgn_ref[...]
    cs_ref[:, :LANES] = cos
    cs_ref[:, LANES:] = sin_signed
    kr = _rms(z[:, _O_KR:_O_KR + LANES], krn_ref[...], MLA_ROPE)
    kpe_ref[...] = _rope_tile(kr, cos, sin_signed).astype(BF16)

    u = _gelu(z[:, _O_U:_O_U + GMLP_WIDTH])
    vg = _gelu(z[:, _O_VG:_O_VG + GMLP_WIDTH])
    vg = _rms(vg, gvn_ref[...]).astype(BF16)
    row = lax.broadcasted_iota(jnp.int32, (CHUNK, CHUNK), 0)
    col = lax.broadcasted_iota(jnp.int32, (CHUNK, CHUNK), 1)
    sp_cols = []
    for g in range(GMLP_GROUPS):
        wsg = jnp.where(col <= row, ws_ref[g], 0.0).astype(BF16)
        sp_rows = [_dot(wsg, vg[c * CHUNK:(c + 1) * CHUNK, g * GMLP_CH:(g + 1) * GMLP_CH])
                   for c in range(tm // CHUNK)]
        sp_cols.append(jnp.concatenate(sp_rows, axis=0))
    sp = jnp.concatenate(sp_cols, axis=1) + jnp.concatenate([bs_ref[...]] * (tm // CHUNK), axis=0)
    ogm_ref[:, :GMLP_WIDTH] = _rms(u * sp, gon_ref[...]).astype(BF16)

    o_heads = []
    for hd in range(MEM_HEADS):
        sl = slice(hd * MEM_HEAD_DIM, (hd + 1) * MEM_HEAD_DIM)
        q = z[:, _O_QM + hd * MEM_HEAD_DIM:_O_QM + (hd + 1) * MEM_HEAD_DIM]
        qn = (_rms(q, mqn_ref[...]) * (MEM_HEAD_DIM ** -0.5)).astype(BF16)
        s = _dot_nt(qn, mk_ref[0, :, sl])
        s = s - jnp.max(s, axis=-1, keepdims=True)
        p = jnp.exp(s)
        p = p / jnp.sum(p, axis=-1, keepdims=True)
        o_heads.append(_dot(p.astype(BF16), mv_ref[0, :, sl]))
    o_mem = jnp.concatenate(o_heads, axis=1)
    ogm_ref[:, GMLP_WIDTH:] = _rms(o_mem, mon_ref[...]).astype(BF16)


def _inproj(x2, pos, tables, attn_norm, w_in_r, q_a_norm, kv_a_norm, krn_tile, gmlp_v_norm,
            w_spatial, bs_tile, mk, mv, mem_q_norm, gmlp_out_norm, mem_out_norm, seq):
    N, D = x2.shape
    tm = TOK_TILE
    tiles_per_seq = seq // tm
    invf_tile, sgn_tile = tables
    M = mk.shape[1]
    row = lambda w: pl.BlockSpec((tm, w), lambda i: (i, 0))
    batch_blk = pl.BlockSpec((1, M, MEM_WIDTH), lambda i: (i // tiles_per_seq, 0, 0))
    return pl.pallas_call(
        _inproj_kernel,
        grid=(N // tm,),
        in_specs=[row(D), row(1), _const_spec((1, LANES)), _const_spec((1, LANES)),
                  _const_spec((1, D)), _const_spec((D, _IN_COLS_PAD)),
                  _const_spec((1, Q_LORA)), _const_spec((1, KV_LORA)), _const_spec((1, LANES)),
                  _const_spec((1, GMLP_WIDTH)), _const_spec((GMLP_GROUPS, CHUNK, CHUNK)),
                  _const_spec((CHUNK, GMLP_WIDTH)), batch_blk, batch_blk,
                  _const_spec((1, MEM_HEAD_DIM)), _const_spec((1, GMLP_WIDTH)),
                  _const_spec((1, MEM_WIDTH))],
        out_specs=[row(Q_LORA), row(KV_LORA), row(LANES), row(GMLP_WIDTH + MEM_WIDTH),
                   row(2 * LANES)],
        out_shape=[jax.ShapeDtypeStruct((N, Q_LORA), BF16),
                   jax.ShapeDtypeStruct((N, KV_LORA), BF16),
                   jax.ShapeDtypeStruct((N, LANES), BF16),
                   jax.ShapeDtypeStruct((N, GMLP_WIDTH + MEM_WIDTH), BF16),
                   jax.ShapeDtypeStruct((N, 2 * LANES), F32)],
        compiler_params=pltpu.CompilerParams(dimension_semantics=("arbitrary",),
                                             vmem_limit_bytes=VMEM_LIMIT),
        name="inproj",
    )(x2, pos, invf_tile, sgn_tile, attn_norm.reshape(1, D), w_in_r,
      q_a_norm.reshape(1, -1), kv_a_norm.reshape(1, -1), krn_tile, gmlp_v_norm.reshape(1, -1),
      w_spatial, bs_tile, mk, mv, mem_q_norm.reshape(1, -1), gmlp_out_norm.reshape(1, -1),
      mem_out_norm.reshape(1, -1))


def _qkv_kernel(cq_ref, ckv_ref, kpe_ref, cs_ref, wq_ref, wkv_ref, qnn_ref, qrn_ref, knn_ref,
                q_ref, k_ref, v_ref):
    cos = cs_ref[:, :LANES]
    sin_signed = cs_ref[:, LANES:]
    scale = (MLA_NOPE + MLA_ROPE) ** -0.5
    qr = _dot(cq_ref[...], wq_ref[...])
    kvr = _dot(ckv_ref[...], wkv_ref[...])
    kpe = kpe_ref[...]
    for h in range(MLA_HEADS):
        o = h * HEAD_PAD
        qn = _rms(qr[:, o:o + MLA_NOPE], qnn_ref[...]) * scale
        qt = _rms(qr[:, o + MLA_NOPE:o + HEAD_PAD], qrn_ref[...], MLA_ROPE)
        qt = _rope_tile(qt, cos, sin_signed) * scale
        q_ref[:, o:o + MLA_NOPE] = qn.astype(BF16)
        q_ref[:, o + MLA_NOPE:o + HEAD_PAD] = qt.astype(BF16)
        ko = h * (MLA_NOPE + MLA_V)
        k_ref[:, o:o + MLA_NOPE] = _rms(kvr[:, ko:ko + MLA_NOPE], knn_ref[...]).astype(BF16)
        k_ref[:, o + MLA_NOPE:o + HEAD_PAD] = kpe
        v_ref[:, h * MLA_V:(h + 1) * MLA_V] = kvr[:, ko + MLA_NOPE:ko + MLA_NOPE + MLA_V].astype(BF16)


def _qkv(cq, ckv, kpe, cs, wq_pad, wkv, q_nope_norm, qrn_tile, k_nope_norm):
    N = cq.shape[0]
    tm = TOK_TILE
    row = lambda w: pl.BlockSpec((tm, w), lambda i: (i, 0))
    return pl.pallas_call(
        _qkv_kernel,
        grid=(N // tm,),
        in_specs=[row(Q_LORA), row(KV_LORA), row(LANES), row(2 * LANES),
                  _const_spec(wq_pad.shape), _const_spec(wkv.shape),
                  _const_spec((1, MLA_NOPE)), _const_spec((1, LANES)), _const_spec((1, MLA_NOPE))],
        out_specs=[row(MLA_HEADS * HEAD_PAD), row(MLA_HEADS * HEAD_PAD), row(MLA_HEADS * MLA_V)],
        out_shape=[jax.ShapeDtypeStruct((N, MLA_HEADS * HEAD_PAD), BF16),
                   jax.ShapeDtypeStruct((N, MLA_HEADS * HEAD_PAD), BF16),
                   jax.ShapeDtypeStruct((N, MLA_HEADS * MLA_V), BF16)],
        compiler_params=pltpu.CompilerParams(dimension_semantics=("arbitrary",),
                                             vmem_limit_bytes=VMEM_LIMIT),
        name="qkv",
    )(cq, ckv, kpe, cs, wq_pad, wkv, q_nope_norm.reshape(1, -1), qrn_tile,
      k_nope_norm.reshape(1, -1))


def _attn_kernel(q_ref, k_ref, v_ref, o_ref):
    tq, tk = ATT_TQ, ATT_TK
    qi = pl.program_id(2)
    q = q_ref[0]

    def step(kb, carry, masked):
        m, l, acc = carry
        k0 = pl.multiple_of(kb * tk, tk)
        s = _dot_nt(q, k_ref[0, pl.ds(k0, tk), :])
        if masked:
            qpos = qi * tq + lax.broadcasted_iota(jnp.int32, (tq, tk), 0)
            kpos = kb * tk + lax.broadcasted_iota(jnp.int32, (tq, tk), 1)
            s = jnp.where(kpos <= qpos, s, -jnp.inf)
        m_new = jnp.maximum(m, jnp.max(s, axis=-1, keepdims=True))
        alpha = jnp.exp(m - m_new)
        p = jnp.exp(s - m_new)
        l = alpha * l + jnp.sum(p, axis=-1, keepdims=True)
        acc = alpha * acc + _dot(p.astype(BF16), v_ref[0, pl.ds(k0, tk), :])
        return m_new, l, acc

    init = (jnp.full((tq, 1), -jnp.inf, F32), jnp.zeros((tq, 1), F32), jnp.zeros((tq, MLA_V), F32))
    n_full = (qi * tq) // tk
    carry = lax.fori_loop(0, n_full, lambda kb, c: step(kb, c, False), init)
    for d in range(tq // tk):
        carry = step(n_full + d, carry, True)
    _, l, acc = carry
    o_ref[0] = (acc / l).astype(BF16)


def _attention(q, k, v, batch, seq):
    q3 = q.reshape(batch, seq, MLA_HEADS * HEAD_PAD)
    k3 = k.reshape(batch, seq, MLA_HEADS * HEAD_PAD)
    v3 = v.reshape(batch, seq, MLA_HEADS * MLA_V)
    return pl.pallas_call(
        _attn_kernel,
        grid=(batch, MLA_HEADS, seq // ATT_TQ),
        in_specs=[pl.BlockSpec((1, ATT_TQ, HEAD_PAD), lambda b, h, i: (b, i, h)),
                  pl.BlockSpec((1, seq, HEAD_PAD), lambda b, h, i: (b, 0, h)),
                  pl.BlockSpec((1, seq, MLA_V), lambda b, h, i: (b, 0, h))],
        out_specs=pl.BlockSpec((1, ATT_TQ, MLA_V), lambda b, h, i: (b, i, h)),
        out_shape=jax.ShapeDtypeStruct((batch, seq, MLA_HEADS * MLA_V), BF16),
        compiler_params=pltpu.CompilerParams(
            dimension_semantics=("arbitrary", "arbitrary", "arbitrary"),
            vmem_limit_bytes=VMEM_LIMIT),
        name="attn",
    )(q3, k3, v3)


def _outproj_kernel(x_ref, oa_ref, ogm_ref, aon_ref, woa_ref, wob_ref, fn_ref, wr_ref, br_ref,
                    x1_ref, h2_ref, ti_ref, tg_ref):
    tm = x_ref.shape[0]
    oa = _rms(oa_ref[...].astype(F32), aon_ref[...]).astype(BF16)
    x1 = x_ref[...] + _dot(oa, woa_ref[...]) + _dot(ogm_ref[...], wob_ref[...])
    x1_ref[...] = x1
    h2 = _rms(x1, fn_ref[...])
    h2_ref[...] = h2
    h_hi = h2.astype(BF16)
    h_lo = (h2 - h_hi.astype(F32)).astype(BF16)
    wr = wr_ref[...]
    w_hi = wr.astype(BF16)
    w_lo = (wr - w_hi.astype(F32)).astype(BF16)
    logits = _dot(h_hi, w_hi) + (_dot(h_lo, w_hi) + _dot(h_hi, w_lo)) + br_ref[...]
    lane = lax.broadcasted_iota(jnp.int32, (tm, LANES), 1)
    lg = jnp.where(lane < N_EXPERTS, logits, -jnp.inf)
    vals, idxs = [], []
    for _ in range(TOP_K):
        m = jnp.max(lg, axis=-1, keepdims=True)
        am = jnp.min(jnp.where(lg == m, lane, LANES), axis=-1, keepdims=True)
        vals.append(m)
        idxs.append(am)
        lg = jnp.where(lane == am, -jnp.inf, lg)
    es = [jnp.exp(v - vals[0]) for v in vals]
    denom = es[0] + es[1] + es[2] + es[3]
    ti = jnp.zeros((tm, LANES), jnp.int32)
    tg = jnp.zeros((tm, LANES), F32)
    for kk in range(TOP_K):
        ti = jnp.where(lane == kk, idxs[kk], ti)
        tg = jnp.where(lane == kk, es[kk] / denom, tg)
    ti_ref[...] = ti
    tg_ref[...] = tg


def _outproj(x2, o_mla, ogm, mla_out_norm, wo_a, wo_b, ffn_norm, wr_pad, br_pad):
    N, D = x2.shape
    tm = TOK_TILE
    row = lambda w: pl.BlockSpec((tm, w), lambda i: (i, 0))
    wa = o_mla.shape[1]
    wb = ogm.shape[1]
    return pl.pallas_call(
        _outproj_kernel,
        grid=(N // tm,),
        in_specs=[row(D), row(wa), row(wb), _const_spec((1, wa)), _const_spec((wa, D)),
                  _const_spec((wb, D)), _const_spec((1, D)), _const_spec((D, LANES)),
                  _const_spec((1, LANES))],
        out_specs=[row(D), row(D), row(LANES), row(LANES)],
        out_shape=[jax.ShapeDtypeStruct((N, D), F32), jax.ShapeDtypeStruct((N, D), F32),
                   jax.ShapeDtypeStruct((N, LANES), jnp.int32),
                   jax.ShapeDtypeStruct((N, LANES), F32)],
        compiler_params=pltpu.CompilerParams(dimension_semantics=("arbitrary",),
                                             vmem_limit_bytes=VMEM_LIMIT),
        name="outproj",
    )(x2, o_mla, ogm, mla_out_norm.reshape(1, -1), wo_a, wo_b, ffn_norm.reshape(1, -1), wr_pad,
      br_pad)


def _moe_kernel(ie_ref, ist_ref, inb_ref, tail_ref, tok_ref,
                h2_hbm, wg_ref, wu_ref, wd_ref, bg_ref, bu_ref, bd_ref,
                ys_hbm,
                xs32, xsb, acc, zblk, sem_g, sem_o, sem_z):
    tm = MOE_TM
    i = pl.program_id(0)
    j = pl.program_id(1)
    nj = pl.num_programs(1)
    nblk = inb_ref[i]
    start = ist_ref[i]
    n_blocks = ys_hbm.shape[0] // tm

    def zero_copy(b):
        d0 = pl.multiple_of(b * tm, tm)
        return pltpu.make_async_copy(zblk, ys_hbm.at[pl.ds(d0, tm), :], sem_z)

    @pl.when(jnp.logical_and(i == 0, j == 0))
    def _zero_tail_start():
        zblk[...] = jnp.zeros(zblk.shape, zblk.dtype)

        def issue(b, c):
            zero_copy(b).start()
            return c
        lax.fori_loop(tail_ref[0], n_blocks, issue, 0)

    @pl.when(jnp.logical_and(i == pl.num_programs(0) - 1, j == nj - 1))
    def _zero_tail_wait():
        def finish(b, c):
            zero_copy(b).wait()
            return c
        lax.fori_loop(tail_ref[0], n_blocks, finish, 0)

    def row_copy(r):
        t = tok_ref[start + r]
        return pltpu.make_async_copy(h2_hbm.at[pl.ds(t, 1), :], xs32.at[pl.ds(r, 1), :], sem_g)

    def block_wait_copy(m):
        r0 = pl.multiple_of(m * tm, tm)
        return pltpu.make_async_copy(h2_hbm.at[pl.ds(0, tm), :], xs32.at[pl.ds(r0, tm), :], sem_g)

    def out_copy(m):
        r0 = pl.multiple_of(m * tm, tm)
        d0 = pl.multiple_of(start + m * tm, tm)
        return pltpu.make_async_copy(acc.at[pl.ds(r0, tm), :], ys_hbm.at[pl.ds(d0, tm), :], sem_o)

    @pl.when(jnp.logical_and(j == 0, nblk > 0))
    def _gather():
        def issue(r, c):
            row_copy(r).start()
            return c
        lax.fori_loop(0, nblk * tm, issue, 0)

        def wait_block(m, c):
            block_wait_copy(m).wait()
            return c
        lax.fori_loop(0, nblk, wait_block, 0)

        def finish(m, c):
            r0 = pl.multiple_of(m * tm, tm)
            xsb[pl.ds(r0, tm), :] = xs32[pl.ds(r0, tm), :].astype(BF16)
            acc[pl.ds(r0, tm), :] = jnp.broadcast_to(bd_ref[0], (tm, acc.shape[1]))
            return c
        lax.fori_loop(0, nblk, finish, 0)

    @pl.when(nblk > 0)
    def _compute():
        def ffn_rows(r0, n):
            x = xsb[pl.ds(r0, n), :]
            g = jnp.minimum(_dot(x, wg_ref[0].astype(BF16)) + bg_ref[0], SWIGLU_LIMIT)
            u = jnp.clip(_dot(x, wu_ref[0].astype(BF16)) + bu_ref[0], -SWIGLU_LIMIT, SWIGLU_LIMIT)
            a = (u + 1.0) * (g * jax.nn.sigmoid(SWIGLU_ALPHA * g))
            acc[pl.ds(r0, n), :] += _dot(a.astype(BF16), wd_ref[0].astype(BF16))

        def pair(m, c):
            ffn_rows(pl.multiple_of(m * (2 * tm), 2 * tm), 2 * tm)
            return c
        lax.fori_loop(0, nblk // 2, pair, 0)

        @pl.when(nblk % 2 == 1)
        def _odd_block():
            ffn_rows(pl.multiple_of((nblk - 1) * tm, tm), tm)

    @pl.when(jnp.logical_and(j == nj - 1, nblk > 0))
    def _writeback():
        def issue(m, c):
            out_copy(m).start()
            return c
        lax.fori_loop(0, nblk, issue, 0)

        def finish(m, c):
            out_copy(m).wait()
            return c
        lax.fori_loop(0, nblk, finish, 0)


def _moe(h2, row_tok, item_e, item_start, item_nblk, tail_blk, w_gate_up, b_gate_up, w_down,
         b_down, n_rows):
    N, D = h2.shape
    E, _, F2 = w_gate_up.shape
    F = F2 // 2
    tf = MOE_TF
    nj = F // tf
    n_items = item_e.shape[0]
    rows = MOE_BPI * MOE_TM

    def jj(i, j, inb):
        return jnp.where(inb[i] > 0, j, nj - 1)

    def spec(shape, index):
        return pl.BlockSpec(shape, lambda i, j, ie, ist, inb, tail, tok: index(ie[i], jj(i, j, inb)))

    wg_spec = spec((1, D, tf), lambda e, c: (e, 0, c))
    wu_spec = spec((1, D, tf), lambda e, c: (e, 0, nj + c))
    wd_spec = spec((1, tf, D), lambda e, c: (e, c, 0))
    bg_spec = spec((1, 1, tf), lambda e, c: (e, 0, c))
    bu_spec = spec((1, 1, tf), lambda e, c: (e, 0, nj + c))
    bd_spec = spec((1, 1, D), lambda e, c: (e, 0, 0))
    any_spec = pl.BlockSpec(memory_space=pl.ANY)
    grid_spec = pltpu.PrefetchScalarGridSpec(
        num_scalar_prefetch=5,
        grid=(n_items, nj),
        in_specs=[any_spec, wg_spec, wu_spec, wd_spec, bg_spec, bu_spec, bd_spec],
        out_specs=any_spec,
        scratch_shapes=[pltpu.VMEM((rows, D), F32), pltpu.VMEM((rows, D), BF16),
                        pltpu.VMEM((rows, D), F32), pltpu.VMEM((MOE_TM, D), F32),
                        pltpu.SemaphoreType.DMA(()), pltpu.SemaphoreType.DMA(()),
                        pltpu.SemaphoreType.DMA(())],
    )
    bgu = b_gate_up.reshape(E, 1, F2)
    return pl.pallas_call(
        _moe_kernel,
        grid_spec=grid_spec,
        out_shape=jax.ShapeDtypeStruct((n_rows, D), F32),
        compiler_params=pltpu.CompilerParams(dimension_semantics=("arbitrary", "arbitrary"),
                                             vmem_limit_bytes=VMEM_LIMIT),
        name="moe",
    )(item_e, item_start, item_nblk, tail_blk, row_tok, h2, w_gate_up, w_gate_up, w_down, bgu, bgu,
      b_down.reshape(E, 1, D))


def _combine_kernel(pos_ref, x1_ref, g_ref, ys_hbm, o_ref, buf, sem):
    tm = CMB_TILE
    i = pl.program_id(0)
    base = i * (tm * TOP_K)

    def issue(r, c):
        for kk in range(TOP_K):
            p = pos_ref[base + r * TOP_K + kk]
            pltpu.make_async_copy(ys_hbm.at[pl.ds(p, 1), :], buf.at[kk, pl.ds(r, 1), :], sem).start()
        return c
    lax.fori_loop(0, tm, issue, 0)
    out = x1_ref[...]
    for kk in range(TOP_K):
        pltpu.make_async_copy(ys_hbm.at[pl.ds(0, tm), :], buf.at[kk], sem).wait()
    for kk in range(TOP_K):
        out = out + g_ref[:, kk:kk + 1] * buf[kk]
    o_ref[...] = out


def _combine(x1, gates, ys, dest):
    N, D = x1.shape
    tm = CMB_TILE
    grid_spec = pltpu.PrefetchScalarGridSpec(
        num_scalar_prefetch=1,
        grid=(N // tm,),
        in_specs=[pl.BlockSpec((tm, D), lambda i, pos: (i, 0)),
                  pl.BlockSpec((tm, LANES), lambda i, pos: (i, 0)),
                  pl.BlockSpec(memory_space=pl.ANY)],
        out_specs=pl.BlockSpec((tm, D), lambda i, pos: (i, 0)),
        scratch_shapes=[pltpu.VMEM((TOP_K, tm, D), F32), pltpu.SemaphoreType.DMA(())],
    )
    return pl.pallas_call(
        _combine_kernel,
        grid_spec=grid_spec,
        out_shape=jax.ShapeDtypeStruct((N, D), F32),
        compiler_params=pltpu.CompilerParams(dimension_semantics=("arbitrary",),
                                             vmem_limit_bytes=VMEM_LIMIT),
        name="combine",
    )(dest, x1, gates, ys)


def _routing(top_idx, n_tok):
    tm, bpi = MOE_TM, MOE_BPI
    nk = n_tok * TOP_K
    e_flat = top_idx.reshape(-1)
    onehot = (e_flat[:, None] == jnp.arange(N_EXPERTS, dtype=jnp.int32)[None, :]).astype(jnp.int32)
    csum = jnp.cumsum(onehot, axis=0)
    rank = jnp.take_along_axis(csum, e_flat[:, None], axis=1)[:, 0] - 1
    counts = csum[-1]
    nb = (counts + tm - 1) // tm
    bend = jnp.cumsum(nb)
    bstart = bend - nb
    dest = (bstart[e_flat] * tm + rank).astype(jnp.int32)
    n_blocks = -(-(nk + N_EXPERTS * (tm - 1)) // tm)
    n_rows = n_blocks * tm
    row_tok = jnp.zeros((n_rows,), jnp.int32).at[dest].set(jnp.arange(nk, dtype=jnp.int32) // TOP_K)
    n_items = n_blocks // bpi + N_EXPERTS
    items_e = (nb + bpi - 1) // bpi
    iend = jnp.cumsum(items_e)
    istart = iend - items_e
    slot = jnp.arange(n_items, dtype=jnp.int32)
    valid = slot < iend[-1]
    exp_of = jnp.minimum(jnp.searchsorted(iend, slot, side="right"), N_EXPERTS - 1).astype(jnp.int32)
    last_e = jnp.max(jnp.where(nb > 0, jnp.arange(N_EXPERTS, dtype=jnp.int32), 0))
    local = slot - istart[exp_of]
    item_e = jnp.where(valid, exp_of, last_e).astype(jnp.int32)
    item_start = jnp.where(valid, (bstart[exp_of] + local * bpi) * tm, 0).astype(jnp.int32)
    item_nblk = jnp.where(valid, jnp.minimum(bpi, nb[exp_of] - local * bpi), 0).astype(jnp.int32)
    tail_blk = bend[-1:].astype(jnp.int32)
    return dest, row_tok, item_e, item_start, item_nblk, tail_blk, n_rows


def _rope_lane_tile(v):
    half = MLA_ROPE // 2
    z = jnp.zeros((half,), v.dtype)
    return jnp.concatenate([v[:half], z, v[half:], z]).reshape(1, LANES)


def _rope_cols(w):
    half = MLA_ROPE // 2
    z = jnp.zeros((w.shape[0], half), w.dtype)
    return jnp.concatenate([w[:, :half], z, w[:, half:], z], axis=1)


def _layer(x, mem, positions, attn_norm, w_in, q_a_norm, w_q_b, kv_a_norm, w_kv_b,
           q_nope_norm, q_rope_norm, k_nope_norm, k_rope_norm,
           gmlp_v_norm, w_spatial, b_spatial,
           mem_norm, w_mem_kv, mem_q_norm, mem_k_norm,
           mla_out_norm, gmlp_out_norm, mem_out_norm, w_o,
           ffn_norm, w_router, b_router, w_gate_up, b_gate_up, w_down, b_down):
    B, S, D = x.shape
    N = B * S
    x2 = x.reshape(N, D)

    o1 = Q_LORA
    o2 = o1 + KV_LORA
    o3 = o2 + MLA_ROPE
    o4 = o3 + 2 * GMLP_WIDTH
    w_in_r = jnp.concatenate([w_in[:, :o2], w_in[:, o3:o4], w_in[:, o4:], _rope_cols(w_in[:, o2:o3])],
                             axis=1).astype(BF16)
    wq = w_q_b.reshape(Q_LORA, MLA_HEADS, MLA_NOPE + MLA_ROPE)
    wq_rope = jax.vmap(_rope_cols, in_axes=1, out_axes=1)(wq[:, :, MLA_NOPE:])
    wq_pad = jnp.concatenate([wq[:, :, :MLA_NOPE], wq_rope], axis=2).reshape(Q_LORA, MLA_HEADS * HEAD_PAD)
    wq_pad = wq_pad.astype(BF16)
    half = MLA_ROPE // 2
    inv_freq = ROPE_BASE ** (-jnp.arange(half, dtype=F32) / half)
    zf = jnp.zeros((half,), F32)
    invf_tile = jnp.concatenate([inv_freq, zf, inv_freq, zf]).reshape(1, LANES)
    sgn_tile = jnp.concatenate([-jnp.ones((half,), F32), zf, jnp.ones((half,), F32), zf]).reshape(1, LANES)
    bs_tile = jnp.repeat(b_spatial.T, GMLP_CH, axis=1)

    mk, mv = _mem_kv(mem, mem_norm, w_mem_kv, mem_k_norm)
    cq, ckv, kpe, ogm, cs = _inproj(
        x2, positions.reshape(N, 1), (invf_tile, sgn_tile), attn_norm, w_in_r, q_a_norm, kv_a_norm,
        _rope_lane_tile(k_rope_norm), gmlp_v_norm, w_spatial, bs_tile, mk, mv, mem_q_norm,
        gmlp_out_norm, mem_out_norm, S)
    q, k, v = _qkv(cq, ckv, kpe, cs, wq_pad, w_kv_b.astype(BF16), q_nope_norm,
                   _rope_lane_tile(q_rope_norm), k_nope_norm)
    o_mla = _attention(q, k, v, B, S).reshape(N, MLA_HEADS * MLA_V)

    wo = w_o.astype(BF16)
    wr_pad = jnp.pad(w_router, ((0, 0), (0, LANES - N_EXPERTS)))
    br_pad = jnp.pad(b_router, (0, LANES - N_EXPERTS)).reshape(1, LANES)
    x1, h2, ti, tg = _outproj(x2, o_mla, ogm, mla_out_norm, wo[:MLA_HEADS * MLA_V],
                              wo[MLA_HEADS * MLA_V:], ffn_norm, wr_pad, br_pad)

    dest, row_tok, item_e, item_start, item_nblk, tail_blk, n_rows = _routing(ti[:, :TOP_K], N)
    ys = _moe(h2, row_tok, item_e, item_start, item_nblk, tail_blk, w_gate_up, b_gate_up, w_down,
              b_down, n_rows)
    out = _combine(x1, tg, ys, dest)
    return out.reshape(B, S, D)


def kernel(x, mem, positions, attn_norm, w_in, q_a_norm, w_q_b, kv_a_norm, w_kv_b, q_nope_norm, q_rope_norm, k_nope_norm, k_rope_norm, gmlp_v_norm, w_spatial, b_spatial, mem_norm, w_mem_kv, mem_q_norm, mem_k_norm, mla_out_norm, gmlp_out_norm, mem_out_norm, w_o, ffn_norm, w_router, b_router, w_gate_up, b_gate_up, w_down, b_down):
    depth = attn_norm.shape[0]
    for l in range(depth):
        x = _layer(x, mem, positions, attn_norm[l], w_in[l], q_a_norm[l], w_q_b[l], kv_a_norm[l],
                   w_kv_b[l], q_nope_norm[l], q_rope_norm[l], k_nope_norm[l], k_rope_norm[l],
                   gmlp_v_norm[l], w_spatial[l], b_spatial[l], mem_norm[l], w_mem_kv[l],
                   mem_q_norm[l], mem_k_norm[l], mla_out_norm[l], gmlp_out_norm[l],
                   mem_out_norm[l], w_o[l], ffn_norm[l], w_router[l], b_router[l], w_gate_up[l],
                   b_gate_up[l], w_down[l], b_down[l])
    return x
```

```python
import functools

import jax
import jax.numpy as jnp
import numpy as np
from jax import lax
from jax.experimental import pallas as pl
from jax.experimental.pallas import tpu as pltpu

F32 = jnp.float32
BF16 = jnp.bfloat16

EPS = 1e-6
LANES = 128
VMEM_LIMIT = 56 * 1024 * 1024

MLA_HEADS = 8
MLA_NOPE = 128
MLA_ROPE = 64
MLA_V = 128
Q_LORA = 512
KV_LORA = 512
GMLP_GROUPS = 4
GMLP_CH = 128
GMLP_WIDTH = GMLP_GROUPS * GMLP_CH
CHUNK = 128
MEM_HEADS = 4
MEM_HEAD_DIM = 128
MEM_WIDTH = MEM_HEADS * MEM_HEAD_DIM
N_EXPERTS = 32
TOP_K = 4
SWIGLU_ALPHA = 1.702
SWIGLU_LIMIT = 7.0
ROPE_BASE = 10000.0
HEAD_PAD = 2 * LANES

TOK_TILE = 512
ATT_TQ = 512
ATT_TK = 512
ATT_HPS = 2
MOE_TM = 256
MOE_BPI = 4
MOE_TF = 256
CMB_TILE = 256


def _rms(x, g, n=None):
    n = x.shape[-1] if n is None else n
    ms = jnp.sum(x * x, axis=-1, keepdims=True) * (1.0 / n)
    return x * lax.rsqrt(ms + EPS) * g


def _gelu(x):
    return 0.5 * x * (1.0 + lax.erf(x * (2.0 ** -0.5)))


def _dot(a, b):
    return jnp.dot(a, b, preferred_element_type=F32)


def _dot_nt(a, b):
    return lax.dot_general(a, b, (((1,), (1,)), ((), ())), preferred_element_type=F32)


def _const_spec(shape):
    nd = len(shape)
    return pl.BlockSpec(shape, lambda *_: (0,) * nd)


def _mem_kv_kernel(mem_ref, g_ref, w_ref, kg_ref, mk_ref, mv_ref):
    m = mem_ref[0]
    hn = _rms(m, g_ref[...]).astype(BF16)
    kv = _dot(hn, w_ref[...])
    for h in range(MEM_HEADS):
        k = kv[:, h * MEM_HEAD_DIM:(h + 1) * MEM_HEAD_DIM]
        mk_ref[0, :, h * MEM_HEAD_DIM:(h + 1) * MEM_HEAD_DIM] = _rms(k, kg_ref[...]).astype(BF16)
    mv_ref[0] = kv[:, MEM_WIDTH:].astype(BF16)


def _mem_kv(mem, mem_norm, w_mem_kv, mem_k_norm):
    B, M, D = mem.shape
    return pl.pallas_call(
        _mem_kv_kernel,
        grid=(B,),
        in_specs=[pl.BlockSpec((1, M, D), lambda b: (b, 0, 0)),
                  _const_spec((1, D)),
                  _const_spec((D, 2 * MEM_WIDTH)),
                  _const_spec((1, MEM_HEAD_DIM))],
        out_specs=[pl.BlockSpec((1, M, MEM_WIDTH), lambda b: (b, 0, 0)),
                   pl.BlockSpec((1, M, MEM_WIDTH), lambda b: (b, 0, 0))],
        out_shape=[jax.ShapeDtypeStruct((B, M, MEM_WIDTH), BF16),
                   jax.ShapeDtypeStruct((B, M, MEM_WIDTH), BF16)],
        compiler_params=pltpu.CompilerParams(dimension_semantics=("arbitrary",),
                                             vmem_limit_bytes=VMEM_LIMIT),
        name="mem_kv",
    )(mem, mem_norm.reshape(1, D), w_mem_kv.astype(BF16), mem_k_norm.reshape(1, MEM_HEAD_DIM))


_O_CQ = 0
_O_CKV = _O_CQ + Q_LORA
_O_U = _O_CKV + KV_LORA
_O_VG = _O_U + GMLP_WIDTH
_O_QM = _O_VG + GMLP_WIDTH
_O_KR = _O_QM + MEM_WIDTH
_IN_COLS_PAD = _O_KR + LANES


def _rope_tile(t, cos, sin_signed):
    return t * cos + pltpu.roll(t, LANES // 2, axis=1) * sin_signed


def _inproj_kernel(x_ref, pos_ref, invf_ref, sgn_ref, an_ref, w_ref, qan_ref, kvan_ref, krn_ref,
                   gvn_ref, ws_ref, bs_ref, mk_ref, mv_ref, mqn_ref, gon_ref, mon_ref,
                   cq_ref, ckv_ref, kpe_ref, ogm_ref, cs_ref):
    tm = x_ref.shape[0]
    h = _rms(x_ref[...], an_ref[...]).astype(BF16)
    z = _dot(h, w_ref[...])

    cq_ref[...] = _rms(z[:, _O_CQ:_O_CQ + Q_LORA], qan_ref[...]).astype(BF16)
    ckv_ref[...] = _rms(z[:, _O_CKV:_O_CKV + KV_LORA], kvan_ref[...]).astype(BF16)

    ang = pos_ref[...].astype(F32) * invf_ref[...]
    cos = jnp.cos(ang)
    sin_signed = jnp.sin(ang) * sgn_ref[...]
    cs_ref[:, :LANES] = cos
    cs_ref[:, LANES:] = sin_signed
    kr = _rms(z[:, _O_KR:_O_KR + LANES], krn_ref[...], MLA_ROPE)
    kpe_ref[...] = _rope_tile(kr, cos, sin_signed).astype(BF16)

    u = _gelu(z[:, _O_U:_O_U + GMLP_WIDTH])
    vg = _gelu(z[:, _O_VG:_O_VG + GMLP_WIDTH])
    vg = _rms(vg, gvn_ref[...]).astype(BF16)
    row = lax.broadcasted_iota(jnp.int32, (CHUNK, CHUNK), 0)
    col = lax.broadcasted_iota(jnp.int32, (CHUNK, CHUNK), 1)
    sp_cols = []
    for g in range(GMLP_GROUPS):
        wsg = jnp.where(col <= row, ws_ref[g], 0.0).astype(BF16)
        sp_rows = [_dot(wsg, vg[c * CHUNK:(c + 1) * CHUNK, g * GMLP_CH:(g + 1) * GMLP_CH])
                   for c in range(tm // CHUNK)]
        sp_cols.append(jnp.concatenate(sp_rows, axis=0))
    sp = jnp.concatenate(sp_cols, axis=1) + jnp.concatenate([bs_ref[...]] * (tm // CHUNK), axis=0)
    ogm_ref[:, :GMLP_WIDTH] = _rms(u * sp, gon_ref[...]).astype(BF16)

    o_heads = []
    for hd in range(MEM_HEADS):
        sl = slice(hd * MEM_HEAD_DIM, (hd + 1) * MEM_HEAD_DIM)
        q = z[:, _O_QM + hd * MEM_HEAD_DIM:_O_QM + (hd + 1) * MEM_HEAD_DIM]
        qn = (_rms(q, mqn_ref[...]) * (MEM_HEAD_DIM ** -0.5)).astype(BF16)
        s = _dot_nt(qn, mk_ref[0, :, sl])
        s = s - jnp.max(s, axis=-1, keepdims=True)
        p = jnp.exp(s)
        p = p / jnp.sum(p, axis=-1, keepdims=True)
        o_heads.append(_dot(p.astype(BF16), mv_ref[0, :, sl]))
    o_mem = jnp.concatenate(o_heads, axis=1)
    ogm_ref[:, GMLP_WIDTH:] = _rms(o_mem, mon_ref[...]).astype(BF16)


def _inproj(x2, pos, tables, attn_norm, w_in_r, q_a_norm, kv_a_norm, krn_tile, gmlp_v_norm,
            w_spatial, bs_tile, mk, mv, mem_q_norm, gmlp_out_norm, mem_out_norm, seq):
    N, D = x2.shape
    tm = TOK_TILE
    tiles_per_seq = seq // tm
    invf_tile, sgn_tile = tables
    M = mk.shape[1]
    row = lambda w: pl.BlockSpec((tm, w), lambda i: (i, 0))
    batch_blk = pl.BlockSpec((1, M, MEM_WIDTH), lambda i: (i // tiles_per_seq, 0, 0))
    return pl.pallas_call(
        _inproj_kernel,
        grid=(N // tm,),
        in_specs=[row(D), row(1), _const_spec((1, LANES)), _const_spec((1, LANES)),
                  _const_spec((1, D)), _const_spec((D, _IN_COLS_PAD)),
                  _const_spec((1, Q_LORA)), _const_spec((1, KV_LORA)), _const_spec((1, LANES)),
                  _const_spec((1, GMLP_WIDTH)), _const_spec((GMLP_GROUPS, CHUNK, CHUNK)),
                  _const_spec((CHUNK, GMLP_WIDTH)), batch_blk, batch_blk,
                  _const_spec((1, MEM_HEAD_DIM)), _const_spec((1, GMLP_WIDTH)),
                  _const_spec((1, MEM_WIDTH))],
        out_specs=[row(Q_LORA), row(KV_LORA), row(LANES), row(GMLP_WIDTH + MEM_WIDTH),
                   row(2 * LANES)],
        out_shape=[jax.ShapeDtypeStruct((N, Q_LORA), BF16),
                   jax.ShapeDtypeStruct((N, KV_LORA), BF16),
                   jax.ShapeDtypeStruct((N, LANES), BF16),
                   jax.ShapeDtypeStruct((N, GMLP_WIDTH + MEM_WIDTH), BF16),
                   jax.ShapeDtypeStruct((N, 2 * LANES), F32)],
        compiler_params=pltpu.CompilerParams(dimension_semantics=("arbitrary",),
                                             vmem_limit_bytes=VMEM_LIMIT),
        name="inproj",
    )(x2, pos, invf_tile, sgn_tile, attn_norm.reshape(1, D), w_in_r,
      q_a_norm.reshape(1, -1), kv_a_norm.reshape(1, -1), krn_tile, gmlp_v_norm.reshape(1, -1),
      w_spatial, bs_tile, mk, mv, mem_q_norm.reshape(1, -1), gmlp_out_norm.reshape(1, -1),
      mem_out_norm.reshape(1, -1))


def _qkv_kernel(cq_ref, ckv_ref, kpe_ref, cs_ref, wq_ref, wkv_ref, qnn_ref, qrn_ref, knn_ref,
                q_ref, k_ref, v_ref):
    cos = cs_ref[:, :LANES]
    sin_signed = cs_ref[:, LANES:]
    scale = (MLA_NOPE + MLA_ROPE) ** -0.5
    qr = _dot(cq_ref[...], wq_ref[...])
    kvr = _dot(ckv_ref[...], wkv_ref[...])
    kpe = kpe_ref[...]
    for h in range(MLA_HEADS):
        o = h * HEAD_PAD
        qn = _rms(qr[:, o:o + MLA_NOPE], qnn_ref[...]) * scale
        qt = _rms(qr[:, o + MLA_NOPE:o + HEAD_PAD], qrn_ref[...], MLA_ROPE)
        qt = _rope_tile(qt, cos, sin_signed) * scale
        q_ref[:, o:o + MLA_NOPE] = qn.astype(BF16)
        q_ref[:, o + MLA_NOPE:o + HEAD_PAD] = qt.astype(BF16)
        ko = h * (MLA_NOPE + MLA_V)
        k_ref[:, o:o + MLA_NOPE] = _rms(kvr[:, ko:ko + MLA_NOPE], knn_ref[...]).astype(BF16)
        k_ref[:, o + MLA_NOPE:o + HEAD_PAD] = kpe
        v_ref[:, h * MLA_V:(h + 1) * MLA_V] = kvr[:, ko + MLA_NOPE:ko + MLA_NOPE + MLA_V].astype(BF16)


def _qkv(cq, ckv, kpe, cs, wq_pad, wkv, q_nope_norm, qrn_tile, k_nope_norm):
    N = cq.shape[0]
    tm = TOK_TILE
    row = lambda w: pl.BlockSpec((tm, w), lambda i: (i, 0))
    return pl.pallas_call(
        _qkv_kernel,
        grid=(N // tm,),
        in_specs=[row(Q_LORA), row(KV_LORA), row(LANES), row(2 * LANES),
                  _const_spec(wq_pad.shape), _const_spec(wkv.shape),
                  _const_spec((1, MLA_NOPE)), _const_spec((1, LANES)), _const_spec((1, MLA_NOPE))],
        out_specs=[row(MLA_HEADS * HEAD_PAD), row(MLA_HEADS * HEAD_PAD), row(MLA_HEADS * MLA_V)],
        out_shape=[jax.ShapeDtypeStruct((N, MLA_HEADS * HEAD_PAD), BF16),
                   jax.ShapeDtypeStruct((N, MLA_HEADS * HEAD_PAD), BF16),
                   jax.ShapeDtypeStruct((N, MLA_HEADS * MLA_V), BF16)],
        compiler_params=pltpu.CompilerParams(dimension_semantics=("arbitrary",),
                                             vmem_limit_bytes=VMEM_LIMIT),
        name="qkv",
    )(cq, ckv, kpe, cs, wq_pad, wkv, q_nope_norm.reshape(1, -1), qrn_tile,
      k_nope_norm.reshape(1, -1))


def _attn_kernel(q_ref, k_ref, v_ref, o_ref):
    tq, tk = ATT_TQ, ATT_TK
    qi = pl.program_id(2)
    qs = [q_ref[0, :, h * HEAD_PAD:(h + 1) * HEAD_PAD] for h in range(ATT_HPS)]

    def head_step(h, kb, carry, masked):
        m, l, acc = carry
        k0 = pl.multiple_of(kb * tk, tk)
        s = _dot_nt(qs[h], k_ref[0, pl.ds(k0, tk), h * HEAD_PAD:(h + 1) * HEAD_PAD])
        if masked:
            qpos = qi * tq + lax.broadcasted_iota(jnp.int32, (tq, tk), 0)
            kpos = kb * tk + lax.broadcasted_iota(jnp.int32, (tq, tk), 1)
            s = jnp.where(kpos <= qpos, s, -jnp.inf)
        m_new = jnp.maximum(m, jnp.max(s, axis=-1, keepdims=True))
        alpha = jnp.exp(m - m_new)
        p = jnp.exp(s - m_new)
        l = alpha * l + jnp.sum(p, axis=-1, keepdims=True)
        pv = _dot(p.astype(BF16), v_ref[0, pl.ds(k0, tk), h * MLA_V:(h + 1) * MLA_V])
        return m_new, l, alpha * acc + pv

    def step(kb, carries, masked):
        return tuple(head_step(h, kb, carries[h], masked) for h in range(ATT_HPS))

    init = tuple((jnp.full((tq, 1), -jnp.inf, F32), jnp.zeros((tq, 1), F32),
                  jnp.zeros((tq, MLA_V), F32)) for _ in range(ATT_HPS))
    n_full = (qi * tq) // tk
    carries = lax.fori_loop(0, n_full, lambda kb, c: step(kb, c, False), init)
    for d in range(tq // tk):
        carries = step(n_full + d, carries, True)
    for h in range(ATT_HPS):
        _, l, acc = carries[h]
        o_ref[0, :, h * MLA_V:(h + 1) * MLA_V] = (acc / l).astype(BF16)


def _attention(q, k, v, batch, seq):
    q3 = q.reshape(batch, seq, MLA_HEADS * HEAD_PAD)
    k3 = k.reshape(batch, seq, MLA_HEADS * HEAD_PAD)
    v3 = v.reshape(batch, seq, MLA_HEADS * MLA_V)
    hp, hv = ATT_HPS * HEAD_PAD, ATT_HPS * MLA_V
    return pl.pallas_call(
        _attn_kernel,
        grid=(batch, MLA_HEADS // ATT_HPS, seq // ATT_TQ),
        in_specs=[pl.BlockSpec((1, ATT_TQ, hp), lambda b, h, i: (b, i, h)),
                  pl.BlockSpec((1, seq, hp), lambda b, h, i: (b, 0, h)),
                  pl.BlockSpec((1, seq, hv), lambda b, h, i: (b, 0, h))],
        out_specs=pl.BlockSpec((1, ATT_TQ, hv), lambda b, h, i: (b, i, h)),
        out_shape=jax.ShapeDtypeStruct((batch, seq, MLA_HEADS * MLA_V), BF16),
        compiler_params=pltpu.CompilerParams(
            dimension_semantics=("arbitrary", "arbitrary", "arbitrary"),
            vmem_limit_bytes=VMEM_LIMIT),
        name="attn",
    )(q3, k3, v3)


def _outproj_kernel(x_ref, oa_ref, ogm_ref, aon_ref, woa_ref, wob_ref, fn_ref, wr_ref, br_ref,
                    x1_ref, h2_ref, ti_ref, tg_ref):
    tm = x_ref.shape[0]
    oa = _rms(oa_ref[...].astype(F32), aon_ref[...]).astype(BF16)
    x1 = x_ref[...] + _dot(oa, woa_ref[...]) + _dot(ogm_ref[...], wob_ref[...])
    x1_ref[...] = x1
    h2 = _rms(x1, fn_ref[...])
    h2_ref[...] = h2
    h_hi = h2.astype(BF16)
    h_lo = (h2 - h_hi.astype(F32)).astype(BF16)
    wr = wr_ref[...]
    w_hi = wr.astype(BF16)
    w_lo = (wr - w_hi.astype(F32)).astype(BF16)
    logits = _dot(h_hi, w_hi) + (_dot(h_lo, w_hi) + _dot(h_hi, w_lo)) + br_ref[...]
    lane = lax.broadcasted_iota(jnp.int32, (tm, LANES), 1)
    lg = jnp.where(lane < N_EXPERTS, logits, -jnp.inf)
    vals, idxs = [], []
    for _ in range(TOP_K):
        m = jnp.max(lg, axis=-1, keepdims=True)
        am = jnp.min(jnp.where(lg == m, lane, LANES), axis=-1, keepdims=True)
        vals.append(m)
        idxs.append(am)
        lg = jnp.where(lane == am, -jnp.inf, lg)
    es = [jnp.exp(v - vals[0]) for v in vals]
    denom = es[0] + es[1] + es[2] + es[3]
    ti = jnp.zeros((tm, LANES), jnp.int32)
    tg = jnp.zeros((tm, LANES), F32)
    for kk in range(TOP_K):
        ti = jnp.where(lane == kk, idxs[kk], ti)
        tg = jnp.where(lane == kk, es[kk] / denom, tg)
    ti_ref[...] = ti
    tg_ref[...] = tg


def _outproj(x2, o_mla, ogm, mla_out_norm, wo_a, wo_b, ffn_norm, wr_pad, br_pad):
    N, D = x2.shape
    tm = TOK_TILE
    row = lambda w: pl.BlockSpec((tm, w), lambda i: (i, 0))
    wa = o_mla.shape[1]
    wb = ogm.shape[1]
    return pl.pallas_call(
        _outproj_kernel,
        grid=(N // tm,),
        in_specs=[row(D), row(wa), row(wb), _const_spec((1, wa)), _const_spec((wa, D)),
                  _const_spec((wb, D)), _const_spec((1, D)), _const_spec((D, LANES)),
                  _const_spec((1, LANES))],
        out_specs=[row(D), row(D), row(LANES), row(LANES)],
        out_shape=[jax.ShapeDtypeStruct((N, D), F32), jax.ShapeDtypeStruct((N, D), F32),
                   jax.ShapeDtypeStruct((N, LANES), jnp.int32),
                   jax.ShapeDtypeStruct((N, LANES), F32)],
        compiler_params=pltpu.CompilerParams(dimension_semantics=("arbitrary",),
                                             vmem_limit_bytes=VMEM_LIMIT),
        name="outproj",
    )(x2, o_mla, ogm, mla_out_norm.reshape(1, -1), wo_a, wo_b, ffn_norm.reshape(1, -1), wr_pad,
      br_pad)


def _dispatch_kernel(dest_ref, zb_ref, nzb_ref, h2_ref, xs_hbm, zblk, sem, sem_z):
    tm = h2_ref.shape[0]
    bm = MOE_TM
    i = pl.program_id(0)
    base = i * (tm * TOP_K)

    @pl.when(i == 0)
    def _zero_blocks():
        zblk[...] = jnp.zeros(zblk.shape, zblk.dtype)

        def zcopy(n):
            d0 = pl.multiple_of(zb_ref[n] * bm, bm)
            return pltpu.make_async_copy(zblk, xs_hbm.at[pl.ds(d0, bm), :], sem_z)

        def issue(n, c):
            zcopy(n).start()
            return c
        lax.fori_loop(0, nzb_ref[0], issue, 0)

        def finish(n, c):
            zcopy(n).wait()
            return c
        lax.fori_loop(0, nzb_ref[0], finish, 0)

    def issue(r, c):
        for kk in range(TOP_K):
            d = dest_ref[base + r * TOP_K + kk]
            pltpu.make_async_copy(h2_ref.at[pl.ds(r, 1), :], xs_hbm.at[pl.ds(d, 1), :], sem).start()
        return c
    lax.fori_loop(0, tm, issue, 0)
    for kk in range(TOP_K):
        pltpu.make_async_copy(h2_ref, xs_hbm.at[pl.ds(0, tm), :], sem).wait()


def _dispatch(h2, dest, zero_blocks, n_zero, n_rows):
    N, D = h2.shape
    tm = TOK_TILE
    grid_spec = pltpu.PrefetchScalarGridSpec(
        num_scalar_prefetch=3,
        grid=(N // tm,),
        in_specs=[pl.BlockSpec((tm, D), lambda i, *_: (i, 0))],
        out_specs=pl.BlockSpec(memory_space=pl.ANY),
        scratch_shapes=[pltpu.VMEM((MOE_TM, D), F32), pltpu.SemaphoreType.DMA(()),
                        pltpu.SemaphoreType.DMA(())],
    )
    return pl.pallas_call(
        _dispatch_kernel,
        grid_spec=grid_spec,
        out_shape=jax.ShapeDtypeStruct((n_rows, D), F32),
        compiler_params=pltpu.CompilerParams(dimension_semantics=("arbitrary",),
                                             vmem_limit_bytes=VMEM_LIMIT),
        name="dispatch",
    )(dest, zero_blocks, n_zero, h2)


def _moe_kernel(ie_ref, ist_ref, inb_ref, tail_ref,
                xs_hbm, wg_ref, wu_ref, wd_ref, bg_ref, bu_ref, bd_ref,
                ys_hbm,
                xbuf, acc, zblk, sem_x, sem_o, sem_z):
    tm = MOE_TM
    i = pl.program_id(0)
    j = pl.program_id(1)
    n_items = pl.num_programs(0)
    nj = pl.num_programs(1)
    nblk = inb_ref[i]
    slot = i % 2
    n_blocks = ys_hbm.shape[0] // tm

    def x_copy(it, m):
        s0 = pl.multiple_of(ist_ref[it] + m * tm, tm)
        r0 = pl.multiple_of(m * tm, tm)
        return pltpu.make_async_copy(xs_hbm.at[pl.ds(s0, tm), :],
                                     xbuf.at[it % 2, pl.ds(r0, tm), :], sem_x.at[it % 2])

    def y_copy(it, m):
        d0 = pl.multiple_of(ist_ref[it] + m * tm, tm)
        r0 = pl.multiple_of(m * tm, tm)
        return pltpu.make_async_copy(acc.at[it % 2, pl.ds(r0, tm), :],
                                     ys_hbm.at[pl.ds(d0, tm), :], sem_o.at[it % 2])

    def for_blocks(it, fn):
        def body(m, c):
            fn(it, m)
            return c
        lax.fori_loop(0, inb_ref[it], body, 0)

    start_x = lambda it: for_blocks(it, lambda a, m: x_copy(a, m).start())
    wait_x = lambda it: for_blocks(it, lambda a, m: x_copy(a, m).wait())
    start_y = lambda it: for_blocks(it, lambda a, m: y_copy(a, m).start())
    wait_y = lambda it: for_blocks(it, lambda a, m: y_copy(a, m).wait())

    def zero_copy(b):
        d0 = pl.multiple_of(b * tm, tm)
        return pltpu.make_async_copy(zblk, ys_hbm.at[pl.ds(d0, tm), :], sem_z)

    @pl.when(jnp.logical_and(i == 0, j == 0))
    def _first_step():
        zblk[...] = jnp.zeros(zblk.shape, zblk.dtype)

        def issue(b, c):
            zero_copy(b).start()
            return c
        lax.fori_loop(tail_ref[0], n_blocks, issue, 0)
        start_x(0)

    @pl.when(j == 0)
    def _item_start():
        wait_x(i)

        @pl.when(i + 1 < n_items)
        def _():
            start_x(i + 1)

        @pl.when(i >= 2)
        def _():
            wait_y(i - 2)

        def init(m, c):
            r0 = pl.multiple_of(m * tm, tm)
            acc[slot, pl.ds(r0, tm), :] = jnp.broadcast_to(bd_ref[0], (tm, acc.shape[2]))
            return c
        lax.fori_loop(0, nblk, init, 0)

    @pl.when(nblk > 0)
    def _compute():
        def ffn_rows(r0, n):
            x = xbuf[slot, pl.ds(r0, n), :].astype(BF16)
            g = jnp.minimum(_dot(x, wg_ref[0].astype(BF16)) + bg_ref[0], SWIGLU_LIMIT)
            u = jnp.clip(_dot(x, wu_ref[0].astype(BF16)) + bu_ref[0], -SWIGLU_LIMIT, SWIGLU_LIMIT)
            a = (u + 1.0) * (g * jax.nn.sigmoid(SWIGLU_ALPHA * g))
            acc[slot, pl.ds(r0, n), :] += _dot(a.astype(BF16), wd_ref[0].astype(BF16))

        def pair(m, c):
            ffn_rows(pl.multiple_of(m * (2 * tm), 2 * tm), 2 * tm)
            return c
        lax.fori_loop(0, nblk // 2, pair, 0)

        @pl.when(nblk % 2 == 1)
        def _odd_block():
            ffn_rows(pl.multiple_of((nblk - 1) * tm, tm), tm)

    @pl.when(j == nj - 1)
    def _item_end():
        start_y(i)

        @pl.when(i == n_items - 1)
        def _last_step():
            @pl.when(i >= 1)
            def _():
                wait_y(i - 1)
            wait_y(i)

            def finish(b, c):
                zero_copy(b).wait()
                return c
            lax.fori_loop(tail_ref[0], n_blocks, finish, 0)


def _moe(xs, item_e, item_start, item_nblk, tail_blk, w_gate_up, b_gate_up, w_down, b_down):
    n_rows, D = xs.shape
    E, _, F2 = w_gate_up.shape
    F = F2 // 2
    tf = MOE_TF
    nj = F // tf
    n_items = item_e.shape[0]
    rows = MOE_BPI * MOE_TM

    def jj(i, j, inb):
        return jnp.where(inb[i] > 0, j, nj - 1)

    def spec(shape, index):
        return pl.BlockSpec(shape, lambda i, j, ie, ist, inb, tail: index(ie[i], jj(i, j, inb)))

    wg_spec = spec((1, D, tf), lambda e, c: (e, 0, c))
    wu_spec = spec((1, D, tf), lambda e, c: (e, 0, nj + c))
    wd_spec = spec((1, tf, D), lambda e, c: (e, c, 0))
    bg_spec = spec((1, 1, tf), lambda e, c: (e, 0, c))
    bu_spec = spec((1, 1, tf), lambda e, c: (e, 0, nj + c))
    bd_spec = spec((1, 1, D), lambda e, c: (e, 0, 0))
    any_spec = pl.BlockSpec(memory_space=pl.ANY)
    grid_spec = pltpu.PrefetchScalarGridSpec(
        num_scalar_prefetch=4,
        grid=(n_items, nj),
        in_specs=[any_spec, wg_spec, wu_spec, wd_spec, bg_spec, bu_spec, bd_spec],
        out_specs=any_spec,
        scratch_shapes=[pltpu.VMEM((2, rows, D), F32), pltpu.VMEM((2, rows, D), F32),
                        pltpu.VMEM((MOE_TM, D), F32),
                        pltpu.SemaphoreType.DMA((2,)), pltpu.SemaphoreType.DMA((2,)),
                        pltpu.SemaphoreType.DMA(())],
    )
    bgu = b_gate_up.reshape(E, 1, F2)
    return pl.pallas_call(
        _moe_kernel,
        grid_spec=grid_spec,
        out_shape=jax.ShapeDtypeStruct((n_rows, D), F32),
        compiler_params=pltpu.CompilerParams(dimension_semantics=("arbitrary", "arbitrary"),
                                             vmem_limit_bytes=VMEM_LIMIT),
        name="moe",
    )(item_e, item_start, item_nblk, tail_blk, xs, w_gate_up, w_gate_up, w_down, bgu, bgu,
      b_down.reshape(E, 1, D))


def _combine_kernel(pos_ref, x1_ref, g_ref, ys_hbm, o_ref, buf, sem):
    tm = CMB_TILE
    i = pl.program_id(0)
    base = i * (tm * TOP_K)

    def issue(r, c):
        for kk in range(TOP_K):
            p = pos_ref[base + r * TOP_K + kk]
            pltpu.make_async_copy(ys_hbm.at[pl.ds(p, 1), :], buf.at[kk, pl.ds(r, 1), :], sem).start()
        return c
    lax.fori_loop(0, tm, issue, 0)
    out = x1_ref[...]
    for kk in range(TOP_K):
        pltpu.make_async_copy(ys_hbm.at[pl.ds(0, tm), :], buf.at[kk], sem).wait()
    for kk in range(TOP_K):
        out = out + g_ref[:, kk:kk + 1] * buf[kk]
    o_ref[...] = out


def _combine(x1, gates, ys, dest):
    N, D = x1.shape
    tm = CMB_TILE
    grid_spec = pltpu.PrefetchScalarGridSpec(
        num_scalar_prefetch=1,
        grid=(N // tm,),
        in_specs=[pl.BlockSpec((tm, D), lambda i, pos: (i, 0)),
                  pl.BlockSpec((tm, LANES), lambda i, pos: (i, 0)),
                  pl.BlockSpec(memory_space=pl.ANY)],
        out_specs=pl.BlockSpec((tm, D), lambda i, pos: (i, 0)),
        scratch_shapes=[pltpu.VMEM((TOP_K, tm, D), F32), pltpu.SemaphoreType.DMA(())],
    )
    return pl.pallas_call(
        _combine_kernel,
        grid_spec=grid_spec,
        out_shape=jax.ShapeDtypeStruct((N, D), F32),
        compiler_params=pltpu.CompilerParams(dimension_semantics=("arbitrary",),
                                             vmem_limit_bytes=VMEM_LIMIT),
        name="combine",
    )(dest, x1, gates, ys)


def _routing(top_idx, n_tok):
    tm, bpi = MOE_TM, MOE_BPI
    nk = n_tok * TOP_K
    experts = jnp.arange(N_EXPERTS, dtype=jnp.int32)
    e_flat = top_idx.reshape(-1)
    onehot = (e_flat[:, None] == experts[None, :]).astype(jnp.int32)
    csum = jnp.cumsum(onehot, axis=0)
    rank = jnp.sum(csum * onehot, axis=1) - 1
    counts = csum[-1]
    nb = (counts + tm - 1) // tm
    bend = jnp.cumsum(nb)
    bstart = bend - nb
    dest = (jnp.sum(onehot * bstart[None, :], axis=1) * tm + rank).astype(jnp.int32)
    n_blocks = -(-(nk + N_EXPERTS * (tm - 1)) // tm)
    n_rows = n_blocks * tm
    tail_blk = bend[-1:].astype(jnp.int32)
    zb_e = jnp.where(nb > 0, bend - 1, -1)
    zb_t = jnp.arange(n_blocks, dtype=jnp.int32)
    zb_all = jnp.concatenate([zb_e, jnp.where(zb_t >= bend[-1], zb_t, -1)]).astype(jnp.int32)
    order = jnp.argsort(zb_all < 0, stable=True)
    zero_blocks = zb_all[order]
    n_zero = jnp.sum(zb_all >= 0).astype(jnp.int32).reshape(1)
    n_items = n_blocks // bpi + N_EXPERTS
    items_e = (nb + bpi - 1) // bpi
    iend = jnp.cumsum(items_e)
    istart = iend - items_e
    slot = jnp.arange(n_items, dtype=jnp.int32)
    valid = slot < iend[-1]
    exp_of = jnp.minimum(jnp.searchsorted(iend, slot, side="right"), N_EXPERTS - 1).astype(jnp.int32)
    last_e = jnp.max(jnp.where(nb > 0, experts, 0))
    local = slot - istart[exp_of]
    item_e = jnp.where(valid, exp_of, last_e).astype(jnp.int32)
    item_start = jnp.where(valid, (bstart[exp_of] + local * bpi) * tm, 0).astype(jnp.int32)
    item_nblk = jnp.where(valid, jnp.minimum(bpi, nb[exp_of] - local * bpi), 0).astype(jnp.int32)
    return dest, zero_blocks, n_zero, item_e, item_start, item_nblk, tail_blk, n_rows


def _rope_lane_tile(v):
    half = MLA_ROPE // 2
    z = jnp.zeros((half,), v.dtype)
    return jnp.concatenate([v[:half], z, v[half:], z]).reshape(1, LANES)


def _rope_cols(w):
    half = MLA_ROPE // 2
    z = jnp.zeros((w.shape[0], half), w.dtype)
    return jnp.concatenate([w[:, :half], z, w[:, half:], z], axis=1)


def _layer(x, mem, positions, attn_norm, w_in, q_a_norm, w_q_b, kv_a_norm, w_kv_b,
           q_nope_norm, q_rope_norm, k_nope_norm, k_rope_norm,
           gmlp_v_norm, w_spatial, b_spatial,
           mem_norm, w_mem_kv, mem_q_norm, mem_k_norm,
           mla_out_norm, gmlp_out_norm, mem_out_norm, w_o,
           ffn_norm, w_router, b_router, w_gate_up, b_gate_up, w_down, b_down):
    B, S, D = x.shape
    N = B * S
    x2 = x.reshape(N, D)

    o1 = Q_LORA
    o2 = o1 + KV_LORA
    o3 = o2 + MLA_ROPE
    o4 = o3 + 2 * GMLP_WIDTH
    w_in_r = jnp.concatenate([w_in[:, :o2], w_in[:, o3:o4], w_in[:, o4:], _rope_cols(w_in[:, o2:o3])],
                             axis=1).astype(BF16)
    wq = w_q_b.reshape(Q_LORA, MLA_HEADS, MLA_NOPE + MLA_ROPE)
    wq_rope = jax.vmap(_rope_cols, in_axes=1, out_axes=1)(wq[:, :, MLA_NOPE:])
    wq_pad = jnp.concatenate([wq[:, :, :MLA_NOPE], wq_rope], axis=2).reshape(Q_LORA, MLA_HEADS * HEAD_PAD)
    wq_pad = wq_pad.astype(BF16)
    half = MLA_ROPE // 2
    inv_freq = ROPE_BASE ** (-jnp.arange(half, dtype=F32) / half)
    zf = jnp.zeros((half,), F32)
    invf_tile = jnp.concatenate([inv_freq, zf, inv_freq, zf]).reshape(1, LANES)
    sgn_tile = jnp.concatenate([-jnp.ones((half,), F32), zf, jnp.ones((half,), F32), zf]).reshape(1, LANES)
    bs_tile = jnp.repeat(b_spatial.T, GMLP_CH, axis=1)

    mk, mv = _mem_kv(mem, mem_norm, w_mem_kv, mem_k_norm)
    cq, ckv, kpe, ogm, cs = _inproj(
        x2, positions.reshape(N, 1), (invf_tile, sgn_tile), attn_norm, w_in_r, q_a_norm, kv_a_norm,
        _rope_lane_tile(k_rope_norm), gmlp_v_norm, w_spatial, bs_tile, mk, mv, mem_q_norm,
        gmlp_out_norm, mem_out_norm, S)
    q, k, v = _qkv(cq, ckv, kpe, cs, wq_pad, w_kv_b.astype(BF16), q_nope_norm,
                   _rope_lane_tile(q_rope_norm), k_nope_norm)
    o_mla = _attention(q, k, v, B, S).reshape(N, MLA_HEADS * MLA_V)

    wo = w_o.astype(BF16)
    wr_pad = jnp.pad(w_router, ((0, 0), (0, LANES - N_EXPERTS)))
    br_pad = jnp.pad(b_router, (0, LANES - N_EXPERTS)).reshape(1, LANES)
    x1, h2, ti, tg = _outproj(x2, o_mla, ogm, mla_out_norm, wo[:MLA_HEADS * MLA_V],
                              wo[MLA_HEADS * MLA_V:], ffn_norm, wr_pad, br_pad)

    dest, zero_blocks, n_zero, item_e, item_start, item_nblk, tail_blk, n_rows = _routing(
        ti[:, :TOP_K], N)
    xs = _dispatch(h2, dest, zero_blocks, n_zero, n_rows)
    ys = _moe(xs, item_e, item_start, item_nblk, tail_blk, w_gate_up, b_gate_up, w_down, b_down)
    out = _combine(x1, tg, ys, dest)
    return out.reshape(B, S, D)


def kernel(x, mem, positions, attn_norm, w_in, q_a_norm, w_q_b, kv_a_norm, w_kv_b, q_nope_norm, q_rope_norm, k_nope_norm, k_rope_norm, gmlp_v_norm, w_spatial, b_spatial, mem_norm, w_mem_kv, mem_q_norm, mem_k_norm, mla_out_norm, gmlp_out_norm, mem_out_norm, w_o, ffn_norm, w_router, b_router, w_gate_up, b_gate_up, w_down, b_down):
    depth = attn_norm.shape[0]
    for l in range(depth):
        x = _layer(x, mem, positions, attn_norm[l], w_in[l], q_a_norm[l], w_q_b[l], kv_a_norm[l],
                   w_kv_b[l], q_nope_norm[l], q_rope_norm[l], k_nope_norm[l], k_rope_norm[l],
                   gmlp_v_norm[l], w_spatial[l], b_spatial[l], mem_norm[l], w_mem_kv[l],
                   mem_q_norm[l], mem_k_norm[l], mla_out_norm[l], gmlp_out_norm[l],
                   mem_out_norm[l], w_o[l], ffn_norm[l], w_router[l], b_router[l], w_gate_up[l],
                   b_gate_up[l], w_down[l], b_down[l])
    return x
```

```python
import functools

import jax
import jax.numpy as jnp
import numpy as np
from jax import lax
from jax.experimental import pallas as pl
from jax.experimental.pallas import tpu as pltpu

F32 = jnp.float32
BF16 = jnp.bfloat16

EPS = 1e-6
LANES = 128
SUBLANES = 8
VMEM_LIMIT = 56 * 1024 * 1024

MLA_HEADS = 8
MLA_NOPE = 128
MLA_ROPE = 64
MLA_V = 128
Q_LORA = 512
KV_LORA = 512
GMLP_GROUPS = 4
GMLP_CH = 128
GMLP_WIDTH = GMLP_GROUPS * GMLP_CH
CHUNK = 128
MEM_HEADS = 4
MEM_HEAD_DIM = 128
MEM_WIDTH = MEM_HEADS * MEM_HEAD_DIM
N_EXPERTS = 32
TOP_K = 4
SWIGLU_ALPHA = 1.702
SWIGLU_LIMIT = 7.0
ROPE_BASE = 10000.0
HEAD_PAD = 2 * LANES

TOK_TILE = 512
ATT_TQ = 512
ATT_TK = 512
ATT_HPS = 2
MOE_TM = 256
MOE_BPI = 4
MOE_TF = 256
DSP_TILE = 1024
CMB_TILE = 256


def _rms(x, g, n=None):
    n = x.shape[-1] if n is None else n
    ms = jnp.sum(x * x, axis=-1, keepdims=True) * (1.0 / n)
    return x * lax.rsqrt(ms + EPS) * g


def _gelu(x):
    return 0.5 * x * (1.0 + lax.erf(x * (2.0 ** -0.5)))


def _dot(a, b):
    return jnp.dot(a, b, preferred_element_type=F32)


def _dot_nt(a, b):
    return lax.dot_general(a, b, (((1,), (1,)), ((), ())), preferred_element_type=F32)


def _const_spec(shape):
    nd = len(shape)
    return pl.BlockSpec(shape, lambda *_: (0,) * nd)


def _mem_kv_kernel(mem_ref, g_ref, w_ref, kg_ref, mk_ref, mv_ref):
    m = mem_ref[0]
    hn = _rms(m, g_ref[...]).astype(BF16)
    kv = _dot(hn, w_ref[...])
    for h in range(MEM_HEADS):
        k = kv[:, h * MEM_HEAD_DIM:(h + 1) * MEM_HEAD_DIM]
        mk_ref[0, :, h * MEM_HEAD_DIM:(h + 1) * MEM_HEAD_DIM] = _rms(k, kg_ref[...]).astype(BF16)
    mv_ref[0] = kv[:, MEM_WIDTH:].astype(BF16)


def _mem_kv(mem, mem_norm, w_mem_kv, mem_k_norm):
    B, M, D = mem.shape
    return pl.pallas_call(
        _mem_kv_kernel,
        grid=(B,),
        in_specs=[pl.BlockSpec((1, M, D), lambda b: (b, 0, 0)),
                  _const_spec((1, D)),
                  _const_spec((D, 2 * MEM_WIDTH)),
                  _const_spec((1, MEM_HEAD_DIM))],
        out_specs=[pl.BlockSpec((1, M, MEM_WIDTH), lambda b: (b, 0, 0)),
                   pl.BlockSpec((1, M, MEM_WIDTH), lambda b: (b, 0, 0))],
        out_shape=[jax.ShapeDtypeStruct((B, M, MEM_WIDTH), BF16),
                   jax.ShapeDtypeStruct((B, M, MEM_WIDTH), BF16)],
        compiler_params=pltpu.CompilerParams(dimension_semantics=("arbitrary",),
                                             vmem_limit_bytes=VMEM_LIMIT),
        name="mem_kv",
    )(mem, mem_norm.reshape(1, D), w_mem_kv.astype(BF16), mem_k_norm.reshape(1, MEM_HEAD_DIM))


_O_CQ = 0
_O_CKV = _O_CQ + Q_LORA
_O_U = _O_CKV + KV_LORA
_O_VG = _O_U + GMLP_WIDTH
_O_QM = _O_VG + GMLP_WIDTH
_O_KR = _O_QM + MEM_WIDTH
_IN_COLS_PAD = _O_KR + LANES


def _rope_tile(t, cos, sin_signed):
    return t * cos + pltpu.roll(t, LANES // 2, axis=1) * sin_signed


def _inproj_kernel(x_ref, pos_ref, invf_ref, sgn_ref, an_ref, w_ref, qan_ref, kvan_ref, krn_ref,
                   gvn_ref, ws_ref, bs_ref, mk_ref, mv_ref, mqn_ref, gon_ref, mon_ref,
                   cq_ref, ckv_ref, kpe_ref, ogm_ref, cs_ref):
    tm = x_ref.shape[0]
    h = _rms(x_ref[...], an_ref[...]).astype(BF16)
    z = _dot(h, w_ref[...])

    cq_ref[...] = _rms(z[:, _O_CQ:_O_CQ + Q_LORA], qan_ref[...]).astype(BF16)
    ckv_ref[...] = _rms(z[:, _O_CKV:_O_CKV + KV_LORA], kvan_ref[...]).astype(BF16)

    ang = pos_ref[...].astype(F32) * invf_ref[...]
    cos = jnp.cos(ang)
    sin_signed = jnp.sin(ang) * sgn_ref[...]
    cs_ref[:, :LANES] = cos
    cs_ref[:, LANES:] = sin_signed
    kr = _rms(z[:, _O_KR:_O_KR + LANES], krn_ref[...], MLA_ROPE)
    kpe_ref[...] = _rope_tile(kr, cos, sin_signed).astype(BF16)

    u = _gelu(z[:, _O_U:_O_U + GMLP_WIDTH])
    vg = _gelu(z[:, _O_VG:_O_VG + GMLP_WIDTH])
    vg = _rms(vg, gvn_ref[...]).astype(BF16)
    row = lax.broadcasted_iota(jnp.int32, (CHUNK, CHUNK), 0)
    col = lax.broadcasted_iota(jnp.int32, (CHUNK, CHUNK), 1)
    sp_cols = []
    for g in range(GMLP_GROUPS):
        wsg = jnp.where(col <= row, ws_ref[g], 0.0).astype(BF16)
        sp_rows = [_dot(wsg, vg[c * CHUNK:(c + 1) * CHUNK, g * GMLP_CH:(g + 1) * GMLP_CH])
                   for c in range(tm // CHUNK)]
        sp_cols.append(jnp.concatenate(sp_rows, axis=0))
    sp = jnp.concatenate(sp_cols, axis=1) + jnp.concatenate([bs_ref[...]] * (tm // CHUNK), axis=0)
    ogm_ref[:, :GMLP_WIDTH] = _rms(u * sp, gon_ref[...]).astype(BF16)

    o_heads = []
    for hd in range(MEM_HEADS):
        sl = slice(hd * MEM_HEAD_DIM, (hd + 1) * MEM_HEAD_DIM)
        q = z[:, _O_QM + hd * MEM_HEAD_DIM:_O_QM + (hd + 1) * MEM_HEAD_DIM]
        qn = (_rms(q, mqn_ref[...]) * (MEM_HEAD_DIM ** -0.5)).astype(BF16)
        s = _dot_nt(qn, mk_ref[0, :, sl])
        s = s - jnp.max(s, axis=-1, keepdims=True)
        p = jnp.exp(s)
        p = p / jnp.sum(p, axis=-1, keepdims=True)
        o_heads.append(_dot(p.astype(BF16), mv_ref[0, :, sl]))
    o_mem = jnp.concatenate(o_heads, axis=1)
    ogm_ref[:, GMLP_WIDTH:] = _rms(o_mem, mon_ref[...]).astype(BF16)


def _inproj(x2, pos, tables, attn_norm, w_in_r, q_a_norm, kv_a_norm, krn_tile, gmlp_v_norm,
            w_spatial, bs_tile, mk, mv, mem_q_norm, gmlp_out_norm, mem_out_norm, seq):
    N, D = x2.shape
    tm = TOK_TILE
    tiles_per_seq = seq // tm
    invf_tile, sgn_tile = tables
    M = mk.shape[1]
    row = lambda w: pl.BlockSpec((tm, w), lambda i: (i, 0))
    batch_blk = pl.BlockSpec((1, M, MEM_WIDTH), lambda i: (i // tiles_per_seq, 0, 0))
    return pl.pallas_call(
        _inproj_kernel,
        grid=(N // tm,),
        in_specs=[row(D), row(1), _const_spec((1, LANES)), _const_spec((1, LANES)),
                  _const_spec((1, D)), _const_spec((D, _IN_COLS_PAD)),
                  _const_spec((1, Q_LORA)), _const_spec((1, KV_LORA)), _const_spec((1, LANES)),
                  _const_spec((1, GMLP_WIDTH)), _const_spec((GMLP_GROUPS, CHUNK, CHUNK)),
                  _const_spec((CHUNK, GMLP_WIDTH)), batch_blk, batch_blk,
                  _const_spec((1, MEM_HEAD_DIM)), _const_spec((1, GMLP_WIDTH)),
                  _const_spec((1, MEM_WIDTH))],
        out_specs=[row(Q_LORA), row(KV_LORA), row(LANES), row(GMLP_WIDTH + MEM_WIDTH),
                   row(2 * LANES)],
        out_shape=[jax.ShapeDtypeStruct((N, Q_LORA), BF16),
                   jax.ShapeDtypeStruct((N, KV_LORA), BF16),
                   jax.ShapeDtypeStruct((N, LANES), BF16),
                   jax.ShapeDtypeStruct((N, GMLP_WIDTH + MEM_WIDTH), BF16),
                   jax.ShapeDtypeStruct((N, 2 * LANES), F32)],
        compiler_params=pltpu.CompilerParams(dimension_semantics=("arbitrary",),
                                             vmem_limit_bytes=VMEM_LIMIT),
        name="inproj",
    )(x2, pos, invf_tile, sgn_tile, attn_norm.reshape(1, D), w_in_r,
      q_a_norm.reshape(1, -1), kv_a_norm.reshape(1, -1), krn_tile, gmlp_v_norm.reshape(1, -1),
      w_spatial, bs_tile, mk, mv, mem_q_norm.reshape(1, -1), gmlp_out_norm.reshape(1, -1),
      mem_out_norm.reshape(1, -1))


def _qkv_kernel(cq_ref, ckv_ref, kpe_ref, cs_ref, wq_ref, wkv_ref, qnn_ref, qrn_ref, knn_ref,
                q_ref, k_ref, v_ref):
    cos = cs_ref[:, :LANES]
    sin_signed = cs_ref[:, LANES:]
    scale = (MLA_NOPE + MLA_ROPE) ** -0.5
    qr = _dot(cq_ref[...], wq_ref[...])
    kvr = _dot(ckv_ref[...], wkv_ref[...])
    kpe = kpe_ref[...]
    for h in range(MLA_HEADS):
        o = h * HEAD_PAD
        qn = _rms(qr[:, o:o + MLA_NOPE], qnn_ref[...]) * scale
        qt = _rms(qr[:, o + MLA_NOPE:o + HEAD_PAD], qrn_ref[...], MLA_ROPE)
        qt = _rope_tile(qt, cos, sin_signed) * scale
        q_ref[:, o:o + MLA_NOPE] = qn.astype(BF16)
        q_ref[:, o + MLA_NOPE:o + HEAD_PAD] = qt.astype(BF16)
        ko = h * (MLA_NOPE + MLA_V)
        k_ref[:, o:o + MLA_NOPE] = _rms(kvr[:, ko:ko + MLA_NOPE], knn_ref[...]).astype(BF16)
        k_ref[:, o + MLA_NOPE:o + HEAD_PAD] = kpe
        v_ref[:, h * MLA_V:(h + 1) * MLA_V] = kvr[:, ko + MLA_NOPE:ko + MLA_NOPE + MLA_V].astype(BF16)


def _qkv(cq, ckv, kpe, cs, wq_pad, wkv, q_nope_norm, qrn_tile, k_nope_norm):
    N = cq.shape[0]
    tm = TOK_TILE
    row = lambda w: pl.BlockSpec((tm, w), lambda i: (i, 0))
    return pl.pallas_call(
        _qkv_kernel,
        grid=(N // tm,),
        in_specs=[row(Q_LORA), row(KV_LORA), row(LANES), row(2 * LANES),
                  _const_spec(wq_pad.shape), _const_spec(wkv.shape),
                  _const_spec((1, MLA_NOPE)), _const_spec((1, LANES)), _const_spec((1, MLA_NOPE))],
        out_specs=[row(MLA_HEADS * HEAD_PAD), row(MLA_HEADS * HEAD_PAD), row(MLA_HEADS * MLA_V)],
        out_shape=[jax.ShapeDtypeStruct((N, MLA_HEADS * HEAD_PAD), BF16),
                   jax.ShapeDtypeStruct((N, MLA_HEADS * HEAD_PAD), BF16),
                   jax.ShapeDtypeStruct((N, MLA_HEADS * MLA_V), BF16)],
        compiler_params=pltpu.CompilerParams(dimension_semantics=("arbitrary",),
                                             vmem_limit_bytes=VMEM_LIMIT),
        name="qkv",
    )(cq, ckv, kpe, cs, wq_pad, wkv, q_nope_norm.reshape(1, -1), qrn_tile,
      k_nope_norm.reshape(1, -1))


def _attn_kernel(q_ref, k_ref, v_ref, o_ref):
    tq, tk = ATT_TQ, ATT_TK
    qi = pl.program_id(2)
    qs = [q_ref[0, :, h * HEAD_PAD:(h + 1) * HEAD_PAD] for h in range(ATT_HPS)]

    def head_step(h, kb, carry, masked):
        m, l, acc = carry
        k0 = pl.multiple_of(kb * tk, tk)
        s = _dot_nt(qs[h], k_ref[0, pl.ds(k0, tk), h * HEAD_PAD:(h + 1) * HEAD_PAD])
        if masked:
            qpos = qi * tq + lax.broadcasted_iota(jnp.int32, (tq, tk), 0)
            kpos = kb * tk + lax.broadcasted_iota(jnp.int32, (tq, tk), 1)
            s = jnp.where(kpos <= qpos, s, -jnp.inf)
        m_new = jnp.maximum(m, jnp.max(s, axis=-1, keepdims=True))
        alpha = jnp.exp(m - m_new)
        p = jnp.exp(s - m_new)
        l = alpha * l + jnp.sum(p, axis=-1, keepdims=True)
        pv = _dot(p.astype(BF16), v_ref[0, pl.ds(k0, tk), h * MLA_V:(h + 1) * MLA_V])
        return m_new, l, alpha * acc + pv

    def step(kb, carries, masked):
        return tuple(head_step(h, kb, carries[h], masked) for h in range(ATT_HPS))

    init = tuple((jnp.full((tq, 1), -jnp.inf, F32), jnp.zeros((tq, 1), F32),
                  jnp.zeros((tq, MLA_V), F32)) for _ in range(ATT_HPS))
    n_full = (qi * tq) // tk
    carries = lax.fori_loop(0, n_full, lambda kb, c: step(kb, c, False), init)
    for d in range(tq // tk):
        carries = step(n_full + d, carries, True)
    for h in range(ATT_HPS):
        _, l, acc = carries[h]
        o_ref[0, :, h * MLA_V:(h + 1) * MLA_V] = (acc / l).astype(BF16)


def _attention(q, k, v, batch, seq):
    q3 = q.reshape(batch, seq, MLA_HEADS * HEAD_PAD)
    k3 = k.reshape(batch, seq, MLA_HEADS * HEAD_PAD)
    v3 = v.reshape(batch, seq, MLA_HEADS * MLA_V)
    hp, hv = ATT_HPS * HEAD_PAD, ATT_HPS * MLA_V
    return pl.pallas_call(
        _attn_kernel,
        grid=(batch, MLA_HEADS // ATT_HPS, seq // ATT_TQ),
        in_specs=[pl.BlockSpec((1, ATT_TQ, hp), lambda b, h, i: (b, i, h)),
                  pl.BlockSpec((1, seq, hp), lambda b, h, i: (b, 0, h)),
                  pl.BlockSpec((1, seq, hv), lambda b, h, i: (b, 0, h))],
        out_specs=pl.BlockSpec((1, ATT_TQ, hv), lambda b, h, i: (b, i, h)),
        out_shape=jax.ShapeDtypeStruct((batch, seq, MLA_HEADS * MLA_V), BF16),
        compiler_params=pltpu.CompilerParams(
            dimension_semantics=("arbitrary", "arbitrary", "arbitrary"),
            vmem_limit_bytes=VMEM_LIMIT),
        name="attn",
    )(q3, k3, v3)


def _outproj_kernel(x_ref, oa_ref, ogm_ref, aon_ref, woa_ref, wob_ref, fn_ref, wr_ref, br_ref,
                    x1_ref, h2_ref, ti_ref, tg_ref):
    tm = x_ref.shape[0]
    oa = _rms(oa_ref[...].astype(F32), aon_ref[...]).astype(BF16)
    x1 = x_ref[...] + _dot(oa, woa_ref[...]) + _dot(ogm_ref[...], wob_ref[...])
    x1_ref[...] = x1
    h2 = _rms(x1, fn_ref[...])
    h2_ref[...] = h2
    h_hi = h2.astype(BF16)
    h_lo = (h2 - h_hi.astype(F32)).astype(BF16)
    wr = wr_ref[...]
    w_hi = wr.astype(BF16)
    w_lo = (wr - w_hi.astype(F32)).astype(BF16)
    hh = _dot(h_hi, jnp.concatenate([w_hi, w_lo], axis=1))
    logits = hh[:, :LANES] + (_dot(h_lo, w_hi) + hh[:, LANES:]) + br_ref[...]
    lane = lax.broadcasted_iota(jnp.int32, (tm, LANES), 1)
    lg = jnp.where(lane < N_EXPERTS, logits, -jnp.inf)
    vals, idxs = [], []
    for _ in range(TOP_K):
        m = jnp.max(lg, axis=-1, keepdims=True)
        am = jnp.min(jnp.where(lg == m, lane, LANES), axis=-1, keepdims=True)
        vals.append(m)
        idxs.append(am)
        lg = jnp.where(lane == am, -jnp.inf, lg)
    es = [jnp.exp(v - vals[0]) for v in vals]
    denom = es[0] + es[1] + es[2] + es[3]
    ti = jnp.zeros((tm, LANES), jnp.int32)
    tg = jnp.zeros((tm, LANES), F32)
    for kk in range(TOP_K):
        ti = jnp.where(lane == kk, idxs[kk], ti)
        tg = jnp.where(lane == kk, es[kk] / denom, tg)
    ti_ref[...] = ti
    tg_ref[...] = tg


def _outproj(x2, o_mla, ogm, mla_out_norm, wo_a, wo_b, ffn_norm, wr_pad, br_pad):
    N, D = x2.shape
    tm = TOK_TILE
    row = lambda w: pl.BlockSpec((tm, w), lambda i: (i, 0))
    wa = o_mla.shape[1]
    wb = ogm.shape[1]
    return pl.pallas_call(
        _outproj_kernel,
        grid=(N // tm,),
        in_specs=[row(D), row(wa), row(wb), _const_spec((1, wa)), _const_spec((wa, D)),
                  _const_spec((wb, D)), _const_spec((1, D)), _const_spec((D, LANES)),
                  _const_spec((1, LANES))],
        out_specs=[row(D), row(D), row(LANES), row(LANES)],
        out_shape=[jax.ShapeDtypeStruct((N, D), F32), jax.ShapeDtypeStruct((N, D), F32),
                   jax.ShapeDtypeStruct((N, LANES), jnp.int32),
                   jax.ShapeDtypeStruct((N, LANES), F32)],
        compiler_params=pltpu.CompilerParams(dimension_semantics=("arbitrary",),
                                             vmem_limit_bytes=VMEM_LIMIT),
        name="outproj",
    )(x2, o_mla, ogm, mla_out_norm.reshape(1, -1), wo_a, wo_b, ffn_norm.reshape(1, -1), wr_pad,
      br_pad)


def _dispatch_kernel(dest_ref, zb_ref, nzb_ref, h2_ref, xs_hbm, zblk, sem, sem_z):
    tm = h2_ref.shape[0]
    bm = MOE_TM
    i = pl.program_id(0)
    base = i * (tm * TOP_K)

    @pl.when(i == 0)
    def _zero_blocks():
        zblk[...] = jnp.zeros(zblk.shape, zblk.dtype)

        def zcopy(n):
            d0 = pl.multiple_of(zb_ref[n] * bm, bm)
            return pltpu.make_async_copy(zblk, xs_hbm.at[pl.ds(d0, bm), :], sem_z)

        def issue(n, c):
            zcopy(n).start()
            return c
        lax.fori_loop(0, nzb_ref[0], issue, 0)

        def finish(n, c):
            zcopy(n).wait()
            return c
        lax.fori_loop(0, nzb_ref[0], finish, 0)

    def issue(g, c):
        r0 = pl.multiple_of(g * SUBLANES, SUBLANES)
        rows = h2_ref.at[pl.ds(r0, SUBLANES), :]
        for s in range(SUBLANES):
            for kk in range(TOP_K):
                d = dest_ref[base + (r0 + s) * TOP_K + kk]
                pltpu.make_async_copy(rows.at[pl.ds(s, 1), :], xs_hbm.at[pl.ds(d, 1), :], sem).start()
        return c
    lax.fori_loop(0, tm // SUBLANES, issue, 0)
    for kk in range(TOP_K):
        pltpu.make_async_copy(h2_ref, xs_hbm.at[pl.ds(0, tm), :], sem).wait()


def _dispatch(h2, dest, zero_blocks, n_zero, n_rows):
    N, D = h2.shape
    tm = DSP_TILE
    grid_spec = pltpu.PrefetchScalarGridSpec(
        num_scalar_prefetch=3,
        grid=(N // tm,),
        in_specs=[pl.BlockSpec((tm, D), lambda i, *_: (i, 0))],
        out_specs=pl.BlockSpec(memory_space=pl.ANY),
        scratch_shapes=[pltpu.VMEM((MOE_TM, D), F32), pltpu.SemaphoreType.DMA(()),
                        pltpu.SemaphoreType.DMA(())],
    )
    return pl.pallas_call(
        _dispatch_kernel,
        grid_spec=grid_spec,
        out_shape=jax.ShapeDtypeStruct((n_rows, D), F32),
        compiler_params=pltpu.CompilerParams(dimension_semantics=("arbitrary",),
                                             vmem_limit_bytes=VMEM_LIMIT),
        name="dispatch",
    )(dest, zero_blocks, n_zero, h2)


def _moe_kernel(ie_ref, ist_ref, inb_ref, tail_ref,
                xs_hbm, wg_ref, wu_ref, wd_ref, bg_ref, bu_ref, bd_ref,
                ys_hbm,
                xbuf, acc, zblk, sem_x, sem_o, sem_z):
    tm = MOE_TM
    i = pl.program_id(0)
    j = pl.program_id(1)
    n_items = pl.num_programs(0)
    nj = pl.num_programs(1)
    nblk = inb_ref[i]
    slot = i % 2
    n_blocks = ys_hbm.shape[0] // tm

    def x_copy(it, m):
        s0 = pl.multiple_of(ist_ref[it] + m * tm, tm)
        r0 = pl.multiple_of(m * tm, tm)
        return pltpu.make_async_copy(xs_hbm.at[pl.ds(s0, tm), :],
                                     xbuf.at[it % 2, pl.ds(r0, tm), :], sem_x.at[it % 2])

    def y_copy(it, m):
        d0 = pl.multiple_of(ist_ref[it] + m * tm, tm)
        r0 = pl.multiple_of(m * tm, tm)
        return pltpu.make_async_copy(acc.at[it % 2, pl.ds(r0, tm), :],
                                     ys_hbm.at[pl.ds(d0, tm), :], sem_o.at[it % 2])

    def for_blocks(it, fn):
        def body(m, c):
            fn(it, m)
            return c
        lax.fori_loop(0, inb_ref[it], body, 0)

    start_x = lambda it: for_blocks(it, lambda a, m: x_copy(a, m).start())
    wait_x = lambda it: for_blocks(it, lambda a, m: x_copy(a, m).wait())
    start_y = lambda it: for_blocks(it, lambda a, m: y_copy(a, m).start())
    wait_y = lambda it: for_blocks(it, lambda a, m: y_copy(a, m).wait())

    def zero_copy(b):
        d0 = pl.multiple_of(b * tm, tm)
        return pltpu.make_async_copy(zblk, ys_hbm.at[pl.ds(d0, tm), :], sem_z)

    @pl.when(jnp.logical_and(i == 0, j == 0))
    def _first_step():
        zblk[...] = jnp.zeros(zblk.shape, zblk.dtype)

        def issue(b, c):
            zero_copy(b).start()
            return c
        lax.fori_loop(tail_ref[0], n_blocks, issue, 0)
        start_x(0)

    @pl.when(j == 0)
    def _item_start():
        wait_x(i)

        @pl.when(i + 1 < n_items)
        def _():
            start_x(i + 1)

        @pl.when(i >= 2)
        def _():
            wait_y(i - 2)

        def init(m, c):
            r0 = pl.multiple_of(m * tm, tm)
            acc[slot, pl.ds(r0, tm), :] = jnp.broadcast_to(bd_ref[0], (tm, acc.shape[2]))
            return c
        lax.fori_loop(0, nblk, init, 0)

    def ffn_rows(n):
        x = xbuf[slot, pl.ds(0, n), :].astype(BF16)
        g = jnp.minimum(_dot(x, wg_ref[0].astype(BF16)) + bg_ref[0], SWIGLU_LIMIT)
        u = jnp.clip(_dot(x, wu_ref[0].astype(BF16)) + bu_ref[0], -SWIGLU_LIMIT, SWIGLU_LIMIT)
        a = (u + 1.0) * (g * jax.nn.sigmoid(SWIGLU_ALPHA * g))
        acc[slot, pl.ds(0, n), :] += _dot(a.astype(BF16), wd_ref[0].astype(BF16))

    for nb in range(1, MOE_BPI + 1):
        pl.when(nblk == nb)(functools.partial(ffn_rows, nb * tm))

    @pl.when(j == nj - 1)
    def _item_end():
        start_y(i)

        @pl.when(i == n_items - 1)
        def _last_step():
            @pl.when(i >= 1)
            def _():
                wait_y(i - 1)
            wait_y(i)

            def finish(b, c):
                zero_copy(b).wait()
                return c
            lax.fori_loop(tail_ref[0], n_blocks, finish, 0)


def _moe(xs, item_e, item_start, item_nblk, tail_blk, w_gate_up, b_gate_up, w_down, b_down):
    n_rows, D = xs.shape
    E, _, F2 = w_gate_up.shape
    F = F2 // 2
    tf = MOE_TF
    nj = F // tf
    n_items = item_e.shape[0]
    rows = MOE_BPI * MOE_TM

    def jj(i, j, inb):
        return jnp.where(inb[i] > 0, j, nj - 1)

    def spec(shape, index):
        return pl.BlockSpec(shape, lambda i, j, ie, ist, inb, tail: index(ie[i], jj(i, j, inb)))

    wg_spec = spec((1, D, tf), lambda e, c: (e, 0, c))
    wu_spec = spec((1, D, tf), lambda e, c: (e, 0, nj + c))
    wd_spec = spec((1, tf, D), lambda e, c: (e, c, 0))
    bg_spec = spec((1, 1, tf), lambda e, c: (e, 0, c))
    bu_spec = spec((1, 1, tf), lambda e, c: (e, 0, nj + c))
    bd_spec = spec((1, 1, D), lambda e, c: (e, 0, 0))
    any_spec = pl.BlockSpec(memory_space=pl.ANY)
    grid_spec = pltpu.PrefetchScalarGridSpec(
        num_scalar_prefetch=4,
        grid=(n_items, nj),
        in_specs=[any_spec, wg_spec, wu_spec, wd_spec, bg_spec, bu_spec, bd_spec],
        out_specs=any_spec,
        scratch_shapes=[pltpu.VMEM((2, rows, D), F32), pltpu.VMEM((2, rows, D), F32),
                        pltpu.VMEM((MOE_TM, D), F32),
                        pltpu.SemaphoreType.DMA((2,)), pltpu.SemaphoreType.DMA((2,)),
                        pltpu.SemaphoreType.DMA(())],
    )
    bgu = b_gate_up.reshape(E, 1, F2)
    return pl.pallas_call(
        _moe_kernel,
        grid_spec=grid_spec,
        out_shape=jax.ShapeDtypeStruct((n_rows, D), F32),
        compiler_params=pltpu.CompilerParams(dimension_semantics=("arbitrary", "arbitrary"),
                                             vmem_limit_bytes=VMEM_LIMIT),
        name="moe",
    )(item_e, item_start, item_nblk, tail_blk, xs, w_gate_up, w_gate_up, w_down, bgu, bgu,
      b_down.reshape(E, 1, D))


def _combine_kernel(pos_ref, x1_ref, g_ref, ys_hbm, o_ref, buf, sem):
    tm = CMB_TILE
    i = pl.program_id(0)

    def start_gather(t):
        base = t * (tm * TOP_K)
        b = t % 2

        def issue(g, c):
            r0 = pl.multiple_of(g * SUBLANES, SUBLANES)
            for s in range(SUBLANES):
                for kk in range(TOP_K):
                    p = pos_ref[base + (r0 + s) * TOP_K + kk]
                    pltpu.make_async_copy(ys_hbm.at[pl.ds(p, 1), :],
                                          buf.at[b, kk, pl.ds(r0, SUBLANES), :].at[pl.ds(s, 1), :],
                                          sem.at[b]).start()
            return c
        lax.fori_loop(0, tm // SUBLANES, issue, 0)

    @pl.when(i == 0)
    def _():
        start_gather(0)

    @pl.when(i + 1 < pl.num_programs(0))
    def _():
        start_gather(i + 1)

    b = i % 2
    for kk in range(TOP_K):
        pltpu.make_async_copy(ys_hbm.at[pl.ds(0, tm), :], buf.at[b, kk], sem.at[b]).wait()
    out = x1_ref[...]
    for kk in range(TOP_K):
        out = out + g_ref[:, kk:kk + 1] * buf[b, kk]
    o_ref[...] = out


def _combine(x1, gates, ys, dest):
    N, D = x1.shape
    tm = CMB_TILE
    grid_spec = pltpu.PrefetchScalarGridSpec(
        num_scalar_prefetch=1,
        grid=(N // tm,),
        in_specs=[pl.BlockSpec((tm, D), lambda i, pos: (i, 0)),
                  pl.BlockSpec((tm, LANES), lambda i, pos: (i, 0)),
                  pl.BlockSpec(memory_space=pl.ANY)],
        out_specs=pl.BlockSpec((tm, D), lambda i, pos: (i, 0)),
        scratch_shapes=[pltpu.VMEM((2, TOP_K, tm, D), F32), pltpu.SemaphoreType.DMA((2,))],
    )
    return pl.pallas_call(
        _combine_kernel,
        grid_spec=grid_spec,
        out_shape=jax.ShapeDtypeStruct((N, D), F32),
        compiler_params=pltpu.CompilerParams(dimension_semantics=("arbitrary",),
                                             vmem_limit_bytes=VMEM_LIMIT),
        name="combine",
    )(dest, x1, gates, ys)


def _routing(top_idx, n_tok):
    tm, bpi = MOE_TM, MOE_BPI
    nk = n_tok * TOP_K
    experts = jnp.arange(N_EXPERTS, dtype=jnp.int32)
    e_flat = top_idx.reshape(-1)
    onehot = (e_flat[:, None] == experts[None, :]).astype(jnp.int32)
    csum = jnp.cumsum(onehot, axis=0)
    rank = jnp.sum(csum * onehot, axis=1) - 1
    counts = csum[-1]
    nb = (counts + tm - 1) // tm
    bend = jnp.cumsum(nb)
    bstart = bend - nb
    dest = (jnp.sum(onehot * bstart[None, :], axis=1) * tm + rank).astype(jnp.int32)
    n_blocks = -(-(nk + N_EXPERTS * (tm - 1)) // tm)
    n_rows = n_blocks * tm
    tail_blk = bend[-1:].astype(jnp.int32)
    zb_e = jnp.where(nb > 0, bend - 1, -1)
    zb_t = jnp.arange(n_blocks, dtype=jnp.int32)
    zb_all = jnp.concatenate([zb_e, jnp.where(zb_t >= bend[-1], zb_t, -1)]).astype(jnp.int32)
    order = jnp.argsort(zb_all < 0, stable=True)
    zero_blocks = zb_all[order]
    n_zero = jnp.sum(zb_all >= 0).astype(jnp.int32).reshape(1)
    n_items = n_blocks // bpi + N_EXPERTS
    items_e = (nb + bpi - 1) // bpi
    iend = jnp.cumsum(items_e)
    istart = iend - items_e
    slot = jnp.arange(n_items, dtype=jnp.int32)
    valid = slot < iend[-1]
    exp_of = jnp.minimum(jnp.searchsorted(iend, slot, side="right"), N_EXPERTS - 1).astype(jnp.int32)
    last_e = jnp.max(jnp.where(nb > 0, experts, 0))
    local = slot - istart[exp_of]
    n_it = jnp.maximum(items_e[exp_of], 1)
    base, rem = nb[exp_of] // n_it, nb[exp_of] % n_it
    first_blk = bstart[exp_of] + local * base + jnp.minimum(local, rem)
    item_e = jnp.where(valid, exp_of, last_e).astype(jnp.int32)
    item_start = jnp.where(valid, first_blk * tm, 0).astype(jnp.int32)
    item_nblk = jnp.where(valid, base + (local < rem), 0).astype(jnp.int32)
    return dest, zero_blocks, n_zero, item_e, item_start, item_nblk, tail_blk, n_rows


def _rope_lane_tile(v):
    half = MLA_ROPE // 2
    z = jnp.zeros((half,), v.dtype)
    return jnp.concatenate([v[:half], z, v[half:], z]).reshape(1, LANES)


def _rope_cols(w):
    half = MLA_ROPE // 2
    z = jnp.zeros((w.shape[0], half), w.dtype)
    return jnp.concatenate([w[:, :half], z, w[:, half:], z], axis=1)


def _layer(x, mem, positions, attn_norm, w_in, q_a_norm, w_q_b, kv_a_norm, w_kv_b,
           q_nope_norm, q_rope_norm, k_nope_norm, k_rope_norm,
           gmlp_v_norm, w_spatial, b_spatial,
           mem_norm, w_mem_kv, mem_q_norm, mem_k_norm,
           mla_out_norm, gmlp_out_norm, mem_out_norm, w_o,
           ffn_norm, w_router, b_router, w_gate_up, b_gate_up, w_down, b_down):
    B, S, D = x.shape
    N = B * S
    x2 = x.reshape(N, D)

    o1 = Q_LORA
    o2 = o1 + KV_LORA
    o3 = o2 + MLA_ROPE
    o4 = o3 + 2 * GMLP_WIDTH
    w_in_r = jnp.concatenate([w_in[:, :o2], w_in[:, o3:o4], w_in[:, o4:], _rope_cols(w_in[:, o2:o3])],
                             axis=1).astype(BF16)
    wq = w_q_b.reshape(Q_LORA, MLA_HEADS, MLA_NOPE + MLA_ROPE)
    wq_rope = jax.vmap(_rope_cols, in_axes=1, out_axes=1)(wq[:, :, MLA_NOPE:])
    wq_pad = jnp.concatenate([wq[:, :, :MLA_NOPE], wq_rope], axis=2).reshape(Q_LORA, MLA_HEADS * HEAD_PAD)
    wq_pad = wq_pad.astype(BF16)
    half = MLA_ROPE // 2
    inv_freq = ROPE_BASE ** (-jnp.arange(half, dtype=F32) / half)
    zf = jnp.zeros((half,), F32)
    invf_tile = jnp.concatenate([inv_freq, zf, inv_freq, zf]).reshape(1, LANES)
    sgn_tile = jnp.concatenate([-jnp.ones((half,), F32), zf, jnp.ones((half,), F32), zf]).reshape(1, LANES)
    bs_tile = jnp.repeat(b_spatial.T, GMLP_CH, axis=1)

    mk, mv = _mem_kv(mem, mem_norm, w_mem_kv, mem_k_norm)
    cq, ckv, kpe, ogm, cs = _inproj(
        x2, positions.reshape(N, 1), (invf_tile, sgn_tile), attn_norm, w_in_r, q_a_norm, kv_a_norm,
        _rope_lane_tile(k_rope_norm), gmlp_v_norm, w_spatial, bs_tile, mk, mv, mem_q_norm,
        gmlp_out_norm, mem_out_norm, S)
    q, k, v = _qkv(cq, ckv, kpe, cs, wq_pad, w_kv_b.astype(BF16), q_nope_norm,
                   _rope_lane_tile(q_rope_norm), k_nope_norm)
    o_mla = _attention(q, k, v, B, S).reshape(N, MLA_HEADS * MLA_V)

    wo = w_o.astype(BF16)
    wr_pad = jnp.pad(w_router, ((0, 0), (0, LANES - N_EXPERTS)))
    br_pad = jnp.pad(b_router, (0, LANES - N_EXPERTS)).reshape(1, LANES)
    x1, h2, ti, tg = _outproj(x2, o_mla, ogm, mla_out_norm, wo[:MLA_HEADS * MLA_V],
                              wo[MLA_HEADS * MLA_V:], ffn_norm, wr_pad, br_pad)

    dest, zero_blocks, n_zero, item_e, item_start, item_nblk, tail_blk, n_rows = _routing(
        ti[:, :TOP_K], N)
    xs = _dispatch(h2, dest, zero_blocks, n_zero, n_rows)
    ys = _moe(xs, item_e, item_start, item_nblk, tail_blk, w_gate_up, b_gate_up, w_down, b_down)
    out = _combine(x1, tg, ys, dest)
    return out.reshape(B, S, D)


def kernel(x, mem, positions, attn_norm, w_in, q_a_norm, w_q_b, kv_a_norm, w_kv_b, q_nope_norm, q_rope_norm, k_nope_norm, k_rope_norm, gmlp_v_norm, w_spatial, b_spatial, mem_norm, w_mem_kv, mem_q_norm, mem_k_norm, mla_out_norm, gmlp_out_norm, mem_out_norm, w_o, ffn_norm, w_router, b_router, w_gate_up, b_gate_up, w_down, b_down):
    depth = attn_norm.shape[0]
    for l in range(depth):
        x = _layer(x, mem, positions, attn_norm[l], w_in[l], q_a_norm[l], w_q_b[l], kv_a_norm[l],
                   w_kv_b[l], q_nope_norm[l], q_rope_norm[l], k_nope_norm[l], k_rope_norm[l],
                   gmlp_v_norm[l], w_spatial[l], b_spatial[l], mem_norm[l], w_mem_kv[l],
                   mem_q_norm[l], mem_k_norm[l], mla_out_norm[l], gmlp_out_norm[l],
                   mem_out_norm[l], w_o[l], ffn_norm[l], w_router[l], b_router[l], w_gate_up[l],
                   b_gate_up[l], w_down[l], b_down[l])
    return x
```

```python
import functools

import jax
import jax.numpy as jnp
import numpy as np
from jax import lax
from jax.experimental import pallas as pl
from jax.experimental.pallas import tpu as pltpu

F32 = jnp.float32
BF16 = jnp.bfloat16

EPS = 1e-6
LANES = 128
SUBLANES = 8
VMEM_LIMIT = 56 * 1024 * 1024

MLA_HEADS = 8
MLA_NOPE = 128
MLA_ROPE = 64
MLA_V = 128
Q_LORA = 512
KV_LORA = 512
GMLP_GROUPS = 4
GMLP_CH = 128
GMLP_WIDTH = GMLP_GROUPS * GMLP_CH
CHUNK = 128
MEM_HEADS = 4
MEM_HEAD_DIM = 128
MEM_WIDTH = MEM_HEADS * MEM_HEAD_DIM
N_EXPERTS = 32
TOP_K = 4
SWIGLU_ALPHA = 1.702
SWIGLU_LIMIT = 7.0
ROPE_BASE = 10000.0
HEAD_PAD = 2 * LANES

TOK_TILE = 512
ATT_TQ = 512
ATT_TK = 512
ATT_HPS = 2
MOE_TM = 256
MOE_BPI = 4
MOE_TF = 256
DSP_TILE = 1024
CMB_TILE = 256


def _rms(x, g, n=None):
    n = x.shape[-1] if n is None else n
    ms = jnp.sum(x * x, axis=-1, keepdims=True) * (1.0 / n)
    return x * lax.rsqrt(ms + EPS) * g


def _gelu(x):
    return 0.5 * x * (1.0 + lax.erf(x * (2.0 ** -0.5)))


def _dot(a, b):
    return jnp.dot(a, b, preferred_element_type=F32)


def _dot_nt(a, b):
    return lax.dot_general(a, b, (((1,), (1,)), ((), ())), preferred_element_type=F32)


def _const_spec(shape):
    nd = len(shape)
    return pl.BlockSpec(shape, lambda *_: (0,) * nd)


def _mem_kv_kernel(mem_ref, g_ref, w_ref, kg_ref, mk_ref, mv_ref):
    m = mem_ref[0]
    hn = _rms(m, g_ref[...]).astype(BF16)
    kv = _dot(hn, w_ref[...])
    for h in range(MEM_HEADS):
        k = kv[:, h * MEM_HEAD_DIM:(h + 1) * MEM_HEAD_DIM]
        mk_ref[0, :, h * MEM_HEAD_DIM:(h + 1) * MEM_HEAD_DIM] = _rms(k, kg_ref[...]).astype(BF16)
    mv_ref[0] = kv[:, MEM_WIDTH:].astype(BF16)


def _mem_kv(mem, mem_norm, w_mem_kv, mem_k_norm):
    B, M, D = mem.shape
    return pl.pallas_call(
        _mem_kv_kernel,
        grid=(B,),
        in_specs=[pl.BlockSpec((1, M, D), lambda b: (b, 0, 0)),
                  _const_spec((1, D)),
                  _const_spec((D, 2 * MEM_WIDTH)),
                  _const_spec((1, MEM_HEAD_DIM))],
        out_specs=[pl.BlockSpec((1, M, MEM_WIDTH), lambda b: (b, 0, 0)),
                   pl.BlockSpec((1, M, MEM_WIDTH), lambda b: (b, 0, 0))],
        out_shape=[jax.ShapeDtypeStruct((B, M, MEM_WIDTH), BF16),
                   jax.ShapeDtypeStruct((B, M, MEM_WIDTH), BF16)],
        compiler_params=pltpu.CompilerParams(dimension_semantics=("arbitrary",),
                                             vmem_limit_bytes=VMEM_LIMIT),
        name="mem_kv",
    )(mem, mem_norm.reshape(1, D), w_mem_kv.astype(BF16), mem_k_norm.reshape(1, MEM_HEAD_DIM))


_O_CQ = 0
_O_CKV = _O_CQ + Q_LORA
_O_U = _O_CKV + KV_LORA
_O_VG = _O_U + GMLP_WIDTH
_O_QM = _O_VG + GMLP_WIDTH
_O_KR = _O_QM + MEM_WIDTH
_IN_COLS_PAD = _O_KR + LANES


def _rope_tile(t, cos, sin_signed):
    return t * cos + pltpu.roll(t, LANES // 2, axis=1) * sin_signed


def _inproj_kernel(x_ref, pos_ref, invf_ref, sgn_ref, an_ref, w_ref, qan_ref, kvan_ref, krn_ref,
                   gvn_ref, ws_ref, bs_ref, mk_ref, mv_ref, mqn_ref, gon_ref, mon_ref,
                   cq_ref, ckv_ref, kpe_ref, ogm_ref, cs_ref):
    tm = x_ref.shape[0]
    h = _rms(x_ref[...], an_ref[...]).astype(BF16)
    z = _dot(h, w_ref[...])

    cq_ref[...] = _rms(z[:, _O_CQ:_O_CQ + Q_LORA], qan_ref[...]).astype(BF16)
    ckv_ref[...] = _rms(z[:, _O_CKV:_O_CKV + KV_LORA], kvan_ref[...]).astype(BF16)

    ang = pos_ref[...].astype(F32) * invf_ref[...]
    cos = jnp.cos(ang)
    sin_signed = jnp.sin(ang) * sgn_ref[...]
    cs_ref[:, :LANES] = cos
    cs_ref[:, LANES:] = sin_signed
    kr = _rms(z[:, _O_KR:_O_KR + LANES], krn_ref[...], MLA_ROPE)
    kpe_ref[...] = _rope_tile(kr, cos, sin_signed).astype(BF16)

    u = _gelu(z[:, _O_U:_O_U + GMLP_WIDTH])
    vg = _gelu(z[:, _O_VG:_O_VG + GMLP_WIDTH])
    vg = _rms(vg, gvn_ref[...]).astype(BF16)
    row = lax.broadcasted_iota(jnp.int32, (CHUNK, CHUNK), 0)
    col = lax.broadcasted_iota(jnp.int32, (CHUNK, CHUNK), 1)
    sp_cols = []
    for g in range(GMLP_GROUPS):
        wsg = jnp.where(col <= row, ws_ref[g], 0.0).astype(BF16)
        sp_rows = [_dot(wsg, vg[c * CHUNK:(c + 1) * CHUNK, g * GMLP_CH:(g + 1) * GMLP_CH])
                   for c in range(tm // CHUNK)]
        sp_cols.append(jnp.concatenate(sp_rows, axis=0))
    sp = jnp.concatenate(sp_cols, axis=1) + jnp.concatenate([bs_ref[...]] * (tm // CHUNK), axis=0)
    ogm_ref[:, :GMLP_WIDTH] = _rms(u * sp, gon_ref[...]).astype(BF16)

    o_heads = []
    for hd in range(MEM_HEADS):
        sl = slice(hd * MEM_HEAD_DIM, (hd + 1) * MEM_HEAD_DIM)
        q = z[:, _O_QM + hd * MEM_HEAD_DIM:_O_QM + (hd + 1) * MEM_HEAD_DIM]
        qn = (_rms(q, mqn_ref[...]) * (MEM_HEAD_DIM ** -0.5)).astype(BF16)
        s = _dot_nt(qn, mk_ref[0, :, sl])
        s = s - jnp.max(s, axis=-1, keepdims=True)
        p = jnp.exp(s)
        p = p / jnp.sum(p, axis=-1, keepdims=True)
        o_heads.append(_dot(p.astype(BF16), mv_ref[0, :, sl]))
    o_mem = jnp.concatenate(o_heads, axis=1)
    ogm_ref[:, GMLP_WIDTH:] = _rms(o_mem, mon_ref[...]).astype(BF16)


def _inproj(x2, pos, tables, attn_norm, w_in_r, q_a_norm, kv_a_norm, krn_tile, gmlp_v_norm,
            w_spatial, bs_tile, mk, mv, mem_q_norm, gmlp_out_norm, mem_out_norm, seq):
    N, D = x2.shape
    tm = TOK_TILE
    tiles_per_seq = seq // tm
    invf_tile, sgn_tile = tables
    M = mk.shape[1]
    row = lambda w: pl.BlockSpec((tm, w), lambda i: (i, 0))
    batch_blk = pl.BlockSpec((1, M, MEM_WIDTH), lambda i: (i // tiles_per_seq, 0, 0))
    return pl.pallas_call(
        _inproj_kernel,
        grid=(N // tm,),
        in_specs=[row(D), row(1), _const_spec((1, LANES)), _const_spec((1, LANES)),
                  _const_spec((1, D)), _const_spec((D, _IN_COLS_PAD)),
                  _const_spec((1, Q_LORA)), _const_spec((1, KV_LORA)), _const_spec((1, LANES)),
                  _const_spec((1, GMLP_WIDTH)), _const_spec((GMLP_GROUPS, CHUNK, CHUNK)),
                  _const_spec((CHUNK, GMLP_WIDTH)), batch_blk, batch_blk,
                  _const_spec((1, MEM_HEAD_DIM)), _const_spec((1, GMLP_WIDTH)),
                  _const_spec((1, MEM_WIDTH))],
        out_specs=[row(Q_LORA), row(KV_LORA), row(LANES), row(GMLP_WIDTH + MEM_WIDTH),
                   row(2 * LANES)],
        out_shape=[jax.ShapeDtypeStruct((N, Q_LORA), BF16),
                   jax.ShapeDtypeStruct((N, KV_LORA), BF16),
                   jax.ShapeDtypeStruct((N, LANES), BF16),
                   jax.ShapeDtypeStruct((N, GMLP_WIDTH + MEM_WIDTH), BF16),
                   jax.ShapeDtypeStruct((N, 2 * LANES), F32)],
        compiler_params=pltpu.CompilerParams(dimension_semantics=("arbitrary",),
                                             vmem_limit_bytes=VMEM_LIMIT),
        name="inproj",
    )(x2, pos, invf_tile, sgn_tile, attn_norm.reshape(1, D), w_in_r,
      q_a_norm.reshape(1, -1), kv_a_norm.reshape(1, -1), krn_tile, gmlp_v_norm.reshape(1, -1),
      w_spatial, bs_tile, mk, mv, mem_q_norm.reshape(1, -1), gmlp_out_norm.reshape(1, -1),
      mem_out_norm.reshape(1, -1))


def _qkv_kernel(cq_ref, ckv_ref, kpe_ref, cs_ref, wq_ref, wkv_ref, qnn_ref, qrn_ref, knn_ref,
                q_ref, k_ref, v_ref):
    cos = cs_ref[:, :LANES]
    sin_signed = cs_ref[:, LANES:]
    scale = (MLA_NOPE + MLA_ROPE) ** -0.5 * np.log2(np.e)
    qr = _dot(cq_ref[...], wq_ref[...])
    kvr = _dot(ckv_ref[...], wkv_ref[...])
    kpe = kpe_ref[...]
    for h in range(MLA_HEADS):
        o = h * HEAD_PAD
        qn = _rms(qr[:, o:o + MLA_NOPE], qnn_ref[...]) * scale
        qt = _rms(qr[:, o + MLA_NOPE:o + HEAD_PAD], qrn_ref[...], MLA_ROPE)
        qt = _rope_tile(qt, cos, sin_signed) * scale
        q_ref[:, o:o + MLA_NOPE] = qn.astype(BF16)
        q_ref[:, o + MLA_NOPE:o + HEAD_PAD] = qt.astype(BF16)
        ko = h * (MLA_NOPE + MLA_V)
        k_ref[:, o:o + MLA_NOPE] = _rms(kvr[:, ko:ko + MLA_NOPE], knn_ref[...]).astype(BF16)
        k_ref[:, o + MLA_NOPE:o + HEAD_PAD] = kpe
        v_ref[:, h * MLA_V:(h + 1) * MLA_V] = kvr[:, ko + MLA_NOPE:ko + MLA_NOPE + MLA_V].astype(BF16)


def _qkv(cq, ckv, kpe, cs, wq_pad, wkv, q_nope_norm, qrn_tile, k_nope_norm):
    N = cq.shape[0]
    tm = TOK_TILE
    row = lambda w: pl.BlockSpec((tm, w), lambda i: (i, 0))
    return pl.pallas_call(
        _qkv_kernel,
        grid=(N // tm,),
        in_specs=[row(Q_LORA), row(KV_LORA), row(LANES), row(2 * LANES),
                  _const_spec(wq_pad.shape), _const_spec(wkv.shape),
                  _const_spec((1, MLA_NOPE)), _const_spec((1, LANES)), _const_spec((1, MLA_NOPE))],
        out_specs=[row(MLA_HEADS * HEAD_PAD), row(MLA_HEADS * HEAD_PAD), row(MLA_HEADS * MLA_V)],
        out_shape=[jax.ShapeDtypeStruct((N, MLA_HEADS * HEAD_PAD), BF16),
                   jax.ShapeDtypeStruct((N, MLA_HEADS * HEAD_PAD), BF16),
                   jax.ShapeDtypeStruct((N, MLA_HEADS * MLA_V), BF16)],
        compiler_params=pltpu.CompilerParams(dimension_semantics=("arbitrary",),
                                             vmem_limit_bytes=VMEM_LIMIT),
        name="qkv",
    )(cq, ckv, kpe, cs, wq_pad, wkv, q_nope_norm.reshape(1, -1), qrn_tile,
      k_nope_norm.reshape(1, -1))


def _attn_kernel(q_ref, k_ref, v_ref, o_ref):
    tq, tk = ATT_TQ, ATT_TK
    qi = pl.program_id(2)
    qs = [q_ref[0, :, h * HEAD_PAD:(h + 1) * HEAD_PAD] for h in range(ATT_HPS)]

    def head_step(h, kb, carry, masked):
        m, l, acc = carry
        k0 = pl.multiple_of(kb * tk, tk)
        s = _dot_nt(qs[h], k_ref[0, pl.ds(k0, tk), h * HEAD_PAD:(h + 1) * HEAD_PAD])
        if masked:
            qpos = qi * tq + lax.broadcasted_iota(jnp.int32, (tq, tk), 0)
            kpos = kb * tk + lax.broadcasted_iota(jnp.int32, (tq, tk), 1)
            s = jnp.where(kpos <= qpos, s, -jnp.inf)
        m_new = jnp.maximum(m, jnp.max(s, axis=-1, keepdims=True))
        alpha = jnp.exp2(m - m_new)
        p = jnp.exp2(s - m_new)
        l = alpha * l + jnp.sum(p, axis=-1, keepdims=True)
        pv = _dot(p.astype(BF16), v_ref[0, pl.ds(k0, tk), h * MLA_V:(h + 1) * MLA_V])
        return m_new, l, alpha * acc + pv

    def step(kb, carries, masked):
        return tuple(head_step(h, kb, carries[h], masked) for h in range(ATT_HPS))

    init = tuple((jnp.full((tq, 1), -jnp.inf, F32), jnp.zeros((tq, 1), F32),
                  jnp.zeros((tq, MLA_V), F32)) for _ in range(ATT_HPS))
    n_full = (qi * tq) // tk
    carries = lax.fori_loop(0, n_full, lambda kb, c: step(kb, c, False), init)
    carries = step(n_full, carries, True)
    for h in range(ATT_HPS):
        _, l, acc = carries[h]
        o_ref[0, :, h * MLA_V:(h + 1) * MLA_V] = (acc / l).astype(BF16)


def _attention(q, k, v, batch, seq):
    q3 = q.reshape(batch, seq, MLA_HEADS * HEAD_PAD)
    k3 = k.reshape(batch, seq, MLA_HEADS * HEAD_PAD)
    v3 = v.reshape(batch, seq, MLA_HEADS * MLA_V)
    hp, hv = ATT_HPS * HEAD_PAD, ATT_HPS * MLA_V
    assert ATT_TQ == ATT_TK and seq % ATT_TQ == 0
    return pl.pallas_call(
        _attn_kernel,
        grid=(batch, MLA_HEADS // ATT_HPS, seq // ATT_TQ),
        in_specs=[pl.BlockSpec((1, ATT_TQ, hp), lambda b, h, i: (b, i, h)),
                  pl.BlockSpec((1, seq, hp), lambda b, h, i: (b, 0, h)),
                  pl.BlockSpec((1, seq, hv), lambda b, h, i: (b, 0, h))],
        out_specs=pl.BlockSpec((1, ATT_TQ, hv), lambda b, h, i: (b, i, h)),
        out_shape=jax.ShapeDtypeStruct((batch, seq, MLA_HEADS * MLA_V), BF16),
        compiler_params=pltpu.CompilerParams(
            dimension_semantics=("arbitrary", "arbitrary", "arbitrary"),
            vmem_limit_bytes=VMEM_LIMIT),
        name="attn",
    )(q3, k3, v3)


def _outproj_kernel(x_ref, oa_ref, ogm_ref, aon_ref, woa_ref, wob_ref, fn_ref, wr_ref, br_ref,
                    x1_ref, h2_ref, ti_ref, tg_ref):
    tm = x_ref.shape[0]
    oa = _rms(oa_ref[...].astype(F32), aon_ref[...]).astype(BF16)
    x1 = x_ref[...] + _dot(oa, woa_ref[...]) + _dot(ogm_ref[...], wob_ref[...])
    x1_ref[...] = x1
    h2 = _rms(x1, fn_ref[...])
    h2_ref[...] = h2
    h_hi = h2.astype(BF16)
    h_lo = (h2 - h_hi.astype(F32)).astype(BF16)
    wr = wr_ref[...]
    w_hi = wr.astype(BF16)
    w_lo = (wr - w_hi.astype(F32)).astype(BF16)
    hh = _dot(h_hi, jnp.concatenate([w_hi, w_lo], axis=1))
    logits = hh[:, :LANES] + (_dot(h_lo, w_hi) + hh[:, LANES:]) + br_ref[...]
    lane = lax.broadcasted_iota(jnp.int32, (tm, LANES), 1)
    lg = jnp.where(lane < N_EXPERTS, logits, -jnp.inf)
    vals, idxs = [], []
    for _ in range(TOP_K):
        m = jnp.max(lg, axis=-1, keepdims=True)
        am = jnp.min(jnp.where(lg == m, lane, LANES), axis=-1, keepdims=True)
        vals.append(m)
        idxs.append(am)
        lg = jnp.where(lane == am, -jnp.inf, lg)
    es = [jnp.exp(v - vals[0]) for v in vals]
    denom = es[0] + es[1] + es[2] + es[3]
    ti = jnp.zeros((tm, LANES), jnp.int32)
    tg = jnp.zeros((tm, LANES), F32)
    for kk in range(TOP_K):
        ti = jnp.where(lane == kk, idxs[kk], ti)
        tg = jnp.where(lane == kk, es[kk] / denom, tg)
    ti_ref[...] = ti
    tg_ref[...] = tg


def _outproj(x2, o_mla, ogm, mla_out_norm, wo_a, wo_b, ffn_norm, wr_pad, br_pad):
    N, D = x2.shape
    tm = TOK_TILE
    row = lambda w: pl.BlockSpec((tm, w), lambda i: (i, 0))
    wa = o_mla.shape[1]
    wb = ogm.shape[1]
    return pl.pallas_call(
        _outproj_kernel,
        grid=(N // tm,),
        in_specs=[row(D), row(wa), row(wb), _const_spec((1, wa)), _const_spec((wa, D)),
                  _const_spec((wb, D)), _const_spec((1, D)), _const_spec((D, LANES)),
                  _const_spec((1, LANES))],
        out_specs=[row(D), row(D), row(LANES), row(LANES)],
        out_shape=[jax.ShapeDtypeStruct((N, D), F32), jax.ShapeDtypeStruct((N, D), F32),
                   jax.ShapeDtypeStruct((N, LANES), jnp.int32),
                   jax.ShapeDtypeStruct((N, LANES), F32)],
        compiler_params=pltpu.CompilerParams(dimension_semantics=("arbitrary",),
                                             vmem_limit_bytes=VMEM_LIMIT),
        name="outproj",
    )(x2, o_mla, ogm, mla_out_norm.reshape(1, -1), wo_a, wo_b, ffn_norm.reshape(1, -1), wr_pad,
      br_pad)


def _dispatch_kernel(dest_ref, zb_ref, nzb_ref, h2_ref, xs_hbm, zblk, sem, sem_z):
    tm = h2_ref.shape[0]
    bm = MOE_TM
    i = pl.program_id(0)
    base = i * (tm * TOP_K)

    @pl.when(i == 0)
    def _zero_blocks():
        zblk[...] = jnp.zeros(zblk.shape, zblk.dtype)

        def zcopy(n):
            d0 = pl.multiple_of(zb_ref[n] * bm, bm)
            return pltpu.make_async_copy(zblk, xs_hbm.at[pl.ds(d0, bm), :], sem_z)

        def issue(n, c):
            zcopy(n).start()
            return c
        lax.fori_loop(0, nzb_ref[0], issue, 0)

        def finish(n, c):
            zcopy(n).wait()
            return c
        lax.fori_loop(0, nzb_ref[0], finish, 0)

    def issue(g, c):
        r0 = pl.multiple_of(g * SUBLANES, SUBLANES)
        rows = h2_ref.at[pl.ds(r0, SUBLANES), :]
        for s in range(SUBLANES):
            for kk in range(TOP_K):
                d = dest_ref[base + (r0 + s) * TOP_K + kk]
                pltpu.make_async_copy(rows.at[pl.ds(s, 1), :], xs_hbm.at[pl.ds(d, 1), :], sem).start()
        return c
    lax.fori_loop(0, tm // SUBLANES, issue, 0)
    for kk in range(TOP_K):
        pltpu.make_async_copy(h2_ref, xs_hbm.at[pl.ds(0, tm), :], sem).wait()


def _dispatch(h2, dest, zero_blocks, n_zero, n_rows):
    N, D = h2.shape
    tm = DSP_TILE
    grid_spec = pltpu.PrefetchScalarGridSpec(
        num_scalar_prefetch=3,
        grid=(N // tm,),
        in_specs=[pl.BlockSpec((tm, D), lambda i, *_: (i, 0))],
        out_specs=pl.BlockSpec(memory_space=pl.ANY),
        scratch_shapes=[pltpu.VMEM((MOE_TM, D), F32), pltpu.SemaphoreType.DMA(()),
                        pltpu.SemaphoreType.DMA(())],
    )
    return pl.pallas_call(
        _dispatch_kernel,
        grid_spec=grid_spec,
        out_shape=jax.ShapeDtypeStruct((n_rows, D), F32),
        compiler_params=pltpu.CompilerParams(dimension_semantics=("arbitrary",),
                                             vmem_limit_bytes=VMEM_LIMIT),
        name="dispatch",
    )(dest, zero_blocks, n_zero, h2)


def _moe_kernel(ie_ref, ist_ref, inb_ref, tail_ref,
                xs_hbm, wg_ref, wu_ref, wd_ref, bg_ref, bu_ref, bd_ref,
                ys_hbm,
                xbuf, acc, zblk, sem_x, sem_o, sem_z):
    tm = MOE_TM
    u = pl.program_id(0)
    j = pl.program_id(1)
    slot = pl.program_id(2)
    n_pairs = pl.num_programs(0)
    nj = pl.num_programs(1)
    i = 2 * u + slot
    nblk = inb_ref[i]
    n_blocks = ys_hbm.shape[0] // tm

    def x_copy(it, m):
        s0 = pl.multiple_of(ist_ref[it] + m * tm, tm)
        r0 = pl.multiple_of(m * tm, tm)
        return pltpu.make_async_copy(xs_hbm.at[pl.ds(s0, tm), :],
                                     xbuf.at[it % 2, pl.ds(r0, tm), :], sem_x.at[it % 2])

    def y_copy(it, m):
        d0 = pl.multiple_of(ist_ref[it] + m * tm, tm)
        r0 = pl.multiple_of(m * tm, tm)
        return pltpu.make_async_copy(acc.at[it % 2, pl.ds(r0, tm), :],
                                     ys_hbm.at[pl.ds(d0, tm), :], sem_o.at[it % 2])

    def for_blocks(it, fn):
        def body(m, c):
            fn(it, m)
            return c
        lax.fori_loop(0, inb_ref[it], body, 0)

    start_x = lambda it: for_blocks(it, lambda a, m: x_copy(a, m).start())
    wait_x = lambda it: for_blocks(it, lambda a, m: x_copy(a, m).wait())
    start_y = lambda it: for_blocks(it, lambda a, m: y_copy(a, m).start())
    wait_y = lambda it: for_blocks(it, lambda a, m: y_copy(a, m).wait())

    def zero_copy(b):
        d0 = pl.multiple_of(b * tm, tm)
        return pltpu.make_async_copy(zblk, ys_hbm.at[pl.ds(d0, tm), :], sem_z)

    @pl.when(jnp.logical_and(i == 0, j == 0))
    def _first_step():
        zblk[...] = jnp.zeros(zblk.shape, zblk.dtype)

        def issue(b, c):
            zero_copy(b).start()
            return c
        lax.fori_loop(tail_ref[0], n_blocks, issue, 0)
        start_x(0)

    @pl.when(j == 0)
    def _item_start():
        wait_x(i)

        @pl.when(slot == 0)
        def _():
            start_x(i + 1)

        @pl.when(u >= 1)
        def _():
            wait_y(i - 2)

        def init(m, c):
            r0 = pl.multiple_of(m * tm, tm)
            acc[slot, pl.ds(r0, tm), :] = jnp.broadcast_to(bd_ref[0], (tm, acc.shape[2]))
            return c
        lax.fori_loop(0, nblk, init, 0)

    def ffn_rows(n):
        x = xbuf[slot, pl.ds(0, n), :].astype(BF16)
        g = jnp.minimum(_dot(x, wg_ref[0].astype(BF16)) + bg_ref[0], SWIGLU_LIMIT)
        u = jnp.clip(_dot(x, wu_ref[0].astype(BF16)) + bu_ref[0], -SWIGLU_LIMIT, SWIGLU_LIMIT)
        a = (u + 1.0) * (g * jax.nn.sigmoid(SWIGLU_ALPHA * g))
        acc[slot, pl.ds(0, n), :] += _dot(a.astype(BF16), wd_ref[0].astype(BF16))

    for nb in range(1, MOE_BPI + 1):
        pl.when(nblk == nb)(functools.partial(ffn_rows, nb * tm))

    @pl.when(j == nj - 1)
    def _item_end():
        start_y(i)

        @pl.when(jnp.logical_and(slot == 1, u + 1 < n_pairs))
        def _():
            start_x(i + 1)

        @pl.when(jnp.logical_and(slot == 1, u == n_pairs - 1))
        def _last_step():
            wait_y(i - 1)
            wait_y(i)

            def finish(b, c):
                zero_copy(b).wait()
                return c
            lax.fori_loop(tail_ref[0], n_blocks, finish, 0)


def _moe(xs, pair_e, item_start, item_nblk, tail_blk, w_gate_up, b_gate_up, w_down, b_down):
    n_rows, D = xs.shape
    E, _, F2 = w_gate_up.shape
    F = F2 // 2
    tf = MOE_TF
    nj = F // tf
    n_pairs = pair_e.shape[0]
    rows = MOE_BPI * MOE_TM

    def jj(u, j, inb):
        return jnp.where(inb[2 * u] > 0, j, nj - 1)

    def spec(shape, index):
        return pl.BlockSpec(shape, lambda u, j, s, pe, ist, inb, tail: index(pe[u], jj(u, j, inb)))

    wg_spec = spec((1, D, tf), lambda e, c: (e, 0, c))
    wu_spec = spec((1, D, tf), lambda e, c: (e, 0, nj + c))
    wd_spec = spec((1, tf, D), lambda e, c: (e, c, 0))
    bg_spec = spec((1, 1, tf), lambda e, c: (e, 0, c))
    bu_spec = spec((1, 1, tf), lambda e, c: (e, 0, nj + c))
    bd_spec = spec((1, 1, D), lambda e, c: (e, 0, 0))
    any_spec = pl.BlockSpec(memory_space=pl.ANY)
    grid_spec = pltpu.PrefetchScalarGridSpec(
        num_scalar_prefetch=4,
        grid=(n_pairs, nj, 2),
        in_specs=[any_spec, wg_spec, wu_spec, wd_spec, bg_spec, bu_spec, bd_spec],
        out_specs=any_spec,
        scratch_shapes=[pltpu.VMEM((2, rows, D), F32), pltpu.VMEM((2, rows, D), F32),
                        pltpu.VMEM((MOE_TM, D), F32),
                        pltpu.SemaphoreType.DMA((2,)), pltpu.SemaphoreType.DMA((2,)),
                        pltpu.SemaphoreType.DMA(())],
    )
    bgu = b_gate_up.reshape(E, 1, F2)
    return pl.pallas_call(
        _moe_kernel,
        grid_spec=grid_spec,
        out_shape=jax.ShapeDtypeStruct((n_rows, D), F32),
        compiler_params=pltpu.CompilerParams(
            dimension_semantics=("arbitrary", "arbitrary", "arbitrary"),
            vmem_limit_bytes=VMEM_LIMIT),
        name="moe",
    )(pair_e, item_start, item_nblk, tail_blk, xs, w_gate_up, w_gate_up, w_down, bgu, bgu,
      b_down.reshape(E, 1, D))


def _combine_kernel(pos_ref, x1_ref, g_ref, ys_hbm, o_ref, buf, sem):
    tm = CMB_TILE
    i = pl.program_id(0)

    def start_gather(t):
        base = t * (tm * TOP_K)
        b = t % 2

        def issue(g, c):
            r0 = pl.multiple_of(g * SUBLANES, SUBLANES)
            for s in range(SUBLANES):
                for kk in range(TOP_K):
                    p = pos_ref[base + (r0 + s) * TOP_K + kk]
                    pltpu.make_async_copy(ys_hbm.at[pl.ds(p, 1), :],
                                          buf.at[b, kk, pl.ds(r0, SUBLANES), :].at[pl.ds(s, 1), :],
                                          sem.at[b]).start()
            return c
        lax.fori_loop(0, tm // SUBLANES, issue, 0)

    @pl.when(i == 0)
    def _():
        start_gather(0)

    @pl.when(i + 1 < pl.num_programs(0))
    def _():
        start_gather(i + 1)

    b = i % 2
    for kk in range(TOP_K):
        pltpu.make_async_copy(ys_hbm.at[pl.ds(0, tm), :], buf.at[b, kk], sem.at[b]).wait()
    out = x1_ref[...]
    for kk in range(TOP_K):
        out = out + g_ref[:, kk:kk + 1] * buf[b, kk]
    o_ref[...] = out


def _combine(x1, gates, ys, dest):
    N, D = x1.shape
    tm = CMB_TILE
    grid_spec = pltpu.PrefetchScalarGridSpec(
        num_scalar_prefetch=1,
        grid=(N // tm,),
        in_specs=[pl.BlockSpec((tm, D), lambda i, pos: (i, 0)),
                  pl.BlockSpec((tm, LANES), lambda i, pos: (i, 0)),
                  pl.BlockSpec(memory_space=pl.ANY)],
        out_specs=pl.BlockSpec((tm, D), lambda i, pos: (i, 0)),
        scratch_shapes=[pltpu.VMEM((2, TOP_K, tm, D), F32), pltpu.SemaphoreType.DMA((2,))],
    )
    return pl.pallas_call(
        _combine_kernel,
        grid_spec=grid_spec,
        out_shape=jax.ShapeDtypeStruct((N, D), F32),
        compiler_params=pltpu.CompilerParams(dimension_semantics=("arbitrary",),
                                             vmem_limit_bytes=VMEM_LIMIT),
        name="combine",
    )(dest, x1, gates, ys)


def _routing(top_idx, n_tok):
    tm, bpi = MOE_TM, MOE_BPI
    nk = n_tok * TOP_K
    experts = jnp.arange(N_EXPERTS, dtype=jnp.int32)
    e_flat = top_idx.reshape(-1)
    onehot = (e_flat[:, None] == experts[None, :]).astype(jnp.int32)
    csum = jnp.cumsum(onehot, axis=0)
    rank = jnp.sum(csum * onehot, axis=1) - 1
    counts = csum[-1]
    nb = (counts + tm - 1) // tm
    bend = jnp.cumsum(nb)
    bstart = bend - nb
    dest = (jnp.sum(onehot * bstart[None, :], axis=1) * tm + rank).astype(jnp.int32)
    n_blocks = -(-(nk + N_EXPERTS * (tm - 1)) // tm)
    n_rows = n_blocks * tm
    tail_blk = bend[-1:].astype(jnp.int32)
    zb_e = jnp.where(nb > 0, bend - 1, -1)
    zb_t = jnp.arange(n_blocks, dtype=jnp.int32)
    zb_all = jnp.concatenate([zb_e, jnp.where(zb_t >= bend[-1], zb_t, -1)]).astype(jnp.int32)
    order = jnp.argsort(zb_all < 0, stable=True)
    zero_blocks = zb_all[order]
    n_zero = jnp.sum(zb_all >= 0).astype(jnp.int32).reshape(1)
    n_pairs = n_blocks // (2 * bpi) + N_EXPERTS
    pairs_e = (nb + 2 * bpi - 1) // (2 * bpi)
    pend = jnp.cumsum(pairs_e)
    pstart = pend - pairs_e
    slot = jnp.arange(n_pairs, dtype=jnp.int32)
    valid = slot < pend[-1]
    exp_of = jnp.minimum(jnp.searchsorted(pend, slot, side="right"), N_EXPERTS - 1).astype(jnp.int32)
    last_e = jnp.max(jnp.where(nb > 0, experts, 0))
    local = slot - pstart[exp_of]
    n_pr = jnp.maximum(pairs_e[exp_of], 1)
    base, rem = nb[exp_of] // n_pr, nb[exp_of] % n_pr
    first_blk = bstart[exp_of] + local * base + jnp.minimum(local, rem)
    size = jnp.where(valid, base + (local < rem), 0)
    n0 = (size + 1) // 2
    pair_e = jnp.where(valid, exp_of, last_e).astype(jnp.int32)
    item_start = jnp.stack([first_blk, first_blk + n0], axis=1) * tm
    item_start = jnp.where(valid[:, None], item_start, 0).reshape(-1).astype(jnp.int32)
    item_nblk = jnp.stack([n0, size - n0], axis=1).reshape(-1).astype(jnp.int32)
    return dest, zero_blocks, n_zero, pair_e, item_start, item_nblk, tail_blk, n_rows


def _rope_lane_tile(v):
    half = MLA_ROPE // 2
    z = jnp.zeros((half,), v.dtype)
    return jnp.concatenate([v[:half], z, v[half:], z]).reshape(1, LANES)


def _rope_cols(w):
    half = MLA_ROPE // 2
    z = jnp.zeros((w.shape[0], half), w.dtype)
    return jnp.concatenate([w[:, :half], z, w[:, half:], z], axis=1)


def _layer(x, mem, positions, attn_norm, w_in, q_a_norm, w_q_b, kv_a_norm, w_kv_b,
           q_nope_norm, q_rope_norm, k_nope_norm, k_rope_norm,
           gmlp_v_norm, w_spatial, b_spatial,
           mem_norm, w_mem_kv, mem_q_norm, mem_k_norm,
           mla_out_norm, gmlp_out_norm, mem_out_norm, w_o,
           ffn_norm, w_router, b_router, w_gate_up, b_gate_up, w_down, b_down):
    B, S, D = x.shape
    N = B * S
    x2 = x.reshape(N, D)

    o1 = Q_LORA
    o2 = o1 + KV_LORA
    o3 = o2 + MLA_ROPE
    o4 = o3 + 2 * GMLP_WIDTH
    w_in_r = jnp.concatenate([w_in[:, :o2], w_in[:, o3:o4], w_in[:, o4:], _rope_cols(w_in[:, o2:o3])],
                             axis=1).astype(BF16)
    wq = w_q_b.reshape(Q_LORA, MLA_HEADS, MLA_NOPE + MLA_ROPE)
    wq_rope = jax.vmap(_rope_cols, in_axes=1, out_axes=1)(wq[:, :, MLA_NOPE:])
    wq_pad = jnp.concatenate([wq[:, :, :MLA_NOPE], wq_rope], axis=2).reshape(Q_LORA, MLA_HEADS * HEAD_PAD)
    wq_pad = wq_pad.astype(BF16)
    half = MLA_ROPE // 2
    inv_freq = ROPE_BASE ** (-jnp.arange(half, dtype=F32) / half)
    zf = jnp.zeros((half,), F32)
    invf_tile = jnp.concatenate([inv_freq, zf, inv_freq, zf]).reshape(1, LANES)
    sgn_tile = jnp.concatenate([-jnp.ones((half,), F32), zf, jnp.ones((half,), F32), zf]).reshape(1, LANES)
    bs_tile = jnp.repeat(b_spatial.T, GMLP_CH, axis=1)

    mk, mv = _mem_kv(mem, mem_norm, w_mem_kv, mem_k_norm)
    cq, ckv, kpe, ogm, cs = _inproj(
        x2, positions.reshape(N, 1), (invf_tile, sgn_tile), attn_norm, w_in_r, q_a_norm, kv_a_norm,
        _rope_lane_tile(k_rope_norm), gmlp_v_norm, w_spatial, bs_tile, mk, mv, mem_q_norm,
        gmlp_out_norm, mem_out_norm, S)
    q, k, v = _qkv(cq, ckv, kpe, cs, wq_pad, w_kv_b.astype(BF16), q_nope_norm,
                   _rope_lane_tile(q_rope_norm), k_nope_norm)
    o_mla = _attention(q, k, v, B, S).reshape(N, MLA_HEADS * MLA_V)

    wo = w_o.astype(BF16)
    wr_pad = jnp.pad(w_router, ((0, 0), (0, LANES - N_EXPERTS)))
    br_pad = jnp.pad(b_router, (0, LANES - N_EXPERTS)).reshape(1, LANES)
    x1, h2, ti, tg = _outproj(x2, o_mla, ogm, mla_out_norm, wo[:MLA_HEADS * MLA_V],
                              wo[MLA_HEADS * MLA_V:], ffn_norm, wr_pad, br_pad)

    dest, zero_blocks, n_zero, pair_e, item_start, item_nblk, tail_blk, n_rows = _routing(
        ti[:, :TOP_K], N)
    xs = _dispatch(h2, dest, zero_blocks, n_zero, n_rows)
    ys = _moe(xs, pair_e, item_start, item_nblk, tail_blk, w_gate_up, b_gate_up, w_down, b_down)
    out = _combine(x1, tg, ys, dest)
    return out.reshape(B, S, D)


def kernel(x, mem, positions, attn_norm, w_in, q_a_norm, w_q_b, kv_a_norm, w_kv_b, q_nope_norm, q_rope_norm, k_nope_norm, k_rope_norm, gmlp_v_norm, w_spatial, b_spatial, mem_norm, w_mem_kv, mem_q_norm, mem_k_norm, mla_out_norm, gmlp_out_norm, mem_out_norm, w_o, ffn_norm, w_router, b_router, w_gate_up, b_gate_up, w_down, b_down):
    depth = attn_norm.shape[0]
    for l in range(depth):
        x = _layer(x, mem, positions, attn_norm[l], w_in[l], q_a_norm[l], w_q_b[l], kv_a_norm[l],
                   w_kv_b[l], q_nope_norm[l], q_rope_norm[l], k_nope_norm[l], k_rope_norm[l],
                   gmlp_v_norm[l], w_spatial[l], b_spatial[l], mem_norm[l], w_mem_kv[l],
                   mem_q_norm[l], mem_k_norm[l], mla_out_norm[l], gmlp_out_norm[l],
                   mem_out_norm[l], w_o[l], ffn_norm[l], w_router[l], b_router[l], w_gate_up[l],
                   b_gate_up[l], w_down[l], b_down[l])
    return x
```

```python
import functools

import jax
import jax.numpy as jnp
import numpy as np
from jax import lax
from jax.experimental import pallas as pl
from jax.experimental.pallas import tpu as pltpu

F32 = jnp.float32
BF16 = jnp.bfloat16

EPS = 1e-6
LANES = 128
SUBLANES = 8
VMEM_LIMIT = 56 * 1024 * 1024

MLA_HEADS = 8
MLA_NOPE = 128
MLA_ROPE = 64
MLA_V = 128
Q_LORA = 512
KV_LORA = 512
GMLP_GROUPS = 4
GMLP_CH = 128
GMLP_WIDTH = GMLP_GROUPS * GMLP_CH
CHUNK = 128
MEM_HEADS = 4
MEM_HEAD_DIM = 128
MEM_WIDTH = MEM_HEADS * MEM_HEAD_DIM
N_EXPERTS = 32
TOP_K = 4
SWIGLU_ALPHA = 1.702
SWIGLU_LIMIT = 7.0
ROPE_BASE = 10000.0
HEAD_PAD = 2 * LANES

TOK_TILE = 512
ATT_TQ = 512
ATT_TK = 512
ATT_HPS = 2
MOE_TM = 256
MOE_BPI = 4
MOE_TF = 256
DSP_TILE = 1024
CMB_TILE = 256


def _rms(x, g, n=None):
    n = x.shape[-1] if n is None else n
    ms = jnp.sum(x * x, axis=-1, keepdims=True) * (1.0 / n)
    return x * lax.rsqrt(ms + EPS) * g


def _gelu(x):
    return 0.5 * x * (1.0 + lax.erf(x * (2.0 ** -0.5)))


def _dot(a, b):
    return jnp.dot(a, b, preferred_element_type=F32)


def _dot_nt(a, b):
    return lax.dot_general(a, b, (((1,), (1,)), ((), ())), preferred_element_type=F32)


def _const_spec(shape):
    nd = len(shape)
    return pl.BlockSpec(shape, lambda *_: (0,) * nd)


def _mem_kv_kernel(mem_ref, g_ref, w_ref, kg_ref, mk_ref, mv_ref):
    m = mem_ref[0]
    hn = _rms(m, g_ref[...]).astype(BF16)
    kv = _dot(hn, w_ref[...])
    for h in range(MEM_HEADS):
        k = kv[:, h * MEM_HEAD_DIM:(h + 1) * MEM_HEAD_DIM]
        mk_ref[0, :, h * MEM_HEAD_DIM:(h + 1) * MEM_HEAD_DIM] = _rms(k, kg_ref[...]).astype(BF16)
    mv_ref[0] = kv[:, MEM_WIDTH:].astype(BF16)


def _mem_kv(mem, mem_norm, w_mem_kv, mem_k_norm):
    B, M, D = mem.shape
    return pl.pallas_call(
        _mem_kv_kernel,
        grid=(B,),
        in_specs=[pl.BlockSpec((1, M, D), lambda b: (b, 0, 0)),
                  _const_spec((1, D)),
                  _const_spec((D, 2 * MEM_WIDTH)),
                  _const_spec((1, MEM_HEAD_DIM))],
        out_specs=[pl.BlockSpec((1, M, MEM_WIDTH), lambda b: (b, 0, 0)),
                   pl.BlockSpec((1, M, MEM_WIDTH), lambda b: (b, 0, 0))],
        out_shape=[jax.ShapeDtypeStruct((B, M, MEM_WIDTH), BF16),
                   jax.ShapeDtypeStruct((B, M, MEM_WIDTH), BF16)],
        compiler_params=pltpu.CompilerParams(dimension_semantics=("arbitrary",),
                                             vmem_limit_bytes=VMEM_LIMIT),
        name="mem_kv",
    )(mem, mem_norm.reshape(1, D), w_mem_kv.astype(BF16), mem_k_norm.reshape(1, MEM_HEAD_DIM))


_O_CQ = 0
_O_CKV = _O_CQ + Q_LORA
_O_U = _O_CKV + KV_LORA
_O_VG = _O_U + GMLP_WIDTH
_O_QM = _O_VG + GMLP_WIDTH
_O_KR = _O_QM + MEM_WIDTH
_IN_COLS_PAD = _O_KR + LANES


def _rope_tile(t, cos, sin_signed):
    return t * cos + pltpu.roll(t, LANES // 2, axis=1) * sin_signed


def _inproj_kernel(x_ref, pos_ref, invf_ref, sgn_ref, an_ref, w_ref, qan_ref, kvan_ref, krn_ref,
                   gvn_ref, ws_ref, bs_ref, mk_ref, mv_ref, mqn_ref, gon_ref, mon_ref,
                   cq_ref, ckv_ref, kpe_ref, ogm_ref, cs_ref):
    tm = x_ref.shape[0]
    h = _rms(x_ref[...], an_ref[...]).astype(BF16)
    z = _dot(h, w_ref[...])

    cq_ref[...] = _rms(z[:, _O_CQ:_O_CQ + Q_LORA], qan_ref[...]).astype(BF16)
    ckv_ref[...] = _rms(z[:, _O_CKV:_O_CKV + KV_LORA], kvan_ref[...]).astype(BF16)

    ang = pos_ref[...].astype(F32) * invf_ref[...]
    cos = jnp.cos(ang)
    sin_signed = jnp.sin(ang) * sgn_ref[...]
    cs_ref[:, :LANES] = cos
    cs_ref[:, LANES:] = sin_signed
    kr = _rms(z[:, _O_KR:_O_KR + LANES], krn_ref[...], MLA_ROPE)
    kpe_ref[...] = _rope_tile(kr, cos, sin_signed).astype(BF16)

    u = _gelu(z[:, _O_U:_O_U + GMLP_WIDTH])
    vg = _gelu(z[:, _O_VG:_O_VG + GMLP_WIDTH])
    vg = _rms(vg, gvn_ref[...]).astype(BF16)
    row = lax.broadcasted_iota(jnp.int32, (CHUNK, CHUNK), 0)
    col = lax.broadcasted_iota(jnp.int32, (CHUNK, CHUNK), 1)
    sp_cols = []
    for g in range(GMLP_GROUPS):
        wsg = jnp.where(col <= row, ws_ref[g], 0.0).astype(BF16)
        sp_rows = [_dot(wsg, vg[c * CHUNK:(c + 1) * CHUNK, g * GMLP_CH:(g + 1) * GMLP_CH])
                   for c in range(tm // CHUNK)]
        sp_cols.append(jnp.concatenate(sp_rows, axis=0))
    sp = jnp.concatenate(sp_cols, axis=1) + jnp.concatenate([bs_ref[...]] * (tm // CHUNK), axis=0)
    ogm_ref[:, :GMLP_WIDTH] = _rms(u * sp, gon_ref[...]).astype(BF16)

    o_heads = []
    for hd in range(MEM_HEADS):
        sl = slice(hd * MEM_HEAD_DIM, (hd + 1) * MEM_HEAD_DIM)
        q = z[:, _O_QM + hd * MEM_HEAD_DIM:_O_QM + (hd + 1) * MEM_HEAD_DIM]
        qn = (_rms(q, mqn_ref[...]) * (MEM_HEAD_DIM ** -0.5)).astype(BF16)
        s = _dot_nt(qn, mk_ref[0, :, sl])
        s = s - jnp.max(s, axis=-1, keepdims=True)
        p = jnp.exp(s)
        p = p / jnp.sum(p, axis=-1, keepdims=True)
        o_heads.append(_dot(p.astype(BF16), mv_ref[0, :, sl]))
    o_mem = jnp.concatenate(o_heads, axis=1)
    ogm_ref[:, GMLP_WIDTH:] = _rms(o_mem, mon_ref[...]).astype(BF16)


def _inproj(x2, pos, tables, attn_norm, w_in_r, q_a_norm, kv_a_norm, krn_tile, gmlp_v_norm,
            w_spatial, bs_tile, mk, mv, mem_q_norm, gmlp_out_norm, mem_out_norm, seq):
    N, D = x2.shape
    tm = TOK_TILE
    tiles_per_seq = seq // tm
    invf_tile, sgn_tile = tables
    M = mk.shape[1]
    row = lambda w: pl.BlockSpec((tm, w), lambda i: (i, 0))
    batch_blk = pl.BlockSpec((1, M, MEM_WIDTH), lambda i: (i // tiles_per_seq, 0, 0))
    return pl.pallas_call(
        _inproj_kernel,
        grid=(N // tm,),
        in_specs=[row(D), row(1), _const_spec((1, LANES)), _const_spec((1, LANES)),
                  _const_spec((1, D)), _const_spec((D, _IN_COLS_PAD)),
                  _const_spec((1, Q_LORA)), _const_spec((1, KV_LORA)), _const_spec((1, LANES)),
                  _const_spec((1, GMLP_WIDTH)), _const_spec((GMLP_GROUPS, CHUNK, CHUNK)),
                  _const_spec((CHUNK, GMLP_WIDTH)), batch_blk, batch_blk,
                  _const_spec((1, MEM_HEAD_DIM)), _const_spec((1, GMLP_WIDTH)),
                  _const_spec((1, MEM_WIDTH))],
        out_specs=[row(Q_LORA), row(KV_LORA), row(LANES), row(GMLP_WIDTH + MEM_WIDTH),
                   row(2 * LANES)],
        out_shape=[jax.ShapeDtypeStruct((N, Q_LORA), BF16),
                   jax.ShapeDtypeStruct((N, KV_LORA), BF16),
                   jax.ShapeDtypeStruct((N, LANES), BF16),
                   jax.ShapeDtypeStruct((N, GMLP_WIDTH + MEM_WIDTH), BF16),
                   jax.ShapeDtypeStruct((N, 2 * LANES), F32)],
        compiler_params=pltpu.CompilerParams(dimension_semantics=("arbitrary",),
                                             vmem_limit_bytes=VMEM_LIMIT),
        name="inproj",
    )(x2, pos, invf_tile, sgn_tile, attn_norm.reshape(1, D), w_in_r,
      q_a_norm.reshape(1, -1), kv_a_norm.reshape(1, -1), krn_tile, gmlp_v_norm.reshape(1, -1),
      w_spatial, bs_tile, mk, mv, mem_q_norm.reshape(1, -1), gmlp_out_norm.reshape(1, -1),
      mem_out_norm.reshape(1, -1))


def _qkv_kernel(cq_ref, ckv_ref, kpe_ref, cs_ref, wq_ref, wkv_ref, qnn_ref, qrn_ref, knn_ref,
                q_ref, k_ref, v_ref):
    cos = cs_ref[:, :LANES]
    sin_signed = cs_ref[:, LANES:]
    scale = (MLA_NOPE + MLA_ROPE) ** -0.5 * np.log2(np.e)
    qr = _dot(cq_ref[...], wq_ref[...])
    kvr = _dot(ckv_ref[...], wkv_ref[...])
    kpe = kpe_ref[...]
    for h in range(MLA_HEADS):
        o = h * HEAD_PAD
        qn = _rms(qr[:, o:o + MLA_NOPE], qnn_ref[...]) * scale
        qt = _rms(qr[:, o + MLA_NOPE:o + HEAD_PAD], qrn_ref[...], MLA_ROPE)
        qt = _rope_tile(qt, cos, sin_signed) * scale
        q_ref[:, o:o + MLA_NOPE] = qn.astype(BF16)
        q_ref[:, o + MLA_NOPE:o + HEAD_PAD] = qt.astype(BF16)
        ko = h * (MLA_NOPE + MLA_V)
        k_ref[:, o:o + MLA_NOPE] = _rms(kvr[:, ko:ko + MLA_NOPE], knn_ref[...]).astype(BF16)
        k_ref[:, o + MLA_NOPE:o + HEAD_PAD] = kpe
        v_ref[:, h * MLA_V:(h + 1) * MLA_V] = kvr[:, ko + MLA_NOPE:ko + MLA_NOPE + MLA_V].astype(BF16)


def _qkv(cq, ckv, kpe, cs, wq_pad, wkv, q_nope_norm, qrn_tile, k_nope_norm):
    N = cq.shape[0]
    tm = TOK_TILE
    row = lambda w: pl.BlockSpec((tm, w), lambda i: (i, 0))
    return pl.pallas_call(
        _qkv_kernel,
        grid=(N // tm,),
        in_specs=[row(Q_LORA), row(KV_LORA), row(LANES), row(2 * LANES),
                  _const_spec(wq_pad.shape), _const_spec(wkv.shape),
                  _const_spec((1, MLA_NOPE)), _const_spec((1, LANES)), _const_spec((1, MLA_NOPE))],
        out_specs=[row(MLA_HEADS * HEAD_PAD), row(MLA_HEADS * HEAD_PAD), row(MLA_HEADS * MLA_V)],
        out_shape=[jax.ShapeDtypeStruct((N, MLA_HEADS * HEAD_PAD), BF16),
                   jax.ShapeDtypeStruct((N, MLA_HEADS * HEAD_PAD), BF16),
                   jax.ShapeDtypeStruct((N, MLA_HEADS * MLA_V), BF16)],
        compiler_params=pltpu.CompilerParams(dimension_semantics=("arbitrary",),
                                             vmem_limit_bytes=VMEM_LIMIT),
        name="qkv",
    )(cq, ckv, kpe, cs, wq_pad, wkv, q_nope_norm.reshape(1, -1), qrn_tile,
      k_nope_norm.reshape(1, -1))


def _attn_kernel(q_ref, k_ref, v_ref, o_ref):
    tq, tk = ATT_TQ, ATT_TK
    qi = pl.program_id(2)
    qs = [q_ref[0, :, h * HEAD_PAD:(h + 1) * HEAD_PAD] for h in range(ATT_HPS)]

    def head_step(h, kb, carry, masked):
        m, l, acc = carry
        k0 = pl.multiple_of(kb * tk, tk)
        s = _dot_nt(qs[h], k_ref[0, pl.ds(k0, tk), h * HEAD_PAD:(h + 1) * HEAD_PAD])
        if masked:
            qpos = qi * tq + lax.broadcasted_iota(jnp.int32, (tq, tk), 0)
            kpos = kb * tk + lax.broadcasted_iota(jnp.int32, (tq, tk), 1)
            s = jnp.where(kpos <= qpos, s, -jnp.inf)
        m_new = jnp.maximum(m, jnp.max(s, axis=-1, keepdims=True))
        alpha = jnp.exp2(m - m_new)
        p = jnp.exp2(s - m_new)
        l = alpha * l + jnp.sum(p, axis=-1, keepdims=True)
        pv = _dot(p.astype(BF16), v_ref[0, pl.ds(k0, tk), h * MLA_V:(h + 1) * MLA_V])
        return m_new, l, alpha * acc + pv

    def step(kb, carries, masked):
        return tuple(head_step(h, kb, carries[h], masked) for h in range(ATT_HPS))

    init = tuple((jnp.full((tq, 1), -jnp.inf, F32), jnp.zeros((tq, 1), F32),
                  jnp.zeros((tq, MLA_V), F32)) for _ in range(ATT_HPS))
    n_full = (qi * tq) // tk
    carries = lax.fori_loop(0, n_full, lambda kb, c: step(kb, c, False), init)
    carries = step(n_full, carries, True)
    for h in range(ATT_HPS):
        _, l, acc = carries[h]
        o_ref[0, :, h * MLA_V:(h + 1) * MLA_V] = (acc / l).astype(BF16)


def _attention(q, k, v, batch, seq):
    q3 = q.reshape(batch, seq, MLA_HEADS * HEAD_PAD)
    k3 = k.reshape(batch, seq, MLA_HEADS * HEAD_PAD)
    v3 = v.reshape(batch, seq, MLA_HEADS * MLA_V)
    hp, hv = ATT_HPS * HEAD_PAD, ATT_HPS * MLA_V
    assert ATT_TQ == ATT_TK and seq % ATT_TQ == 0
    return pl.pallas_call(
        _attn_kernel,
        grid=(batch, MLA_HEADS // ATT_HPS, seq // ATT_TQ),
        in_specs=[pl.BlockSpec((1, ATT_TQ, hp), lambda b, h, i: (b, i, h)),
                  pl.BlockSpec((1, seq, hp), lambda b, h, i: (b, 0, h)),
                  pl.BlockSpec((1, seq, hv), lambda b, h, i: (b, 0, h))],
        out_specs=pl.BlockSpec((1, ATT_TQ, hv), lambda b, h, i: (b, i, h)),
        out_shape=jax.ShapeDtypeStruct((batch, seq, MLA_HEADS * MLA_V), BF16),
        compiler_params=pltpu.CompilerParams(
            dimension_semantics=("arbitrary", "arbitrary", "arbitrary"),
            vmem_limit_bytes=VMEM_LIMIT),
        name="attn",
    )(q3, k3, v3)


def _outproj_kernel(x_ref, oa_ref, ogm_ref, aon_ref, woa_ref, wob_ref, fn_ref, wr_ref, br_ref,
                    x1_ref, h2_ref, ti_ref, tg_ref):
    tm = x_ref.shape[0]
    oa = _rms(oa_ref[...].astype(F32), aon_ref[...]).astype(BF16)
    x1 = x_ref[...] + _dot(oa, woa_ref[...]) + _dot(ogm_ref[...], wob_ref[...])
    x1_ref[...] = x1
    h2 = _rms(x1, fn_ref[...])
    h2_ref[...] = h2
    h_hi = h2.astype(BF16)
    h_lo = (h2 - h_hi.astype(F32)).astype(BF16)
    wr = wr_ref[...]
    w_hi = wr.astype(BF16)
    w_lo = (wr - w_hi.astype(F32)).astype(BF16)
    hh = _dot(h_hi, jnp.concatenate([w_hi, w_lo], axis=1))
    logits = hh[:, :LANES] + (_dot(h_lo, w_hi) + hh[:, LANES:]) + br_ref[...]
    lane = lax.broadcasted_iota(jnp.int32, (tm, LANES), 1)
    lg = jnp.where(lane < N_EXPERTS, logits, -jnp.inf)
    vals, idxs = [], []
    for _ in range(TOP_K):
        m = jnp.max(lg, axis=-1, keepdims=True)
        am = jnp.min(jnp.where(lg == m, lane, LANES), axis=-1, keepdims=True)
        vals.append(m)
        idxs.append(am)
        lg = jnp.where(lane == am, -jnp.inf, lg)
    es = [jnp.exp(v - vals[0]) for v in vals]
    denom = es[0] + es[1] + es[2] + es[3]
    ti = jnp.zeros((tm, LANES), jnp.int32)
    tg = jnp.zeros((tm, LANES), F32)
    for kk in range(TOP_K):
        ti = jnp.where(lane == kk, idxs[kk], ti)
        tg = jnp.where(lane == kk, es[kk] / denom, tg)
    ti_ref[...] = ti
    tg_ref[...] = tg


def _outproj(x2, o_mla, ogm, mla_out_norm, wo_a, wo_b, ffn_norm, wr_pad, br_pad):
    N, D = x2.shape
    tm = TOK_TILE
    row = lambda w: pl.BlockSpec((tm, w), lambda i: (i, 0))
    wa = o_mla.shape[1]
    wb = ogm.shape[1]
    return pl.pallas_call(
        _outproj_kernel,
        grid=(N // tm,),
        in_specs=[row(D), row(wa), row(wb), _const_spec((1, wa)), _const_spec((wa, D)),
                  _const_spec((wb, D)), _const_spec((1, D)), _const_spec((D, LANES)),
                  _const_spec((1, LANES))],
        out_specs=[row(D), row(D), row(LANES), row(LANES)],
        out_shape=[jax.ShapeDtypeStruct((N, D), F32), jax.ShapeDtypeStruct((N, D), F32),
                   jax.ShapeDtypeStruct((N, LANES), jnp.int32),
                   jax.ShapeDtypeStruct((N, LANES), F32)],
        compiler_params=pltpu.CompilerParams(dimension_semantics=("arbitrary",),
                                             vmem_limit_bytes=VMEM_LIMIT),
        name="outproj",
    )(x2, o_mla, ogm, mla_out_norm.reshape(1, -1), wo_a, wo_b, ffn_norm.reshape(1, -1), wr_pad,
      br_pad)


def _dispatch_kernel(dest_ref, zb_ref, nzb_ref, h2_ref, xs_hbm, zblk, sem, sem_z):
    tm = h2_ref.shape[0]
    bm = MOE_TM
    i = pl.program_id(0)
    base = i * (tm * TOP_K)

    @pl.when(i == 0)
    def _zero_blocks():
        zblk[...] = jnp.zeros(zblk.shape, zblk.dtype)

        def zcopy(n):
            d0 = pl.multiple_of(zb_ref[n] * bm, bm)
            return pltpu.make_async_copy(zblk, xs_hbm.at[pl.ds(d0, bm), :], sem_z)

        def issue(n, c):
            zcopy(n).start()
            return c
        lax.fori_loop(0, nzb_ref[0], issue, 0)

        def finish(n, c):
            zcopy(n).wait()
            return c
        lax.fori_loop(0, nzb_ref[0], finish, 0)

    def issue(g, c):
        r0 = pl.multiple_of(g * SUBLANES, SUBLANES)
        rows = h2_ref.at[pl.ds(r0, SUBLANES), :]
        for s in range(SUBLANES):
            for kk in range(TOP_K):
                d = dest_ref[base + (r0 + s) * TOP_K + kk]
                pltpu.make_async_copy(rows.at[pl.ds(s, 1), :], xs_hbm.at[pl.ds(d, 1), :], sem).start()
        return c
    lax.fori_loop(0, tm // SUBLANES, issue, 0)
    for kk in range(TOP_K):
        pltpu.make_async_copy(h2_ref, xs_hbm.at[pl.ds(0, tm), :], sem).wait()


def _dispatch(h2, dest, zero_blocks, n_zero, n_rows):
    N, D = h2.shape
    tm = DSP_TILE
    grid_spec = pltpu.PrefetchScalarGridSpec(
        num_scalar_prefetch=3,
        grid=(N // tm,),
        in_specs=[pl.BlockSpec((tm, D), lambda i, *_: (i, 0))],
        out_specs=pl.BlockSpec(memory_space=pl.ANY),
        scratch_shapes=[pltpu.VMEM((MOE_TM, D), F32), pltpu.SemaphoreType.DMA(()),
                        pltpu.SemaphoreType.DMA(())],
    )
    return pl.pallas_call(
        _dispatch_kernel,
        grid_spec=grid_spec,
        out_shape=jax.ShapeDtypeStruct((n_rows, D), F32),
        compiler_params=pltpu.CompilerParams(dimension_semantics=("arbitrary",),
                                             vmem_limit_bytes=VMEM_LIMIT),
        name="dispatch",
    )(dest, zero_blocks, n_zero, h2)


def _moe_kernel(ie_ref, ist_ref, inb_ref, tail_ref,
                xs_hbm, wgu_hbm, wd_hbm, bgu_ref, bd_ref,
                ys_hbm,
                xbuf, acc, wg_buf, wu_buf, wd_buf, zblk, sem_x, sem_o, sem_w, sem_z):
    tm = MOE_TM
    i = pl.program_id(0)
    n_items = pl.num_programs(0)
    nblk = inb_ref[i]
    slot = i % 2
    n_blocks = ys_hbm.shape[0] // tm
    tf = wg_buf.shape[2]
    nj = wd_hbm.shape[1] // tf

    def x_copy(it, m):
        s0 = pl.multiple_of(ist_ref[it] + m * tm, tm)
        r0 = pl.multiple_of(m * tm, tm)
        return pltpu.make_async_copy(xs_hbm.at[pl.ds(s0, tm), :],
                                     xbuf.at[it % 2, pl.ds(r0, tm), :], sem_x.at[it % 2])

    def y_copy(it, m):
        d0 = pl.multiple_of(ist_ref[it] + m * tm, tm)
        r0 = pl.multiple_of(m * tm, tm)
        return pltpu.make_async_copy(acc.at[it % 2, pl.ds(r0, tm), :],
                                     ys_hbm.at[pl.ds(d0, tm), :], sem_o.at[it % 2])

    def for_blocks(it, fn):
        def body(m, c):
            fn(it, m)
            return c
        lax.fori_loop(0, inb_ref[it], body, 0)

    start_x = lambda it: for_blocks(it, lambda a, m: x_copy(a, m).start())
    wait_x = lambda it: for_blocks(it, lambda a, m: x_copy(a, m).wait())
    start_y = lambda it: for_blocks(it, lambda a, m: y_copy(a, m).start())
    wait_y = lambda it: for_blocks(it, lambda a, m: y_copy(a, m).wait())

    def zero_copy(b):
        d0 = pl.multiple_of(b * tm, tm)
        return pltpu.make_async_copy(zblk, ys_hbm.at[pl.ds(d0, tm), :], sem_z)

    def w_copies(it, j):
        e = ie_ref[it]
        ws = j % 2
        c0 = pl.multiple_of(j * tf, tf)
        c1 = pl.multiple_of(nj * tf + j * tf, tf)
        return (pltpu.make_async_copy(wgu_hbm.at[e, :, pl.ds(c0, tf)], wg_buf.at[ws], sem_w.at[ws]),
                pltpu.make_async_copy(wgu_hbm.at[e, :, pl.ds(c1, tf)], wu_buf.at[ws], sem_w.at[ws]),
                pltpu.make_async_copy(wd_hbm.at[e, pl.ds(c0, tf), :], wd_buf.at[ws], sem_w.at[ws]))

    def start_w(it, j):
        for cp in w_copies(it, j):
            cp.start()

    def wait_w(it, j):
        for cp in w_copies(it, j):
            cp.wait()

    @pl.when(i == 0)
    def _first_step():
        zblk[...] = jnp.zeros(zblk.shape, zblk.dtype)

        def issue(b, c):
            zero_copy(b).start()
            return c
        lax.fori_loop(tail_ref[0], n_blocks, issue, 0)
        start_x(0)

        @pl.when(nblk > 0)
        def _():
            start_w(0, 0)

    wait_x(i)

    @pl.when(i + 1 < n_items)
    def _():
        start_x(i + 1)

    @pl.when(i >= 2)
    def _():
        wait_y(i - 2)

    def init(m, c):
        r0 = pl.multiple_of(m * tm, tm)
        acc[slot, pl.ds(r0, tm), :] = jnp.broadcast_to(bd_ref[0], (tm, acc.shape[2]))
        return c
    lax.fori_loop(0, nblk, init, 0)

    def ffn_rows(j, n):
        ws = j % 2
        x = xbuf[slot, pl.ds(0, n), :].astype(BF16)
        g = jnp.minimum(_dot(x, wg_buf[ws].astype(BF16)) + bgu_ref[0, pl.ds(j, 1), :], SWIGLU_LIMIT)
        u = jnp.clip(_dot(x, wu_buf[ws].astype(BF16)) + bgu_ref[0, pl.ds(nj + j, 1), :],
                     -SWIGLU_LIMIT, SWIGLU_LIMIT)
        a = (u + 1.0) * (g * jax.nn.sigmoid(SWIGLU_ALPHA * g))
        acc[slot, pl.ds(0, n), :] += _dot(a.astype(BF16), wd_buf[ws].astype(BF16))

    def chunk(j, c):
        wait_w(i, j)

        @pl.when(j + 1 < nj)
        def _():
            start_w(i, j + 1)

        @pl.when(jnp.logical_and(j + 1 == nj, i + 1 < n_items))
        def _():
            @pl.when(inb_ref[i + 1] > 0)
            def _():
                start_w(i + 1, 0)

        for nb in range(1, MOE_BPI + 1):
            pl.when(nblk == nb)(functools.partial(ffn_rows, j, nb * tm))
        return c

    @pl.when(nblk > 0)
    def _chunks():
        lax.fori_loop(0, nj, chunk, 0)

    start_y(i)

    @pl.when(i == n_items - 1)
    def _last_step():
        @pl.when(i >= 1)
        def _():
            wait_y(i - 1)
        wait_y(i)

        def finish(b, c):
            zero_copy(b).wait()
            return c
        lax.fori_loop(tail_ref[0], n_blocks, finish, 0)


def _moe(xs, item_e, item_start, item_nblk, tail_blk, w_gate_up, b_gate_up, w_down, b_down):
    n_rows, D = xs.shape
    E, _, F2 = w_gate_up.shape
    F = F2 // 2
    tf = MOE_TF
    nj = F // tf
    assert nj % 2 == 0
    n_items = item_e.shape[0]
    rows = MOE_BPI * MOE_TM
    any_spec = pl.BlockSpec(memory_space=pl.ANY)
    grid_spec = pltpu.PrefetchScalarGridSpec(
        num_scalar_prefetch=4,
        grid=(n_items,),
        in_specs=[any_spec, any_spec, any_spec,
                  pl.BlockSpec((1, 2 * nj, tf), lambda i, ie, ist, inb, tail: (ie[i], 0, 0)),
                  pl.BlockSpec((1, 1, D), lambda i, ie, ist, inb, tail: (ie[i], 0, 0))],
        out_specs=any_spec,
        scratch_shapes=[pltpu.VMEM((2, rows, D), F32), pltpu.VMEM((2, rows, D), F32),
                        pltpu.VMEM((2, D, tf), F32), pltpu.VMEM((2, D, tf), F32),
                        pltpu.VMEM((2, tf, D), F32), pltpu.VMEM((MOE_TM, D), F32),
                        pltpu.SemaphoreType.DMA((2,)), pltpu.SemaphoreType.DMA((2,)),
                        pltpu.SemaphoreType.DMA((2,)), pltpu.SemaphoreType.DMA(())],
    )
    return pl.pallas_call(
        _moe_kernel,
        grid_spec=grid_spec,
        out_shape=jax.ShapeDtypeStruct((n_rows, D), F32),
        compiler_params=pltpu.CompilerParams(dimension_semantics=("arbitrary",),
                                             vmem_limit_bytes=VMEM_LIMIT),
        name="moe",
    )(item_e, item_start, item_nblk, tail_blk, xs, w_gate_up, w_down,
      b_gate_up.reshape(E, 2 * nj, tf), b_down.reshape(E, 1, D))


def _combine_kernel(pos_ref, x1_ref, g_ref, ys_hbm, o_ref, buf, sem):
    tm = CMB_TILE
    i = pl.program_id(0)

    def start_gather(t):
        base = t * (tm * TOP_K)
        b = t % 2

        def issue(g, c):
            r0 = pl.multiple_of(g * SUBLANES, SUBLANES)
            for s in range(SUBLANES):
                for kk in range(TOP_K):
                    p = pos_ref[base + (r0 + s) * TOP_K + kk]
                    pltpu.make_async_copy(ys_hbm.at[pl.ds(p, 1), :],
                                          buf.at[b, kk, pl.ds(r0, SUBLANES), :].at[pl.ds(s, 1), :],
                                          sem.at[b]).start()
            return c
        lax.fori_loop(0, tm // SUBLANES, issue, 0)

    @pl.when(i == 0)
    def _():
        start_gather(0)

    @pl.when(i + 1 < pl.num_programs(0))
    def _():
        start_gather(i + 1)

    b = i % 2
    for kk in range(TOP_K):
        pltpu.make_async_copy(ys_hbm.at[pl.ds(0, tm), :], buf.at[b, kk], sem.at[b]).wait()
    out = x1_ref[...]
    for kk in range(TOP_K):
        out = out + g_ref[:, kk:kk + 1] * buf[b, kk]
    o_ref[...] = out


def _combine(x1, gates, ys, dest):
    N, D = x1.shape
    tm = CMB_TILE
    grid_spec = pltpu.PrefetchScalarGridSpec(
        num_scalar_prefetch=1,
        grid=(N // tm,),
        in_specs=[pl.BlockSpec((tm, D), lambda i, pos: (i, 0)),
                  pl.BlockSpec((tm, LANES), lambda i, pos: (i, 0)),
                  pl.BlockSpec(memory_space=pl.ANY)],
        out_specs=pl.BlockSpec((tm, D), lambda i, pos: (i, 0)),
        scratch_shapes=[pltpu.VMEM((2, TOP_K, tm, D), F32), pltpu.SemaphoreType.DMA((2,))],
    )
    return pl.pallas_call(
        _combine_kernel,
        grid_spec=grid_spec,
        out_shape=jax.ShapeDtypeStruct((N, D), F32),
        compiler_params=pltpu.CompilerParams(dimension_semantics=("arbitrary",),
                                             vmem_limit_bytes=VMEM_LIMIT),
        name="combine",
    )(dest, x1, gates, ys)


def _routing(top_idx, n_tok):
    tm, bpi = MOE_TM, MOE_BPI
    nk = n_tok * TOP_K
    experts = jnp.arange(N_EXPERTS, dtype=jnp.int32)
    e_flat = top_idx.reshape(-1)
    onehot = (e_flat[:, None] == experts[None, :]).astype(jnp.int32)
    csum = jnp.cumsum(onehot, axis=0)
    rank = jnp.sum(csum * onehot, axis=1) - 1
    counts = csum[-1]
    nb = (counts + tm - 1) // tm
    bend = jnp.cumsum(nb)
    bstart = bend - nb
    dest = (jnp.sum(onehot * bstart[None, :], axis=1) * tm + rank).astype(jnp.int32)
    n_blocks = -(-(nk + N_EXPERTS * (tm - 1)) // tm)
    n_rows = n_blocks * tm
    tail_blk = bend[-1:].astype(jnp.int32)
    zb_e = jnp.where(nb > 0, bend - 1, -1)
    zb_t = jnp.arange(n_blocks, dtype=jnp.int32)
    zb_all = jnp.concatenate([zb_e, jnp.where(zb_t >= bend[-1], zb_t, -1)]).astype(jnp.int32)
    order = jnp.argsort(zb_all < 0, stable=True)
    zero_blocks = zb_all[order]
    n_zero = jnp.sum(zb_all >= 0).astype(jnp.int32).reshape(1)
    n_items = n_blocks // bpi + N_EXPERTS
    items_e = (nb + bpi - 1) // bpi
    iend = jnp.cumsum(items_e)
    istart = iend - items_e
    slot = jnp.arange(n_items, dtype=jnp.int32)
    valid = slot < iend[-1]
    exp_of = jnp.minimum(jnp.searchsorted(iend, slot, side="right"), N_EXPERTS - 1).astype(jnp.int32)
    last_e = jnp.max(jnp.where(nb > 0, experts, 0))
    local = slot - istart[exp_of]
    n_it = jnp.maximum(items_e[exp_of], 1)
    base, rem = nb[exp_of] // n_it, nb[exp_of] % n_it
    first_blk = bstart[exp_of] + local * base + jnp.minimum(local, rem)
    item_e = jnp.where(valid, exp_of, last_e).astype(jnp.int32)
    item_start = jnp.where(valid, first_blk * tm, 0).astype(jnp.int32)
    item_nblk = jnp.where(valid, base + (local < rem), 0).astype(jnp.int32)
    return dest, zero_blocks, n_zero, item_e, item_start, item_nblk, tail_blk, n_rows


def _rope_lane_tile(v):
    half = MLA_ROPE // 2
    z = jnp.zeros((half,), v.dtype)
    return jnp.concatenate([v[:half], z, v[half:], z]).reshape(1, LANES)


def _rope_cols(w):
    half = MLA_ROPE // 2
    z = jnp.zeros((w.shape[0], half), w.dtype)
    return jnp.concatenate([w[:, :half], z, w[:, half:], z], axis=1)


def _layer(x, mem, positions, attn_norm, w_in, q_a_norm, w_q_b, kv_a_norm, w_kv_b,
           q_nope_norm, q_rope_norm, k_nope_norm, k_rope_norm,
           gmlp_v_norm, w_spatial, b_spatial,
           mem_norm, w_mem_kv, mem_q_norm, mem_k_norm,
           mla_out_norm, gmlp_out_norm, mem_out_norm, w_o,
           ffn_norm, w_router, b_router, w_gate_up, b_gate_up, w_down, b_down):
    B, S, D = x.shape
    N = B * S
    x2 = x.reshape(N, D)

    o1 = Q_LORA
    o2 = o1 + KV_LORA
    o3 = o2 + MLA_ROPE
    o4 = o3 + 2 * GMLP_WIDTH
    w_in_r = jnp.concatenate([w_in[:, :o2], w_in[:, o3:o4], w_in[:, o4:], _rope_cols(w_in[:, o2:o3])],
                             axis=1).astype(BF16)
    wq = w_q_b.reshape(Q_LORA, MLA_HEADS, MLA_NOPE + MLA_ROPE)
    wq_rope = jax.vmap(_rope_cols, in_axes=1, out_axes=1)(wq[:, :, MLA_NOPE:])
    wq_pad = jnp.concatenate([wq[:, :, :MLA_NOPE], wq_rope], axis=2).reshape(Q_LORA, MLA_HEADS * HEAD_PAD)
    wq_pad = wq_pad.astype(BF16)
    half = MLA_ROPE // 2
    inv_freq = ROPE_BASE ** (-jnp.arange(half, dtype=F32) / half)
    zf = jnp.zeros((half,), F32)
    invf_tile = jnp.concatenate([inv_freq, zf, inv_freq, zf]).reshape(1, LANES)
    sgn_tile = jnp.concatenate([-jnp.ones((half,), F32), zf, jnp.ones((half,), F32), zf]).reshape(1, LANES)
    bs_tile = jnp.repeat(b_spatial.T, GMLP_CH, axis=1)

    mk, mv = _mem_kv(mem, mem_norm, w_mem_kv, mem_k_norm)
    cq, ckv, kpe, ogm, cs = _inproj(
        x2, positions.reshape(N, 1), (invf_tile, sgn_tile), attn_norm, w_in_r, q_a_norm, kv_a_norm,
        _rope_lane_tile(k_rope_norm), gmlp_v_norm, w_spatial, bs_tile, mk, mv, mem_q_norm,
        gmlp_out_norm, mem_out_norm, S)
    q, k, v = _qkv(cq, ckv, kpe, cs, wq_pad, w_kv_b.astype(BF16), q_nope_norm,
                   _rope_lane_tile(q_rope_norm), k_nope_norm)
    o_mla = _attention(q, k, v, B, S).reshape(N, MLA_HEADS * MLA_V)

    wo = w_o.astype(BF16)
    wr_pad = jnp.pad(w_router, ((0, 0), (0, LANES - N_EXPERTS)))
    br_pad = jnp.pad(b_router, (0, LANES - N_EXPERTS)).reshape(1, LANES)
    x1, h2, ti, tg = _outproj(x2, o_mla, ogm, mla_out_norm, wo[:MLA_HEADS * MLA_V],
                              wo[MLA_HEADS * MLA_V:], ffn_norm, wr_pad, br_pad)

    dest, zero_blocks, n_zero, item_e, item_start, item_nblk, tail_blk, n_rows = _routing(
        ti[:, :TOP_K], N)
    xs = _dispatch(h2, dest, zero_blocks, n_zero, n_rows)
    ys = _moe(xs, item_e, item_start, item_nblk, tail_blk, w_gate_up, b_gate_up, w_down, b_down)
    out = _combine(x1, tg, ys, dest)
    return out.reshape(B, S, D)


def kernel(x, mem, positions, attn_norm, w_in, q_a_norm, w_q_b, kv_a_norm, w_kv_b, q_nope_norm, q_rope_norm, k_nope_norm, k_rope_norm, gmlp_v_norm, w_spatial, b_spatial, mem_norm, w_mem_kv, mem_q_norm, mem_k_norm, mla_out_norm, gmlp_out_norm, mem_out_norm, w_o, ffn_norm, w_router, b_router, w_gate_up, b_gate_up, w_down, b_down):
    depth = attn_norm.shape[0]
    for l in range(depth):
        x = _layer(x, mem, positions, attn_norm[l], w_in[l], q_a_norm[l], w_q_b[l], kv_a_norm[l],
                   w_kv_b[l], q_nope_norm[l], q_rope_norm[l], k_nope_norm[l], k_rope_norm[l],
                   gmlp_v_norm[l], w_spatial[l], b_spatial[l], mem_norm[l], w_mem_kv[l],
                   mem_q_norm[l], mem_k_norm[l], mla_out_norm[l], gmlp_out_norm[l],
                   mem_out_norm[l], w_o[l], ffn_norm[l], w_router[l], b_router[l], w_gate_up[l],
                   b_gate_up[l], w_down[l], b_down[l])
    return x
```

```python
import functools

import jax
import jax.numpy as jnp
import numpy as np
from jax import lax
from jax.experimental import pallas as pl
from jax.experimental.pallas import tpu as pltpu

F32 = jnp.float32
BF16 = jnp.bfloat16

EPS = 1e-6
LANES = 128
SUBLANES = 8
VMEM_LIMIT = 56 * 1024 * 1024

MLA_HEADS = 8
MLA_NOPE = 128
MLA_ROPE = 64
MLA_V = 128
Q_LORA = 512
KV_LORA = 512
GMLP_GROUPS = 4
GMLP_CH = 128
GMLP_WIDTH = GMLP_GROUPS * GMLP_CH
CHUNK = 128
MEM_HEADS = 4
MEM_HEAD_DIM = 128
MEM_WIDTH = MEM_HEADS * MEM_HEAD_DIM
N_EXPERTS = 32
TOP_K = 4
SWIGLU_ALPHA = 1.702
SWIGLU_LIMIT = 7.0
ROPE_BASE = 10000.0
HEAD_PAD = 2 * LANES

TOK_TILE = 512
ATT_TQ = 512
ATT_TK = 512
ATT_HPS = 2
MOE_TM = 256
MOE_BPI = 4
MOE_TF = 256
MOE_WSLOTS = 3
DSP_TILE = 1024
CMB_TILE = 256


def _rms(x, g, n=None):
    n = x.shape[-1] if n is None else n
    ms = jnp.sum(x * x, axis=-1, keepdims=True) * (1.0 / n)
    return x * lax.rsqrt(ms + EPS) * g


def _gelu(x):
    return 0.5 * x * (1.0 + lax.erf(x * (2.0 ** -0.5)))


def _dot(a, b):
    return jnp.dot(a, b, preferred_element_type=F32)


def _dot_nt(a, b):
    return lax.dot_general(a, b, (((1,), (1,)), ((), ())), preferred_element_type=F32)


def _const_spec(shape):
    nd = len(shape)
    return pl.BlockSpec(shape, lambda *_: (0,) * nd)


def _mem_kv_kernel(mem_ref, g_ref, w_ref, kg_ref, mk_ref, mv_ref):
    m = mem_ref[0]
    hn = _rms(m, g_ref[...]).astype(BF16)
    kv = _dot(hn, w_ref[...])
    for h in range(MEM_HEADS):
        k = kv[:, h * MEM_HEAD_DIM:(h + 1) * MEM_HEAD_DIM]
        mk_ref[0, :, h * MEM_HEAD_DIM:(h + 1) * MEM_HEAD_DIM] = _rms(k, kg_ref[...]).astype(BF16)
    mv_ref[0] = kv[:, MEM_WIDTH:].astype(BF16)


def _mem_kv(mem, mem_norm, w_mem_kv, mem_k_norm):
    B, M, D = mem.shape
    return pl.pallas_call(
        _mem_kv_kernel,
        grid=(B,),
        in_specs=[pl.BlockSpec((1, M, D), lambda b: (b, 0, 0)),
                  _const_spec((1, D)),
                  _const_spec((D, 2 * MEM_WIDTH)),
                  _const_spec((1, MEM_HEAD_DIM))],
        out_specs=[pl.BlockSpec((1, M, MEM_WIDTH), lambda b: (b, 0, 0)),
                   pl.BlockSpec((1, M, MEM_WIDTH), lambda b: (b, 0, 0))],
        out_shape=[jax.ShapeDtypeStruct((B, M, MEM_WIDTH), BF16),
                   jax.ShapeDtypeStruct((B, M, MEM_WIDTH), BF16)],
        compiler_params=pltpu.CompilerParams(dimension_semantics=("arbitrary",),
                                             vmem_limit_bytes=VMEM_LIMIT),
        name="mem_kv",
    )(mem, mem_norm.reshape(1, D), w_mem_kv.astype(BF16), mem_k_norm.reshape(1, MEM_HEAD_DIM))


_O_CQ = 0
_O_CKV = _O_CQ + Q_LORA
_O_U = _O_CKV + KV_LORA
_O_VG = _O_U + GMLP_WIDTH
_O_QM = _O_VG + GMLP_WIDTH
_O_KR = _O_QM + MEM_WIDTH
_IN_COLS_PAD = _O_KR + LANES


def _rope_tile(t, cos, sin_signed):
    return t * cos + pltpu.roll(t, LANES // 2, axis=1) * sin_signed


def _inproj_kernel(x_ref, pos_ref, invf_ref, sgn_ref, an_ref, w_ref, qan_ref, kvan_ref, krn_ref,
                   gvn_ref, ws_ref, bs_ref, mk_ref, mv_ref, mqn_ref, gon_ref, mon_ref,
                   cq_ref, ckv_ref, kpe_ref, ogm_ref, cs_ref):
    tm = x_ref.shape[0]
    h = _rms(x_ref[...], an_ref[...]).astype(BF16)
    z = _dot(h, w_ref[...])

    cq_ref[...] = _rms(z[:, _O_CQ:_O_CQ + Q_LORA], qan_ref[...]).astype(BF16)
    ckv_ref[...] = _rms(z[:, _O_CKV:_O_CKV + KV_LORA], kvan_ref[...]).astype(BF16)

    ang = pos_ref[...].astype(F32) * invf_ref[...]
    cos = jnp.cos(ang)
    sin_signed = jnp.sin(ang) * sgn_ref[...]
    cs_ref[:, :LANES] = cos
    cs_ref[:, LANES:] = sin_signed
    kr = _rms(z[:, _O_KR:_O_KR + LANES], krn_ref[...], MLA_ROPE)
    kpe_ref[...] = _rope_tile(kr, cos, sin_signed).astype(BF16)

    u = _gelu(z[:, _O_U:_O_U + GMLP_WIDTH])
    vg = _gelu(z[:, _O_VG:_O_VG + GMLP_WIDTH])
    vg = _rms(vg, gvn_ref[...]).astype(BF16)
    row = lax.broadcasted_iota(jnp.int32, (CHUNK, CHUNK), 0)
    col = lax.broadcasted_iota(jnp.int32, (CHUNK, CHUNK), 1)
    sp_cols = []
    for g in range(GMLP_GROUPS):
        wsg = jnp.where(col <= row, ws_ref[g], 0.0).astype(BF16)
        sp_rows = [_dot(wsg, vg[c * CHUNK:(c + 1) * CHUNK, g * GMLP_CH:(g + 1) * GMLP_CH])
                   for c in range(tm // CHUNK)]
        sp_cols.append(jnp.concatenate(sp_rows, axis=0))
    sp = jnp.concatenate(sp_cols, axis=1) + jnp.concatenate([bs_ref[...]] * (tm // CHUNK), axis=0)
    ogm_ref[:, :GMLP_WIDTH] = _rms(u * sp, gon_ref[...]).astype(BF16)

    o_heads = []
    for hd in range(MEM_HEADS):
        sl = slice(hd * MEM_HEAD_DIM, (hd + 1) * MEM_HEAD_DIM)
        q = z[:, _O_QM + hd * MEM_HEAD_DIM:_O_QM + (hd + 1) * MEM_HEAD_DIM]
        qn = (_rms(q, mqn_ref[...]) * (MEM_HEAD_DIM ** -0.5)).astype(BF16)
        s = _dot_nt(qn, mk_ref[0, :, sl])
        s = s - jnp.max(s, axis=-1, keepdims=True)
        p = jnp.exp(s)
        p = p / jnp.sum(p, axis=-1, keepdims=True)
        o_heads.append(_dot(p.astype(BF16), mv_ref[0, :, sl]))
    o_mem = jnp.concatenate(o_heads, axis=1)
    ogm_ref[:, GMLP_WIDTH:] = _rms(o_mem, mon_ref[...]).astype(BF16)


def _inproj(x2, pos, tables, attn_norm, w_in_r, q_a_norm, kv_a_norm, krn_tile, gmlp_v_norm,
            w_spatial, bs_tile, mk, mv, mem_q_norm, gmlp_out_norm, mem_out_norm, seq):
    N, D = x2.shape
    tm = TOK_TILE
    tiles_per_seq = seq // tm
    invf_tile, sgn_tile = tables
    M = mk.shape[1]
    row = lambda w: pl.BlockSpec((tm, w), lambda i: (i, 0))
    batch_blk = pl.BlockSpec((1, M, MEM_WIDTH), lambda i: (i // tiles_per_seq, 0, 0))
    return pl.pallas_call(
        _inproj_kernel,
        grid=(N // tm,),
        in_specs=[row(D), row(1), _const_spec((1, LANES)), _const_spec((1, LANES)),
                  _const_spec((1, D)), _const_spec((D, _IN_COLS_PAD)),
                  _const_spec((1, Q_LORA)), _const_spec((1, KV_LORA)), _const_spec((1, LANES)),
                  _const_spec((1, GMLP_WIDTH)), _const_spec((GMLP_GROUPS, CHUNK, CHUNK)),
                  _const_spec((CHUNK, GMLP_WIDTH)), batch_blk, batch_blk,
                  _const_spec((1, MEM_HEAD_DIM)), _const_spec((1, GMLP_WIDTH)),
                  _const_spec((1, MEM_WIDTH))],
        out_specs=[row(Q_LORA), row(KV_LORA), row(LANES), row(GMLP_WIDTH + MEM_WIDTH),
                   row(2 * LANES)],
        out_shape=[jax.ShapeDtypeStruct((N, Q_LORA), BF16),
                   jax.ShapeDtypeStruct((N, KV_LORA), BF16),
                   jax.ShapeDtypeStruct((N, LANES), BF16),
                   jax.ShapeDtypeStruct((N, GMLP_WIDTH + MEM_WIDTH), BF16),
                   jax.ShapeDtypeStruct((N, 2 * LANES), F32)],
        compiler_params=pltpu.CompilerParams(dimension_semantics=("arbitrary",),
                                             vmem_limit_bytes=VMEM_LIMIT),
        name="inproj",
    )(x2, pos, invf_tile, sgn_tile, attn_norm.reshape(1, D), w_in_r,
      q_a_norm.reshape(1, -1), kv_a_norm.reshape(1, -1), krn_tile, gmlp_v_norm.reshape(1, -1),
      w_spatial, bs_tile, mk, mv, mem_q_norm.reshape(1, -1), gmlp_out_norm.reshape(1, -1),
      mem_out_norm.reshape(1, -1))


def _qkv_kernel(cq_ref, ckv_ref, kpe_ref, cs_ref, wq_ref, wkv_ref, qnn_ref, qrn_ref, knn_ref,
                q_ref, k_ref, v_ref):
    cos = cs_ref[:, :LANES]
    sin_signed = cs_ref[:, LANES:]
    scale = (MLA_NOPE + MLA_ROPE) ** -0.5 * np.log2(np.e)
    qr = _dot(cq_ref[...], wq_ref[...])
    kvr = _dot(ckv_ref[...], wkv_ref[...])
    kpe = kpe_ref[...]
    for h in range(MLA_HEADS):
        o = h * HEAD_PAD
        qn = _rms(qr[:, o:o + MLA_NOPE], qnn_ref[...]) * scale
        qt = _rms(qr[:, o + MLA_NOPE:o + HEAD_PAD], qrn_ref[...], MLA_ROPE)
        qt = _rope_tile(qt, cos, sin_signed) * scale
        q_ref[:, o:o + MLA_NOPE] = qn.astype(BF16)
        q_ref[:, o + MLA_NOPE:o + HEAD_PAD] = qt.astype(BF16)
        ko = h * (MLA_NOPE + MLA_V)
        k_ref[:, o:o + MLA_NOPE] = _rms(kvr[:, ko:ko + MLA_NOPE], knn_ref[...]).astype(BF16)
        k_ref[:, o + MLA_NOPE:o + HEAD_PAD] = kpe
        v_ref[:, h * MLA_V:(h + 1) * MLA_V] = kvr[:, ko + MLA_NOPE:ko + MLA_NOPE + MLA_V].astype(BF16)


def _qkv(cq, ckv, kpe, cs, wq_pad, wkv, q_nope_norm, qrn_tile, k_nope_norm):
    N = cq.shape[0]
    tm = TOK_TILE
    row = lambda w: pl.BlockSpec((tm, w), lambda i: (i, 0))
    return pl.pallas_call(
        _qkv_kernel,
        grid=(N // tm,),
        in_specs=[row(Q_LORA), row(KV_LORA), row(LANES), row(2 * LANES),
                  _const_spec(wq_pad.shape), _const_spec(wkv.shape),
                  _const_spec((1, MLA_NOPE)), _const_spec((1, LANES)), _const_spec((1, MLA_NOPE))],
        out_specs=[row(MLA_HEADS * HEAD_PAD), row(MLA_HEADS * HEAD_PAD), row(MLA_HEADS * MLA_V)],
        out_shape=[jax.ShapeDtypeStruct((N, MLA_HEADS * HEAD_PAD), BF16),
                   jax.ShapeDtypeStruct((N, MLA_HEADS * HEAD_PAD), BF16),
                   jax.ShapeDtypeStruct((N, MLA_HEADS * MLA_V), BF16)],
        compiler_params=pltpu.CompilerParams(dimension_semantics=("arbitrary",),
                                             vmem_limit_bytes=VMEM_LIMIT),
        name="qkv",
    )(cq, ckv, kpe, cs, wq_pad, wkv, q_nope_norm.reshape(1, -1), qrn_tile,
      k_nope_norm.reshape(1, -1))


def _attn_kernel(q_ref, k_ref, v_ref, o_ref):
    tq, tk = ATT_TQ, ATT_TK
    qi = pl.program_id(2)
    qs = [q_ref[0, :, h * HEAD_PAD:(h + 1) * HEAD_PAD] for h in range(ATT_HPS)]

    def head_step(h, kb, carry, masked):
        m, l, acc = carry
        k0 = pl.multiple_of(kb * tk, tk)
        s = _dot_nt(qs[h], k_ref[0, pl.ds(k0, tk), h * HEAD_PAD:(h + 1) * HEAD_PAD])
        if masked:
            qpos = qi * tq + lax.broadcasted_iota(jnp.int32, (tq, tk), 0)
            kpos = kb * tk + lax.broadcasted_iota(jnp.int32, (tq, tk), 1)
            s = jnp.where(kpos <= qpos, s, -jnp.inf)
        m_new = jnp.maximum(m, jnp.max(s, axis=-1, keepdims=True))
        alpha = jnp.exp2(m - m_new)
        p = jnp.exp2(s - m_new)
        l = alpha * l + jnp.sum(p, axis=-1, keepdims=True)
        pv = _dot(p.astype(BF16), v_ref[0, pl.ds(k0, tk), h * MLA_V:(h + 1) * MLA_V])
        return m_new, l, alpha * acc + pv

    def step(kb, carries, masked):
        return tuple(head_step(h, kb, carries[h], masked) for h in range(ATT_HPS))

    init = tuple((jnp.full((tq, 1), -jnp.inf, F32), jnp.zeros((tq, 1), F32),
                  jnp.zeros((tq, MLA_V), F32)) for _ in range(ATT_HPS))
    n_full = (qi * tq) // tk
    carries = lax.fori_loop(0, n_full, lambda kb, c: step(kb, c, False), init)
    carries = step(n_full, carries, True)
    for h in range(ATT_HPS):
        _, l, acc = carries[h]
        o_ref[0, :, h * MLA_V:(h + 1) * MLA_V] = (acc / l).astype(BF16)


def _attention(q, k, v, batch, seq):
    q3 = q.reshape(batch, seq, MLA_HEADS * HEAD_PAD)
    k3 = k.reshape(batch, seq, MLA_HEADS * HEAD_PAD)
    v3 = v.reshape(batch, seq, MLA_HEADS * MLA_V)
    hp, hv = ATT_HPS * HEAD_PAD, ATT_HPS * MLA_V
    assert ATT_TQ == ATT_TK and seq % ATT_TQ == 0
    return pl.pallas_call(
        _attn_kernel,
        grid=(batch, MLA_HEADS // ATT_HPS, seq // ATT_TQ),
        in_specs=[pl.BlockSpec((1, ATT_TQ, hp), lambda b, h, i: (b, i, h)),
                  pl.BlockSpec((1, seq, hp), lambda b, h, i: (b, 0, h)),
                  pl.BlockSpec((1, seq, hv), lambda b, h, i: (b, 0, h))],
        out_specs=pl.BlockSpec((1, ATT_TQ, hv), lambda b, h, i: (b, i, h)),
        out_shape=jax.ShapeDtypeStruct((batch, seq, MLA_HEADS * MLA_V), BF16),
        compiler_params=pltpu.CompilerParams(
            dimension_semantics=("arbitrary", "arbitrary", "arbitrary"),
            vmem_limit_bytes=VMEM_LIMIT),
        name="attn",
    )(q3, k3, v3)


def _outproj_kernel(x_ref, oa_ref, ogm_ref, aon_ref, woa_ref, wob_ref, fn_ref, wr_ref, br_ref,
                    x1_ref, h2_ref, ti_ref, tg_ref):
    tm = x_ref.shape[0]
    oa = _rms(oa_ref[...].astype(F32), aon_ref[...]).astype(BF16)
    x1 = x_ref[...] + _dot(oa, woa_ref[...]) + _dot(ogm_ref[...], wob_ref[...])
    x1_ref[...] = x1
    h2 = _rms(x1, fn_ref[...])
    h2_ref[...] = h2
    h_hi = h2.astype(BF16)
    h_lo = (h2 - h_hi.astype(F32)).astype(BF16)
    wr = wr_ref[...]
    w_hi = wr.astype(BF16)
    w_lo = (wr - w_hi.astype(F32)).astype(BF16)
    hh = _dot(h_hi, jnp.concatenate([w_hi, w_lo], axis=1))
    logits = hh[:, :LANES] + (_dot(h_lo, w_hi) + hh[:, LANES:]) + br_ref[...]
    lane = lax.broadcasted_iota(jnp.int32, (tm, LANES), 1)
    lg = jnp.where(lane < N_EXPERTS, logits, -jnp.inf)
    vals, idxs = [], []
    for _ in range(TOP_K):
        m = jnp.max(lg, axis=-1, keepdims=True)
        am = jnp.min(jnp.where(lg == m, lane, LANES), axis=-1, keepdims=True)
        vals.append(m)
        idxs.append(am)
        lg = jnp.where(lane == am, -jnp.inf, lg)
    es = [jnp.exp(v - vals[0]) for v in vals]
    denom = es[0] + es[1] + es[2] + es[3]
    ti = jnp.zeros((tm, LANES), jnp.int32)
    tg = jnp.zeros((tm, LANES), F32)
    for kk in range(TOP_K):
        ti = jnp.where(lane == kk, idxs[kk], ti)
        tg = jnp.where(lane == kk, es[kk] / denom, tg)
    ti_ref[...] = ti
    tg_ref[...] = tg


def _outproj(x2, o_mla, ogm, mla_out_norm, wo_a, wo_b, ffn_norm, wr_pad, br_pad):
    N, D = x2.shape
    tm = TOK_TILE
    row = lambda w: pl.BlockSpec((tm, w), lambda i: (i, 0))
    wa = o_mla.shape[1]
    wb = ogm.shape[1]
    return pl.pallas_call(
        _outproj_kernel,
        grid=(N // tm,),
        in_specs=[row(D), row(wa), row(wb), _const_spec((1, wa)), _const_spec((wa, D)),
                  _const_spec((wb, D)), _const_spec((1, D)), _const_spec((D, LANES)),
                  _const_spec((1, LANES))],
        out_specs=[row(D), row(D), row(LANES), row(LANES)],
        out_shape=[jax.ShapeDtypeStruct((N, D), F32), jax.ShapeDtypeStruct((N, D), F32),
                   jax.ShapeDtypeStruct((N, LANES), jnp.int32),
                   jax.ShapeDtypeStruct((N, LANES), F32)],
        compiler_params=pltpu.CompilerParams(dimension_semantics=("arbitrary",),
                                             vmem_limit_bytes=VMEM_LIMIT),
        name="outproj",
    )(x2, o_mla, ogm, mla_out_norm.reshape(1, -1), wo_a, wo_b, ffn_norm.reshape(1, -1), wr_pad,
      br_pad)


def _dispatch_kernel(dest_ref, zb_ref, nzb_ref, h2_ref, xs_hbm, zblk, sem, sem_z):
    tm = h2_ref.shape[0]
    bm = MOE_TM
    i = pl.program_id(0)
    base = i * (tm * TOP_K)

    @pl.when(i == 0)
    def _zero_blocks():
        zblk[...] = jnp.zeros(zblk.shape, zblk.dtype)

        def zcopy(n):
            d0 = pl.multiple_of(zb_ref[n] * bm, bm)
            return pltpu.make_async_copy(zblk, xs_hbm.at[pl.ds(d0, bm), :], sem_z)

        def issue(n, c):
            zcopy(n).start()
            return c
        lax.fori_loop(0, nzb_ref[0], issue, 0)

        def finish(n, c):
            zcopy(n).wait()
            return c
        lax.fori_loop(0, nzb_ref[0], finish, 0)

    def issue(g, c):
        r0 = pl.multiple_of(g * SUBLANES, SUBLANES)
        rows = h2_ref.at[pl.ds(r0, SUBLANES), :]
        for s in range(SUBLANES):
            for kk in range(TOP_K):
                d = dest_ref[base + (r0 + s) * TOP_K + kk]
                pltpu.make_async_copy(rows.at[pl.ds(s, 1), :], xs_hbm.at[pl.ds(d, 1), :], sem).start()
        return c
    lax.fori_loop(0, tm // SUBLANES, issue, 0)
    for kk in range(TOP_K):
        pltpu.make_async_copy(h2_ref, xs_hbm.at[pl.ds(0, tm), :], sem).wait()


def _dispatch(h2, dest, zero_blocks, n_zero, n_rows):
    N, D = h2.shape
    tm = DSP_TILE
    grid_spec = pltpu.PrefetchScalarGridSpec(
        num_scalar_prefetch=3,
        grid=(N // tm,),
        in_specs=[pl.BlockSpec((tm, D), lambda i, *_: (i, 0))],
        out_specs=pl.BlockSpec(memory_space=pl.ANY),
        scratch_shapes=[pltpu.VMEM((MOE_TM, D), F32), pltpu.SemaphoreType.DMA(()),
                        pltpu.SemaphoreType.DMA(())],
    )
    return pl.pallas_call(
        _dispatch_kernel,
        grid_spec=grid_spec,
        out_shape=jax.ShapeDtypeStruct((n_rows, D), F32),
        compiler_params=pltpu.CompilerParams(dimension_semantics=("arbitrary",),
                                             vmem_limit_bytes=VMEM_LIMIT),
        name="dispatch",
    )(dest, zero_blocks, n_zero, h2)


def _moe_kernel(ie_ref, ist_ref, inb_ref, tail_ref,
                xs_hbm, wgu_hbm, wd_hbm, bgu_ref, bd_ref,
                ys_hbm,
                xbuf, acc, wg_buf, wu_buf, wd_buf, zblk, sem_x, sem_o, sem_w, sem_z):
    tm = MOE_TM
    i = pl.program_id(0)
    n_items = pl.num_programs(0)
    nblk = inb_ref[i]
    slot = i % 2
    n_blocks = ys_hbm.shape[0] // tm
    tf = wg_buf.shape[2]
    nj = wd_hbm.shape[1] // tf

    def x_copy(it, m):
        s0 = pl.multiple_of(ist_ref[it] + m * tm, tm)
        r0 = pl.multiple_of(m * tm, tm)
        return pltpu.make_async_copy(xs_hbm.at[pl.ds(s0, tm), :],
                                     xbuf.at[it % 2, pl.ds(r0, tm), :], sem_x.at[it % 2])

    def y_copy(it, m):
        d0 = pl.multiple_of(ist_ref[it] + m * tm, tm)
        r0 = pl.multiple_of(m * tm, tm)
        return pltpu.make_async_copy(acc.at[it % 2, pl.ds(r0, tm), :],
                                     ys_hbm.at[pl.ds(d0, tm), :], sem_o.at[it % 2])

    def for_blocks(it, fn):
        def body(m, c):
            fn(it, m)
            return c
        lax.fori_loop(0, inb_ref[it], body, 0)

    start_x = lambda it: for_blocks(it, lambda a, m: x_copy(a, m).start())
    wait_x = lambda it: for_blocks(it, lambda a, m: x_copy(a, m).wait())
    start_y = lambda it: for_blocks(it, lambda a, m: y_copy(a, m).start())
    wait_y = lambda it: for_blocks(it, lambda a, m: y_copy(a, m).wait())

    def zero_copy(b):
        d0 = pl.multiple_of(b * tm, tm)
        return pltpu.make_async_copy(zblk, ys_hbm.at[pl.ds(d0, tm), :], sem_z)

    n_ws = wg_buf.shape[0]

    def w_slot(it, j):
        return (it * nj + j) % n_ws

    def w_copies(it, j):
        e = ie_ref[it]
        ws = w_slot(it, j)
        c0 = pl.multiple_of(j * tf, tf)
        c1 = pl.multiple_of(nj * tf + j * tf, tf)
        return (pltpu.make_async_copy(wgu_hbm.at[e, :, pl.ds(c0, tf)], wg_buf.at[ws], sem_w.at[ws]),
                pltpu.make_async_copy(wgu_hbm.at[e, :, pl.ds(c1, tf)], wu_buf.at[ws], sem_w.at[ws]),
                pltpu.make_async_copy(wd_hbm.at[e, pl.ds(c0, tf), :], wd_buf.at[ws], sem_w.at[ws]))

    def start_w(it, j):
        for cp in w_copies(it, j):
            cp.start()

    def wait_w(it, j):
        for cp in w_copies(it, j):
            cp.wait()

    @pl.when(i == 0)
    def _first_step():
        zblk[...] = jnp.zeros(zblk.shape, zblk.dtype)

        def issue(b, c):
            zero_copy(b).start()
            return c
        lax.fori_loop(tail_ref[0], n_blocks, issue, 0)
        start_x(0)

        @pl.when(nblk > 0)
        def _():
            for a in range(n_ws - 1):
                start_w(0, a)

    wait_x(i)

    @pl.when(i + 1 < n_items)
    def _():
        start_x(i + 1)

    @pl.when(i >= 2)
    def _():
        wait_y(i - 2)

    def init(m, c):
        r0 = pl.multiple_of(m * tm, tm)
        acc[slot, pl.ds(r0, tm), :] = jnp.broadcast_to(bd_ref[0], (tm, acc.shape[2]))
        return c
    lax.fori_loop(0, nblk, init, 0)

    def ffn_rows(j, n):
        ws = w_slot(i, j)
        x = xbuf[slot, pl.ds(0, n), :].astype(BF16)
        g = jnp.minimum(_dot(x, wg_buf[ws].astype(BF16)) + bgu_ref[0, pl.ds(j, 1), :], SWIGLU_LIMIT)
        u = jnp.clip(_dot(x, wu_buf[ws].astype(BF16)) + bgu_ref[0, pl.ds(nj + j, 1), :],
                     -SWIGLU_LIMIT, SWIGLU_LIMIT)
        a = (u + 1.0) * (g * jax.nn.sigmoid(SWIGLU_ALPHA * g))
        acc[slot, pl.ds(0, n), :] += _dot(a.astype(BF16), wd_buf[ws].astype(BF16))

    def chunk(j, c):
        wait_w(i, j)
        ja = j + (n_ws - 1)

        @pl.when(ja < nj)
        def _():
            start_w(i, ja)

        @pl.when(jnp.logical_and(ja >= nj, i + 1 < n_items))
        def _():
            @pl.when(inb_ref[i + 1] > 0)
            def _():
                start_w(i + 1, ja - nj)

        for nb in range(1, MOE_BPI + 1):
            pl.when(nblk == nb)(functools.partial(ffn_rows, j, nb * tm))
        return c

    @pl.when(nblk > 0)
    def _chunks():
        lax.fori_loop(0, nj, chunk, 0)

    start_y(i)

    @pl.when(i == n_items - 1)
    def _last_step():
        @pl.when(i >= 1)
        def _():
            wait_y(i - 1)
        wait_y(i)

        def finish(b, c):
            zero_copy(b).wait()
            return c
        lax.fori_loop(tail_ref[0], n_blocks, finish, 0)


def _moe(xs, item_e, item_start, item_nblk, tail_blk, w_gate_up, b_gate_up, w_down, b_down):
    n_rows, D = xs.shape
    E, _, F2 = w_gate_up.shape
    F = F2 // 2
    tf = MOE_TF
    nj = F // tf
    nw = MOE_WSLOTS
    assert 2 <= nw <= nj + 1
    n_items = item_e.shape[0]
    rows = MOE_BPI * MOE_TM
    any_spec = pl.BlockSpec(memory_space=pl.ANY)
    grid_spec = pltpu.PrefetchScalarGridSpec(
        num_scalar_prefetch=4,
        grid=(n_items,),
        in_specs=[any_spec, any_spec, any_spec,
                  pl.BlockSpec((1, 2 * nj, tf), lambda i, ie, ist, inb, tail: (ie[i], 0, 0)),
                  pl.BlockSpec((1, 1, D), lambda i, ie, ist, inb, tail: (ie[i], 0, 0))],
        out_specs=any_spec,
        scratch_shapes=[pltpu.VMEM((2, rows, D), F32), pltpu.VMEM((2, rows, D), F32),
                        pltpu.VMEM((nw, D, tf), F32), pltpu.VMEM((nw, D, tf), F32),
                        pltpu.VMEM((nw, tf, D), F32), pltpu.VMEM((MOE_TM, D), F32),
                        pltpu.SemaphoreType.DMA((2,)), pltpu.SemaphoreType.DMA((2,)),
                        pltpu.SemaphoreType.DMA((nw,)), pltpu.SemaphoreType.DMA(())],
    )
    return pl.pallas_call(
        _moe_kernel,
        grid_spec=grid_spec,
        out_shape=jax.ShapeDtypeStruct((n_rows, D), F32),
        compiler_params=pltpu.CompilerParams(dimension_semantics=("arbitrary",),
                                             vmem_limit_bytes=VMEM_LIMIT),
        name="moe",
    )(item_e, item_start, item_nblk, tail_blk, xs, w_gate_up, w_down,
      b_gate_up.reshape(E, 2 * nj, tf), b_down.reshape(E, 1, D))


def _combine_kernel(pos_ref, x1_ref, g_ref, ys_hbm, o_ref, buf, sem):
    tm = CMB_TILE
    i = pl.program_id(0)

    def start_gather(t):
        base = t * (tm * TOP_K)
        b = t % 2

        def issue(g, c):
            r0 = pl.multiple_of(g * SUBLANES, SUBLANES)
            for s in range(SUBLANES):
                for kk in range(TOP_K):
                    p = pos_ref[base + (r0 + s) * TOP_K + kk]
                    pltpu.make_async_copy(ys_hbm.at[pl.ds(p, 1), :],
                                          buf.at[b, kk, pl.ds(r0, SUBLANES), :].at[pl.ds(s, 1), :],
                                          sem.at[b]).start()
            return c
        lax.fori_loop(0, tm // SUBLANES, issue, 0)

    @pl.when(i == 0)
    def _():
        start_gather(0)

    @pl.when(i + 1 < pl.num_programs(0))
    def _():
        start_gather(i + 1)

    b = i % 2
    for kk in range(TOP_K):
        pltpu.make_async_copy(ys_hbm.at[pl.ds(0, tm), :], buf.at[b, kk], sem.at[b]).wait()
    out = x1_ref[...]
    for kk in range(TOP_K):
        out = out + g_ref[:, kk:kk + 1] * buf[b, kk]
    o_ref[...] = out


def _combine(x1, gates, ys, dest):
    N, D = x1.shape
    tm = CMB_TILE
    grid_spec = pltpu.PrefetchScalarGridSpec(
        num_scalar_prefetch=1,
        grid=(N // tm,),
        in_specs=[pl.BlockSpec((tm, D), lambda i, pos: (i, 0)),
                  pl.BlockSpec((tm, LANES), lambda i, pos: (i, 0)),
                  pl.BlockSpec(memory_space=pl.ANY)],
        out_specs=pl.BlockSpec((tm, D), lambda i, pos: (i, 0)),
        scratch_shapes=[pltpu.VMEM((2, TOP_K, tm, D), F32), pltpu.SemaphoreType.DMA((2,))],
    )
    return pl.pallas_call(
        _combine_kernel,
        grid_spec=grid_spec,
        out_shape=jax.ShapeDtypeStruct((N, D), F32),
        compiler_params=pltpu.CompilerParams(dimension_semantics=("arbitrary",),
                                             vmem_limit_bytes=VMEM_LIMIT),
        name="combine",
    )(dest, x1, gates, ys)


def _routing(top_idx, n_tok):
    tm, bpi = MOE_TM, MOE_BPI
    nk = n_tok * TOP_K
    experts = jnp.arange(N_EXPERTS, dtype=jnp.int32)
    e_flat = top_idx.reshape(-1)
    onehot = (e_flat[:, None] == experts[None, :]).astype(jnp.int32)
    csum = jnp.cumsum(onehot, axis=0)
    rank = jnp.sum(csum * onehot, axis=1) - 1
    counts = csum[-1]
    nb = (counts + tm - 1) // tm
    bend = jnp.cumsum(nb)
    bstart = bend - nb
    dest = (jnp.sum(onehot * bstart[None, :], axis=1) * tm + rank).astype(jnp.int32)
    n_blocks = -(-(nk + N_EXPERTS * (tm - 1)) // tm)
    n_rows = n_blocks * tm
    tail_blk = bend[-1:].astype(jnp.int32)
    zb_e = jnp.where(nb > 0, bend - 1, -1)
    zb_t = jnp.arange(n_blocks, dtype=jnp.int32)
    zb_all = jnp.concatenate([zb_e, jnp.where(zb_t >= bend[-1], zb_t, -1)]).astype(jnp.int32)
    order = jnp.argsort(zb_all < 0, stable=True)
    zero_blocks = zb_all[order]
    n_zero = jnp.sum(zb_all >= 0).astype(jnp.int32).reshape(1)
    n_items = n_blocks // bpi + N_EXPERTS
    items_e = (nb + bpi - 1) // bpi
    iend = jnp.cumsum(items_e)
    istart = iend - items_e
    slot = jnp.arange(n_items, dtype=jnp.int32)
    valid = slot < iend[-1]
    exp_of = jnp.minimum(jnp.searchsorted(iend, slot, side="right"), N_EXPERTS - 1).astype(jnp.int32)
    last_e = jnp.max(jnp.where(nb > 0, experts, 0))
    local = slot - istart[exp_of]
    n_it = jnp.maximum(items_e[exp_of], 1)
    base, rem = nb[exp_of] // n_it, nb[exp_of] % n_it
    first_blk = bstart[exp_of] + local * base + jnp.minimum(local, rem)
    item_e = jnp.where(valid, exp_of, last_e).astype(jnp.int32)
    item_start = jnp.where(valid, first_blk * tm, 0).astype(jnp.int32)
    item_nblk = jnp.where(valid, base + (local < rem), 0).astype(jnp.int32)
    return dest, zero_blocks, n_zero, item_e, item_start, item_nblk, tail_blk, n_rows


def _rope_lane_tile(v):
    half = MLA_ROPE // 2
    z = jnp.zeros((half,), v.dtype)
    return jnp.concatenate([v[:half], z, v[half:], z]).reshape(1, LANES)


def _rope_cols(w):
    half = MLA_ROPE // 2
    z = jnp.zeros((w.shape[0], half), w.dtype)
    return jnp.concatenate([w[:, :half], z, w[:, half:], z], axis=1)


def _layer(x, mem, positions, attn_norm, w_in, q_a_norm, w_q_b, kv_a_norm, w_kv_b,
           q_nope_norm, q_rope_norm, k_nope_norm, k_rope_norm,
           gmlp_v_norm, w_spatial, b_spatial,
           mem_norm, w_mem_kv, mem_q_norm, mem_k_norm,
           mla_out_norm, gmlp_out_norm, mem_out_norm, w_o,
           ffn_norm, w_router, b_router, w_gate_up, b_gate_up, w_down, b_down):
    B, S, D = x.shape
    N = B * S
    x2 = x.reshape(N, D)

    o1 = Q_LORA
    o2 = o1 + KV_LORA
    o3 = o2 + MLA_ROPE
    o4 = o3 + 2 * GMLP_WIDTH
    w_in_r = jnp.concatenate([w_in[:, :o2], w_in[:, o3:o4], w_in[:, o4:], _rope_cols(w_in[:, o2:o3])],
                             axis=1).astype(BF16)
    wq = w_q_b.reshape(Q_LORA, MLA_HEADS, MLA_NOPE + MLA_ROPE)
    wq_rope = jax.vmap(_rope_cols, in_axes=1, out_axes=1)(wq[:, :, MLA_NOPE:])
    wq_pad = jnp.concatenate([wq[:, :, :MLA_NOPE], wq_rope], axis=2).reshape(Q_LORA, MLA_HEADS * HEAD_PAD)
    wq_pad = wq_pad.astype(BF16)
    half = MLA_ROPE // 2
    inv_freq = ROPE_BASE ** (-jnp.arange(half, dtype=F32) / half)
    zf = jnp.zeros((half,), F32)
    invf_tile = jnp.concatenate([inv_freq, zf, inv_freq, zf]).reshape(1, LANES)
    sgn_tile = jnp.concatenate([-jnp.ones((half,), F32), zf, jnp.ones((half,), F32), zf]).reshape(1, LANES)
    bs_tile = jnp.repeat(b_spatial.T, GMLP_CH, axis=1)

    mk, mv = _mem_kv(mem, mem_norm, w_mem_kv, mem_k_norm)
    cq, ckv, kpe, ogm, cs = _inproj(
        x2, positions.reshape(N, 1), (invf_tile, sgn_tile), attn_norm, w_in_r, q_a_norm, kv_a_norm,
        _rope_lane_tile(k_rope_norm), gmlp_v_norm, w_spatial, bs_tile, mk, mv, mem_q_norm,
        gmlp_out_norm, mem_out_norm, S)
    q, k, v = _qkv(cq, ckv, kpe, cs, wq_pad, w_kv_b.astype(BF16), q_nope_norm,
                   _rope_lane_tile(q_rope_norm), k_nope_norm)
    o_mla = _attention(q, k, v, B, S).reshape(N, MLA_HEADS * MLA_V)

    wo = w_o.astype(BF16)
    wr_pad = jnp.pad(w_router, ((0, 0), (0, LANES - N_EXPERTS)))
    br_pad = jnp.pad(b_router, (0, LANES - N_EXPERTS)).reshape(1, LANES)
    x1, h2, ti, tg = _outproj(x2, o_mla, ogm, mla_out_norm, wo[:MLA_HEADS * MLA_V],
                              wo[MLA_HEADS * MLA_V:], ffn_norm, wr_pad, br_pad)

    dest, zero_blocks, n_zero, item_e, item_start, item_nblk, tail_blk, n_rows = _routing(
        ti[:, :TOP_K], N)
    xs = _dispatch(h2, dest, zero_blocks, n_zero, n_rows)
    ys = _moe(xs, item_e, item_start, item_nblk, tail_blk, w_gate_up, b_gate_up, w_down, b_down)
    out = _combine(x1, tg, ys, dest)
    return out.reshape(B, S, D)


def kernel(x, mem, positions, attn_norm, w_in, q_a_norm, w_q_b, kv_a_norm, w_kv_b, q_nope_norm, q_rope_norm, k_nope_norm, k_rope_norm, gmlp_v_norm, w_spatial, b_spatial, mem_norm, w_mem_kv, mem_q_norm, mem_k_norm, mla_out_norm, gmlp_out_norm, mem_out_norm, w_o, ffn_norm, w_router, b_router, w_gate_up, b_gate_up, w_down, b_down):
    depth = attn_norm.shape[0]
    for l in range(depth):
        x = _layer(x, mem, positions, attn_norm[l], w_in[l], q_a_norm[l], w_q_b[l], kv_a_norm[l],
                   w_kv_b[l], q_nope_norm[l], q_rope_norm[l], k_nope_norm[l], k_rope_norm[l],
                   gmlp_v_norm[l], w_spatial[l], b_spatial[l], mem_norm[l], w_mem_kv[l],
                   mem_q_norm[l], mem_k_norm[l], mla_out_norm[l], gmlp_out_norm[l],
                   mem_out_norm[l], w_o[l], ffn_norm[l], w_router[l], b_router[l], w_gate_up[l],
                   b_gate_up[l], w_down[l], b_down[l])
    return x
```

```python
import functools

import jax
import jax.numpy as jnp
import numpy as np
from jax import lax
from jax.experimental import pallas as pl
from jax.experimental.pallas import tpu as pltpu

F32 = jnp.float32
BF16 = jnp.bfloat16

EPS = 1e-6
LANES = 128
SUBLANES = 8
VMEM_LIMIT = 56 * 1024 * 1024

MLA_HEADS = 8
MLA_NOPE = 128
MLA_ROPE = 64
MLA_V = 128
Q_LORA = 512
KV_LORA = 512
GMLP_GROUPS = 4
GMLP_CH = 128
GMLP_WIDTH = GMLP_GROUPS * GMLP_CH
CHUNK = 128
MEM_HEADS = 4
MEM_HEAD_DIM = 128
MEM_WIDTH = MEM_HEADS * MEM_HEAD_DIM
N_EXPERTS = 32
TOP_K = 4
SWIGLU_ALPHA = 1.702
SWIGLU_LIMIT = 7.0
ROPE_BASE = 10000.0
HEAD_PAD = 2 * LANES

TOK_TILE = 512
ATT_TQ = 512
ATT_TK = 512
ATT_HPS = 2
MOE_TM = 256
MOE_BPI = 5
MOE_TF = 256
MOE_WSLOTS = 3
DSP_TILE = 1024
CMB_TILE = 256


def _rms(x, g, n=None):
    n = x.shape[-1] if n is None else n
    ms = jnp.sum(x * x, axis=-1, keepdims=True) * (1.0 / n)
    return x * lax.rsqrt(ms + EPS) * g


def _gelu(x):
    return 0.5 * x * (1.0 + lax.erf(x * (2.0 ** -0.5)))


def _dot(a, b):
    return jnp.dot(a, b, preferred_element_type=F32)


def _dot_nt(a, b):
    return lax.dot_general(a, b, (((1,), (1,)), ((), ())), preferred_element_type=F32)


def _const_spec(shape):
    nd = len(shape)
    return pl.BlockSpec(shape, lambda *_: (0,) * nd)


def _mem_kv_kernel(mem_ref, g_ref, w_ref, kg_ref, mk_ref, mv_ref):
    m = mem_ref[0]
    hn = _rms(m, g_ref[...]).astype(BF16)
    kv = _dot(hn, w_ref[...])
    for h in range(MEM_HEADS):
        k = kv[:, h * MEM_HEAD_DIM:(h + 1) * MEM_HEAD_DIM]
        mk_ref[0, :, h * MEM_HEAD_DIM:(h + 1) * MEM_HEAD_DIM] = _rms(k, kg_ref[...]).astype(BF16)
    mv_ref[0] = kv[:, MEM_WIDTH:].astype(BF16)


def _mem_kv(mem, mem_norm, w_mem_kv, mem_k_norm):
    B, M, D = mem.shape
    return pl.pallas_call(
        _mem_kv_kernel,
        grid=(B,),
        in_specs=[pl.BlockSpec((1, M, D), lambda b: (b, 0, 0)),
                  _const_spec((1, D)),
                  _const_spec((D, 2 * MEM_WIDTH)),
                  _const_spec((1, MEM_HEAD_DIM))],
        out_specs=[pl.BlockSpec((1, M, MEM_WIDTH), lambda b: (b, 0, 0)),
                   pl.BlockSpec((1, M, MEM_WIDTH), lambda b: (b, 0, 0))],
        out_shape=[jax.ShapeDtypeStruct((B, M, MEM_WIDTH), BF16),
                   jax.ShapeDtypeStruct((B, M, MEM_WIDTH), BF16)],
        compiler_params=pltpu.CompilerParams(dimension_semantics=("arbitrary",),
                                             vmem_limit_bytes=VMEM_LIMIT),
        name="mem_kv",
    )(mem, mem_norm.reshape(1, D), w_mem_kv.astype(BF16), mem_k_norm.reshape(1, MEM_HEAD_DIM))


_O_CQ = 0
_O_CKV = _O_CQ + Q_LORA
_O_U = _O_CKV + KV_LORA
_O_VG = _O_U + GMLP_WIDTH
_O_QM = _O_VG + GMLP_WIDTH
_O_KR = _O_QM + MEM_WIDTH
_IN_COLS_PAD = _O_KR + LANES


def _rope_tile(t, cos, sin_signed):
    return t * cos + pltpu.roll(t, LANES // 2, axis=1) * sin_signed


def _inproj_kernel(x_ref, pos_ref, invf_ref, sgn_ref, an_ref, w_ref, qan_ref, kvan_ref, krn_ref,
                   gvn_ref, ws_ref, bs_ref, mk_ref, mv_ref, mqn_ref, gon_ref, mon_ref,
                   cq_ref, ckv_ref, kpe_ref, ogm_ref, cs_ref):
    tm = x_ref.shape[0]
    h = _rms(x_ref[...], an_ref[...]).astype(BF16)
    z = _dot(h, w_ref[...])

    cq_ref[...] = _rms(z[:, _O_CQ:_O_CQ + Q_LORA], qan_ref[...]).astype(BF16)
    ckv_ref[...] = _rms(z[:, _O_CKV:_O_CKV + KV_LORA], kvan_ref[...]).astype(BF16)

    ang = pos_ref[...].astype(F32) * invf_ref[...]
    cos = jnp.cos(ang)
    sin_signed = jnp.sin(ang) * sgn_ref[...]
    cs_ref[:, :LANES] = cos
    cs_ref[:, LANES:] = sin_signed
    kr = _rms(z[:, _O_KR:_O_KR + LANES], krn_ref[...], MLA_ROPE)
    kpe_ref[...] = _rope_tile(kr, cos, sin_signed).astype(BF16)

    u = _gelu(z[:, _O_U:_O_U + GMLP_WIDTH])
    vg = _gelu(z[:, _O_VG:_O_VG + GMLP_WIDTH])
    vg = _rms(vg, gvn_ref[...]).astype(BF16)
    row = lax.broadcasted_iota(jnp.int32, (CHUNK, CHUNK), 0)
    col = lax.broadcasted_iota(jnp.int32, (CHUNK, CHUNK), 1)
    sp_cols = []
    for g in range(GMLP_GROUPS):
        wsg = jnp.where(col <= row, ws_ref[g], 0.0).astype(BF16)
        sp_rows = [_dot(wsg, vg[c * CHUNK:(c + 1) * CHUNK, g * GMLP_CH:(g + 1) * GMLP_CH])
                   for c in range(tm // CHUNK)]
        sp_cols.append(jnp.concatenate(sp_rows, axis=0))
    sp = jnp.concatenate(sp_cols, axis=1) + jnp.concatenate([bs_ref[...]] * (tm // CHUNK), axis=0)
    ogm_ref[:, :GMLP_WIDTH] = _rms(u * sp, gon_ref[...]).astype(BF16)

    o_heads = []
    for hd in range(MEM_HEADS):
        sl = slice(hd * MEM_HEAD_DIM, (hd + 1) * MEM_HEAD_DIM)
        q = z[:, _O_QM + hd * MEM_HEAD_DIM:_O_QM + (hd + 1) * MEM_HEAD_DIM]
        qn = (_rms(q, mqn_ref[...]) * (MEM_HEAD_DIM ** -0.5)).astype(BF16)
        s = _dot_nt(qn, mk_ref[0, :, sl])
        s = s - jnp.max(s, axis=-1, keepdims=True)
        p = jnp.exp(s)
        p = p / jnp.sum(p, axis=-1, keepdims=True)
        o_heads.append(_dot(p.astype(BF16), mv_ref[0, :, sl]))
    o_mem = jnp.concatenate(o_heads, axis=1)
    ogm_ref[:, GMLP_WIDTH:] = _rms(o_mem, mon_ref[...]).astype(BF16)


def _inproj(x2, pos, tables, attn_norm, w_in_r, q_a_norm, kv_a_norm, krn_tile, gmlp_v_norm,
            w_spatial, bs_tile, mk, mv, mem_q_norm, gmlp_out_norm, mem_out_norm, seq):
    N, D = x2.shape
    tm = TOK_TILE
    tiles_per_seq = seq // tm
    invf_tile, sgn_tile = tables
    M = mk.shape[1]
    row = lambda w: pl.BlockSpec((tm, w), lambda i: (i, 0))
    batch_blk = pl.BlockSpec((1, M, MEM_WIDTH), lambda i: (i // tiles_per_seq, 0, 0))
    return pl.pallas_call(
        _inproj_kernel,
        grid=(N // tm,),
        in_specs=[row(D), row(1), _const_spec((1, LANES)), _const_spec((1, LANES)),
                  _const_spec((1, D)), _const_spec((D, _IN_COLS_PAD)),
                  _const_spec((1, Q_LORA)), _const_spec((1, KV_LORA)), _const_spec((1, LANES)),
                  _const_spec((1, GMLP_WIDTH)), _const_spec((GMLP_GROUPS, CHUNK, CHUNK)),
                  _const_spec((CHUNK, GMLP_WIDTH)), batch_blk, batch_blk,
                  _const_spec((1, MEM_HEAD_DIM)), _const_spec((1, GMLP_WIDTH)),
                  _const_spec((1, MEM_WIDTH))],
        out_specs=[row(Q_LORA), row(KV_LORA), row(LANES), row(GMLP_WIDTH + MEM_WIDTH),
                   row(2 * LANES)],
        out_shape=[jax.ShapeDtypeStruct((N, Q_LORA), BF16),
                   jax.ShapeDtypeStruct((N, KV_LORA), BF16),
                   jax.ShapeDtypeStruct((N, LANES), BF16),
                   jax.ShapeDtypeStruct((N, GMLP_WIDTH + MEM_WIDTH), BF16),
                   jax.ShapeDtypeStruct((N, 2 * LANES), F32)],
        compiler_params=pltpu.CompilerParams(dimension_semantics=("arbitrary",),
                                             vmem_limit_bytes=VMEM_LIMIT),
        name="inproj",
    )(x2, pos, invf_tile, sgn_tile, attn_norm.reshape(1, D), w_in_r,
      q_a_norm.reshape(1, -1), kv_a_norm.reshape(1, -1), krn_tile, gmlp_v_norm.reshape(1, -1),
      w_spatial, bs_tile, mk, mv, mem_q_norm.reshape(1, -1), gmlp_out_norm.reshape(1, -1),
      mem_out_norm.reshape(1, -1))


def _qkv_kernel(cq_ref, ckv_ref, kpe_ref, cs_ref, wq_ref, wkv_ref, qnn_ref, qrn_ref, knn_ref,
                q_ref, k_ref, v_ref):
    cos = cs_ref[:, :LANES]
    sin_signed = cs_ref[:, LANES:]
    scale = (MLA_NOPE + MLA_ROPE) ** -0.5 * np.log2(np.e)
    qr = _dot(cq_ref[...], wq_ref[...])
    kvr = _dot(ckv_ref[...], wkv_ref[...])
    kpe = kpe_ref[...]
    for h in range(MLA_HEADS):
        o = h * HEAD_PAD
        qn = _rms(qr[:, o:o + MLA_NOPE], qnn_ref[...]) * scale
        qt = _rms(qr[:, o + MLA_NOPE:o + HEAD_PAD], qrn_ref[...], MLA_ROPE)
        qt = _rope_tile(qt, cos, sin_signed) * scale
        q_ref[:, o:o + MLA_NOPE] = qn.astype(BF16)
        q_ref[:, o + MLA_NOPE:o + HEAD_PAD] = qt.astype(BF16)
        ko = h * (MLA_NOPE + MLA_V)
        k_ref[:, o:o + MLA_NOPE] = _rms(kvr[:, ko:ko + MLA_NOPE], knn_ref[...]).astype(BF16)
        k_ref[:, o + MLA_NOPE:o + HEAD_PAD] = kpe
        v_ref[:, h * MLA_V:(h + 1) * MLA_V] = kvr[:, ko + MLA_NOPE:ko + MLA_NOPE + MLA_V].astype(BF16)


def _qkv(cq, ckv, kpe, cs, wq_pad, wkv, q_nope_norm, qrn_tile, k_nope_norm):
    N = cq.shape[0]
    tm = TOK_TILE
    row = lambda w: pl.BlockSpec((tm, w), lambda i: (i, 0))
    return pl.pallas_call(
        _qkv_kernel,
        grid=(N // tm,),
        in_specs=[row(Q_LORA), row(KV_LORA), row(LANES), row(2 * LANES),
                  _const_spec(wq_pad.shape), _const_spec(wkv.shape),
                  _const_spec((1, MLA_NOPE)), _const_spec((1, LANES)), _const_spec((1, MLA_NOPE))],
        out_specs=[row(MLA_HEADS * HEAD_PAD), row(MLA_HEADS * HEAD_PAD), row(MLA_HEADS * MLA_V)],
        out_shape=[jax.ShapeDtypeStruct((N, MLA_HEADS * HEAD_PAD), BF16),
                   jax.ShapeDtypeStruct((N, MLA_HEADS * HEAD_PAD), BF16),
                   jax.ShapeDtypeStruct((N, MLA_HEADS * MLA_V), BF16)],
        compiler_params=pltpu.CompilerParams(dimension_semantics=("arbitrary",),
                                             vmem_limit_bytes=VMEM_LIMIT),
        name="qkv",
    )(cq, ckv, kpe, cs, wq_pad, wkv, q_nope_norm.reshape(1, -1), qrn_tile,
      k_nope_norm.reshape(1, -1))


def _attn_kernel(q_ref, k_ref, v_ref, o_ref):
    tq, tk = ATT_TQ, ATT_TK
    qi = pl.program_id(2)
    qs = [q_ref[0, :, h * HEAD_PAD:(h + 1) * HEAD_PAD] for h in range(ATT_HPS)]

    def head_step(h, kb, carry, masked):
        m, l, acc = carry
        k0 = pl.multiple_of(kb * tk, tk)
        s = _dot_nt(qs[h], k_ref[0, pl.ds(k0, tk), h * HEAD_PAD:(h + 1) * HEAD_PAD])
        if masked:
            qpos = qi * tq + lax.broadcasted_iota(jnp.int32, (tq, tk), 0)
            kpos = kb * tk + lax.broadcasted_iota(jnp.int32, (tq, tk), 1)
            s = jnp.where(kpos <= qpos, s, -jnp.inf)
        m_new = jnp.maximum(m, jnp.max(s, axis=-1, keepdims=True))
        alpha = jnp.exp2(m - m_new)
        p = jnp.exp2(s - m_new)
        l = alpha * l + jnp.sum(p, axis=-1, keepdims=True)
        pv = _dot(p.astype(BF16), v_ref[0, pl.ds(k0, tk), h * MLA_V:(h + 1) * MLA_V])
        return m_new, l, alpha * acc + pv

    def step(kb, carries, masked):
        return tuple(head_step(h, kb, carries[h], masked) for h in range(ATT_HPS))

    init = tuple((jnp.full((tq, 1), -jnp.inf, F32), jnp.zeros((tq, 1), F32),
                  jnp.zeros((tq, MLA_V), F32)) for _ in range(ATT_HPS))
    n_full = (qi * tq) // tk
    carries = lax.fori_loop(0, n_full, lambda kb, c: step(kb, c, False), init)
    carries = step(n_full, carries, True)
    for h in range(ATT_HPS):
        _, l, acc = carries[h]
        o_ref[0, :, h * MLA_V:(h + 1) * MLA_V] = (acc / l).astype(BF16)


def _attention(q, k, v, batch, seq):
    q3 = q.reshape(batch, seq, MLA_HEADS * HEAD_PAD)
    k3 = k.reshape(batch, seq, MLA_HEADS * HEAD_PAD)
    v3 = v.reshape(batch, seq, MLA_HEADS * MLA_V)
    hp, hv = ATT_HPS * HEAD_PAD, ATT_HPS * MLA_V
    assert ATT_TQ == ATT_TK and seq % ATT_TQ == 0
    return pl.pallas_call(
        _attn_kernel,
        grid=(batch, MLA_HEADS // ATT_HPS, seq // ATT_TQ),
        in_specs=[pl.BlockSpec((1, ATT_TQ, hp), lambda b, h, i: (b, i, h)),
                  pl.BlockSpec((1, seq, hp), lambda b, h, i: (b, 0, h)),
                  pl.BlockSpec((1, seq, hv), lambda b, h, i: (b, 0, h))],
        out_specs=pl.BlockSpec((1, ATT_TQ, hv), lambda b, h, i: (b, i, h)),
        out_shape=jax.ShapeDtypeStruct((batch, seq, MLA_HEADS * MLA_V), BF16),
        compiler_params=pltpu.CompilerParams(
            dimension_semantics=("arbitrary", "arbitrary", "arbitrary"),
            vmem_limit_bytes=VMEM_LIMIT),
        name="attn",
    )(q3, k3, v3)


def _outproj_kernel(x_ref, oa_ref, ogm_ref, aon_ref, woa_ref, wob_ref, fn_ref, wr_ref, br_ref,
                    x1_ref, h2_ref, ti_ref, tg_ref):
    tm = x_ref.shape[0]
    oa = _rms(oa_ref[...].astype(F32), aon_ref[...]).astype(BF16)
    x1 = x_ref[...] + _dot(oa, woa_ref[...]) + _dot(ogm_ref[...], wob_ref[...])
    x1_ref[...] = x1
    h2 = _rms(x1, fn_ref[...])
    h2_ref[...] = h2
    h_hi = h2.astype(BF16)
    h_lo = (h2 - h_hi.astype(F32)).astype(BF16)
    wr = wr_ref[...]
    w_hi = wr.astype(BF16)
    w_lo = (wr - w_hi.astype(F32)).astype(BF16)
    hh = _dot(h_hi, jnp.concatenate([w_hi, w_lo], axis=1))
    logits = hh[:, :LANES] + (_dot(h_lo, w_hi) + hh[:, LANES:]) + br_ref[...]
    lane = lax.broadcasted_iota(jnp.int32, (tm, LANES), 1)
    lg = jnp.where(lane < N_EXPERTS, logits, -jnp.inf)
    vals, idxs = [], []
    for _ in range(TOP_K):
        m = jnp.max(lg, axis=-1, keepdims=True)
        am = jnp.min(jnp.where(lg == m, lane, LANES), axis=-1, keepdims=True)
        vals.append(m)
        idxs.append(am)
        lg = jnp.where(lane == am, -jnp.inf, lg)
    es = [jnp.exp(v - vals[0]) for v in vals]
    denom = es[0] + es[1] + es[2] + es[3]
    ti = jnp.zeros((tm, LANES), jnp.int32)
    tg = jnp.zeros((tm, LANES), F32)
    for kk in range(TOP_K):
        ti = jnp.where(lane == kk, idxs[kk], ti)
        tg = jnp.where(lane == kk, es[kk] / denom, tg)
    ti_ref[...] = ti
    tg_ref[...] = tg


def _outproj(x2, o_mla, ogm, mla_out_norm, wo_a, wo_b, ffn_norm, wr_pad, br_pad):
    N, D = x2.shape
    tm = TOK_TILE
    row = lambda w: pl.BlockSpec((tm, w), lambda i: (i, 0))
    wa = o_mla.shape[1]
    wb = ogm.shape[1]
    return pl.pallas_call(
        _outproj_kernel,
        grid=(N // tm,),
        in_specs=[row(D), row(wa), row(wb), _const_spec((1, wa)), _const_spec((wa, D)),
                  _const_spec((wb, D)), _const_spec((1, D)), _const_spec((D, LANES)),
                  _const_spec((1, LANES))],
        out_specs=[row(D), row(D), row(LANES), row(LANES)],
        out_shape=[jax.ShapeDtypeStruct((N, D), F32), jax.ShapeDtypeStruct((N, D), F32),
                   jax.ShapeDtypeStruct((N, LANES), jnp.int32),
                   jax.ShapeDtypeStruct((N, LANES), F32)],
        compiler_params=pltpu.CompilerParams(dimension_semantics=("arbitrary",),
                                             vmem_limit_bytes=VMEM_LIMIT),
        name="outproj",
    )(x2, o_mla, ogm, mla_out_norm.reshape(1, -1), wo_a, wo_b, ffn_norm.reshape(1, -1), wr_pad,
      br_pad)


def _pack_bf16_pairs(x):
    k = x.shape[1] // 2
    lo = pltpu.bitcast(x[:, :k].astype(BF16).astype(F32), jnp.uint32)
    hi = pltpu.bitcast(x[:, k:].astype(BF16).astype(F32), jnp.uint32)
    return hi | (lo >> 16)


def _unpack_bf16_pairs(w):
    lo = pltpu.bitcast(w << 16, F32).astype(BF16)
    hi = pltpu.bitcast(w & jnp.uint32(0xFFFF0000), F32).astype(BF16)
    return jnp.concatenate([lo, hi], axis=1)


def _dispatch_kernel(dest_ref, zb_ref, nzb_ref, h2_ref, xs_hbm, pk, zblk, sem, sem_z):
    tm = h2_ref.shape[0]
    bm = MOE_TM
    i = pl.program_id(0)
    base = i * (tm * TOP_K)
    pk[...] = _pack_bf16_pairs(h2_ref[...])

    @pl.when(i == 0)
    def _zero_blocks():
        zblk[...] = jnp.zeros(zblk.shape, zblk.dtype)

        def zcopy(n):
            d0 = pl.multiple_of(zb_ref[n] * bm, bm)
            return pltpu.make_async_copy(zblk, xs_hbm.at[pl.ds(d0, bm), :], sem_z)

        def issue(n, c):
            zcopy(n).start()
            return c
        lax.fori_loop(0, nzb_ref[0], issue, 0)

        def finish(n, c):
            zcopy(n).wait()
            return c
        lax.fori_loop(0, nzb_ref[0], finish, 0)

    def issue(g, c):
        r0 = pl.multiple_of(g * SUBLANES, SUBLANES)
        rows = pk.at[pl.ds(r0, SUBLANES), :]
        for s in range(SUBLANES):
            for kk in range(TOP_K):
                d = dest_ref[base + (r0 + s) * TOP_K + kk]
                pltpu.make_async_copy(rows.at[pl.ds(s, 1), :], xs_hbm.at[pl.ds(d, 1), :], sem).start()
        return c
    lax.fori_loop(0, tm // SUBLANES, issue, 0)
    for kk in range(TOP_K):
        pltpu.make_async_copy(pk, xs_hbm.at[pl.ds(0, tm), :], sem).wait()


def _dispatch(h2, dest, zero_blocks, n_zero, n_rows):
    N, D = h2.shape
    tm = DSP_TILE
    grid_spec = pltpu.PrefetchScalarGridSpec(
        num_scalar_prefetch=3,
        grid=(N // tm,),
        in_specs=[pl.BlockSpec((tm, D), lambda i, *_: (i, 0))],
        out_specs=pl.BlockSpec(memory_space=pl.ANY),
        scratch_shapes=[pltpu.VMEM((tm, D // 2), jnp.uint32), pltpu.VMEM((MOE_TM, D // 2), jnp.uint32),
                        pltpu.SemaphoreType.DMA(()), pltpu.SemaphoreType.DMA(())],
    )
    return pl.pallas_call(
        _dispatch_kernel,
        grid_spec=grid_spec,
        out_shape=jax.ShapeDtypeStruct((n_rows, D // 2), jnp.uint32),
        compiler_params=pltpu.CompilerParams(dimension_semantics=("arbitrary",),
                                             vmem_limit_bytes=VMEM_LIMIT),
        name="dispatch",
    )(dest, zero_blocks, n_zero, h2)


def _moe_kernel(ie_ref, ist_ref, inb_ref, tail_ref,
                xs_hbm, wgu_hbm, wd_hbm, bgu_ref, bd_ref,
                ys_hbm,
                xbuf, acc, wg_buf, wu_buf, wd_buf, zblk, sem_x, sem_o, sem_w, sem_z):
    tm = MOE_TM
    i = pl.program_id(0)
    n_items = pl.num_programs(0)
    nblk = inb_ref[i]
    slot = i % 2
    n_blocks = ys_hbm.shape[0] // tm
    tf = wg_buf.shape[2]
    nj = wd_hbm.shape[1] // tf

    def x_copy(it, m):
        s0 = pl.multiple_of(ist_ref[it] + m * tm, tm)
        r0 = pl.multiple_of(m * tm, tm)
        return pltpu.make_async_copy(xs_hbm.at[pl.ds(s0, tm), :],
                                     xbuf.at[it % 2, pl.ds(r0, tm), :], sem_x.at[it % 2])

    def y_copy(it, m):
        d0 = pl.multiple_of(ist_ref[it] + m * tm, tm)
        r0 = pl.multiple_of(m * tm, tm)
        return pltpu.make_async_copy(acc.at[it % 2, pl.ds(r0, tm), :],
                                     ys_hbm.at[pl.ds(d0, tm), :], sem_o.at[it % 2])

    def for_blocks(it, fn):
        def body(m, c):
            fn(it, m)
            return c
        lax.fori_loop(0, inb_ref[it], body, 0)

    start_x = lambda it: for_blocks(it, lambda a, m: x_copy(a, m).start())
    wait_x = lambda it: for_blocks(it, lambda a, m: x_copy(a, m).wait())
    start_y = lambda it: for_blocks(it, lambda a, m: y_copy(a, m).start())
    wait_y = lambda it: for_blocks(it, lambda a, m: y_copy(a, m).wait())

    def zero_copy(b):
        d0 = pl.multiple_of(b * tm, tm)
        return pltpu.make_async_copy(zblk, ys_hbm.at[pl.ds(d0, tm), :], sem_z)

    n_ws = wg_buf.shape[0]

    def w_slot(it, j):
        return (it * nj + j) % n_ws

    def w_copies(it, j):
        e = ie_ref[it]
        ws = w_slot(it, j)
        c0 = pl.multiple_of(j * tf, tf)
        c1 = pl.multiple_of(nj * tf + j * tf, tf)
        return (pltpu.make_async_copy(wgu_hbm.at[e, :, pl.ds(c0, tf)], wg_buf.at[ws], sem_w.at[ws]),
                pltpu.make_async_copy(wgu_hbm.at[e, :, pl.ds(c1, tf)], wu_buf.at[ws], sem_w.at[ws]),
                pltpu.make_async_copy(wd_hbm.at[e, pl.ds(c0, tf), :], wd_buf.at[ws], sem_w.at[ws]))

    def start_w(it, j):
        for cp in w_copies(it, j):
            cp.start()

    def wait_w(it, j):
        for cp in w_copies(it, j):
            cp.wait()

    @pl.when(i == 0)
    def _first_step():
        zblk[...] = jnp.zeros(zblk.shape, zblk.dtype)

        def issue(b, c):
            zero_copy(b).start()
            return c
        lax.fori_loop(tail_ref[0], n_blocks, issue, 0)
        start_x(0)

        @pl.when(nblk > 0)
        def _():
            for a in range(n_ws - 1):
                start_w(0, a)

    wait_x(i)

    @pl.when(i + 1 < n_items)
    def _():
        start_x(i + 1)

    @pl.when(i >= 2)
    def _():
        wait_y(i - 2)

    def init(m, c):
        r0 = pl.multiple_of(m * tm, tm)
        acc[slot, pl.ds(r0, tm), :] = jnp.broadcast_to(bd_ref[0], (tm, acc.shape[2]))
        return c
    lax.fori_loop(0, nblk, init, 0)

    def ffn_rows(j, n):
        ws = w_slot(i, j)
        x = _unpack_bf16_pairs(xbuf[slot, pl.ds(0, n), :])
        g = jnp.minimum(_dot(x, wg_buf[ws].astype(BF16)) + bgu_ref[0, pl.ds(j, 1), :], SWIGLU_LIMIT)
        u = jnp.clip(_dot(x, wu_buf[ws].astype(BF16)) + bgu_ref[0, pl.ds(nj + j, 1), :],
                     -SWIGLU_LIMIT, SWIGLU_LIMIT)
        a = (u + 1.0) * (g * jax.nn.sigmoid(SWIGLU_ALPHA * g))
        acc[slot, pl.ds(0, n), :] += _dot(a.astype(BF16), wd_buf[ws].astype(BF16))

    def chunk(j, c):
        wait_w(i, j)
        ja = j + (n_ws - 1)

        @pl.when(ja < nj)
        def _():
            start_w(i, ja)

        @pl.when(jnp.logical_and(ja >= nj, i + 1 < n_items))
        def _():
            @pl.when(inb_ref[i + 1] > 0)
            def _():
                start_w(i + 1, ja - nj)

        for nb in range(1, MOE_BPI + 1):
            pl.when(nblk == nb)(functools.partial(ffn_rows, j, nb * tm))
        return c

    @pl.when(nblk > 0)
    def _chunks():
        lax.fori_loop(0, nj, chunk, 0)

    start_y(i)

    @pl.when(i == n_items - 1)
    def _last_step():
        @pl.when(i >= 1)
        def _():
            wait_y(i - 1)
        wait_y(i)

        def finish(b, c):
            zero_copy(b).wait()
            return c
        lax.fori_loop(tail_ref[0], n_blocks, finish, 0)


def _moe(xs, item_e, item_start, item_nblk, tail_blk, w_gate_up, b_gate_up, w_down, b_down):
    n_rows = xs.shape[0]
    E, D, F2 = w_gate_up.shape
    F = F2 // 2
    tf = MOE_TF
    nj = F // tf
    nw = MOE_WSLOTS
    assert 2 <= nw <= nj + 1
    n_items = item_e.shape[0]
    rows = MOE_BPI * MOE_TM
    any_spec = pl.BlockSpec(memory_space=pl.ANY)
    grid_spec = pltpu.PrefetchScalarGridSpec(
        num_scalar_prefetch=4,
        grid=(n_items,),
        in_specs=[any_spec, any_spec, any_spec,
                  pl.BlockSpec((1, 2 * nj, tf), lambda i, ie, ist, inb, tail: (ie[i], 0, 0)),
                  pl.BlockSpec((1, 1, D), lambda i, ie, ist, inb, tail: (ie[i], 0, 0))],
        out_specs=any_spec,
        scratch_shapes=[pltpu.VMEM((2, rows, D // 2), jnp.uint32), pltpu.VMEM((2, rows, D), F32),
                        pltpu.VMEM((nw, D, tf), F32), pltpu.VMEM((nw, D, tf), F32),
                        pltpu.VMEM((nw, tf, D), F32), pltpu.VMEM((MOE_TM, D), F32),
                        pltpu.SemaphoreType.DMA((2,)), pltpu.SemaphoreType.DMA((2,)),
                        pltpu.SemaphoreType.DMA((nw,)), pltpu.SemaphoreType.DMA(())],
    )
    return pl.pallas_call(
        _moe_kernel,
        grid_spec=grid_spec,
        out_shape=jax.ShapeDtypeStruct((n_rows, D), F32),
        compiler_params=pltpu.CompilerParams(dimension_semantics=("arbitrary",),
                                             vmem_limit_bytes=VMEM_LIMIT),
        name="moe",
    )(item_e, item_start, item_nblk, tail_blk, xs, w_gate_up, w_down,
      b_gate_up.reshape(E, 2 * nj, tf), b_down.reshape(E, 1, D))


def _combine_kernel(pos_ref, x1_ref, g_ref, ys_hbm, o_ref, buf, sem):
    tm = CMB_TILE
    i = pl.program_id(0)

    def start_gather(t):
        base = t * (tm * TOP_K)
        b = t % 2

        def issue(g, c):
            r0 = pl.multiple_of(g * SUBLANES, SUBLANES)
            for s in range(SUBLANES):
                for kk in range(TOP_K):
                    p = pos_ref[base + (r0 + s) * TOP_K + kk]
                    pltpu.make_async_copy(ys_hbm.at[pl.ds(p, 1), :],
                                          buf.at[b, kk, pl.ds(r0, SUBLANES), :].at[pl.ds(s, 1), :],
                                          sem.at[b]).start()
            return c
        lax.fori_loop(0, tm // SUBLANES, issue, 0)

    @pl.when(i == 0)
    def _():
        start_gather(0)

    @pl.when(i + 1 < pl.num_programs(0))
    def _():
        start_gather(i + 1)

    b = i % 2
    for kk in range(TOP_K):
        pltpu.make_async_copy(ys_hbm.at[pl.ds(0, tm), :], buf.at[b, kk], sem.at[b]).wait()
    out = x1_ref[...]
    for kk in range(TOP_K):
        out = out + g_ref[:, kk:kk + 1] * buf[b, kk]
    o_ref[...] = out


def _combine(x1, gates, ys, dest):
    N, D = x1.shape
    tm = CMB_TILE
    grid_spec = pltpu.PrefetchScalarGridSpec(
        num_scalar_prefetch=1,
        grid=(N // tm,),
        in_specs=[pl.BlockSpec((tm, D), lambda i, pos: (i, 0)),
                  pl.BlockSpec((tm, LANES), lambda i, pos: (i, 0)),
                  pl.BlockSpec(memory_space=pl.ANY)],
        out_specs=pl.BlockSpec((tm, D), lambda i, pos: (i, 0)),
        scratch_shapes=[pltpu.VMEM((2, TOP_K, tm, D), F32), pltpu.SemaphoreType.DMA((2,))],
    )
    return pl.pallas_call(
        _combine_kernel,
        grid_spec=grid_spec,
        out_shape=jax.ShapeDtypeStruct((N, D), F32),
        compiler_params=pltpu.CompilerParams(dimension_semantics=("arbitrary",),
                                             vmem_limit_bytes=VMEM_LIMIT),
        name="combine",
    )(dest, x1, gates, ys)


def _routing(top_idx, n_tok):
    tm, bpi = MOE_TM, MOE_BPI
    nk = n_tok * TOP_K
    experts = jnp.arange(N_EXPERTS, dtype=jnp.int32)
    e_flat = top_idx.reshape(-1)
    onehot = (e_flat[:, None] == experts[None, :]).astype(jnp.int32)
    csum = jnp.cumsum(onehot, axis=0)
    rank = jnp.sum(csum * onehot, axis=1) - 1
    counts = csum[-1]
    nb = (counts + tm - 1) // tm
    bend = jnp.cumsum(nb)
    bstart = bend - nb
    dest = (jnp.sum(onehot * bstart[None, :], axis=1) * tm + rank).astype(jnp.int32)
    n_blocks = -(-(nk + N_EXPERTS * (tm - 1)) // tm)
    n_rows = n_blocks * tm
    tail_blk = bend[-1:].astype(jnp.int32)
    zb_e = jnp.where(nb > 0, bend - 1, -1)
    zb_t = jnp.arange(n_blocks, dtype=jnp.int32)
    zb_all = jnp.concatenate([zb_e, jnp.where(zb_t >= bend[-1], zb_t, -1)]).astype(jnp.int32)
    order = jnp.argsort(zb_all < 0, stable=True)
    zero_blocks = zb_all[order]
    n_zero = jnp.sum(zb_all >= 0).astype(jnp.int32).reshape(1)
    n_items = n_blocks // bpi + N_EXPERTS
    items_e = (nb + bpi - 1) // bpi
    iend = jnp.cumsum(items_e)
    istart = iend - items_e
    slot = jnp.arange(n_items, dtype=jnp.int32)
    valid = slot < iend[-1]
    exp_of = jnp.minimum(jnp.searchsorted(iend, slot, side="right"), N_EXPERTS - 1).astype(jnp.int32)
    last_e = jnp.max(jnp.where(nb > 0, experts, 0))
    local = slot - istart[exp_of]
    n_it = jnp.maximum(items_e[exp_of], 1)
    base, rem = nb[exp_of] // n_it, nb[exp_of] % n_it
    first_blk = bstart[exp_of] + local * base + jnp.minimum(local, rem)
    item_e = jnp.where(valid, exp_of, last_e).astype(jnp.int32)
    item_start = jnp.where(valid, first_blk * tm, 0).astype(jnp.int32)
    item_nblk = jnp.where(valid, base + (local < rem), 0).astype(jnp.int32)
    return dest, zero_blocks, n_zero, item_e, item_start, item_nblk, tail_blk, n_rows


def _rope_lane_tile(v):
    half = MLA_ROPE // 2
    z = jnp.zeros((half,), v.dtype)
    return jnp.concatenate([v[:half], z, v[half:], z]).reshape(1, LANES)


def _rope_cols(w):
    half = MLA_ROPE // 2
    z = jnp.zeros((w.shape[0], half), w.dtype)
    return jnp.concatenate([w[:, :half], z, w[:, half:], z], axis=1)


def _layer(x, mem, positions, attn_norm, w_in, q_a_norm, w_q_b, kv_a_norm, w_kv_b,
           q_nope_norm, q_rope_norm, k_nope_norm, k_rope_norm,
           gmlp_v_norm, w_spatial, b_spatial,
           mem_norm, w_mem_kv, mem_q_norm, mem_k_norm,
           mla_out_norm, gmlp_out_norm, mem_out_norm, w_o,
           ffn_norm, w_router, b_router, w_gate_up, b_gate_up, w_down, b_down):
    B, S, D = x.shape
    N = B * S
    x2 = x.reshape(N, D)

    o1 = Q_LORA
    o2 = o1 + KV_LORA
    o3 = o2 + MLA_ROPE
    o4 = o3 + 2 * GMLP_WIDTH
    w_in_r = jnp.concatenate([w_in[:, :o2], w_in[:, o3:o4], w_in[:, o4:], _rope_cols(w_in[:, o2:o3])],
                             axis=1).astype(BF16)
    wq = w_q_b.reshape(Q_LORA, MLA_HEADS, MLA_NOPE + MLA_ROPE)
    wq_rope = jax.vmap(_rope_cols, in_axes=1, out_axes=1)(wq[:, :, MLA_NOPE:])
    wq_pad = jnp.concatenate([wq[:, :, :MLA_NOPE], wq_rope], axis=2).reshape(Q_LORA, MLA_HEADS * HEAD_PAD)
    wq_pad = wq_pad.astype(BF16)
    half = MLA_ROPE // 2
    inv_freq = ROPE_BASE ** (-jnp.arange(half, dtype=F32) / half)
    zf = jnp.zeros((half,), F32)
    invf_tile = jnp.concatenate([inv_freq, zf, inv_freq, zf]).reshape(1, LANES)
    sgn_tile = jnp.concatenate([-jnp.ones((half,), F32), zf, jnp.ones((half,), F32), zf]).reshape(1, LANES)
    bs_tile = jnp.repeat(b_spatial.T, GMLP_CH, axis=1)

    mk, mv = _mem_kv(mem, mem_norm, w_mem_kv, mem_k_norm)
    cq, ckv, kpe, ogm, cs = _inproj(
        x2, positions.reshape(N, 1), (invf_tile, sgn_tile), attn_norm, w_in_r, q_a_norm, kv_a_norm,
        _rope_lane_tile(k_rope_norm), gmlp_v_norm, w_spatial, bs_tile, mk, mv, mem_q_norm,
        gmlp_out_norm, mem_out_norm, S)
    q, k, v = _qkv(cq, ckv, kpe, cs, wq_pad, w_kv_b.astype(BF16), q_nope_norm,
                   _rope_lane_tile(q_rope_norm), k_nope_norm)
    o_mla = _attention(q, k, v, B, S).reshape(N, MLA_HEADS * MLA_V)

    wo = w_o.astype(BF16)
    wr_pad = jnp.pad(w_router, ((0, 0), (0, LANES - N_EXPERTS)))
    br_pad = jnp.pad(b_router, (0, LANES - N_EXPERTS)).reshape(1, LANES)
    x1, h2, ti, tg = _outproj(x2, o_mla, ogm, mla_out_norm, wo[:MLA_HEADS * MLA_V],
                              wo[MLA_HEADS * MLA_V:], ffn_norm, wr_pad, br_pad)

    dest, zero_blocks, n_zero, item_e, item_start, item_nblk, tail_blk, n_rows = _routing(
        ti[:, :TOP_K], N)
    xs = _dispatch(h2, dest, zero_blocks, n_zero, n_rows)
    ys = _moe(xs, item_e, item_start, item_nblk, tail_blk, w_gate_up, b_gate_up, w_down, b_down)
    out = _combine(x1, tg, ys, dest)
    return out.reshape(B, S, D)


def kernel(x, mem, positions, attn_norm, w_in, q_a_norm, w_q_b, kv_a_norm, w_kv_b, q_nope_norm, q_rope_norm, k_nope_norm, k_rope_norm, gmlp_v_norm, w_spatial, b_spatial, mem_norm, w_mem_kv, mem_q_norm, mem_k_norm, mla_out_norm, gmlp_out_norm, mem_out_norm, w_o, ffn_norm, w_router, b_router, w_gate_up, b_gate_up, w_down, b_down):
    depth = attn_norm.shape[0]
    for l in range(depth):
        x = _layer(x, mem, positions, attn_norm[l], w_in[l], q_a_norm[l], w_q_b[l], kv_a_norm[l],
                   w_kv_b[l], q_nope_norm[l], q_rope_norm[l], k_nope_norm[l], k_rope_norm[l],
                   gmlp_v_norm[l], w_spatial[l], b_spatial[l], mem_norm[l], w_mem_kv[l],
                   mem_q_norm[l], mem_k_norm[l], mla_out_norm[l], gmlp_out_norm[l],
                   mem_out_norm[l], w_o[l], ffn_norm[l], w_router[l], b_router[l], w_gate_up[l],
                   b_gate_up[l], w_down[l], b_down[l])
    return x
```

```python
import functools

import jax
import jax.numpy as jnp
import numpy as np
from jax import lax
from jax.experimental import pallas as pl
from jax.experimental.pallas import tpu as pltpu

F32 = jnp.float32
BF16 = jnp.bfloat16

EPS = 1e-6
LANES = 128
SUBLANES = 8
VMEM_LIMIT = 56 * 1024 * 1024

MLA_HEADS = 8
MLA_NOPE = 128
MLA_ROPE = 64
MLA_V = 128
Q_LORA = 512
KV_LORA = 512
GMLP_GROUPS = 4
GMLP_CH = 128
GMLP_WIDTH = GMLP_GROUPS * GMLP_CH
CHUNK = 128
MEM_HEADS = 4
MEM_HEAD_DIM = 128
MEM_WIDTH = MEM_HEADS * MEM_HEAD_DIM
N_EXPERTS = 32
TOP_K = 4
SWIGLU_ALPHA = 1.702
SWIGLU_LIMIT = 7.0
ROPE_BASE = 10000.0
HEAD_PAD = 2 * LANES

TOK_TILE = 512
ROW_GROUPS = 1
ATT_TQ = 512
ATT_TK = 512
ATT_HPS = 4
MOE_TM = 256
MOE_BPI = 5
MOE_TF = 256
MOE_WSLOTS = 3
DSP_TILE = 1024
CMB_TILE = 256


def _rms(x, g, n=None):
    n = x.shape[-1] if n is None else n
    ms = jnp.sum(x * x, axis=-1, keepdims=True) * (1.0 / n)
    return x * lax.rsqrt(ms + EPS) * g


def _gelu(x):
    return 0.5 * x * (1.0 + lax.erf(x * (2.0 ** -0.5)))


def _dot(a, b):
    return jnp.dot(a, b, preferred_element_type=F32)


def _dot_nt(a, b):
    return lax.dot_general(a, b, (((1,), (1,)), ((), ())), preferred_element_type=F32)


def _const_spec(shape):
    nd = len(shape)
    return pl.BlockSpec(shape, lambda *_: (0,) * nd)


def _mem_kv_kernel(mem_ref, g_ref, w_ref, kg_ref, mk_ref, mv_ref):
    m = mem_ref[0]
    hn = _rms(m, g_ref[...]).astype(BF16)
    kv = _dot(hn, w_ref[...])
    for h in range(MEM_HEADS):
        k = kv[:, h * MEM_HEAD_DIM:(h + 1) * MEM_HEAD_DIM]
        mk_ref[0, :, h * MEM_HEAD_DIM:(h + 1) * MEM_HEAD_DIM] = _rms(k, kg_ref[...]).astype(BF16)
    mv_ref[0] = kv[:, MEM_WIDTH:].astype(BF16)


def _mem_kv(mem, mem_norm, w_mem_kv, mem_k_norm):
    B, M, D = mem.shape
    return pl.pallas_call(
        _mem_kv_kernel,
        grid=(B,),
        in_specs=[pl.BlockSpec((1, M, D), lambda b: (b, 0, 0)),
                  _const_spec((1, D)),
                  _const_spec((D, 2 * MEM_WIDTH)),
                  _const_spec((1, MEM_HEAD_DIM))],
        out_specs=[pl.BlockSpec((1, M, MEM_WIDTH), lambda b: (b, 0, 0)),
                   pl.BlockSpec((1, M, MEM_WIDTH), lambda b: (b, 0, 0))],
        out_shape=[jax.ShapeDtypeStruct((B, M, MEM_WIDTH), BF16),
                   jax.ShapeDtypeStruct((B, M, MEM_WIDTH), BF16)],
        compiler_params=pltpu.CompilerParams(dimension_semantics=("arbitrary",),
                                             vmem_limit_bytes=VMEM_LIMIT),
        name="mem_kv",
    )(mem, mem_norm.reshape(1, D), w_mem_kv.astype(BF16), mem_k_norm.reshape(1, MEM_HEAD_DIM))


_O_CQ = 0
_O_CKV = _O_CQ + Q_LORA
_O_U = _O_CKV + KV_LORA
_O_VG = _O_U + GMLP_WIDTH
_O_QM = _O_VG + GMLP_WIDTH
_O_KR = _O_QM + MEM_WIDTH
_IN_COLS_PAD = _O_KR + LANES


def _rope_tile(t, cos, sin_signed):
    return t * cos + pltpu.roll(t, LANES // 2, axis=1) * sin_signed


def _inproj_kernel(x_ref, pos_ref, invf_ref, sgn_ref, an_ref, w_ref, qan_ref, kvan_ref, krn_ref,
                   gvn_ref, ws_ref, bs_ref, mk_ref, mv_ref, mqn_ref, gon_ref, mon_ref,
                   cq_ref, ckv_ref, kpe_ref, ogm_ref, cs_ref):
    tm = x_ref.shape[0]
    h = _rms(x_ref[...], an_ref[...]).astype(BF16)
    z = _dot(h, w_ref[...])

    cq_ref[...] = _rms(z[:, _O_CQ:_O_CQ + Q_LORA], qan_ref[...]).astype(BF16)
    ckv_ref[...] = _rms(z[:, _O_CKV:_O_CKV + KV_LORA], kvan_ref[...]).astype(BF16)

    ang = pos_ref[...].astype(F32) * invf_ref[...]
    cos = jnp.cos(ang)
    sin_signed = jnp.sin(ang) * sgn_ref[...]
    cs_ref[:, :LANES] = cos
    cs_ref[:, LANES:] = sin_signed
    kr = _rms(z[:, _O_KR:_O_KR + LANES], krn_ref[...], MLA_ROPE)
    kpe_ref[...] = _rope_tile(kr, cos, sin_signed).astype(BF16)

    u = _gelu(z[:, _O_U:_O_U + GMLP_WIDTH])
    vg = _gelu(z[:, _O_VG:_O_VG + GMLP_WIDTH])
    vg = _rms(vg, gvn_ref[...]).astype(BF16)
    row = lax.broadcasted_iota(jnp.int32, (CHUNK, CHUNK), 0)
    col = lax.broadcasted_iota(jnp.int32, (CHUNK, CHUNK), 1)
    sp_cols = []
    for g in range(GMLP_GROUPS):
        wsg = jnp.where(col <= row, ws_ref[g], 0.0).astype(BF16)
        sp_rows = [_dot(wsg, vg[c * CHUNK:(c + 1) * CHUNK, g * GMLP_CH:(g + 1) * GMLP_CH])
                   for c in range(tm // CHUNK)]
        sp_cols.append(jnp.concatenate(sp_rows, axis=0))
    sp = jnp.concatenate(sp_cols, axis=1) + jnp.concatenate([bs_ref[...]] * (tm // CHUNK), axis=0)
    ogm_ref[:, :GMLP_WIDTH] = _rms(u * sp, gon_ref[...]).astype(BF16)

    o_heads = []
    for hd in range(MEM_HEADS):
        sl = slice(hd * MEM_HEAD_DIM, (hd + 1) * MEM_HEAD_DIM)
        q = z[:, _O_QM + hd * MEM_HEAD_DIM:_O_QM + (hd + 1) * MEM_HEAD_DIM]
        qn = (_rms(q, mqn_ref[...]) * (MEM_HEAD_DIM ** -0.5)).astype(BF16)
        s = _dot_nt(qn, mk_ref[0, :, sl])
        s = s - jnp.max(s, axis=-1, keepdims=True)
        p = jnp.exp(s)
        p = p / jnp.sum(p, axis=-1, keepdims=True)
        o_heads.append(_dot(p.astype(BF16), mv_ref[0, :, sl]))
    o_mem = jnp.concatenate(o_heads, axis=1)
    ogm_ref[:, GMLP_WIDTH:] = _rms(o_mem, mon_ref[...]).astype(BF16)


def _inproj(x2, pos, tables, attn_norm, w_in_r, q_a_norm, kv_a_norm, krn_tile, gmlp_v_norm,
            w_spatial, bs_tile, mk, mv, mem_q_norm, gmlp_out_norm, mem_out_norm, seq):
    N, D = x2.shape
    tm = TOK_TILE
    tiles_per_seq = seq // tm
    invf_tile, sgn_tile = tables
    M = mk.shape[1]
    row = lambda w: pl.BlockSpec((tm, w), lambda i: (i, 0))
    batch_blk = pl.BlockSpec((1, M, MEM_WIDTH), lambda i: (i // tiles_per_seq, 0, 0))
    return pl.pallas_call(
        _inproj_kernel,
        grid=(N // tm,),
        in_specs=[row(D), row(1), _const_spec((1, LANES)), _const_spec((1, LANES)),
                  _const_spec((1, D)), _const_spec((D, _IN_COLS_PAD)),
                  _const_spec((1, Q_LORA)), _const_spec((1, KV_LORA)), _const_spec((1, LANES)),
                  _const_spec((1, GMLP_WIDTH)), _const_spec((GMLP_GROUPS, CHUNK, CHUNK)),
                  _const_spec((CHUNK, GMLP_WIDTH)), batch_blk, batch_blk,
                  _const_spec((1, MEM_HEAD_DIM)), _const_spec((1, GMLP_WIDTH)),
                  _const_spec((1, MEM_WIDTH))],
        out_specs=[row(Q_LORA), row(KV_LORA), row(LANES), row(GMLP_WIDTH + MEM_WIDTH),
                   row(2 * LANES)],
        out_shape=[jax.ShapeDtypeStruct((N, Q_LORA), BF16),
                   jax.ShapeDtypeStruct((N, KV_LORA), BF16),
                   jax.ShapeDtypeStruct((N, LANES), BF16),
                   jax.ShapeDtypeStruct((N, GMLP_WIDTH + MEM_WIDTH), BF16),
                   jax.ShapeDtypeStruct((N, 2 * LANES), F32)],
        compiler_params=pltpu.CompilerParams(dimension_semantics=("arbitrary",),
                                             vmem_limit_bytes=VMEM_LIMIT),
        name="inproj",
    )(x2, pos, invf_tile, sgn_tile, attn_norm.reshape(1, D), w_in_r,
      q_a_norm.reshape(1, -1), kv_a_norm.reshape(1, -1), krn_tile, gmlp_v_norm.reshape(1, -1),
      w_spatial, bs_tile, mk, mv, mem_q_norm.reshape(1, -1), gmlp_out_norm.reshape(1, -1),
      mem_out_norm.reshape(1, -1))


def _qkv_kernel(cq_ref, ckv_ref, kpe_ref, cs_ref, wq_ref, wkv_ref, qnn_ref, qrn_ref, knn_ref,
                q_ref, k_ref, v_ref):
    cos = cs_ref[:, :LANES]
    sin_signed = cs_ref[:, LANES:]
    scale = (MLA_NOPE + MLA_ROPE) ** -0.5 * np.log2(np.e)
    qr = _dot(cq_ref[...], wq_ref[...])
    kvr = _dot(ckv_ref[...], wkv_ref[...])
    kpe = kpe_ref[...]
    for h in range(MLA_HEADS):
        o = h * HEAD_PAD
        qn = _rms(qr[:, o:o + MLA_NOPE], qnn_ref[...]) * scale
        qt = _rms(qr[:, o + MLA_NOPE:o + HEAD_PAD], qrn_ref[...], MLA_ROPE)
        qt = _rope_tile(qt, cos, sin_signed) * scale
        q_ref[:, o:o + MLA_NOPE] = qn.astype(BF16)
        q_ref[:, o + MLA_NOPE:o + HEAD_PAD] = qt.astype(BF16)
        ko = h * (MLA_NOPE + MLA_V)
        k_ref[:, o:o + MLA_NOPE] = _rms(kvr[:, ko:ko + MLA_NOPE], knn_ref[...]).astype(BF16)
        k_ref[:, o + MLA_NOPE:o + HEAD_PAD] = kpe
        v_ref[:, h * MLA_V:(h + 1) * MLA_V] = kvr[:, ko + MLA_NOPE:ko + MLA_NOPE + MLA_V].astype(BF16)


def _qkv(cq, ckv, kpe, cs, wq_pad, wkv, q_nope_norm, qrn_tile, k_nope_norm):
    N = cq.shape[0]
    tm = TOK_TILE
    row = lambda w: pl.BlockSpec((tm, w), lambda i: (i, 0))
    return pl.pallas_call(
        _qkv_kernel,
        grid=(N // tm,),
        in_specs=[row(Q_LORA), row(KV_LORA), row(LANES), row(2 * LANES),
                  _const_spec(wq_pad.shape), _const_spec(wkv.shape),
                  _const_spec((1, MLA_NOPE)), _const_spec((1, LANES)), _const_spec((1, MLA_NOPE))],
        out_specs=[row(MLA_HEADS * HEAD_PAD), row(MLA_HEADS * HEAD_PAD), row(MLA_HEADS * MLA_V)],
        out_shape=[jax.ShapeDtypeStruct((N, MLA_HEADS * HEAD_PAD), BF16),
                   jax.ShapeDtypeStruct((N, MLA_HEADS * HEAD_PAD), BF16),
                   jax.ShapeDtypeStruct((N, MLA_HEADS * MLA_V), BF16)],
        compiler_params=pltpu.CompilerParams(dimension_semantics=("arbitrary",),
                                             vmem_limit_bytes=VMEM_LIMIT),
        name="qkv",
    )(cq, ckv, kpe, cs, wq_pad, wkv, q_nope_norm.reshape(1, -1), qrn_tile,
      k_nope_norm.reshape(1, -1))


def _attn_kernel(q_ref, k_ref, v_ref, o_ref):
    tq, tk = ATT_TQ, ATT_TK
    qi = pl.program_id(2)
    qs = [q_ref[0, :, h * HEAD_PAD:(h + 1) * HEAD_PAD] for h in range(ATT_HPS)]

    def scores(h, kb):
        k0 = pl.multiple_of(kb * tk, tk)
        return _dot_nt(k_ref[0, pl.ds(k0, tk), h * HEAD_PAD:(h + 1) * HEAD_PAD], qs[h])

    def head_step(h, kb, s, carry, masked):
        m, l, acc = carry
        k0 = pl.multiple_of(kb * tk, tk)
        if masked:
            kpos = kb * tk + lax.broadcasted_iota(jnp.int32, (tk, tq), 0)
            qpos = qi * tq + lax.broadcasted_iota(jnp.int32, (tk, tq), 1)
            s = jnp.where(kpos <= qpos, s, -jnp.inf)
        m_new = jnp.maximum(m, jnp.max(s, axis=0, keepdims=True))
        alpha = jnp.exp2(m - m_new)
        p = jnp.exp2(s - m_new)
        l = alpha * l + jnp.sum(p, axis=0, keepdims=True)
        v = v_ref[0, pl.ds(k0, tk), h * MLA_V:(h + 1) * MLA_V]
        pv = lax.dot_general(v, p.astype(BF16), (((0,), (0,)), ((), ())),
                             preferred_element_type=F32)
        return m_new, l, alpha * acc + pv

    def step(kb, carries, masked):
        ss = [scores(h, kb) for h in range(ATT_HPS)]
        return tuple(head_step(h, kb, ss[h], carries[h], masked) for h in range(ATT_HPS))

    init = tuple((jnp.full((1, tq), -jnp.inf, F32), jnp.zeros((1, tq), F32),
                  jnp.zeros((MLA_V, tq), F32)) for _ in range(ATT_HPS))
    n_full = (qi * tq) // tk
    carries = lax.fori_loop(0, n_full, lambda kb, c: step(kb, c, False), init)
    carries = step(n_full, carries, True)
    for h in range(ATT_HPS):
        _, l, acc = carries[h]
        o_ref[0, :, h * MLA_V:(h + 1) * MLA_V] = jnp.transpose(acc / l).astype(BF16)


def _attention(q, k, v, batch, seq):
    q3 = q.reshape(batch, seq, MLA_HEADS * HEAD_PAD)
    k3 = k.reshape(batch, seq, MLA_HEADS * HEAD_PAD)
    v3 = v.reshape(batch, seq, MLA_HEADS * MLA_V)
    hp, hv = ATT_HPS * HEAD_PAD, ATT_HPS * MLA_V
    assert ATT_TK % ATT_TQ == 0 and seq % ATT_TK == 0
    return pl.pallas_call(
        _attn_kernel,
        grid=(batch, MLA_HEADS // ATT_HPS, seq // ATT_TQ),
        in_specs=[pl.BlockSpec((1, ATT_TQ, hp), lambda b, h, i: (b, i, h)),
                  pl.BlockSpec((1, seq, hp), lambda b, h, i: (b, 0, h)),
                  pl.BlockSpec((1, seq, hv), lambda b, h, i: (b, 0, h))],
        out_specs=pl.BlockSpec((1, ATT_TQ, hv), lambda b, h, i: (b, i, h)),
        out_shape=jax.ShapeDtypeStruct((batch, seq, MLA_HEADS * MLA_V), BF16),
        compiler_params=pltpu.CompilerParams(
            dimension_semantics=("arbitrary", "arbitrary", "arbitrary"),
            vmem_limit_bytes=VMEM_LIMIT),
        name="attn",
    )(q3, k3, v3)


def _outproj_kernel(x_ref, oa_ref, ogm_ref, aon_ref, woa_ref, wob_ref, fn_ref, wr_ref, br_ref,
                    x1_ref, h2_ref, ti_ref, tg_ref):
    wr = wr_ref[...]
    w_hi = wr.astype(BF16)
    w_lo = (wr - w_hi.astype(F32)).astype(BF16)
    w_cat = jnp.concatenate([w_hi, w_lo], axis=1)

    def rows(sl, n):
        oa = _rms(oa_ref[sl, :].astype(F32), aon_ref[...]).astype(BF16)
        x1 = x_ref[sl, :] + _dot(oa, woa_ref[...]) + _dot(ogm_ref[sl, :], wob_ref[...])
        x1_ref[sl, :] = x1
        h2 = _rms(x1, fn_ref[...])
        h2_ref[sl, :] = h2
        h_hi = h2.astype(BF16)
        h_lo = (h2 - h_hi.astype(F32)).astype(BF16)
        hh = _dot(h_hi, w_cat)
        logits = hh[:, :LANES] + (_dot(h_lo, w_hi) + hh[:, LANES:]) + br_ref[...]
        lane = lax.broadcasted_iota(jnp.int32, (n, LANES), 1)
        lg = jnp.where(lane < N_EXPERTS, logits, -jnp.inf)
        vals, idxs = [], []
        for _ in range(TOP_K):
            m = jnp.max(lg, axis=-1, keepdims=True)
            am = jnp.min(jnp.where(lg == m, lane, LANES), axis=-1, keepdims=True)
            vals.append(m)
            idxs.append(am)
            lg = jnp.where(lane == am, -jnp.inf, lg)
        es = [jnp.exp(v - vals[0]) for v in vals]
        denom = es[0] + es[1] + es[2] + es[3]
        ti = jnp.zeros((n, LANES), jnp.int32)
        tg = jnp.zeros((n, LANES), F32)
        for kk in range(TOP_K):
            ti = jnp.where(lane == kk, idxs[kk], ti)
            tg = jnp.where(lane == kk, es[kk] / denom, tg)
        ti_ref[sl, :] = ti
        tg_ref[sl, :] = tg

    n = x_ref.shape[0] // ROW_GROUPS
    for r in range(ROW_GROUPS):
        rows(pl.ds(r * n, n), n)


def _outproj(x2, o_mla, ogm, mla_out_norm, wo_a, wo_b, ffn_norm, wr_pad, br_pad):
    N, D = x2.shape
    tm = TOK_TILE
    row = lambda w: pl.BlockSpec((tm, w), lambda i: (i, 0))
    wa = o_mla.shape[1]
    wb = ogm.shape[1]
    return pl.pallas_call(
        _outproj_kernel,
        grid=(N // tm,),
        in_specs=[row(D), row(wa), row(wb), _const_spec((1, wa)), _const_spec((wa, D)),
                  _const_spec((wb, D)), _const_spec((1, D)), _const_spec((D, LANES)),
                  _const_spec((1, LANES))],
        out_specs=[row(D), row(D), row(LANES), row(LANES)],
        out_shape=[jax.ShapeDtypeStruct((N, D), F32), jax.ShapeDtypeStruct((N, D), F32),
                   jax.ShapeDtypeStruct((N, LANES), jnp.int32),
                   jax.ShapeDtypeStruct((N, LANES), F32)],
        compiler_params=pltpu.CompilerParams(dimension_semantics=("arbitrary",),
                                             vmem_limit_bytes=VMEM_LIMIT),
        name="outproj",
    )(x2, o_mla, ogm, mla_out_norm.reshape(1, -1), wo_a, wo_b, ffn_norm.reshape(1, -1), wr_pad,
      br_pad)


def _pack_bf16_pairs(x):
    k = x.shape[1] // 2
    lo = pltpu.bitcast(x[:, :k].astype(BF16).astype(F32), jnp.uint32)
    hi = pltpu.bitcast(x[:, k:].astype(BF16).astype(F32), jnp.uint32)
    return hi | (lo >> 16)


def _unpack_bf16_pairs(w):
    lo = pltpu.bitcast(w << 16, F32).astype(BF16)
    hi = pltpu.bitcast(w & jnp.uint32(0xFFFF0000), F32).astype(BF16)
    return jnp.concatenate([lo, hi], axis=1)


def _dispatch_kernel(dest_ref, zb_ref, nzb_ref, h2_ref, xs_hbm, pk, zblk, sem, sem_z):
    tm = h2_ref.shape[0]
    bm = MOE_TM
    i = pl.program_id(0)
    base = i * (tm * TOP_K)
    pk[...] = _pack_bf16_pairs(h2_ref[...])

    @pl.when(i == 0)
    def _zero_blocks():
        zblk[...] = jnp.zeros(zblk.shape, zblk.dtype)

        def zcopy(n):
            d0 = pl.multiple_of(zb_ref[n] * bm, bm)
            return pltpu.make_async_copy(zblk, xs_hbm.at[pl.ds(d0, bm), :], sem_z)

        def issue(n, c):
            zcopy(n).start()
            return c
        lax.fori_loop(0, nzb_ref[0], issue, 0)

        def finish(n, c):
            zcopy(n).wait()
            return c
        lax.fori_loop(0, nzb_ref[0], finish, 0)

    def issue(g, c):
        r0 = pl.multiple_of(g * SUBLANES, SUBLANES)
        rows = pk.at[pl.ds(r0, SUBLANES), :]
        for s in range(SUBLANES):
            for kk in range(TOP_K):
                d = dest_ref[base + (r0 + s) * TOP_K + kk]
                pltpu.make_async_copy(rows.at[pl.ds(s, 1), :], xs_hbm.at[pl.ds(d, 1), :], sem).start()
        return c
    lax.fori_loop(0, tm // SUBLANES, issue, 0)
    for kk in range(TOP_K):
        pltpu.make_async_copy(pk, xs_hbm.at[pl.ds(0, tm), :], sem).wait()


def _dispatch(h2, dest, zero_blocks, n_zero, n_rows):
    N, D = h2.shape
    tm = DSP_TILE
    grid_spec = pltpu.PrefetchScalarGridSpec(
        num_scalar_prefetch=3,
        grid=(N // tm,),
        in_specs=[pl.BlockSpec((tm, D), lambda i, *_: (i, 0))],
        out_specs=pl.BlockSpec(memory_space=pl.ANY),
        scratch_shapes=[pltpu.VMEM((tm, D // 2), jnp.uint32), pltpu.VMEM((MOE_TM, D // 2), jnp.uint32),
                        pltpu.SemaphoreType.DMA(()), pltpu.SemaphoreType.DMA(())],
    )
    return pl.pallas_call(
        _dispatch_kernel,
        grid_spec=grid_spec,
        out_shape=jax.ShapeDtypeStruct((n_rows, D // 2), jnp.uint32),
        compiler_params=pltpu.CompilerParams(dimension_semantics=("arbitrary",),
                                             vmem_limit_bytes=VMEM_LIMIT),
        name="dispatch",
    )(dest, zero_blocks, n_zero, h2)


def _moe_kernel(ie_ref, ist_ref, inb_ref, tail_ref,
                xs_hbm, wgu_hbm, wd_hbm, bgu_ref, bd_ref,
                ys_hbm,
                xbuf, acc, wg_buf, wu_buf, wd_buf, zblk, sem_x, sem_o, sem_w, sem_z):
    tm = MOE_TM
    i = pl.program_id(0)
    n_items = pl.num_programs(0)
    nblk = inb_ref[i]
    slot = i % 2
    n_blocks = ys_hbm.shape[0] // tm
    tf = wg_buf.shape[2]
    nj = wd_hbm.shape[1] // tf

    def x_copy(it, m):
        s0 = pl.multiple_of(ist_ref[it] + m * tm, tm)
        r0 = pl.multiple_of(m * tm, tm)
        return pltpu.make_async_copy(xs_hbm.at[pl.ds(s0, tm), :],
                                     xbuf.at[it % 2, pl.ds(r0, tm), :], sem_x.at[it % 2])

    def y_copy(it, m):
        d0 = pl.multiple_of(ist_ref[it] + m * tm, tm)
        r0 = pl.multiple_of(m * tm, tm)
        return pltpu.make_async_copy(acc.at[it % 2, pl.ds(r0, tm), :],
                                     ys_hbm.at[pl.ds(d0, tm), :], sem_o.at[it % 2])

    def for_blocks(it, fn):
        def body(m, c):
            fn(it, m)
            return c
        lax.fori_loop(0, inb_ref[it], body, 0)

    start_x = lambda it: for_blocks(it, lambda a, m: x_copy(a, m).start())
    wait_x = lambda it: for_blocks(it, lambda a, m: x_copy(a, m).wait())
    start_y = lambda it: for_blocks(it, lambda a, m: y_copy(a, m).start())
    wait_y = lambda it: for_blocks(it, lambda a, m: y_copy(a, m).wait())

    def zero_copy(b):
        d0 = pl.multiple_of(b * tm, tm)
        return pltpu.make_async_copy(zblk, ys_hbm.at[pl.ds(d0, tm), :], sem_z)

    n_ws = wg_buf.shape[0]

    def w_slot(it, j):
        return (it * nj + j) % n_ws

    def w_copies(it, j):
        e = ie_ref[it]
        ws = w_slot(it, j)
        c0 = pl.multiple_of(j * tf, tf)
        c1 = pl.multiple_of(nj * tf + j * tf, tf)
        return (pltpu.make_async_copy(wgu_hbm.at[e, :, pl.ds(c0, tf)], wg_buf.at[ws], sem_w.at[ws]),
                pltpu.make_async_copy(wgu_hbm.at[e, :, pl.ds(c1, tf)], wu_buf.at[ws], sem_w.at[ws]),
                pltpu.make_async_copy(wd_hbm.at[e, pl.ds(c0, tf), :], wd_buf.at[ws], sem_w.at[ws]))

    def start_w(it, j):
        for cp in w_copies(it, j):
            cp.start()

    def wait_w(it, j):
        for cp in w_copies(it, j):
            cp.wait()

    @pl.when(i == 0)
    def _first_step():
        zblk[...] = jnp.zeros(zblk.shape, zblk.dtype)

        def issue(b, c):
            zero_copy(b).start()
            return c
        lax.fori_loop(tail_ref[0], n_blocks, issue, 0)
        start_x(0)

        @pl.when(nblk > 0)
        def _():
            for a in range(n_ws - 1):
                start_w(0, a)

    wait_x(i)

    @pl.when(i + 1 < n_items)
    def _():
        start_x(i + 1)

    @pl.when(i >= 2)
    def _():
        wait_y(i - 2)

    def init(m, c):
        r0 = pl.multiple_of(m * tm, tm)
        acc[slot, pl.ds(r0, tm), :] = jnp.broadcast_to(bd_ref[0], (tm, acc.shape[2]))
        return c
    lax.fori_loop(0, nblk, init, 0)

    def ffn_rows(j, n):
        ws = w_slot(i, j)
        x = _unpack_bf16_pairs(xbuf[slot, pl.ds(0, n), :])
        g = jnp.minimum(_dot(x, wg_buf[ws].astype(BF16)) + bgu_ref[0, pl.ds(j, 1), :], SWIGLU_LIMIT)
        u = jnp.clip(_dot(x, wu_buf[ws].astype(BF16)) + bgu_ref[0, pl.ds(nj + j, 1), :],
                     -SWIGLU_LIMIT, SWIGLU_LIMIT)
        a = (u + 1.0) * (g * jax.nn.sigmoid(SWIGLU_ALPHA * g))
        acc[slot, pl.ds(0, n), :] += _dot(a.astype(BF16), wd_buf[ws].astype(BF16))

    def chunk(j, c):
        wait_w(i, j)
        ja = j + (n_ws - 1)

        @pl.when(ja < nj)
        def _():
            start_w(i, ja)

        @pl.when(jnp.logical_and(ja >= nj, i + 1 < n_items))
        def _():
            @pl.when(inb_ref[i + 1] > 0)
            def _():
                start_w(i + 1, ja - nj)

        for nb in range(1, MOE_BPI + 1):
            pl.when(nblk == nb)(functools.partial(ffn_rows, j, nb * tm))
        return c

    @pl.when(nblk > 0)
    def _chunks():
        lax.fori_loop(0, nj, chunk, 0)

    start_y(i)

    @pl.when(i == n_items - 1)
    def _last_step():
        @pl.when(i >= 1)
        def _():
            wait_y(i - 1)
        wait_y(i)

        def finish(b, c):
            zero_copy(b).wait()
            return c
        lax.fori_loop(tail_ref[0], n_blocks, finish, 0)


def _moe(xs, item_e, item_start, item_nblk, tail_blk, w_gate_up, b_gate_up, w_down, b_down):
    n_rows = xs.shape[0]
    E, D, F2 = w_gate_up.shape
    F = F2 // 2
    tf = MOE_TF
    nj = F // tf
    nw = MOE_WSLOTS
    assert 2 <= nw <= nj + 1
    n_items = item_e.shape[0]
    rows = MOE_BPI * MOE_TM
    any_spec = pl.BlockSpec(memory_space=pl.ANY)
    grid_spec = pltpu.PrefetchScalarGridSpec(
        num_scalar_prefetch=4,
        grid=(n_items,),
        in_specs=[any_spec, any_spec, any_spec,
                  pl.BlockSpec((1, 2 * nj, tf), lambda i, ie, ist, inb, tail: (ie[i], 0, 0)),
                  pl.BlockSpec((1, 1, D), lambda i, ie, ist, inb, tail: (ie[i], 0, 0))],
        out_specs=any_spec,
        scratch_shapes=[pltpu.VMEM((2, rows, D // 2), jnp.uint32), pltpu.VMEM((2, rows, D), F32),
                        pltpu.VMEM((nw, D, tf), F32), pltpu.VMEM((nw, D, tf), F32),
                        pltpu.VMEM((nw, tf, D), F32), pltpu.VMEM((MOE_TM, D), F32),
                        pltpu.SemaphoreType.DMA((2,)), pltpu.SemaphoreType.DMA((2,)),
                        pltpu.SemaphoreType.DMA((nw,)), pltpu.SemaphoreType.DMA(())],
    )
    return pl.pallas_call(
        _moe_kernel,
        grid_spec=grid_spec,
        out_shape=jax.ShapeDtypeStruct((n_rows, D), F32),
        compiler_params=pltpu.CompilerParams(dimension_semantics=("arbitrary",),
                                             vmem_limit_bytes=VMEM_LIMIT),
        name="moe",
    )(item_e, item_start, item_nblk, tail_blk, xs, w_gate_up, w_down,
      b_gate_up.reshape(E, 2 * nj, tf), b_down.reshape(E, 1, D))


def _combine_kernel(pos_ref, x1_ref, g_ref, ys_hbm, o_ref, buf, sem):
    tm = CMB_TILE
    i = pl.program_id(0)

    def start_gather(t):
        base = t * (tm * TOP_K)
        b = t % 2

        def issue(g, c):
            r0 = pl.multiple_of(g * SUBLANES, SUBLANES)
            for s in range(SUBLANES):
                for kk in range(TOP_K):
                    p = pos_ref[base + (r0 + s) * TOP_K + kk]
                    pltpu.make_async_copy(ys_hbm.at[pl.ds(p, 1), :],
                                          buf.at[b, kk, pl.ds(r0, SUBLANES), :].at[pl.ds(s, 1), :],
                                          sem.at[b]).start()
            return c
        lax.fori_loop(0, tm // SUBLANES, issue, 0)

    @pl.when(i == 0)
    def _():
        start_gather(0)

    @pl.when(i + 1 < pl.num_programs(0))
    def _():
        start_gather(i + 1)

    b = i % 2
    for kk in range(TOP_K):
        pltpu.make_async_copy(ys_hbm.at[pl.ds(0, tm), :], buf.at[b, kk], sem.at[b]).wait()
    out = x1_ref[...]
    for kk in range(TOP_K):
        out = out + g_ref[:, kk:kk + 1] * buf[b, kk]
    o_ref[...] = out


def _combine(x1, gates, ys, dest):
    N, D = x1.shape
    tm = CMB_TILE
    grid_spec = pltpu.PrefetchScalarGridSpec(
        num_scalar_prefetch=1,
        grid=(N // tm,),
        in_specs=[pl.BlockSpec((tm, D), lambda i, pos: (i, 0)),
                  pl.BlockSpec((tm, LANES), lambda i, pos: (i, 0)),
                  pl.BlockSpec(memory_space=pl.ANY)],
        out_specs=pl.BlockSpec((tm, D), lambda i, pos: (i, 0)),
        scratch_shapes=[pltpu.VMEM((2, TOP_K, tm, D), F32), pltpu.SemaphoreType.DMA((2,))],
    )
    return pl.pallas_call(
        _combine_kernel,
        grid_spec=grid_spec,
        out_shape=jax.ShapeDtypeStruct((N, D), F32),
        compiler_params=pltpu.CompilerParams(dimension_semantics=("arbitrary",),
                                             vmem_limit_bytes=VMEM_LIMIT),
        name="combine",
    )(dest, x1, gates, ys)


def _routing(top_idx, n_tok):
    tm, bpi = MOE_TM, MOE_BPI
    nk = n_tok * TOP_K
    experts = jnp.arange(N_EXPERTS, dtype=jnp.int32)
    e_flat = top_idx.reshape(-1)
    onehot = (e_flat[:, None] == experts[None, :]).astype(jnp.int32)
    csum = jnp.cumsum(onehot, axis=0)
    rank = jnp.sum(csum * onehot, axis=1) - 1
    counts = csum[-1]
    nb = (counts + tm - 1) // tm
    bend = jnp.cumsum(nb)
    bstart = bend - nb
    dest = (jnp.sum(onehot * bstart[None, :], axis=1) * tm + rank).astype(jnp.int32)
    n_blocks = -(-(nk + N_EXPERTS * (tm - 1)) // tm)
    n_rows = n_blocks * tm
    tail_blk = bend[-1:].astype(jnp.int32)
    zb_e = jnp.where(nb > 0, bend - 1, -1)
    zb_t = jnp.arange(n_blocks, dtype=jnp.int32)
    zb_all = jnp.concatenate([zb_e, jnp.where(zb_t >= bend[-1], zb_t, -1)]).astype(jnp.int32)
    order = jnp.argsort(zb_all < 0, stable=True)
    zero_blocks = zb_all[order]
    n_zero = jnp.sum(zb_all >= 0).astype(jnp.int32).reshape(1)
    n_items = n_blocks // bpi + N_EXPERTS
    items_e = (nb + bpi - 1) // bpi
    iend = jnp.cumsum(items_e)
    istart = iend - items_e
    slot = jnp.arange(n_items, dtype=jnp.int32)
    valid = slot < iend[-1]
    exp_of = jnp.minimum(jnp.searchsorted(iend, slot, side="right"), N_EXPERTS - 1).astype(jnp.int32)
    last_e = jnp.max(jnp.where(nb > 0, experts, 0))
    local = slot - istart[exp_of]
    n_it = jnp.maximum(items_e[exp_of], 1)
    base, rem = nb[exp_of] // n_it, nb[exp_of] % n_it
    first_blk = bstart[exp_of] + local * base + jnp.minimum(local, rem)
    item_e = jnp.where(valid, exp_of, last_e).astype(jnp.int32)
    item_start = jnp.where(valid, first_blk * tm, 0).astype(jnp.int32)
    item_nblk = jnp.where(valid, base + (local < rem), 0).astype(jnp.int32)
    return dest, zero_blocks, n_zero, item_e, item_start, item_nblk, tail_blk, n_rows


def _rope_lane_tile(v):
    half = MLA_ROPE // 2
    z = jnp.zeros((half,), v.dtype)
    return jnp.concatenate([v[:half], z, v[half:], z]).reshape(1, LANES)


def _rope_cols(w):
    half = MLA_ROPE // 2
    z = jnp.zeros((w.shape[0], half), w.dtype)
    return jnp.concatenate([w[:, :half], z, w[:, half:], z], axis=1)


def _layer(x, mem, positions, attn_norm, w_in, q_a_norm, w_q_b, kv_a_norm, w_kv_b,
           q_nope_norm, q_rope_norm, k_nope_norm, k_rope_norm,
           gmlp_v_norm, w_spatial, b_spatial,
           mem_norm, w_mem_kv, mem_q_norm, mem_k_norm,
           mla_out_norm, gmlp_out_norm, mem_out_norm, w_o,
           ffn_norm, w_router, b_router, w_gate_up, b_gate_up, w_down, b_down):
    B, S, D = x.shape
    N = B * S
    x2 = x.reshape(N, D)

    o1 = Q_LORA
    o2 = o1 + KV_LORA
    o3 = o2 + MLA_ROPE
    o4 = o3 + 2 * GMLP_WIDTH
    w_in_r = jnp.concatenate([w_in[:, :o2], w_in[:, o3:o4], w_in[:, o4:], _rope_cols(w_in[:, o2:o3])],
                             axis=1).astype(BF16)
    wq = w_q_b.reshape(Q_LORA, MLA_HEADS, MLA_NOPE + MLA_ROPE)
    wq_rope = jax.vmap(_rope_cols, in_axes=1, out_axes=1)(wq[:, :, MLA_NOPE:])
    wq_pad = jnp.concatenate([wq[:, :, :MLA_NOPE], wq_rope], axis=2).reshape(Q_LORA, MLA_HEADS * HEAD_PAD)
    wq_pad = wq_pad.astype(BF16)
    half = MLA_ROPE // 2
    inv_freq = ROPE_BASE ** (-jnp.arange(half, dtype=F32) / half)
    zf = jnp.zeros((half,), F32)
    invf_tile = jnp.concatenate([inv_freq, zf, inv_freq, zf]).reshape(1, LANES)
    sgn_tile = jnp.concatenate([-jnp.ones((half,), F32), zf, jnp.ones((half,), F32), zf]).reshape(1, LANES)
    bs_tile = jnp.repeat(b_spatial.T, GMLP_CH, axis=1)

    mk, mv = _mem_kv(mem, mem_norm, w_mem_kv, mem_k_norm)
    cq, ckv, kpe, ogm, cs = _inproj(
        x2, positions.reshape(N, 1), (invf_tile, sgn_tile), attn_norm, w_in_r, q_a_norm, kv_a_norm,
        _rope_lane_tile(k_rope_norm), gmlp_v_norm, w_spatial, bs_tile, mk, mv, mem_q_norm,
        gmlp_out_norm, mem_out_norm, S)
    q, k, v = _qkv(cq, ckv, kpe, cs, wq_pad, w_kv_b.astype(BF16), q_nope_norm,
                   _rope_lane_tile(q_rope_norm), k_nope_norm)
    o_mla = _attention(q, k, v, B, S).reshape(N, MLA_HEADS * MLA_V)

    wo = w_o.astype(BF16)
    wr_pad = jnp.pad(w_router, ((0, 0), (0, LANES - N_EXPERTS)))
    br_pad = jnp.pad(b_router, (0, LANES - N_EXPERTS)).reshape(1, LANES)
    x1, h2, ti, tg = _outproj(x2, o_mla, ogm, mla_out_norm, wo[:MLA_HEADS * MLA_V],
                              wo[MLA_HEADS * MLA_V:], ffn_norm, wr_pad, br_pad)

    dest, zero_blocks, n_zero, item_e, item_start, item_nblk, tail_blk, n_rows = _routing(
        ti[:, :TOP_K], N)
    xs = _dispatch(h2, dest, zero_blocks, n_zero, n_rows)
    ys = _moe(xs, item_e, item_start, item_nblk, tail_blk, w_gate_up, b_gate_up, w_down, b_down)
    out = _combine(x1, tg, ys, dest)
    return out.reshape(B, S, D)


def kernel(x, mem, positions, attn_norm, w_in, q_a_norm, w_q_b, kv_a_norm, w_kv_b, q_nope_norm, q_rope_norm, k_nope_norm, k_rope_norm, gmlp_v_norm, w_spatial, b_spatial, mem_norm, w_mem_kv, mem_q_norm, mem_k_norm, mla_out_norm, gmlp_out_norm, mem_out_norm, w_o, ffn_norm, w_router, b_router, w_gate_up, b_gate_up, w_down, b_down):
    depth = attn_norm.shape[0]
    for l in range(depth):
        x = _layer(x, mem, positions, attn_norm[l], w_in[l], q_a_norm[l], w_q_b[l], kv_a_norm[l],
                   w_kv_b[l], q_nope_norm[l], q_rope_norm[l], k_nope_norm[l], k_rope_norm[l],
                   gmlp_v_norm[l], w_spatial[l], b_spatial[l], mem_norm[l], w_mem_kv[l],
                   mem_q_norm[l], mem_k_norm[l], mla_out_norm[l], gmlp_out_norm[l],
                   mem_out_norm[l], w_o[l], ffn_norm[l], w_router[l], b_router[l], w_gate_up[l],
                   b_gate_up[l], w_down[l], b_down[l])
    return x
```

```python
import functools

import jax
import jax.numpy as jnp
import numpy as np
from jax import lax
from jax.experimental import pallas as pl
from jax.experimental.pallas import tpu as pltpu

F32 = jnp.float32
BF16 = jnp.bfloat16

EPS = 1e-6
LANES = 128
SUBLANES = 8
VMEM_LIMIT = 56 * 1024 * 1024

MLA_HEADS = 8
MLA_NOPE = 128
MLA_ROPE = 64
MLA_V = 128
Q_LORA = 512
KV_LORA = 512
GMLP_GROUPS = 4
GMLP_CH = 128
GMLP_WIDTH = GMLP_GROUPS * GMLP_CH
CHUNK = 128
MEM_HEADS = 4
MEM_HEAD_DIM = 128
MEM_WIDTH = MEM_HEADS * MEM_HEAD_DIM
N_EXPERTS = 32
TOP_K = 4
SWIGLU_ALPHA = 1.702
SWIGLU_LIMIT = 7.0
ROPE_BASE = 10000.0
HEAD_PAD = 2 * LANES

TOK_TILE = 512
ROW_GROUPS = 1
IN_GROUPS = 1
ATT_TQ = 512
ATT_TK = 512
ATT_HPS = 4
MOE_TM = 128
MOE_BPI = 10
MOE_TF = 256
MOE_WSLOTS = 3
DSP_TILE = 1024
CMB_TILE = 256


def _rms(x, g, n=None):
    n = x.shape[-1] if n is None else n
    ms = jnp.sum(x * x, axis=-1, keepdims=True) * (1.0 / n)
    return x * lax.rsqrt(ms + EPS) * g


def _gelu(x):
    return 0.5 * x * (1.0 + lax.erf(x * (2.0 ** -0.5)))


def _dot(a, b):
    return jnp.dot(a, b, preferred_element_type=F32)


def _dot_nt(a, b):
    return lax.dot_general(a, b, (((1,), (1,)), ((), ())), preferred_element_type=F32)


def _const_spec(shape):
    nd = len(shape)
    return pl.BlockSpec(shape, lambda *_: (0,) * nd)


def _mem_kv_kernel(mem_ref, g_ref, w_ref, kg_ref, mk_ref, mv_ref):
    m = mem_ref[0]
    hn = _rms(m, g_ref[...]).astype(BF16)
    kv = _dot(hn, w_ref[...])
    for h in range(MEM_HEADS):
        k = kv[:, h * MEM_HEAD_DIM:(h + 1) * MEM_HEAD_DIM]
        mk_ref[0, :, h * MEM_HEAD_DIM:(h + 1) * MEM_HEAD_DIM] = _rms(k, kg_ref[...]).astype(BF16)
    mv_ref[0] = kv[:, MEM_WIDTH:].astype(BF16)


def _mem_kv(mem, mem_norm, w_mem_kv, mem_k_norm):
    B, M, D = mem.shape
    return pl.pallas_call(
        _mem_kv_kernel,
        grid=(B,),
        in_specs=[pl.BlockSpec((1, M, D), lambda b: (b, 0, 0)),
                  _const_spec((1, D)),
                  _const_spec((D, 2 * MEM_WIDTH)),
                  _const_spec((1, MEM_HEAD_DIM))],
        out_specs=[pl.BlockSpec((1, M, MEM_WIDTH), lambda b: (b, 0, 0)),
                   pl.BlockSpec((1, M, MEM_WIDTH), lambda b: (b, 0, 0))],
        out_shape=[jax.ShapeDtypeStruct((B, M, MEM_WIDTH), BF16),
                   jax.ShapeDtypeStruct((B, M, MEM_WIDTH), BF16)],
        compiler_params=pltpu.CompilerParams(dimension_semantics=("arbitrary",),
                                             vmem_limit_bytes=VMEM_LIMIT),
        name="mem_kv",
    )(mem, mem_norm.reshape(1, D), w_mem_kv.astype(BF16), mem_k_norm.reshape(1, MEM_HEAD_DIM))


_O_CQ = 0
_O_CKV = _O_CQ + Q_LORA
_O_U = _O_CKV + KV_LORA
_O_VG = _O_U + GMLP_WIDTH
_O_QM = _O_VG + GMLP_WIDTH
_O_KR = _O_QM + MEM_WIDTH
_IN_COLS_PAD = _O_KR + LANES


def _rope_tile(t, cos, sin_signed):
    return t * cos + pltpu.roll(t, LANES // 2, axis=1) * sin_signed


def _inproj_kernel(x_ref, pos_ref, invf_ref, sgn_ref, an_ref, w_ref, qan_ref, kvan_ref, krn_ref,
                   gvn_ref, ws_ref, bs_ref, mk_ref, mv_ref, mqn_ref, gon_ref, mon_ref,
                   cq_ref, ckv_ref, kpe_ref, ogm_ref, cs_ref):
    def project(sl):
        h = _rms(x_ref[sl, :], an_ref[...]).astype(BF16)
        return _dot(h, w_ref[...])

    def finish(sl, n, z):
        cq_ref[sl, :] = _rms(z[:, _O_CQ:_O_CQ + Q_LORA], qan_ref[...]).astype(BF16)
        ckv_ref[sl, :] = _rms(z[:, _O_CKV:_O_CKV + KV_LORA], kvan_ref[...]).astype(BF16)

        ang = pos_ref[sl, :].astype(F32) * invf_ref[...]
        cos = jnp.cos(ang)
        sin_signed = jnp.sin(ang) * sgn_ref[...]
        cs_ref[sl, :LANES] = cos
        cs_ref[sl, LANES:] = sin_signed
        kr = _rms(z[:, _O_KR:_O_KR + LANES], krn_ref[...], MLA_ROPE)
        kpe_ref[sl, :] = _rope_tile(kr, cos, sin_signed).astype(BF16)

        u = _gelu(z[:, _O_U:_O_U + GMLP_WIDTH])
        vg = _gelu(z[:, _O_VG:_O_VG + GMLP_WIDTH])
        vg = _rms(vg, gvn_ref[...]).astype(BF16)
        row = lax.broadcasted_iota(jnp.int32, (CHUNK, CHUNK), 0)
        col = lax.broadcasted_iota(jnp.int32, (CHUNK, CHUNK), 1)
        sp_cols = []
        for g in range(GMLP_GROUPS):
            wsg = jnp.where(col <= row, ws_ref[g], 0.0).astype(BF16)
            sp_rows = [_dot(wsg, vg[c * CHUNK:(c + 1) * CHUNK, g * GMLP_CH:(g + 1) * GMLP_CH])
                       for c in range(n // CHUNK)]
            sp_cols.append(jnp.concatenate(sp_rows, axis=0))
        sp = jnp.concatenate(sp_cols, axis=1) + jnp.concatenate([bs_ref[...]] * (n // CHUNK), axis=0)
        ogm_ref[sl, :GMLP_WIDTH] = _rms(u * sp, gon_ref[...]).astype(BF16)

        o_heads = []
        for hd in range(MEM_HEADS):
            hs = slice(hd * MEM_HEAD_DIM, (hd + 1) * MEM_HEAD_DIM)
            q = z[:, _O_QM + hd * MEM_HEAD_DIM:_O_QM + (hd + 1) * MEM_HEAD_DIM]
            qn = (_rms(q, mqn_ref[...]) * (MEM_HEAD_DIM ** -0.5)).astype(BF16)
            s = _dot_nt(qn, mk_ref[0, :, hs])
            s = s - jnp.max(s, axis=-1, keepdims=True)
            p = jnp.exp(s)
            p = p / jnp.sum(p, axis=-1, keepdims=True)
            o_heads.append(_dot(p.astype(BF16), mv_ref[0, :, hs]))
        o_mem = jnp.concatenate(o_heads, axis=1)
        ogm_ref[sl, GMLP_WIDTH:] = _rms(o_mem, mon_ref[...]).astype(BF16)

    n = x_ref.shape[0] // IN_GROUPS
    groups = [pl.ds(r * n, n) for r in range(IN_GROUPS)]
    zs = [project(sl) for sl in groups]
    for sl, z in zip(groups, zs):
        finish(sl, n, z)


def _inproj(x2, pos, tables, attn_norm, w_in_r, q_a_norm, kv_a_norm, krn_tile, gmlp_v_norm,
            w_spatial, bs_tile, mk, mv, mem_q_norm, gmlp_out_norm, mem_out_norm, seq):
    N, D = x2.shape
    tm = TOK_TILE
    tiles_per_seq = seq // tm
    invf_tile, sgn_tile = tables
    M = mk.shape[1]
    row = lambda w: pl.BlockSpec((tm, w), lambda i: (i, 0))
    batch_blk = pl.BlockSpec((1, M, MEM_WIDTH), lambda i: (i // tiles_per_seq, 0, 0))
    return pl.pallas_call(
        _inproj_kernel,
        grid=(N // tm,),
        in_specs=[row(D), row(1), _const_spec((1, LANES)), _const_spec((1, LANES)),
                  _const_spec((1, D)), _const_spec((D, _IN_COLS_PAD)),
                  _const_spec((1, Q_LORA)), _const_spec((1, KV_LORA)), _const_spec((1, LANES)),
                  _const_spec((1, GMLP_WIDTH)), _const_spec((GMLP_GROUPS, CHUNK, CHUNK)),
                  _const_spec((CHUNK, GMLP_WIDTH)), batch_blk, batch_blk,
                  _const_spec((1, MEM_HEAD_DIM)), _const_spec((1, GMLP_WIDTH)),
                  _const_spec((1, MEM_WIDTH))],
        out_specs=[row(Q_LORA), row(KV_LORA), row(LANES), row(GMLP_WIDTH + MEM_WIDTH),
                   row(2 * LANES)],
        out_shape=[jax.ShapeDtypeStruct((N, Q_LORA), BF16),
                   jax.ShapeDtypeStruct((N, KV_LORA), BF16),
                   jax.ShapeDtypeStruct((N, LANES), BF16),
                   jax.ShapeDtypeStruct((N, GMLP_WIDTH + MEM_WIDTH), BF16),
                   jax.ShapeDtypeStruct((N, 2 * LANES), F32)],
        compiler_params=pltpu.CompilerParams(dimension_semantics=("arbitrary",),
                                             vmem_limit_bytes=VMEM_LIMIT),
        name="inproj",
    )(x2, pos, invf_tile, sgn_tile, attn_norm.reshape(1, D), w_in_r,
      q_a_norm.reshape(1, -1), kv_a_norm.reshape(1, -1), krn_tile, gmlp_v_norm.reshape(1, -1),
      w_spatial, bs_tile, mk, mv, mem_q_norm.reshape(1, -1), gmlp_out_norm.reshape(1, -1),
      mem_out_norm.reshape(1, -1))


def _qkv_kernel(cq_ref, ckv_ref, kpe_ref, cs_ref, wq_ref, wkv_ref, qnn_ref, qrn_ref, knn_ref,
                q_ref, k_ref, v_ref):
    cos = cs_ref[:, :LANES]
    sin_signed = cs_ref[:, LANES:]
    scale = (MLA_NOPE + MLA_ROPE) ** -0.5 * np.log2(np.e)
    qr = _dot(cq_ref[...], wq_ref[...])
    kvr = _dot(ckv_ref[...], wkv_ref[...])
    kpe = kpe_ref[...]
    for h in range(MLA_HEADS):
        o = h * HEAD_PAD
        qn = _rms(qr[:, o:o + MLA_NOPE], qnn_ref[...]) * scale
        qt = _rms(qr[:, o + MLA_NOPE:o + HEAD_PAD], qrn_ref[...], MLA_ROPE)
        qt = _rope_tile(qt, cos, sin_signed) * scale
        q_ref[:, o:o + MLA_NOPE] = qn.astype(BF16)
        q_ref[:, o + MLA_NOPE:o + HEAD_PAD] = qt.astype(BF16)
        ko = h * (MLA_NOPE + MLA_V)
        k_ref[:, o:o + MLA_NOPE] = _rms(kvr[:, ko:ko + MLA_NOPE], knn_ref[...]).astype(BF16)
        k_ref[:, o + MLA_NOPE:o + HEAD_PAD] = kpe
        v_ref[:, h * MLA_V:(h + 1) * MLA_V] = kvr[:, ko + MLA_NOPE:ko + MLA_NOPE + MLA_V].astype(BF16)


def _qkv(cq, ckv, kpe, cs, wq_pad, wkv, q_nope_norm, qrn_tile, k_nope_norm):
    N = cq.shape[0]
    tm = TOK_TILE
    row = lambda w: pl.BlockSpec((tm, w), lambda i: (i, 0))
    return pl.pallas_call(
        _qkv_kernel,
        grid=(N // tm,),
        in_specs=[row(Q_LORA), row(KV_LORA), row(LANES), row(2 * LANES),
                  _const_spec(wq_pad.shape), _const_spec(wkv.shape),
                  _const_spec((1, MLA_NOPE)), _const_spec((1, LANES)), _const_spec((1, MLA_NOPE))],
        out_specs=[row(MLA_HEADS * HEAD_PAD), row(MLA_HEADS * HEAD_PAD), row(MLA_HEADS * MLA_V)],
        out_shape=[jax.ShapeDtypeStruct((N, MLA_HEADS * HEAD_PAD), BF16),
                   jax.ShapeDtypeStruct((N, MLA_HEADS * HEAD_PAD), BF16),
                   jax.ShapeDtypeStruct((N, MLA_HEADS * MLA_V), BF16)],
        compiler_params=pltpu.CompilerParams(dimension_semantics=("arbitrary",),
                                             vmem_limit_bytes=VMEM_LIMIT),
        name="qkv",
    )(cq, ckv, kpe, cs, wq_pad, wkv, q_nope_norm.reshape(1, -1), qrn_tile,
      k_nope_norm.reshape(1, -1))


def _attn_kernel(q_ref, k_ref, v_ref, o_ref):
    tq, tk = ATT_TQ, ATT_TK
    qi = pl.program_id(2)
    qs = [q_ref[0, :, h * HEAD_PAD:(h + 1) * HEAD_PAD] for h in range(ATT_HPS)]

    def scores(h, kb):
        k0 = pl.multiple_of(kb * tk, tk)
        return _dot_nt(k_ref[0, pl.ds(k0, tk), h * HEAD_PAD:(h + 1) * HEAD_PAD], qs[h])

    def head_step(h, kb, s, carry, masked):
        m, l, acc = carry
        k0 = pl.multiple_of(kb * tk, tk)
        if masked:
            kpos = kb * tk + lax.broadcasted_iota(jnp.int32, (tk, tq), 0)
            qpos = qi * tq + lax.broadcasted_iota(jnp.int32, (tk, tq), 1)
            s = jnp.where(kpos <= qpos, s, -jnp.inf)
        m_new = jnp.maximum(m, jnp.max(s, axis=0, keepdims=True))
        alpha = jnp.exp2(m - m_new)
        p = jnp.exp2(s - m_new)
        l = alpha * l + jnp.sum(p, axis=0, keepdims=True)
        v = v_ref[0, pl.ds(k0, tk), h * MLA_V:(h + 1) * MLA_V]
        pv = lax.dot_general(v, p.astype(BF16), (((0,), (0,)), ((), ())),
                             preferred_element_type=F32)
        return m_new, l, alpha * acc + pv

    def step(kb, carries, masked):
        ss = [scores(h, kb) for h in range(ATT_HPS)]
        return tuple(head_step(h, kb, ss[h], carries[h], masked) for h in range(ATT_HPS))

    init = tuple((jnp.full((1, tq), -jnp.inf, F32), jnp.zeros((1, tq), F32),
                  jnp.zeros((MLA_V, tq), F32)) for _ in range(ATT_HPS))
    n_full = (qi * tq) // tk
    carries = lax.fori_loop(0, n_full, lambda kb, c: step(kb, c, False), init)
    carries = step(n_full, carries, True)
    for h in range(ATT_HPS):
        _, l, acc = carries[h]
        o_ref[0, :, h * MLA_V:(h + 1) * MLA_V] = jnp.transpose(acc / l).astype(BF16)


def _attention(q, k, v, batch, seq):
    q3 = q.reshape(batch, seq, MLA_HEADS * HEAD_PAD)
    k3 = k.reshape(batch, seq, MLA_HEADS * HEAD_PAD)
    v3 = v.reshape(batch, seq, MLA_HEADS * MLA_V)
    hp, hv = ATT_HPS * HEAD_PAD, ATT_HPS * MLA_V
    assert ATT_TK % ATT_TQ == 0 and seq % ATT_TK == 0
    return pl.pallas_call(
        _attn_kernel,
        grid=(batch, MLA_HEADS // ATT_HPS, seq // ATT_TQ),
        in_specs=[pl.BlockSpec((1, ATT_TQ, hp), lambda b, h, i: (b, i, h)),
                  pl.BlockSpec((1, seq, hp), lambda b, h, i: (b, 0, h)),
                  pl.BlockSpec((1, seq, hv), lambda b, h, i: (b, 0, h))],
        out_specs=pl.BlockSpec((1, ATT_TQ, hv), lambda b, h, i: (b, i, h)),
        out_shape=jax.ShapeDtypeStruct((batch, seq, MLA_HEADS * MLA_V), BF16),
        compiler_params=pltpu.CompilerParams(
            dimension_semantics=("arbitrary", "arbitrary", "arbitrary"),
            vmem_limit_bytes=VMEM_LIMIT),
        name="attn",
    )(q3, k3, v3)


def _outproj_kernel(x_ref, oa_ref, ogm_ref, aon_ref, woa_ref, wob_ref, fn_ref, wr_ref, br_ref,
                    x1_ref, h2_ref, ti_ref, tg_ref):
    wr = wr_ref[...]
    w_hi = wr.astype(BF16)
    w_lo = (wr - w_hi.astype(F32)).astype(BF16)
    w_cat = jnp.concatenate([w_hi, w_lo], axis=1)

    def rows(sl, n):
        oa = _rms(oa_ref[sl, :].astype(F32), aon_ref[...]).astype(BF16)
        x1 = x_ref[sl, :] + _dot(oa, woa_ref[...]) + _dot(ogm_ref[sl, :], wob_ref[...])
        x1_ref[sl, :] = x1
        h2 = _rms(x1, fn_ref[...])
        h2_ref[sl, :] = h2
        h_hi = h2.astype(BF16)
        h_lo = (h2 - h_hi.astype(F32)).astype(BF16)
        hh = _dot(h_hi, w_cat)
        logits = hh[:, :LANES] + (_dot(h_lo, w_hi) + hh[:, LANES:]) + br_ref[...]
        lane = lax.broadcasted_iota(jnp.int32, (n, LANES), 1)
        lg = jnp.where(lane < N_EXPERTS, logits, -jnp.inf)
        vals, idxs = [], []
        for _ in range(TOP_K):
            m = jnp.max(lg, axis=-1, keepdims=True)
            am = jnp.min(jnp.where(lg == m, lane, LANES), axis=-1, keepdims=True)
            vals.append(m)
            idxs.append(am)
            lg = jnp.where(lane == am, -jnp.inf, lg)
        es = [jnp.exp(v - vals[0]) for v in vals]
        denom = es[0] + es[1] + es[2] + es[3]
        ti = jnp.zeros((n, LANES), jnp.int32)
        tg = jnp.zeros((n, LANES), F32)
        for kk in range(TOP_K):
            ti = jnp.where(lane == kk, idxs[kk], ti)
            tg = jnp.where(lane == kk, es[kk] / denom, tg)
        ti_ref[sl, :] = ti
        tg_ref[sl, :] = tg

    n = x_ref.shape[0] // ROW_GROUPS
    for r in range(ROW_GROUPS):
        rows(pl.ds(r * n, n), n)


def _outproj(x2, o_mla, ogm, mla_out_norm, wo_a, wo_b, ffn_norm, wr_pad, br_pad):
    N, D = x2.shape
    tm = TOK_TILE
    row = lambda w: pl.BlockSpec((tm, w), lambda i: (i, 0))
    wa = o_mla.shape[1]
    wb = ogm.shape[1]
    return pl.pallas_call(
        _outproj_kernel,
        grid=(N // tm,),
        in_specs=[row(D), row(wa), row(wb), _const_spec((1, wa)), _const_spec((wa, D)),
                  _const_spec((wb, D)), _const_spec((1, D)), _const_spec((D, LANES)),
                  _const_spec((1, LANES))],
        out_specs=[row(D), row(D), row(LANES), row(LANES)],
        out_shape=[jax.ShapeDtypeStruct((N, D), F32), jax.ShapeDtypeStruct((N, D), F32),
                   jax.ShapeDtypeStruct((N, LANES), jnp.int32),
                   jax.ShapeDtypeStruct((N, LANES), F32)],
        compiler_params=pltpu.CompilerParams(dimension_semantics=("arbitrary",),
                                             vmem_limit_bytes=VMEM_LIMIT),
        name="outproj",
    )(x2, o_mla, ogm, mla_out_norm.reshape(1, -1), wo_a, wo_b, ffn_norm.reshape(1, -1), wr_pad,
      br_pad)


def _pack_bf16_pairs(x):
    k = x.shape[1] // 2
    lo = pltpu.bitcast(x[:, :k].astype(BF16).astype(F32), jnp.uint32)
    hi = pltpu.bitcast(x[:, k:].astype(BF16).astype(F32), jnp.uint32)
    return hi | (lo >> 16)


def _unpack_bf16_pairs(w):
    lo = pltpu.bitcast(w << 16, F32).astype(BF16)
    hi = pltpu.bitcast(w & jnp.uint32(0xFFFF0000), F32).astype(BF16)
    return jnp.concatenate([lo, hi], axis=1)


def _dispatch_kernel(dest_ref, zb_ref, nzb_ref, h2_ref, xs_hbm, pk, zblk, sem, sem_z):
    tm = h2_ref.shape[0]
    bm = MOE_TM
    i = pl.program_id(0)
    base = i * (tm * TOP_K)
    pk[...] = _pack_bf16_pairs(h2_ref[...])

    @pl.when(i == 0)
    def _zero_blocks():
        zblk[...] = jnp.zeros(zblk.shape, zblk.dtype)

        def zcopy(n):
            d0 = pl.multiple_of(zb_ref[n] * bm, bm)
            return pltpu.make_async_copy(zblk, xs_hbm.at[pl.ds(d0, bm), :], sem_z)

        def issue(n, c):
            zcopy(n).start()
            return c
        lax.fori_loop(0, nzb_ref[0], issue, 0)

        def finish(n, c):
            zcopy(n).wait()
            return c
        lax.fori_loop(0, nzb_ref[0], finish, 0)

    def issue(g, c):
        r0 = pl.multiple_of(g * SUBLANES, SUBLANES)
        rows = pk.at[pl.ds(r0, SUBLANES), :]
        for s in range(SUBLANES):
            for kk in range(TOP_K):
                d = dest_ref[base + (r0 + s) * TOP_K + kk]
                pltpu.make_async_copy(rows.at[pl.ds(s, 1), :], xs_hbm.at[pl.ds(d, 1), :], sem).start()
        return c
    lax.fori_loop(0, tm // SUBLANES, issue, 0)
    for kk in range(TOP_K):
        pltpu.make_async_copy(pk, xs_hbm.at[pl.ds(0, tm), :], sem).wait()


def _dispatch(h2, dest, zero_blocks, n_zero, n_rows):
    N, D = h2.shape
    tm = DSP_TILE
    grid_spec = pltpu.PrefetchScalarGridSpec(
        num_scalar_prefetch=3,
        grid=(N // tm,),
        in_specs=[pl.BlockSpec((tm, D), lambda i, *_: (i, 0))],
        out_specs=pl.BlockSpec(memory_space=pl.ANY),
        scratch_shapes=[pltpu.VMEM((tm, D // 2), jnp.uint32), pltpu.VMEM((MOE_TM, D // 2), jnp.uint32),
                        pltpu.SemaphoreType.DMA(()), pltpu.SemaphoreType.DMA(())],
    )
    return pl.pallas_call(
        _dispatch_kernel,
        grid_spec=grid_spec,
        out_shape=jax.ShapeDtypeStruct((n_rows, D // 2), jnp.uint32),
        compiler_params=pltpu.CompilerParams(dimension_semantics=("arbitrary",),
                                             vmem_limit_bytes=VMEM_LIMIT),
        name="dispatch",
    )(dest, zero_blocks, n_zero, h2)


def _moe_kernel(ie_ref, ist_ref, inb_ref, tail_ref,
                xs_hbm, wgu_hbm, wd_hbm, bgu_ref, bd_ref,
                ys_hbm,
                xbuf, acc, wg_buf, wu_buf, wd_buf, zblk, sem_x, sem_o, sem_w, sem_z):
    tm = MOE_TM
    i = pl.program_id(0)
    n_items = pl.num_programs(0)
    nblk = inb_ref[i]
    slot = i % 2
    n_blocks = ys_hbm.shape[0] // tm
    tf = wg_buf.shape[2]
    nj = wd_hbm.shape[1] // tf

    def x_copy(it, m):
        s0 = pl.multiple_of(ist_ref[it] + m * tm, tm)
        r0 = pl.multiple_of(m * tm, tm)
        return pltpu.make_async_copy(xs_hbm.at[pl.ds(s0, tm), :],
                                     xbuf.at[it % 2, pl.ds(r0, tm), :], sem_x.at[it % 2])

    def y_copy(it, m):
        d0 = pl.multiple_of(ist_ref[it] + m * tm, tm)
        r0 = pl.multiple_of(m * tm, tm)
        return pltpu.make_async_copy(acc.at[it % 2, pl.ds(r0, tm), :],
                                     ys_hbm.at[pl.ds(d0, tm), :], sem_o.at[it % 2])

    def for_blocks(it, fn):
        def body(m, c):
            fn(it, m)
            return c
        lax.fori_loop(0, inb_ref[it], body, 0)

    start_x = lambda it: for_blocks(it, lambda a, m: x_copy(a, m).start())
    wait_x = lambda it: for_blocks(it, lambda a, m: x_copy(a, m).wait())
    start_y = lambda it: for_blocks(it, lambda a, m: y_copy(a, m).start())
    wait_y = lambda it: for_blocks(it, lambda a, m: y_copy(a, m).wait())

    def zero_copy(b):
        d0 = pl.multiple_of(b * tm, tm)
        return pltpu.make_async_copy(zblk, ys_hbm.at[pl.ds(d0, tm), :], sem_z)

    n_ws = wg_buf.shape[0]

    def w_slot(it, j):
        return (it * nj + j) % n_ws

    def w_copies(it, j):
        e = ie_ref[it]
        ws = w_slot(it, j)
        c0 = pl.multiple_of(j * tf, tf)
        c1 = pl.multiple_of(nj * tf + j * tf, tf)
        return (pltpu.make_async_copy(wgu_hbm.at[e, :, pl.ds(c0, tf)], wg_buf.at[ws], sem_w.at[ws]),
                pltpu.make_async_copy(wgu_hbm.at[e, :, pl.ds(c1, tf)], wu_buf.at[ws], sem_w.at[ws]),
                pltpu.make_async_copy(wd_hbm.at[e, pl.ds(c0, tf), :], wd_buf.at[ws], sem_w.at[ws]))

    def start_w(it, j):
        for cp in w_copies(it, j):
            cp.start()

    def wait_w(it, j):
        for cp in w_copies(it, j):
            cp.wait()

    @pl.when(i == 0)
    def _first_step():
        zblk[...] = jnp.zeros(zblk.shape, zblk.dtype)

        def issue(b, c):
            zero_copy(b).start()
            return c
        lax.fori_loop(tail_ref[0], n_blocks, issue, 0)
        start_x(0)

        @pl.when(nblk > 0)
        def _():
            for a in range(n_ws - 1):
                start_w(0, a)

    wait_x(i)

    @pl.when(i + 1 < n_items)
    def _():
        start_x(i + 1)

    @pl.when(i >= 2)
    def _():
        wait_y(i - 2)

    def init(m, c):
        r0 = pl.multiple_of(m * tm, tm)
        acc[slot, pl.ds(r0, tm), :] = jnp.broadcast_to(bd_ref[0], (tm, acc.shape[2]))
        return c
    lax.fori_loop(0, nblk, init, 0)

    def ffn_rows(j, n):
        ws = w_slot(i, j)
        x = _unpack_bf16_pairs(xbuf[slot, pl.ds(0, n), :])
        g = jnp.minimum(_dot(x, wg_buf[ws].astype(BF16)) + bgu_ref[0, pl.ds(j, 1), :], SWIGLU_LIMIT)
        u = jnp.clip(_dot(x, wu_buf[ws].astype(BF16)) + bgu_ref[0, pl.ds(nj + j, 1), :],
                     -SWIGLU_LIMIT, SWIGLU_LIMIT)
        a = (u + 1.0) * (g * jax.nn.sigmoid(SWIGLU_ALPHA * g))
        acc[slot, pl.ds(0, n), :] += _dot(a.astype(BF16), wd_buf[ws].astype(BF16))

    def chunk(j, c):
        wait_w(i, j)
        ja = j + (n_ws - 1)

        @pl.when(ja < nj)
        def _():
            start_w(i, ja)

        @pl.when(jnp.logical_and(ja >= nj, i + 1 < n_items))
        def _():
            @pl.when(inb_ref[i + 1] > 0)
            def _():
                start_w(i + 1, ja - nj)

        for nb in range(1, MOE_BPI + 1):
            pl.when(nblk == nb)(functools.partial(ffn_rows, j, nb * tm))
        return c

    @pl.when(nblk > 0)
    def _chunks():
        lax.fori_loop(0, nj, chunk, 0)

    start_y(i)

    @pl.when(i == n_items - 1)
    def _last_step():
        @pl.when(i >= 1)
        def _():
            wait_y(i - 1)
        wait_y(i)

        def finish(b, c):
            zero_copy(b).wait()
            return c
        lax.fori_loop(tail_ref[0], n_blocks, finish, 0)


def _moe(xs, item_e, item_start, item_nblk, tail_blk, w_gate_up, b_gate_up, w_down, b_down):
    n_rows = xs.shape[0]
    E, D, F2 = w_gate_up.shape
    F = F2 // 2
    tf = MOE_TF
    nj = F // tf
    nw = MOE_WSLOTS
    assert 2 <= nw <= nj + 1
    n_items = item_e.shape[0]
    rows = MOE_BPI * MOE_TM
    any_spec = pl.BlockSpec(memory_space=pl.ANY)
    grid_spec = pltpu.PrefetchScalarGridSpec(
        num_scalar_prefetch=4,
        grid=(n_items,),
        in_specs=[any_spec, any_spec, any_spec,
                  pl.BlockSpec((1, 2 * nj, tf), lambda i, ie, ist, inb, tail: (ie[i], 0, 0)),
                  pl.BlockSpec((1, 1, D), lambda i, ie, ist, inb, tail: (ie[i], 0, 0))],
        out_specs=any_spec,
        scratch_shapes=[pltpu.VMEM((2, rows, D // 2), jnp.uint32), pltpu.VMEM((2, rows, D), F32),
                        pltpu.VMEM((nw, D, tf), F32), pltpu.VMEM((nw, D, tf), F32),
                        pltpu.VMEM((nw, tf, D), F32), pltpu.VMEM((MOE_TM, D), F32),
                        pltpu.SemaphoreType.DMA((2,)), pltpu.SemaphoreType.DMA((2,)),
                        pltpu.SemaphoreType.DMA((nw,)), pltpu.SemaphoreType.DMA(())],
    )
    return pl.pallas_call(
        _moe_kernel,
        grid_spec=grid_spec,
        out_shape=jax.ShapeDtypeStruct((n_rows, D), F32),
        compiler_params=pltpu.CompilerParams(dimension_semantics=("arbitrary",),
                                             vmem_limit_bytes=VMEM_LIMIT),
        name="moe",
    )(item_e, item_start, item_nblk, tail_blk, xs, w_gate_up, w_down,
      b_gate_up.reshape(E, 2 * nj, tf), b_down.reshape(E, 1, D))


def _combine_kernel(pos_ref, x1_ref, g_ref, ys_hbm, o_ref, buf, sem):
    tm = CMB_TILE
    i = pl.program_id(0)

    def start_gather(t):
        base = t * (tm * TOP_K)
        b = t % 2

        def issue(g, c):
            r0 = pl.multiple_of(g * SUBLANES, SUBLANES)
            for s in range(SUBLANES):
                for kk in range(TOP_K):
                    p = pos_ref[base + (r0 + s) * TOP_K + kk]
                    pltpu.make_async_copy(ys_hbm.at[pl.ds(p, 1), :],
                                          buf.at[b, kk, pl.ds(r0, SUBLANES), :].at[pl.ds(s, 1), :],
                                          sem.at[b]).start()
            return c
        lax.fori_loop(0, tm // SUBLANES, issue, 0)

    @pl.when(i == 0)
    def _():
        start_gather(0)

    @pl.when(i + 1 < pl.num_programs(0))
    def _():
        start_gather(i + 1)

    b = i % 2
    for kk in range(TOP_K):
        pltpu.make_async_copy(ys_hbm.at[pl.ds(0, tm), :], buf.at[b, kk], sem.at[b]).wait()
    out = x1_ref[...]
    for kk in range(TOP_K):
        out = out + g_ref[:, kk:kk + 1] * buf[b, kk]
    o_ref[...] = out


def _combine(x1, gates, ys, dest):
    N, D = x1.shape
    tm = CMB_TILE
    grid_spec = pltpu.PrefetchScalarGridSpec(
        num_scalar_prefetch=1,
        grid=(N // tm,),
        in_specs=[pl.BlockSpec((tm, D), lambda i, pos: (i, 0)),
                  pl.BlockSpec((tm, LANES), lambda i, pos: (i, 0)),
                  pl.BlockSpec(memory_space=pl.ANY)],
        out_specs=pl.BlockSpec((tm, D), lambda i, pos: (i, 0)),
        scratch_shapes=[pltpu.VMEM((2, TOP_K, tm, D), F32), pltpu.SemaphoreType.DMA((2,))],
    )
    return pl.pallas_call(
        _combine_kernel,
        grid_spec=grid_spec,
        out_shape=jax.ShapeDtypeStruct((N, D), F32),
        compiler_params=pltpu.CompilerParams(dimension_semantics=("arbitrary",),
                                             vmem_limit_bytes=VMEM_LIMIT),
        name="combine",
    )(dest, x1, gates, ys)


def _routing(top_idx, n_tok):
    tm, bpi = MOE_TM, MOE_BPI
    nk = n_tok * TOP_K
    experts = jnp.arange(N_EXPERTS, dtype=jnp.int32)
    e_flat = top_idx.reshape(-1)
    onehot = (e_flat[:, None] == experts[None, :]).astype(jnp.int32)
    csum = jnp.cumsum(onehot, axis=0)
    rank = jnp.sum(csum * onehot, axis=1) - 1
    counts = csum[-1]
    nb = (counts + tm - 1) // tm
    bend = jnp.cumsum(nb)
    bstart = bend - nb
    dest = (jnp.sum(onehot * bstart[None, :], axis=1) * tm + rank).astype(jnp.int32)
    n_blocks = -(-(nk + N_EXPERTS * (tm - 1)) // tm)
    n_rows = n_blocks * tm
    tail_blk = bend[-1:].astype(jnp.int32)
    zb_e = jnp.where(nb > 0, bend - 1, -1)
    zb_t = jnp.arange(n_blocks, dtype=jnp.int32)
    zb_all = jnp.concatenate([zb_e, jnp.where(zb_t >= bend[-1], zb_t, -1)]).astype(jnp.int32)
    order = jnp.argsort(zb_all < 0, stable=True)
    zero_blocks = zb_all[order]
    n_zero = jnp.sum(zb_all >= 0).astype(jnp.int32).reshape(1)
    n_items = n_blocks // bpi + N_EXPERTS
    items_e = (nb + bpi - 1) // bpi
    iend = jnp.cumsum(items_e)
    istart = iend - items_e
    slot = jnp.arange(n_items, dtype=jnp.int32)
    valid = slot < iend[-1]
    exp_of = jnp.minimum(jnp.searchsorted(iend, slot, side="right"), N_EXPERTS - 1).astype(jnp.int32)
    last_e = jnp.max(jnp.where(nb > 0, experts, 0))
    local = slot - istart[exp_of]
    n_it = jnp.maximum(items_e[exp_of], 1)
    base, rem = nb[exp_of] // n_it, nb[exp_of] % n_it
    first_blk = bstart[exp_of] + local * base + jnp.minimum(local, rem)
    item_e = jnp.where(valid, exp_of, last_e).astype(jnp.int32)
    item_start = jnp.where(valid, first_blk * tm, 0).astype(jnp.int32)
    item_nblk = jnp.where(valid, base + (local < rem), 0).astype(jnp.int32)
    return dest, zero_blocks, n_zero, item_e, item_start, item_nblk, tail_blk, n_rows


def _rope_lane_tile(v):
    half = MLA_ROPE // 2
    z = jnp.zeros((half,), v.dtype)
    return jnp.concatenate([v[:half], z, v[half:], z]).reshape(1, LANES)


def _rope_cols(w):
    half = MLA_ROPE // 2
    z = jnp.zeros((w.shape[0], half), w.dtype)
    return jnp.concatenate([w[:, :half], z, w[:, half:], z], axis=1)


def _layer(x, mem, positions, attn_norm, w_in, q_a_norm, w_q_b, kv_a_norm, w_kv_b,
           q_nope_norm, q_rope_norm, k_nope_norm, k_rope_norm,
           gmlp_v_norm, w_spatial, b_spatial,
           mem_norm, w_mem_kv, mem_q_norm, mem_k_norm,
           mla_out_norm, gmlp_out_norm, mem_out_norm, w_o,
           ffn_norm, w_router, b_router, w_gate_up, b_gate_up, w_down, b_down):
    B, S, D = x.shape
    N = B * S
    x2 = x.reshape(N, D)

    o1 = Q_LORA
    o2 = o1 + KV_LORA
    o3 = o2 + MLA_ROPE
    o4 = o3 + 2 * GMLP_WIDTH
    w_in_r = jnp.concatenate([w_in[:, :o2], w_in[:, o3:o4], w_in[:, o4:], _rope_cols(w_in[:, o2:o3])],
                             axis=1).astype(BF16)
    wq = w_q_b.reshape(Q_LORA, MLA_HEADS, MLA_NOPE + MLA_ROPE)
    wq_rope = jax.vmap(_rope_cols, in_axes=1, out_axes=1)(wq[:, :, MLA_NOPE:])
    wq_pad = jnp.concatenate([wq[:, :, :MLA_NOPE], wq_rope], axis=2).reshape(Q_LORA, MLA_HEADS * HEAD_PAD)
    wq_pad = wq_pad.astype(BF16)
    half = MLA_ROPE // 2
    inv_freq = ROPE_BASE ** (-jnp.arange(half, dtype=F32) / half)
    zf = jnp.zeros((half,), F32)
    invf_tile = jnp.concatenate([inv_freq, zf, inv_freq, zf]).reshape(1, LANES)
    sgn_tile = jnp.concatenate([-jnp.ones((half,), F32), zf, jnp.ones((half,), F32), zf]).reshape(1, LANES)
    bs_tile = jnp.repeat(b_spatial.T, GMLP_CH, axis=1)

    mk, mv = _mem_kv(mem, mem_norm, w_mem_kv, mem_k_norm)
    cq, ckv, kpe, ogm, cs = _inproj(
        x2, positions.reshape(N, 1), (invf_tile, sgn_tile), attn_norm, w_in_r, q_a_norm, kv_a_norm,
        _rope_lane_tile(k_rope_norm), gmlp_v_norm, w_spatial, bs_tile, mk, mv, mem_q_norm,
        gmlp_out_norm, mem_out_norm, S)
    q, k, v = _qkv(cq, ckv, kpe, cs, wq_pad, w_kv_b.astype(BF16), q_nope_norm,
                   _rope_lane_tile(q_rope_norm), k_nope_norm)
    o_mla = _attention(q, k, v, B, S).reshape(N, MLA_HEADS * MLA_V)

    wo = w_o.astype(BF16)
    wr_pad = jnp.pad(w_router, ((0, 0), (0, LANES - N_EXPERTS)))
    br_pad = jnp.pad(b_router, (0, LANES - N_EXPERTS)).reshape(1, LANES)
    x1, h2, ti, tg = _outproj(x2, o_mla, ogm, mla_out_norm, wo[:MLA_HEADS * MLA_V],
                              wo[MLA_HEADS * MLA_V:], ffn_norm, wr_pad, br_pad)

    dest, zero_blocks, n_zero, item_e, item_start, item_nblk, tail_blk, n_rows = _routing(
        ti[:, :TOP_K], N)
    xs = _dispatch(h2, dest, zero_blocks, n_zero, n_rows)
    ys = _moe(xs, item_e, item_start, item_nblk, tail_blk, w_gate_up, b_gate_up, w_down, b_down)
    out = _combine(x1, tg, ys, dest)
    return out.reshape(B, S, D)


def kernel(x, mem, positions, attn_norm, w_in, q_a_norm, w_q_b, kv_a_norm, w_kv_b, q_nope_norm, q_rope_norm, k_nope_norm, k_rope_norm, gmlp_v_norm, w_spatial, b_spatial, mem_norm, w_mem_kv, mem_q_norm, mem_k_norm, mla_out_norm, gmlp_out_norm, mem_out_norm, w_o, ffn_norm, w_router, b_router, w_gate_up, b_gate_up, w_down, b_down):
    depth = attn_norm.shape[0]
    for l in range(depth):
        x = _layer(x, mem, positions, attn_norm[l], w_in[l], q_a_norm[l], w_q_b[l], kv_a_norm[l],
                   w_kv_b[l], q_nope_norm[l], q_rope_norm[l], k_nope_norm[l], k_rope_norm[l],
                   gmlp_v_norm[l], w_spatial[l], b_spatial[l], mem_norm[l], w_mem_kv[l],
                   mem_q_norm[l], mem_k_norm[l], mla_out_norm[l], gmlp_out_norm[l],
                   mem_out_norm[l], w_o[l], ffn_norm[l], w_router[l], b_router[l], w_gate_up[l],
                   b_gate_up[l], w_down[l], b_down[l])
    return x
```

```python
import functools

import jax
import jax.numpy as jnp
import numpy as np
from jax import lax
from jax.experimental import pallas as pl
from jax.experimental.pallas import tpu as pltpu

F32 = jnp.float32
BF16 = jnp.bfloat16

EPS = 1e-6
LANES = 128
SUBLANES = 8
VMEM_LIMIT = 56 * 1024 * 1024

MLA_HEADS = 8
MLA_NOPE = 128
MLA_ROPE = 64
MLA_V = 128
Q_LORA = 512
KV_LORA = 512
GMLP_GROUPS = 4
GMLP_CH = 128
GMLP_WIDTH = GMLP_GROUPS * GMLP_CH
CHUNK = 128
MEM_HEADS = 4
MEM_HEAD_DIM = 128
MEM_WIDTH = MEM_HEADS * MEM_HEAD_DIM
N_EXPERTS = 32
TOP_K = 4
SWIGLU_ALPHA = 1.702
SWIGLU_LIMIT = 7.0
ROPE_BASE = 10000.0
HEAD_PAD = 2 * LANES

TOK_TILE = 512
ATT_TQ = 512
ATT_TK = 512
ATT_HPS = 4
MOE_TM = 128
MOE_BPI = 10
MOE_TF = 256
MOE_WSLOTS = 3
DSP_TILE = 1024
CMB_TILE = 256


def _rms(x, g, n=None):
    n = x.shape[-1] if n is None else n
    ms = jnp.sum(x * x, axis=-1, keepdims=True) * (1.0 / n)
    return x * lax.rsqrt(ms + EPS) * g


def _gelu(x):
    return 0.5 * x * (1.0 + lax.erf(x * (2.0 ** -0.5)))


def _dot(a, b):
    return jnp.dot(a, b, preferred_element_type=F32)


def _dot_nt(a, b):
    return lax.dot_general(a, b, (((1,), (1,)), ((), ())), preferred_element_type=F32)


def _const_spec(shape):
    nd = len(shape)
    return pl.BlockSpec(shape, lambda *_: (0,) * nd)


def _mem_kv_kernel(mem_ref, g_ref, w_ref, kg_ref, mk_ref, mv_ref):
    m = mem_ref[0]
    hn = _rms(m, g_ref[...]).astype(BF16)
    kv = _dot(hn, w_ref[...])
    for h in range(MEM_HEADS):
        k = kv[:, h * MEM_HEAD_DIM:(h + 1) * MEM_HEAD_DIM]
        mk_ref[0, :, h * MEM_HEAD_DIM:(h + 1) * MEM_HEAD_DIM] = _rms(k, kg_ref[...]).astype(BF16)
    mv_ref[0] = kv[:, MEM_WIDTH:].astype(BF16)


def _mem_kv(mem, mem_norm, w_mem_kv, mem_k_norm):
    B, M, D = mem.shape
    return pl.pallas_call(
        _mem_kv_kernel,
        grid=(B,),
        in_specs=[pl.BlockSpec((1, M, D), lambda b: (b, 0, 0)),
                  _const_spec((1, D)),
                  _const_spec((D, 2 * MEM_WIDTH)),
                  _const_spec((1, MEM_HEAD_DIM))],
        out_specs=[pl.BlockSpec((1, M, MEM_WIDTH), lambda b: (b, 0, 0)),
                   pl.BlockSpec((1, M, MEM_WIDTH), lambda b: (b, 0, 0))],
        out_shape=[jax.ShapeDtypeStruct((B, M, MEM_WIDTH), BF16),
                   jax.ShapeDtypeStruct((B, M, MEM_WIDTH), BF16)],
        compiler_params=pltpu.CompilerParams(dimension_semantics=("arbitrary",),
                                             vmem_limit_bytes=VMEM_LIMIT),
        name="mem_kv",
    )(mem, mem_norm.reshape(1, D), w_mem_kv.astype(BF16), mem_k_norm.reshape(1, MEM_HEAD_DIM))


_O_CQ = 0
_O_CKV = _O_CQ + Q_LORA
_O_KR = _O_CKV + KV_LORA
_O_U = _O_KR + LANES
_O_VG = _O_U + GMLP_WIDTH
_O_QM = _O_VG + GMLP_WIDTH
_IN_COLS_PAD = _O_QM + MEM_WIDTH


def _rope_tile(t, cos, sin_signed):
    return t * cos + pltpu.roll(t, LANES // 2, axis=1) * sin_signed


def _inproj_kernel(x_ref, pos_ref, invf_ref, sgn_ref, an_ref, w_ref, qan_ref, kvan_ref, krn_ref,
                   gvn_ref, ws_ref, bs_ref, mk_ref, mv_ref, mqn_ref, gon_ref, mon_ref,
                   cq_ref, ckv_ref, kpe_ref, ogm_ref, cs_ref):
    def project(sl):
        h = _rms(x_ref[sl, :], an_ref[...]).astype(BF16)
        return _dot(h, w_ref[...])

    def finish(sl, n, z):
        cq_ref[sl, :] = _rms(z[:, _O_CQ:_O_CQ + Q_LORA], qan_ref[...]).astype(BF16)
        ckv_ref[sl, :] = _rms(z[:, _O_CKV:_O_CKV + KV_LORA], kvan_ref[...]).astype(BF16)

        ang = pos_ref[sl, :].astype(F32) * invf_ref[...]
        cos = jnp.cos(ang)
        sin_signed = jnp.sin(ang) * sgn_ref[...]
        cs_ref[sl, :LANES] = cos
        cs_ref[sl, LANES:] = sin_signed
        kr = _rms(z[:, _O_KR:_O_KR + LANES], krn_ref[...], MLA_ROPE)
        kpe_ref[sl, :] = _rope_tile(kr, cos, sin_signed).astype(BF16)

        u = _gelu(z[:, _O_U:_O_U + GMLP_WIDTH])
        vg = _gelu(z[:, _O_VG:_O_VG + GMLP_WIDTH])
        vg = _rms(vg, gvn_ref[...]).astype(BF16)
        row = lax.broadcasted_iota(jnp.int32, (CHUNK, CHUNK), 0)
        col = lax.broadcasted_iota(jnp.int32, (CHUNK, CHUNK), 1)
        sp_cols = []
        for g in range(GMLP_GROUPS):
            wsg = jnp.where(col <= row, ws_ref[g], 0.0).astype(BF16)
            sp_rows = [_dot(wsg, vg[c * CHUNK:(c + 1) * CHUNK, g * GMLP_CH:(g + 1) * GMLP_CH])
                       for c in range(n // CHUNK)]
            sp_cols.append(jnp.concatenate(sp_rows, axis=0))
        sp = jnp.concatenate(sp_cols, axis=1) + jnp.concatenate([bs_ref[...]] * (n // CHUNK), axis=0)
        ogm_ref[sl, :GMLP_WIDTH] = _rms(u * sp, gon_ref[...]).astype(BF16)

        o_heads = []
        for hd in range(MEM_HEADS):
            hs = slice(hd * MEM_HEAD_DIM, (hd + 1) * MEM_HEAD_DIM)
            q = z[:, _O_QM + hd * MEM_HEAD_DIM:_O_QM + (hd + 1) * MEM_HEAD_DIM]
            qn = (_rms(q, mqn_ref[...]) * (MEM_HEAD_DIM ** -0.5)).astype(BF16)
            s = _dot_nt(qn, mk_ref[0, :, hs])
            s = s - jnp.max(s, axis=-1, keepdims=True)
            p = jnp.exp(s)
            p = p / jnp.sum(p, axis=-1, keepdims=True)
            o_heads.append(_dot(p.astype(BF16), mv_ref[0, :, hs]))
        o_mem = jnp.concatenate(o_heads, axis=1)
        ogm_ref[sl, GMLP_WIDTH:] = _rms(o_mem, mon_ref[...]).astype(BF16)

    finish(slice(None), x_ref.shape[0], project(slice(None)))


def _inproj(x2, pos, tables, attn_norm, w_in_r, q_a_norm, kv_a_norm, krn_tile, gmlp_v_norm,
            w_spatial, bs_tile, mk, mv, mem_q_norm, gmlp_out_norm, mem_out_norm, seq):
    N, D = x2.shape
    tm = TOK_TILE
    tiles_per_seq = seq // tm
    invf_tile, sgn_tile = tables
    M = mk.shape[1]
    row = lambda w: pl.BlockSpec((tm, w), lambda i: (i, 0))
    batch_blk = pl.BlockSpec((1, M, MEM_WIDTH), lambda i: (i // tiles_per_seq, 0, 0))
    return pl.pallas_call(
        _inproj_kernel,
        grid=(N // tm,),
        in_specs=[row(D), row(1), _const_spec((1, LANES)), _const_spec((1, LANES)),
                  _const_spec((1, D)), _const_spec((D, _IN_COLS_PAD)),
                  _const_spec((1, Q_LORA)), _const_spec((1, KV_LORA)), _const_spec((1, LANES)),
                  _const_spec((1, GMLP_WIDTH)), _const_spec((GMLP_GROUPS, CHUNK, CHUNK)),
                  _const_spec((CHUNK, GMLP_WIDTH)), batch_blk, batch_blk,
                  _const_spec((1, MEM_HEAD_DIM)), _const_spec((1, GMLP_WIDTH)),
                  _const_spec((1, MEM_WIDTH))],
        out_specs=[row(Q_LORA), row(KV_LORA), row(LANES), row(GMLP_WIDTH + MEM_WIDTH),
                   row(2 * LANES)],
        out_shape=[jax.ShapeDtypeStruct((N, Q_LORA), BF16),
                   jax.ShapeDtypeStruct((N, KV_LORA), BF16),
                   jax.ShapeDtypeStruct((N, LANES), BF16),
                   jax.ShapeDtypeStruct((N, GMLP_WIDTH + MEM_WIDTH), BF16),
                   jax.ShapeDtypeStruct((N, 2 * LANES), F32)],
        compiler_params=pltpu.CompilerParams(dimension_semantics=("arbitrary",),
                                             vmem_limit_bytes=VMEM_LIMIT),
        name="inproj",
    )(x2, pos, invf_tile, sgn_tile, attn_norm.reshape(1, D), w_in_r,
      q_a_norm.reshape(1, -1), kv_a_norm.reshape(1, -1), krn_tile, gmlp_v_norm.reshape(1, -1),
      w_spatial, bs_tile, mk, mv, mem_q_norm.reshape(1, -1), gmlp_out_norm.reshape(1, -1),
      mem_out_norm.reshape(1, -1))


def _qkv_kernel(cq_ref, ckv_ref, kpe_ref, cs_ref, wq_ref, wkv_ref, qnn_ref, qrn_ref, knn_ref,
                q_ref, k_ref, v_ref):
    cos = cs_ref[:, :LANES]
    sin_signed = cs_ref[:, LANES:]
    scale = (MLA_NOPE + MLA_ROPE) ** -0.5 * np.log2(np.e)
    qr = _dot(cq_ref[...], wq_ref[...])
    kvr = _dot(ckv_ref[...], wkv_ref[...])
    kpe = kpe_ref[...]
    for h in range(MLA_HEADS):
        o = h * HEAD_PAD
        qn = _rms(qr[:, o:o + MLA_NOPE], qnn_ref[...]) * scale
        qt = _rms(qr[:, o + MLA_NOPE:o + HEAD_PAD], qrn_ref[...], MLA_ROPE)
        qt = _rope_tile(qt, cos, sin_signed) * scale
        q_ref[:, o:o + MLA_NOPE] = qn.astype(BF16)
        q_ref[:, o + MLA_NOPE:o + HEAD_PAD] = qt.astype(BF16)
        ko = h * (MLA_NOPE + MLA_V)
        k_ref[:, o:o + MLA_NOPE] = _rms(kvr[:, ko:ko + MLA_NOPE], knn_ref[...]).astype(BF16)
        k_ref[:, o + MLA_NOPE:o + HEAD_PAD] = kpe
        v_ref[:, h * MLA_V:(h + 1) * MLA_V] = kvr[:, ko + MLA_NOPE:ko + MLA_NOPE + MLA_V].astype(BF16)


def _qkv(cq, ckv, kpe, cs, wq_pad, wkv, q_nope_norm, qrn_tile, k_nope_norm):
    N = cq.shape[0]
    tm = TOK_TILE
    row = lambda w: pl.BlockSpec((tm, w), lambda i: (i, 0))
    return pl.pallas_call(
        _qkv_kernel,
        grid=(N // tm,),
        in_specs=[row(Q_LORA), row(KV_LORA), row(LANES), row(2 * LANES),
                  _const_spec(wq_pad.shape), _const_spec(wkv.shape),
                  _const_spec((1, MLA_NOPE)), _const_spec((1, LANES)), _const_spec((1, MLA_NOPE))],
        out_specs=[row(MLA_HEADS * HEAD_PAD), row(MLA_HEADS * HEAD_PAD), row(MLA_HEADS * MLA_V)],
        out_shape=[jax.ShapeDtypeStruct((N, MLA_HEADS * HEAD_PAD), BF16),
                   jax.ShapeDtypeStruct((N, MLA_HEADS * HEAD_PAD), BF16),
                   jax.ShapeDtypeStruct((N, MLA_HEADS * MLA_V), BF16)],
        compiler_params=pltpu.CompilerParams(dimension_semantics=("arbitrary",),
                                             vmem_limit_bytes=VMEM_LIMIT),
        name="qkv",
    )(cq, ckv, kpe, cs, wq_pad, wkv, q_nope_norm.reshape(1, -1), qrn_tile,
      k_nope_norm.reshape(1, -1))


def _attn_kernel(q_ref, k_ref, v_ref, o_ref):
    tq, tk = ATT_TQ, ATT_TK
    qi = pl.program_id(2)
    qs = [q_ref[0, :, h * HEAD_PAD:(h + 1) * HEAD_PAD] for h in range(ATT_HPS)]

    def scores(h, kb):
        k0 = pl.multiple_of(kb * tk, tk)
        return _dot_nt(k_ref[0, pl.ds(k0, tk), h * HEAD_PAD:(h + 1) * HEAD_PAD], qs[h])

    def head_step(h, kb, s, carry, masked):
        m, l, acc = carry
        k0 = pl.multiple_of(kb * tk, tk)
        if masked:
            kpos = kb * tk + lax.broadcasted_iota(jnp.int32, (tk, tq), 0)
            qpos = qi * tq + lax.broadcasted_iota(jnp.int32, (tk, tq), 1)
            s = jnp.where(kpos <= qpos, s, -jnp.inf)
        m_new = jnp.maximum(m, jnp.max(s, axis=0, keepdims=True))
        alpha = jnp.exp2(m - m_new)
        p = jnp.exp2(s - m_new)
        l = alpha * l + jnp.sum(p, axis=0, keepdims=True)
        v = v_ref[0, pl.ds(k0, tk), h * MLA_V:(h + 1) * MLA_V]
        pv = lax.dot_general(v, p.astype(BF16), (((0,), (0,)), ((), ())),
                             preferred_element_type=F32)
        return m_new, l, alpha * acc + pv

    def step(kb, carries, masked):
        ss = [scores(h, kb) for h in range(ATT_HPS)]
        return tuple(head_step(h, kb, ss[h], carries[h], masked) for h in range(ATT_HPS))

    init = tuple((jnp.full((1, tq), -jnp.inf, F32), jnp.zeros((1, tq), F32),
                  jnp.zeros((MLA_V, tq), F32)) for _ in range(ATT_HPS))
    n_full = (qi * tq) // tk
    carries = lax.fori_loop(0, n_full, lambda kb, c: step(kb, c, False), init)
    carries = step(n_full, carries, True)
    for h in range(ATT_HPS):
        _, l, acc = carries[h]
        o_ref[0, :, h * MLA_V:(h + 1) * MLA_V] = jnp.transpose(acc / l).astype(BF16)


def _attention(q, k, v, batch, seq):
    q3 = q.reshape(batch, seq, MLA_HEADS * HEAD_PAD)
    k3 = k.reshape(batch, seq, MLA_HEADS * HEAD_PAD)
    v3 = v.reshape(batch, seq, MLA_HEADS * MLA_V)
    hp, hv = ATT_HPS * HEAD_PAD, ATT_HPS * MLA_V
    assert ATT_TK % ATT_TQ == 0 and seq % ATT_TK == 0
    return pl.pallas_call(
        _attn_kernel,
        grid=(batch, MLA_HEADS // ATT_HPS, seq // ATT_TQ),
        in_specs=[pl.BlockSpec((1, ATT_TQ, hp), lambda b, h, i: (b, i, h)),
                  pl.BlockSpec((1, seq, hp), lambda b, h, i: (b, 0, h)),
                  pl.BlockSpec((1, seq, hv), lambda b, h, i: (b, 0, h))],
        out_specs=pl.BlockSpec((1, ATT_TQ, hv), lambda b, h, i: (b, i, h)),
        out_shape=jax.ShapeDtypeStruct((batch, seq, MLA_HEADS * MLA_V), BF16),
        compiler_params=pltpu.CompilerParams(
            dimension_semantics=("arbitrary", "arbitrary", "arbitrary"),
            vmem_limit_bytes=VMEM_LIMIT),
        name="attn",
    )(q3, k3, v3)


def _outproj_kernel(x_ref, oa_ref, ogm_ref, aon_ref, woa_ref, wob_ref, fn_ref, wr_ref, br_ref,
                    x1_ref, h2_ref, ti_ref, tg_ref):
    wr = wr_ref[...]
    w_hi = wr.astype(BF16)
    w_lo = (wr - w_hi.astype(F32)).astype(BF16)
    w_cat = jnp.concatenate([w_hi, w_lo], axis=1)

    def rows(sl, n):
        oa = _rms(oa_ref[sl, :].astype(F32), aon_ref[...]).astype(BF16)
        x1 = x_ref[sl, :] + _dot(oa, woa_ref[...]) + _dot(ogm_ref[sl, :], wob_ref[...])
        x1_ref[sl, :] = x1
        h2 = _rms(x1, fn_ref[...])
        h2_ref[sl, :] = h2
        h_hi = h2.astype(BF16)
        h_lo = (h2 - h_hi.astype(F32)).astype(BF16)
        hh = _dot(h_hi, w_cat)
        logits = hh[:, :LANES] + (_dot(h_lo, w_hi) + hh[:, LANES:]) + br_ref[...]
        lane = lax.broadcasted_iota(jnp.int32, (n, LANES), 1)
        lg = jnp.where(lane < N_EXPERTS, logits, -jnp.inf)
        vals, idxs = [], []
        for _ in range(TOP_K):
            m = jnp.max(lg, axis=-1, keepdims=True)
            am = jnp.min(jnp.where(lg == m, lane, LANES), axis=-1, keepdims=True)
            vals.append(m)
            idxs.append(am)
            lg = jnp.where(lane == am, -jnp.inf, lg)
        es = [jnp.exp(v - vals[0]) for v in vals]
        denom = es[0] + es[1] + es[2] + es[3]
        ti = jnp.zeros((n, LANES), jnp.int32)
        tg = jnp.zeros((n, LANES), F32)
        for kk in range(TOP_K):
            ti = jnp.where(lane == kk, idxs[kk], ti)
            tg = jnp.where(lane == kk, es[kk] / denom, tg)
        ti_ref[sl, :] = ti
        tg_ref[sl, :] = tg

    rows(slice(None), x_ref.shape[0])


def _outproj(x2, o_mla, ogm, mla_out_norm, wo, ffn_norm, wr_pad, br_pad):
    N, D = x2.shape
    tm = TOK_TILE
    row = lambda w: pl.BlockSpec((tm, w), lambda i: (i, 0))
    wa = o_mla.shape[1]
    wb = ogm.shape[1]
    assert wa == wb and wo.shape[0] == wa + wb
    return pl.pallas_call(
        _outproj_kernel,
        grid=(N // tm,),
        in_specs=[row(D), row(wa), row(wb), _const_spec((1, wa)),
                  pl.BlockSpec((wa, D), lambda i: (0, 0)), pl.BlockSpec((wb, D), lambda i: (1, 0)),
                  _const_spec((1, D)), _const_spec((D, LANES)), _const_spec((1, LANES))],
        out_specs=[row(D), row(D), row(LANES), row(LANES)],
        out_shape=[jax.ShapeDtypeStruct((N, D), F32), jax.ShapeDtypeStruct((N, D), F32),
                   jax.ShapeDtypeStruct((N, LANES), jnp.int32),
                   jax.ShapeDtypeStruct((N, LANES), F32)],
        compiler_params=pltpu.CompilerParams(dimension_semantics=("arbitrary",),
                                             vmem_limit_bytes=VMEM_LIMIT),
        name="outproj",
    )(x2, o_mla, ogm, mla_out_norm.reshape(1, -1), wo, wo, ffn_norm.reshape(1, -1), wr_pad,
      br_pad)


def _pack_bf16_pairs(x):
    k = x.shape[1] // 2
    lo = pltpu.bitcast(x[:, :k].astype(BF16).astype(F32), jnp.uint32)
    hi = pltpu.bitcast(x[:, k:].astype(BF16).astype(F32), jnp.uint32)
    return hi | (lo >> 16)


def _unpack_bf16_pairs(w):
    lo = pltpu.bitcast(w << 16, F32).astype(BF16)
    hi = pltpu.bitcast(w & jnp.uint32(0xFFFF0000), F32).astype(BF16)
    return jnp.concatenate([lo, hi], axis=1)


def _dispatch_kernel(dest_ref, zb_ref, nzb_ref, h2_ref, xs_hbm, pk, zblk, sem, sem_z):
    tm = h2_ref.shape[0]
    bm = MOE_TM
    i = pl.program_id(0)
    base = i * (tm * TOP_K)
    pk[...] = _pack_bf16_pairs(h2_ref[...])

    @pl.when(i == 0)
    def _zero_blocks():
        zblk[...] = jnp.zeros(zblk.shape, zblk.dtype)

        def zcopy(n):
            d0 = pl.multiple_of(zb_ref[n] * bm, bm)
            return pltpu.make_async_copy(zblk, xs_hbm.at[pl.ds(d0, bm), :], sem_z)

        def issue(n, c):
            zcopy(n).start()
            return c
        lax.fori_loop(0, nzb_ref[0], issue, 0)

        def finish(n, c):
            zcopy(n).wait()
            return c
        lax.fori_loop(0, nzb_ref[0], finish, 0)

    def issue(g, c):
        r0 = pl.multiple_of(g * SUBLANES, SUBLANES)
        rows = pk.at[pl.ds(r0, SUBLANES), :]
        for s in range(SUBLANES):
            for kk in range(TOP_K):
                d = dest_ref[base + (r0 + s) * TOP_K + kk]
                pltpu.make_async_copy(rows.at[pl.ds(s, 1), :], xs_hbm.at[pl.ds(d, 1), :], sem).start()
        return c
    lax.fori_loop(0, tm // SUBLANES, issue, 0)
    for kk in range(TOP_K):
        pltpu.make_async_copy(pk, xs_hbm.at[pl.ds(0, tm), :], sem).wait()


def _dispatch(h2, dest, zero_blocks, n_zero, n_rows):
    N, D = h2.shape
    tm = DSP_TILE
    grid_spec = pltpu.PrefetchScalarGridSpec(
        num_scalar_prefetch=3,
        grid=(N // tm,),
        in_specs=[pl.BlockSpec((tm, D), lambda i, *_: (i, 0))],
        out_specs=pl.BlockSpec(memory_space=pl.ANY),
        scratch_shapes=[pltpu.VMEM((tm, D // 2), jnp.uint32), pltpu.VMEM((MOE_TM, D // 2), jnp.uint32),
                        pltpu.SemaphoreType.DMA(()), pltpu.SemaphoreType.DMA(())],
    )
    return pl.pallas_call(
        _dispatch_kernel,
        grid_spec=grid_spec,
        out_shape=jax.ShapeDtypeStruct((n_rows, D // 2), jnp.uint32),
        compiler_params=pltpu.CompilerParams(dimension_semantics=("arbitrary",),
                                             vmem_limit_bytes=VMEM_LIMIT),
        name="dispatch",
    )(dest, zero_blocks, n_zero, h2)


def _moe_kernel(ie_ref, ist_ref, inb_ref, tail_ref,
                xs_hbm, wgu_hbm, wd_hbm, bgu_ref, bd_ref,
                ys_hbm,
                xbuf, acc, wg_buf, wu_buf, wd_buf, zblk, sem_x, sem_o, sem_w, sem_z):
    tm = MOE_TM
    i = pl.program_id(0)
    n_items = pl.num_programs(0)
    nblk = inb_ref[i]
    slot = i % 2
    n_blocks = ys_hbm.shape[0] // tm
    tf = wg_buf.shape[2]
    nj = wd_hbm.shape[1] // tf

    def x_copy(it, m):
        s0 = pl.multiple_of(ist_ref[it] + m * tm, tm)
        r0 = pl.multiple_of(m * tm, tm)
        return pltpu.make_async_copy(xs_hbm.at[pl.ds(s0, tm), :],
                                     xbuf.at[it % 2, pl.ds(r0, tm), :], sem_x.at[it % 2])

    def y_copy(it, m):
        d0 = pl.multiple_of(ist_ref[it] + m * tm, tm)
        r0 = pl.multiple_of(m * tm, tm)
        return pltpu.make_async_copy(acc.at[it % 2, pl.ds(r0, tm), :],
                                     ys_hbm.at[pl.ds(d0, tm), :], sem_o.at[it % 2])

    def for_blocks(it, fn):
        def body(m, c):
            fn(it, m)
            return c
        lax.fori_loop(0, inb_ref[it], body, 0)

    start_x = lambda it: for_blocks(it, lambda a, m: x_copy(a, m).start())
    wait_x = lambda it: for_blocks(it, lambda a, m: x_copy(a, m).wait())
    start_y = lambda it: for_blocks(it, lambda a, m: y_copy(a, m).start())
    wait_y = lambda it: for_blocks(it, lambda a, m: y_copy(a, m).wait())

    def zero_copy(b):
        d0 = pl.multiple_of(b * tm, tm)
        return pltpu.make_async_copy(zblk, ys_hbm.at[pl.ds(d0, tm), :], sem_z)

    n_ws = wg_buf.shape[0]

    def w_slot(it, j):
        return (it * nj + j) % n_ws

    def w_copies(it, j):
        e = ie_ref[it]
        ws = w_slot(it, j)
        c0 = pl.multiple_of(j * tf, tf)
        c1 = pl.multiple_of(nj * tf + j * tf, tf)
        return (pltpu.make_async_copy(wgu_hbm.at[e, :, pl.ds(c0, tf)], wg_buf.at[ws], sem_w.at[ws]),
                pltpu.make_async_copy(wgu_hbm.at[e, :, pl.ds(c1, tf)], wu_buf.at[ws], sem_w.at[ws]),
                pltpu.make_async_copy(wd_hbm.at[e, pl.ds(c0, tf), :], wd_buf.at[ws], sem_w.at[ws]))

    def start_w(it, j):
        for cp in w_copies(it, j):
            cp.start()

    def wait_w(it, j):
        for cp in w_copies(it, j):
            cp.wait()

    @pl.when(i == 0)
    def _first_step():
        zblk[...] = jnp.zeros(zblk.shape, zblk.dtype)

        def issue(b, c):
            zero_copy(b).start()
            return c
        lax.fori_loop(tail_ref[0], n_blocks, issue, 0)
        start_x(0)

        @pl.when(nblk > 0)
        def _():
            for a in range(n_ws - 1):
                start_w(0, a)

    wait_x(i)

    @pl.when(i + 1 < n_items)
    def _():
        start_x(i + 1)

    @pl.when(i >= 2)
    def _():
        wait_y(i - 2)

    def init(m, c):
        r0 = pl.multiple_of(m * tm, tm)
        acc[slot, pl.ds(r0, tm), :] = jnp.broadcast_to(bd_ref[0], (tm, acc.shape[2]))
        return c
    lax.fori_loop(0, nblk, init, 0)

    def ffn_rows(j, n):
        ws = w_slot(i, j)
        x = _unpack_bf16_pairs(xbuf[slot, pl.ds(0, n), :])
        g = jnp.minimum(_dot(x, wg_buf[ws].astype(BF16)) + bgu_ref[0, pl.ds(j, 1), :], SWIGLU_LIMIT)
        u = jnp.clip(_dot(x, wu_buf[ws].astype(BF16)) + bgu_ref[0, pl.ds(nj + j, 1), :],
                     -SWIGLU_LIMIT, SWIGLU_LIMIT)
        a = (u + 1.0) * (g * jax.nn.sigmoid(SWIGLU_ALPHA * g))
        acc[slot, pl.ds(0, n), :] += _dot(a.astype(BF16), wd_buf[ws].astype(BF16))

    def chunk(j, c):
        wait_w(i, j)
        ja = j + (n_ws - 1)

        @pl.when(ja < nj)
        def _():
            start_w(i, ja)

        @pl.when(jnp.logical_and(ja >= nj, i + 1 < n_items))
        def _():
            @pl.when(inb_ref[i + 1] > 0)
            def _():
                start_w(i + 1, ja - nj)

        for nb in range(1, MOE_BPI + 1):
            pl.when(nblk == nb)(functools.partial(ffn_rows, j, nb * tm))
        return c

    @pl.when(nblk > 0)
    def _chunks():
        lax.fori_loop(0, nj, chunk, 0)

    start_y(i)

    @pl.when(i == n_items - 1)
    def _last_step():
        @pl.when(i >= 1)
        def _():
            wait_y(i - 1)
        wait_y(i)

        def finish(b, c):
            zero_copy(b).wait()
            return c
        lax.fori_loop(tail_ref[0], n_blocks, finish, 0)


def _moe(xs, item_e, item_start, item_nblk, tail_blk, w_gate_up, b_gate_up, w_down, b_down):
    n_rows = xs.shape[0]
    E, D, F2 = w_gate_up.shape
    F = F2 // 2
    tf = MOE_TF
    nj = F // tf
    nw = MOE_WSLOTS
    assert 2 <= nw <= nj + 1
    n_items = item_e.shape[0]
    rows = MOE_BPI * MOE_TM
    any_spec = pl.BlockSpec(memory_space=pl.ANY)
    grid_spec = pltpu.PrefetchScalarGridSpec(
        num_scalar_prefetch=4,
        grid=(n_items,),
        in_specs=[any_spec, any_spec, any_spec,
                  pl.BlockSpec((1, 2 * nj, tf), lambda i, ie, ist, inb, tail: (ie[i], 0, 0)),
                  pl.BlockSpec((1, 1, D), lambda i, ie, ist, inb, tail: (ie[i], 0, 0))],
        out_specs=any_spec,
        scratch_shapes=[pltpu.VMEM((2, rows, D // 2), jnp.uint32), pltpu.VMEM((2, rows, D), F32),
                        pltpu.VMEM((nw, D, tf), F32), pltpu.VMEM((nw, D, tf), F32),
                        pltpu.VMEM((nw, tf, D), F32), pltpu.VMEM((MOE_TM, D), F32),
                        pltpu.SemaphoreType.DMA((2,)), pltpu.SemaphoreType.DMA((2,)),
                        pltpu.SemaphoreType.DMA((nw,)), pltpu.SemaphoreType.DMA(())],
    )
    return pl.pallas_call(
        _moe_kernel,
        grid_spec=grid_spec,
        out_shape=jax.ShapeDtypeStruct((n_rows, D), F32),
        compiler_params=pltpu.CompilerParams(dimension_semantics=("arbitrary",),
                                             vmem_limit_bytes=VMEM_LIMIT),
        name="moe",
    )(item_e, item_start, item_nblk, tail_blk, xs, w_gate_up, w_down,
      b_gate_up.reshape(E, 2 * nj, tf), b_down.reshape(E, 1, D))


def _combine_kernel(pos_ref, x1_ref, g_ref, ys_hbm, o_ref, buf, sem):
    tm = CMB_TILE
    i = pl.program_id(0)

    def start_gather(t):
        base = t * (tm * TOP_K)
        b = t % 2

        def issue(g, c):
            r0 = pl.multiple_of(g * SUBLANES, SUBLANES)
            for s in range(SUBLANES):
                for kk in range(TOP_K):
                    p = pos_ref[base + (r0 + s) * TOP_K + kk]
                    pltpu.make_async_copy(ys_hbm.at[pl.ds(p, 1), :],
                                          buf.at[b, kk, pl.ds(r0, SUBLANES), :].at[pl.ds(s, 1), :],
                                          sem.at[b]).start()
            return c
        lax.fori_loop(0, tm // SUBLANES, issue, 0)

    @pl.when(i == 0)
    def _():
        start_gather(0)

    @pl.when(i + 1 < pl.num_programs(0))
    def _():
        start_gather(i + 1)

    b = i % 2
    for kk in range(TOP_K):
        pltpu.make_async_copy(ys_hbm.at[pl.ds(0, tm), :], buf.at[b, kk], sem.at[b]).wait()
    out = x1_ref[...]
    for kk in range(TOP_K):
        out = out + g_ref[:, kk:kk + 1] * buf[b, kk]
    o_ref[...] = out


def _combine(x1, gates, ys, dest):
    N, D = x1.shape
    tm = CMB_TILE
    grid_spec = pltpu.PrefetchScalarGridSpec(
        num_scalar_prefetch=1,
        grid=(N // tm,),
        in_specs=[pl.BlockSpec((tm, D), lambda i, pos: (i, 0)),
                  pl.BlockSpec((tm, LANES), lambda i, pos: (i, 0)),
                  pl.BlockSpec(memory_space=pl.ANY)],
        out_specs=pl.BlockSpec((tm, D), lambda i, pos: (i, 0)),
        scratch_shapes=[pltpu.VMEM((2, TOP_K, tm, D), F32), pltpu.SemaphoreType.DMA((2,))],
    )
    return pl.pallas_call(
        _combine_kernel,
        grid_spec=grid_spec,
        out_shape=jax.ShapeDtypeStruct((N, D), F32),
        compiler_params=pltpu.CompilerParams(dimension_semantics=("arbitrary",),
                                             vmem_limit_bytes=VMEM_LIMIT),
        name="combine",
    )(dest, x1, gates, ys)


def _routing(top_idx, n_tok):
    tm, bpi = MOE_TM, MOE_BPI
    nk = n_tok * TOP_K
    experts = jnp.arange(N_EXPERTS, dtype=jnp.int32)
    e_flat = top_idx.reshape(-1)
    onehot = (e_flat[:, None] == experts[None, :]).astype(jnp.int32)
    csum = jnp.cumsum(onehot, axis=0)
    rank = jnp.sum(csum * onehot, axis=1) - 1
    counts = csum[-1]
    nb = (counts + tm - 1) // tm
    bend = jnp.cumsum(nb)
    bstart = bend - nb
    dest = (jnp.sum(onehot * bstart[None, :], axis=1) * tm + rank).astype(jnp.int32)
    n_blocks = -(-(nk + N_EXPERTS * (tm - 1)) // tm)
    n_rows = n_blocks * tm
    tail_blk = bend[-1:].astype(jnp.int32)
    zb_e = jnp.where(nb > 0, bend - 1, -1)
    zb_t = jnp.arange(n_blocks, dtype=jnp.int32)
    zb_all = jnp.concatenate([zb_e, jnp.where(zb_t >= bend[-1], zb_t, -1)]).astype(jnp.int32)
    order = jnp.argsort(zb_all < 0, stable=True)
    zero_blocks = zb_all[order]
    n_zero = jnp.sum(zb_all >= 0).astype(jnp.int32).reshape(1)
    n_items = n_blocks // bpi + N_EXPERTS
    items_e = (nb + bpi - 1) // bpi
    iend = jnp.cumsum(items_e)
    istart = iend - items_e
    slot = jnp.arange(n_items, dtype=jnp.int32)
    valid = slot < iend[-1]
    exp_of = jnp.minimum(jnp.searchsorted(iend, slot, side="right"), N_EXPERTS - 1).astype(jnp.int32)
    last_e = jnp.max(jnp.where(nb > 0, experts, 0))
    local = slot - istart[exp_of]
    n_it = jnp.maximum(items_e[exp_of], 1)
    base, rem = nb[exp_of] // n_it, nb[exp_of] % n_it
    first_blk = bstart[exp_of] + local * base + jnp.minimum(local, rem)
    item_e = jnp.where(valid, exp_of, last_e).astype(jnp.int32)
    item_start = jnp.where(valid, first_blk * tm, 0).astype(jnp.int32)
    item_nblk = jnp.where(valid, base + (local < rem), 0).astype(jnp.int32)
    return dest, zero_blocks, n_zero, item_e, item_start, item_nblk, tail_blk, n_rows


def _rope_lane_tile(v):
    half = MLA_ROPE // 2
    z = jnp.zeros((half,), v.dtype)
    return jnp.concatenate([v[:half], z, v[half:], z]).reshape(1, LANES)


def _rope_cols(w):
    half = MLA_ROPE // 2
    z = jnp.zeros((w.shape[0], half), w.dtype)
    return jnp.concatenate([w[:, :half], z, w[:, half:], z], axis=1)


def _layer(x, mem, positions, attn_norm, w_in, q_a_norm, w_q_b, kv_a_norm, w_kv_b,
           q_nope_norm, q_rope_norm, k_nope_norm, k_rope_norm,
           gmlp_v_norm, w_spatial, b_spatial,
           mem_norm, w_mem_kv, mem_q_norm, mem_k_norm,
           mla_out_norm, gmlp_out_norm, mem_out_norm, w_o,
           ffn_norm, w_router, b_router, w_gate_up, b_gate_up, w_down, b_down):
    B, S, D = x.shape
    N = B * S
    x2 = x.reshape(N, D)

    half = MLA_ROPE // 2
    r1 = Q_LORA + KV_LORA + half
    zc = jnp.zeros((D, half), BF16)
    w_in_b = w_in.astype(BF16)
    w_in_r = jnp.concatenate([w_in_b[:, :r1], zc, w_in_b[:, r1:r1 + half], zc, w_in_b[:, r1 + half:]],
                             axis=1)
    wq = w_q_b.reshape(Q_LORA, MLA_HEADS, MLA_NOPE + MLA_ROPE)
    wq_rope = jax.vmap(_rope_cols, in_axes=1, out_axes=1)(wq[:, :, MLA_NOPE:])
    wq_pad = jnp.concatenate([wq[:, :, :MLA_NOPE], wq_rope], axis=2).reshape(Q_LORA, MLA_HEADS * HEAD_PAD)
    wq_pad = wq_pad.astype(BF16)
    inv_freq = ROPE_BASE ** (-jnp.arange(half, dtype=F32) / half)
    zf = jnp.zeros((half,), F32)
    invf_tile = jnp.concatenate([inv_freq, zf, inv_freq, zf]).reshape(1, LANES)
    sgn_tile = jnp.concatenate([-jnp.ones((half,), F32), zf, jnp.ones((half,), F32), zf]).reshape(1, LANES)
    bs_tile = jnp.repeat(b_spatial.T, GMLP_CH, axis=1)

    mk, mv = _mem_kv(mem, mem_norm, w_mem_kv, mem_k_norm)
    cq, ckv, kpe, ogm, cs = _inproj(
        x2, positions.reshape(N, 1), (invf_tile, sgn_tile), attn_norm, w_in_r, q_a_norm, kv_a_norm,
        _rope_lane_tile(k_rope_norm), gmlp_v_norm, w_spatial, bs_tile, mk, mv, mem_q_norm,
        gmlp_out_norm, mem_out_norm, S)
    q, k, v = _qkv(cq, ckv, kpe, cs, wq_pad, w_kv_b.astype(BF16), q_nope_norm,
                   _rope_lane_tile(q_rope_norm), k_nope_norm)
    o_mla = _attention(q, k, v, B, S).reshape(N, MLA_HEADS * MLA_V)

    wr_pad = jnp.pad(w_router, ((0, 0), (0, LANES - N_EXPERTS)))
    br_pad = jnp.pad(b_router, (0, LANES - N_EXPERTS)).reshape(1, LANES)
    x1, h2, ti, tg = _outproj(x2, o_mla, ogm, mla_out_norm, w_o.astype(BF16), ffn_norm, wr_pad,
                              br_pad)

    dest, zero_blocks, n_zero, item_e, item_start, item_nblk, tail_blk, n_rows = _routing(
        ti[:, :TOP_K], N)
    xs = _dispatch(h2, dest, zero_blocks, n_zero, n_rows)
    ys = _moe(xs, item_e, item_start, item_nblk, tail_blk, w_gate_up, b_gate_up, w_down, b_down)
    out = _combine(x1, tg, ys, dest)
    return out.reshape(B, S, D)


def kernel(x, mem, positions, attn_norm, w_in, q_a_norm, w_q_b, kv_a_norm, w_kv_b, q_nope_norm, q_rope_norm, k_nope_norm, k_rope_norm, gmlp_v_norm, w_spatial, b_spatial, mem_norm, w_mem_kv, mem_q_norm, mem_k_norm, mla_out_norm, gmlp_out_norm, mem_out_norm, w_o, ffn_norm, w_router, b_router, w_gate_up, b_gate_up, w_down, b_down):
    depth = attn_norm.shape[0]
    for l in range(depth):
        x = _layer(x, mem, positions, attn_norm[l], w_in[l], q_a_norm[l], w_q_b[l], kv_a_norm[l],
                   w_kv_b[l], q_nope_norm[l], q_rope_norm[l], k_nope_norm[l], k_rope_norm[l],
                   gmlp_v_norm[l], w_spatial[l], b_spatial[l], mem_norm[l], w_mem_kv[l],
                   mem_q_norm[l], mem_k_norm[l], mla_out_norm[l], gmlp_out_norm[l],
                   mem_out_norm[l], w_o[l], ffn_norm[l], w_router[l], b_router[l], w_gate_up[l],
                   b_gate_up[l], w_down[l], b_down[l])
    return x
```

```python
import functools

import jax
import jax.numpy as jnp
import numpy as np
from jax import lax
from jax.experimental import pallas as pl
from jax.experimental.pallas import tpu as pltpu

F32 = jnp.float32
BF16 = jnp.bfloat16

EPS = 1e-6
LANES = 128
SUBLANES = 8
VMEM_LIMIT = 56 * 1024 * 1024

MLA_HEADS = 8
MLA_NOPE = 128
MLA_ROPE = 64
MLA_V = 128
Q_LORA = 512
KV_LORA = 512
GMLP_GROUPS = 4
GMLP_CH = 128
GMLP_WIDTH = GMLP_GROUPS * GMLP_CH
CHUNK = 128
MEM_HEADS = 4
MEM_HEAD_DIM = 128
MEM_WIDTH = MEM_HEADS * MEM_HEAD_DIM
N_EXPERTS = 32
TOP_K = 4
SWIGLU_ALPHA = 1.702
SWIGLU_LIMIT = 7.0
ROPE_BASE = 10000.0
HEAD_PAD = 2 * LANES

TOK_TILE = 512
ATT_TQ = 512
ATT_TK = 512
ATT_HPS = 4
MOE_TM = 64
MOE_BPI = 20
MOE_TF = 256
MOE_WSLOTS = 3
DSP_TILE = 1024
CMB_TILE = 256


def _rms(x, g, n=None):
    n = x.shape[-1] if n is None else n
    ms = jnp.sum(x * x, axis=-1, keepdims=True) * (1.0 / n)
    return x * lax.rsqrt(ms + EPS) * g


def _gelu(x):
    return 0.5 * x * (1.0 + lax.erf(x * (2.0 ** -0.5)))


def _dot(a, b):
    return jnp.dot(a, b, preferred_element_type=F32)


def _dot_nt(a, b):
    return lax.dot_general(a, b, (((1,), (1,)), ((), ())), preferred_element_type=F32)


def _const_spec(shape):
    nd = len(shape)
    return pl.BlockSpec(shape, lambda *_: (0,) * nd)


def _mem_kv_kernel(mem_ref, g_ref, w_ref, kg_ref, mk_ref, mv_ref):
    m = mem_ref[0]
    hn = _rms(m, g_ref[...]).astype(BF16)
    kv = _dot(hn, w_ref[...])
    for h in range(MEM_HEADS):
        k = kv[:, h * MEM_HEAD_DIM:(h + 1) * MEM_HEAD_DIM]
        mk_ref[0, :, h * MEM_HEAD_DIM:(h + 1) * MEM_HEAD_DIM] = _rms(k, kg_ref[...]).astype(BF16)
    mv_ref[0] = kv[:, MEM_WIDTH:].astype(BF16)


def _mem_kv(mem, mem_norm, w_mem_kv, mem_k_norm):
    B, M, D = mem.shape
    return pl.pallas_call(
        _mem_kv_kernel,
        grid=(B,),
        in_specs=[pl.BlockSpec((1, M, D), lambda b: (b, 0, 0)),
                  _const_spec((1, D)),
                  _const_spec((D, 2 * MEM_WIDTH)),
                  _const_spec((1, MEM_HEAD_DIM))],
        out_specs=[pl.BlockSpec((1, M, MEM_WIDTH), lambda b: (b, 0, 0)),
                   pl.BlockSpec((1, M, MEM_WIDTH), lambda b: (b, 0, 0))],
        out_shape=[jax.ShapeDtypeStruct((B, M, MEM_WIDTH), BF16),
                   jax.ShapeDtypeStruct((B, M, MEM_WIDTH), BF16)],
        compiler_params=pltpu.CompilerParams(dimension_semantics=("arbitrary",),
                                             vmem_limit_bytes=VMEM_LIMIT),
        name="mem_kv",
    )(mem, mem_norm.reshape(1, D), w_mem_kv.astype(BF16), mem_k_norm.reshape(1, MEM_HEAD_DIM))


_O_CQ = 0
_O_CKV = _O_CQ + Q_LORA
_O_KR = _O_CKV + KV_LORA
_O_U = _O_KR + LANES
_O_VG = _O_U + GMLP_WIDTH
_O_QM = _O_VG + GMLP_WIDTH
_IN_COLS_PAD = _O_QM + MEM_WIDTH


def _rope_tile(t, cos, sin_signed):
    return t * cos + pltpu.roll(t, LANES // 2, axis=1) * sin_signed


def _inproj_kernel(x_ref, pos_ref, invf_ref, sgn_ref, an_ref, w_ref, qan_ref, kvan_ref, krn_ref,
                   gvn_ref, ws_ref, bs_ref, mk_ref, mv_ref, mqn_ref, gon_ref, mon_ref,
                   cq_ref, ckv_ref, kpe_ref, ogm_ref, cs_ref):
    def project(sl):
        h = _rms(x_ref[sl, :], an_ref[...]).astype(BF16)
        return _dot(h, w_ref[...])

    def finish(sl, n, z):
        cq_ref[sl, :] = _rms(z[:, _O_CQ:_O_CQ + Q_LORA], qan_ref[...]).astype(BF16)
        ckv_ref[sl, :] = _rms(z[:, _O_CKV:_O_CKV + KV_LORA], kvan_ref[...]).astype(BF16)

        ang = pos_ref[sl, :].astype(F32) * invf_ref[...]
        cos = jnp.cos(ang)
        sin_signed = jnp.sin(ang) * sgn_ref[...]
        cs_ref[sl, :LANES] = cos
        cs_ref[sl, LANES:] = sin_signed
        kr = _rms(z[:, _O_KR:_O_KR + LANES], krn_ref[...], MLA_ROPE)
        kpe_ref[sl, :] = _rope_tile(kr, cos, sin_signed).astype(BF16)

        u = _gelu(z[:, _O_U:_O_U + GMLP_WIDTH])
        vg = _gelu(z[:, _O_VG:_O_VG + GMLP_WIDTH])
        vg = _rms(vg, gvn_ref[...]).astype(BF16)
        row = lax.broadcasted_iota(jnp.int32, (CHUNK, CHUNK), 0)
        col = lax.broadcasted_iota(jnp.int32, (CHUNK, CHUNK), 1)
        sp_cols = []
        for g in range(GMLP_GROUPS):
            wsg = jnp.where(col <= row, ws_ref[g], 0.0).astype(BF16)
            sp_rows = [_dot(wsg, vg[c * CHUNK:(c + 1) * CHUNK, g * GMLP_CH:(g + 1) * GMLP_CH])
                       for c in range(n // CHUNK)]
            sp_cols.append(jnp.concatenate(sp_rows, axis=0))
        sp = jnp.concatenate(sp_cols, axis=1) + jnp.concatenate([bs_ref[...]] * (n // CHUNK), axis=0)
        ogm_ref[sl, :GMLP_WIDTH] = _rms(u * sp, gon_ref[...]).astype(BF16)

        o_heads = []
        for hd in range(MEM_HEADS):
            hs = slice(hd * MEM_HEAD_DIM, (hd + 1) * MEM_HEAD_DIM)
            q = z[:, _O_QM + hd * MEM_HEAD_DIM:_O_QM + (hd + 1) * MEM_HEAD_DIM]
            qn = (_rms(q, mqn_ref[...]) * (MEM_HEAD_DIM ** -0.5)).astype(BF16)
            s = _dot_nt(qn, mk_ref[0, :, hs])
            s = s - jnp.max(s, axis=-1, keepdims=True)
            p = jnp.exp(s)
            p = p / jnp.sum(p, axis=-1, keepdims=True)
            o_heads.append(_dot(p.astype(BF16), mv_ref[0, :, hs]))
        o_mem = jnp.concatenate(o_heads, axis=1)
        ogm_ref[sl, GMLP_WIDTH:] = _rms(o_mem, mon_ref[...]).astype(BF16)

    finish(slice(None), x_ref.shape[0], project(slice(None)))


def _inproj(x2, pos, tables, attn_norm, w_in_r, q_a_norm, kv_a_norm, krn_tile, gmlp_v_norm,
            w_spatial, bs_tile, mk, mv, mem_q_norm, gmlp_out_norm, mem_out_norm, seq):
    N, D = x2.shape
    tm = TOK_TILE
    tiles_per_seq = seq // tm
    invf_tile, sgn_tile = tables
    M = mk.shape[1]
    row = lambda w: pl.BlockSpec((tm, w), lambda i: (i, 0))
    batch_blk = pl.BlockSpec((1, M, MEM_WIDTH), lambda i: (i // tiles_per_seq, 0, 0))
    return pl.pallas_call(
        _inproj_kernel,
        grid=(N // tm,),
        in_specs=[row(D), row(1), _const_spec((1, LANES)), _const_spec((1, LANES)),
                  _const_spec((1, D)), _const_spec((D, _IN_COLS_PAD)),
                  _const_spec((1, Q_LORA)), _const_spec((1, KV_LORA)), _const_spec((1, LANES)),
                  _const_spec((1, GMLP_WIDTH)), _const_spec((GMLP_GROUPS, CHUNK, CHUNK)),
                  _const_spec((CHUNK, GMLP_WIDTH)), batch_blk, batch_blk,
                  _const_spec((1, MEM_HEAD_DIM)), _const_spec((1, GMLP_WIDTH)),
                  _const_spec((1, MEM_WIDTH))],
        out_specs=[row(Q_LORA), row(KV_LORA), row(LANES), row(GMLP_WIDTH + MEM_WIDTH),
                   row(2 * LANES)],
        out_shape=[jax.ShapeDtypeStruct((N, Q_LORA), BF16),
                   jax.ShapeDtypeStruct((N, KV_LORA), BF16),
                   jax.ShapeDtypeStruct((N, LANES), BF16),
                   jax.ShapeDtypeStruct((N, GMLP_WIDTH + MEM_WIDTH), BF16),
                   jax.ShapeDtypeStruct((N, 2 * LANES), F32)],
        compiler_params=pltpu.CompilerParams(dimension_semantics=("arbitrary",),
                                             vmem_limit_bytes=VMEM_LIMIT),
        name="inproj",
    )(x2, pos, invf_tile, sgn_tile, attn_norm.reshape(1, D), w_in_r,
      q_a_norm.reshape(1, -1), kv_a_norm.reshape(1, -1), krn_tile, gmlp_v_norm.reshape(1, -1),
      w_spatial, bs_tile, mk, mv, mem_q_norm.reshape(1, -1), gmlp_out_norm.reshape(1, -1),
      mem_out_norm.reshape(1, -1))


def _qkv_kernel(cq_ref, ckv_ref, kpe_ref, cs_ref, wq_ref, wkv_ref, qnn_ref, qrn_ref, knn_ref,
                q_ref, k_ref, v_ref):
    cos = cs_ref[:, :LANES]
    sin_signed = cs_ref[:, LANES:]
    scale = (MLA_NOPE + MLA_ROPE) ** -0.5 * np.log2(np.e)
    qr = _dot(cq_ref[...], wq_ref[...])
    kvr = _dot(ckv_ref[...], wkv_ref[...])
    kpe = kpe_ref[...]
    for h in range(MLA_HEADS):
        o = h * HEAD_PAD
        qn = _rms(qr[:, o:o + MLA_NOPE], qnn_ref[...]) * scale
        qt = _rms(qr[:, o + MLA_NOPE:o + HEAD_PAD], qrn_ref[...], MLA_ROPE)
        qt = _rope_tile(qt, cos, sin_signed) * scale
        q_ref[:, o:o + MLA_NOPE] = qn.astype(BF16)
        q_ref[:, o + MLA_NOPE:o + HEAD_PAD] = qt.astype(BF16)
        ko = h * (MLA_NOPE + MLA_V)
        k_ref[:, o:o + MLA_NOPE] = _rms(kvr[:, ko:ko + MLA_NOPE], knn_ref[...]).astype(BF16)
        k_ref[:, o + MLA_NOPE:o + HEAD_PAD] = kpe
        v_ref[:, h * MLA_V:(h + 1) * MLA_V] = kvr[:, ko + MLA_NOPE:ko + MLA_NOPE + MLA_V].astype(BF16)


def _qkv(cq, ckv, kpe, cs, wq_pad, wkv, q_nope_norm, qrn_tile, k_nope_norm):
    N = cq.shape[0]
    tm = TOK_TILE
    row = lambda w: pl.BlockSpec((tm, w), lambda i: (i, 0))
    return pl.pallas_call(
        _qkv_kernel,
        grid=(N // tm,),
        in_specs=[row(Q_LORA), row(KV_LORA), row(LANES), row(2 * LANES),
                  _const_spec(wq_pad.shape), _const_spec(wkv.shape),
                  _const_spec((1, MLA_NOPE)), _const_spec((1, LANES)), _const_spec((1, MLA_NOPE))],
        out_specs=[row(MLA_HEADS * HEAD_PAD), row(MLA_HEADS * HEAD_PAD), row(MLA_HEADS * MLA_V)],
        out_shape=[jax.ShapeDtypeStruct((N, MLA_HEADS * HEAD_PAD), BF16),
                   jax.ShapeDtypeStruct((N, MLA_HEADS * HEAD_PAD), BF16),
                   jax.ShapeDtypeStruct((N, MLA_HEADS * MLA_V), BF16)],
        compiler_params=pltpu.CompilerParams(dimension_semantics=("arbitrary",),
                                             vmem_limit_bytes=VMEM_LIMIT),
        name="qkv",
    )(cq, ckv, kpe, cs, wq_pad, wkv, q_nope_norm.reshape(1, -1), qrn_tile,
      k_nope_norm.reshape(1, -1))


def _attn_kernel(q_ref, k_ref, v_ref, o_ref):
    tq, tk = ATT_TQ, ATT_TK
    qi = pl.program_id(2)
    qs = [q_ref[0, :, h * HEAD_PAD:(h + 1) * HEAD_PAD] for h in range(ATT_HPS)]

    def scores(h, kb):
        k0 = pl.multiple_of(kb * tk, tk)
        return _dot_nt(k_ref[0, pl.ds(k0, tk), h * HEAD_PAD:(h + 1) * HEAD_PAD], qs[h])

    def head_step(h, kb, s, carry, masked):
        m, l, acc = carry
        k0 = pl.multiple_of(kb * tk, tk)
        if masked:
            kpos = kb * tk + lax.broadcasted_iota(jnp.int32, (tk, tq), 0)
            qpos = qi * tq + lax.broadcasted_iota(jnp.int32, (tk, tq), 1)
            s = jnp.where(kpos <= qpos, s, -jnp.inf)
        m_new = jnp.maximum(m, jnp.max(s, axis=0, keepdims=True))
        alpha = jnp.exp2(m - m_new)
        p = jnp.exp2(s - m_new)
        l = alpha * l + jnp.sum(p, axis=0, keepdims=True)
        v = v_ref[0, pl.ds(k0, tk), h * MLA_V:(h + 1) * MLA_V]
        pv = lax.dot_general(v, p.astype(BF16), (((0,), (0,)), ((), ())),
                             preferred_element_type=F32)
        return m_new, l, alpha * acc + pv

    def step(kb, carries, masked):
        ss = [scores(h, kb) for h in range(ATT_HPS)]
        return tuple(head_step(h, kb, ss[h], carries[h], masked) for h in range(ATT_HPS))

    init = tuple((jnp.full((1, tq), -jnp.inf, F32), jnp.zeros((1, tq), F32),
                  jnp.zeros((MLA_V, tq), F32)) for _ in range(ATT_HPS))
    n_full = (qi * tq) // tk
    carries = lax.fori_loop(0, n_full, lambda kb, c: step(kb, c, False), init)
    carries = step(n_full, carries, True)
    for h in range(ATT_HPS):
        _, l, acc = carries[h]
        o_ref[0, :, h * MLA_V:(h + 1) * MLA_V] = jnp.transpose(acc / l).astype(BF16)


def _attention(q, k, v, batch, seq):
    q3 = q.reshape(batch, seq, MLA_HEADS * HEAD_PAD)
    k3 = k.reshape(batch, seq, MLA_HEADS * HEAD_PAD)
    v3 = v.reshape(batch, seq, MLA_HEADS * MLA_V)
    hp, hv = ATT_HPS * HEAD_PAD, ATT_HPS * MLA_V
    assert ATT_TK % ATT_TQ == 0 and seq % ATT_TK == 0
    return pl.pallas_call(
        _attn_kernel,
        grid=(batch, MLA_HEADS // ATT_HPS, seq // ATT_TQ),
        in_specs=[pl.BlockSpec((1, ATT_TQ, hp), lambda b, h, i: (b, i, h)),
                  pl.BlockSpec((1, seq, hp), lambda b, h, i: (b, 0, h)),
                  pl.BlockSpec((1, seq, hv), lambda b, h, i: (b, 0, h))],
        out_specs=pl.BlockSpec((1, ATT_TQ, hv), lambda b, h, i: (b, i, h)),
        out_shape=jax.ShapeDtypeStruct((batch, seq, MLA_HEADS * MLA_V), BF16),
        compiler_params=pltpu.CompilerParams(
            dimension_semantics=("arbitrary", "arbitrary", "arbitrary"),
            vmem_limit_bytes=VMEM_LIMIT),
        name="attn",
    )(q3, k3, v3)


def _outproj_kernel(x_ref, oa_ref, ogm_ref, aon_ref, woa_ref, wob_ref, fn_ref, wr_ref, br_ref,
                    x1_ref, h2_ref, ti_ref, tg_ref):
    wr = wr_ref[...]
    w_hi = wr.astype(BF16)
    w_lo = (wr - w_hi.astype(F32)).astype(BF16)
    w_cat = jnp.concatenate([w_hi, w_lo], axis=1)

    def rows(sl, n):
        oa = _rms(oa_ref[sl, :].astype(F32), aon_ref[...]).astype(BF16)
        x1 = x_ref[sl, :] + _dot(oa, woa_ref[...]) + _dot(ogm_ref[sl, :], wob_ref[...])
        x1_ref[sl, :] = x1
        h2 = _rms(x1, fn_ref[...])
        h2_ref[sl, :] = h2
        h_hi = h2.astype(BF16)
        h_lo = (h2 - h_hi.astype(F32)).astype(BF16)
        hh = _dot(h_hi, w_cat)
        logits = hh[:, :LANES] + (_dot(h_lo, w_hi) + hh[:, LANES:]) + br_ref[...]
        lane = lax.broadcasted_iota(jnp.int32, (n, LANES), 1)
        lg = jnp.where(lane < N_EXPERTS, logits, -jnp.inf)
        vals, idxs = [], []
        for _ in range(TOP_K):
            m = jnp.max(lg, axis=-1, keepdims=True)
            am = jnp.min(jnp.where(lg == m, lane, LANES), axis=-1, keepdims=True)
            vals.append(m)
            idxs.append(am)
            lg = jnp.where(lane == am, -jnp.inf, lg)
        es = [jnp.exp(v - vals[0]) for v in vals]
        denom = es[0] + es[1] + es[2] + es[3]
        ti = jnp.zeros((n, LANES), jnp.int32)
        tg = jnp.zeros((n, LANES), F32)
        for kk in range(TOP_K):
            ti = jnp.where(lane == kk, idxs[kk], ti)
            tg = jnp.where(lane == kk, es[kk] / denom, tg)
        ti_ref[sl, :] = ti
        tg_ref[sl, :] = tg

    rows(slice(None), x_ref.shape[0])


def _outproj(x2, o_mla, ogm, mla_out_norm, wo, ffn_norm, wr_pad, br_pad):
    N, D = x2.shape
    tm = TOK_TILE
    row = lambda w: pl.BlockSpec((tm, w), lambda i: (i, 0))
    wa = o_mla.shape[1]
    wb = ogm.shape[1]
    assert wa == wb and wo.shape[0] == wa + wb
    return pl.pallas_call(
        _outproj_kernel,
        grid=(N // tm,),
        in_specs=[row(D), row(wa), row(wb), _const_spec((1, wa)),
                  pl.BlockSpec((wa, D), lambda i: (0, 0)), pl.BlockSpec((wb, D), lambda i: (1, 0)),
                  _const_spec((1, D)), _const_spec((D, LANES)), _const_spec((1, LANES))],
        out_specs=[row(D), row(D), row(LANES), row(LANES)],
        out_shape=[jax.ShapeDtypeStruct((N, D), F32), jax.ShapeDtypeStruct((N, D), F32),
                   jax.ShapeDtypeStruct((N, LANES), jnp.int32),
                   jax.ShapeDtypeStruct((N, LANES), F32)],
        compiler_params=pltpu.CompilerParams(dimension_semantics=("arbitrary",),
                                             vmem_limit_bytes=VMEM_LIMIT),
        name="outproj",
    )(x2, o_mla, ogm, mla_out_norm.reshape(1, -1), wo, wo, ffn_norm.reshape(1, -1), wr_pad,
      br_pad)


def _pack_bf16_pairs(x):
    k = x.shape[1] // 2
    lo = pltpu.bitcast(x[:, :k].astype(BF16).astype(F32), jnp.uint32)
    hi = pltpu.bitcast(x[:, k:].astype(BF16).astype(F32), jnp.uint32)
    return hi | (lo >> 16)


def _unpack_bf16_pairs(w):
    lo = pltpu.bitcast(w << 16, F32).astype(BF16)
    hi = pltpu.bitcast(w & jnp.uint32(0xFFFF0000), F32).astype(BF16)
    return jnp.concatenate([lo, hi], axis=1)


def _dispatch_kernel(dest_ref, zb_ref, nzb_ref, h2_ref, xs_hbm, pk, zblk, sem, sem_z):
    tm = h2_ref.shape[0]
    bm = MOE_TM
    i = pl.program_id(0)
    base = i * (tm * TOP_K)
    b = i % 2
    pk[b] = _pack_bf16_pairs(h2_ref[...])

    def wait_tile(bb):
        for kk in range(TOP_K):
            pltpu.make_async_copy(pk.at[bb], xs_hbm.at[pl.ds(0, tm), :], sem.at[bb]).wait()

    @pl.when(i == 0)
    def _zero_blocks():
        zblk[...] = jnp.zeros(zblk.shape, zblk.dtype)

        def zcopy(n):
            d0 = pl.multiple_of(zb_ref[n] * bm, bm)
            return pltpu.make_async_copy(zblk, xs_hbm.at[pl.ds(d0, bm), :], sem_z)

        def issue(n, c):
            zcopy(n).start()
            return c
        lax.fori_loop(0, nzb_ref[0], issue, 0)

        def finish(n, c):
            zcopy(n).wait()
            return c
        lax.fori_loop(0, nzb_ref[0], finish, 0)

    def issue(g, c):
        r0 = pl.multiple_of(g * SUBLANES, SUBLANES)
        rows = pk.at[b, pl.ds(r0, SUBLANES), :]
        for s in range(SUBLANES):
            for kk in range(TOP_K):
                d = dest_ref[base + (r0 + s) * TOP_K + kk]
                pltpu.make_async_copy(rows.at[pl.ds(s, 1), :], xs_hbm.at[pl.ds(d, 1), :],
                                      sem.at[b]).start()
        return c
    lax.fori_loop(0, tm // SUBLANES, issue, 0)

    @pl.when(i >= 1)
    def _():
        wait_tile(1 - b)

    @pl.when(i == pl.num_programs(0) - 1)
    def _():
        wait_tile(b)


def _dispatch(h2, dest, zero_blocks, n_zero, n_rows):
    N, D = h2.shape
    tm = DSP_TILE
    grid_spec = pltpu.PrefetchScalarGridSpec(
        num_scalar_prefetch=3,
        grid=(N // tm,),
        in_specs=[pl.BlockSpec((tm, D), lambda i, *_: (i, 0))],
        out_specs=pl.BlockSpec(memory_space=pl.ANY),
        scratch_shapes=[pltpu.VMEM((2, tm, D // 2), jnp.uint32),
                        pltpu.VMEM((MOE_TM, D // 2), jnp.uint32),
                        pltpu.SemaphoreType.DMA((2,)), pltpu.SemaphoreType.DMA(())],
    )
    return pl.pallas_call(
        _dispatch_kernel,
        grid_spec=grid_spec,
        out_shape=jax.ShapeDtypeStruct((n_rows, D // 2), jnp.uint32),
        compiler_params=pltpu.CompilerParams(dimension_semantics=("arbitrary",),
                                             vmem_limit_bytes=VMEM_LIMIT),
        name="dispatch",
    )(dest, zero_blocks, n_zero, h2)


def _moe_kernel(ie_ref, ist_ref, inb_ref, tail_ref,
                xs_hbm, wgu_hbm, wd_hbm, bgu_ref, bd_ref,
                ys_hbm,
                xbuf, acc, wg_buf, wu_buf, wd_buf, zblk, sem_x, sem_o, sem_w, sem_z):
    tm = MOE_TM
    i = pl.program_id(0)
    n_items = pl.num_programs(0)
    nblk = inb_ref[i]
    slot = i % 2
    n_blocks = ys_hbm.shape[0] // tm
    tf = wg_buf.shape[2]
    nj = wd_hbm.shape[1] // tf

    def x_copy(it, m):
        s0 = pl.multiple_of(ist_ref[it] + m * tm, tm)
        r0 = pl.multiple_of(m * tm, tm)
        return pltpu.make_async_copy(xs_hbm.at[pl.ds(s0, tm), :],
                                     xbuf.at[it % 2, pl.ds(r0, tm), :], sem_x.at[it % 2])

    def y_copy(it, m):
        d0 = pl.multiple_of(ist_ref[it] + m * tm, tm)
        r0 = pl.multiple_of(m * tm, tm)
        return pltpu.make_async_copy(acc.at[it % 2, pl.ds(r0, tm), :],
                                     ys_hbm.at[pl.ds(d0, tm), :], sem_o.at[it % 2])

    def for_blocks(it, fn):
        def body(m, c):
            fn(it, m)
            return c
        lax.fori_loop(0, inb_ref[it], body, 0)

    start_x = lambda it: for_blocks(it, lambda a, m: x_copy(a, m).start())
    wait_x = lambda it: for_blocks(it, lambda a, m: x_copy(a, m).wait())
    start_y = lambda it: for_blocks(it, lambda a, m: y_copy(a, m).start())
    wait_y = lambda it: for_blocks(it, lambda a, m: y_copy(a, m).wait())

    def zero_copy(b):
        d0 = pl.multiple_of(b * tm, tm)
        return pltpu.make_async_copy(zblk, ys_hbm.at[pl.ds(d0, tm), :], sem_z)

    n_ws = wg_buf.shape[0]

    def w_slot(it, j):
        return (it * nj + j) % n_ws

    def w_copies(it, j):
        e = ie_ref[it]
        ws = w_slot(it, j)
        c0 = pl.multiple_of(j * tf, tf)
        c1 = pl.multiple_of(nj * tf + j * tf, tf)
        return (pltpu.make_async_copy(wgu_hbm.at[e, :, pl.ds(c0, tf)], wg_buf.at[ws], sem_w.at[ws]),
                pltpu.make_async_copy(wgu_hbm.at[e, :, pl.ds(c1, tf)], wu_buf.at[ws], sem_w.at[ws]),
                pltpu.make_async_copy(wd_hbm.at[e, pl.ds(c0, tf), :], wd_buf.at[ws], sem_w.at[ws]))

    def start_w(it, j):
        for cp in w_copies(it, j):
            cp.start()

    def wait_w(it, j):
        for cp in w_copies(it, j):
            cp.wait()

    @pl.when(i == 0)
    def _first_step():
        zblk[...] = jnp.zeros(zblk.shape, zblk.dtype)

        def issue(b, c):
            zero_copy(b).start()
            return c
        lax.fori_loop(tail_ref[0], n_blocks, issue, 0)
        start_x(0)

        @pl.when(nblk > 0)
        def _():
            for a in range(n_ws - 1):
                start_w(0, a)

    wait_x(i)

    @pl.when(i + 1 < n_items)
    def _():
        start_x(i + 1)

    @pl.when(i >= 2)
    def _():
        wait_y(i - 2)

    def init(m, c):
        r0 = pl.multiple_of(m * tm, tm)
        acc[slot, pl.ds(r0, tm), :] = jnp.broadcast_to(bd_ref[0], (tm, acc.shape[2]))
        return c
    lax.fori_loop(0, nblk, init, 0)

    def ffn_rows(j, n):
        ws = w_slot(i, j)
        x = _unpack_bf16_pairs(xbuf[slot, pl.ds(0, n), :])
        g = jnp.minimum(_dot(x, wg_buf[ws].astype(BF16)) + bgu_ref[0, pl.ds(j, 1), :], SWIGLU_LIMIT)
        u = jnp.clip(_dot(x, wu_buf[ws].astype(BF16)) + bgu_ref[0, pl.ds(nj + j, 1), :],
                     -SWIGLU_LIMIT, SWIGLU_LIMIT)
        a = (u + 1.0) * (g * jax.nn.sigmoid(SWIGLU_ALPHA * g))
        acc[slot, pl.ds(0, n), :] += _dot(a.astype(BF16), wd_buf[ws].astype(BF16))

    def chunk(j, c):
        wait_w(i, j)
        ja = j + (n_ws - 1)

        @pl.when(ja < nj)
        def _():
            start_w(i, ja)

        @pl.when(jnp.logical_and(ja >= nj, i + 1 < n_items))
        def _():
            @pl.when(inb_ref[i + 1] > 0)
            def _():
                start_w(i + 1, ja - nj)

        for nb in range(1, MOE_BPI + 1):
            pl.when(nblk == nb)(functools.partial(ffn_rows, j, nb * tm))
        return c

    @pl.when(nblk > 0)
    def _chunks():
        lax.fori_loop(0, nj, chunk, 0)

    start_y(i)

    @pl.when(i == n_items - 1)
    def _last_step():
        @pl.when(i >= 1)
        def _():
            wait_y(i - 1)
        wait_y(i)

        def finish(b, c):
            zero_copy(b).wait()
            return c
        lax.fori_loop(tail_ref[0], n_blocks, finish, 0)


def _moe(xs, item_e, item_start, item_nblk, tail_blk, w_gate_up, b_gate_up, w_down, b_down):
    n_rows = xs.shape[0]
    E, D, F2 = w_gate_up.shape
    F = F2 // 2
    tf = MOE_TF
    nj = F // tf
    nw = MOE_WSLOTS
    assert 2 <= nw <= nj + 1
    n_items = item_e.shape[0]
    rows = MOE_BPI * MOE_TM
    any_spec = pl.BlockSpec(memory_space=pl.ANY)
    grid_spec = pltpu.PrefetchScalarGridSpec(
        num_scalar_prefetch=4,
        grid=(n_items,),
        in_specs=[any_spec, any_spec, any_spec,
                  pl.BlockSpec((1, 2 * nj, tf), lambda i, ie, ist, inb, tail: (ie[i], 0, 0)),
                  pl.BlockSpec((1, 1, D), lambda i, ie, ist, inb, tail: (ie[i], 0, 0))],
        out_specs=any_spec,
        scratch_shapes=[pltpu.VMEM((2, rows, D // 2), jnp.uint32), pltpu.VMEM((2, rows, D), F32),
                        pltpu.VMEM((nw, D, tf), F32), pltpu.VMEM((nw, D, tf), F32),
                        pltpu.VMEM((nw, tf, D), F32), pltpu.VMEM((MOE_TM, D), F32),
                        pltpu.SemaphoreType.DMA((2,)), pltpu.SemaphoreType.DMA((2,)),
                        pltpu.SemaphoreType.DMA((nw,)), pltpu.SemaphoreType.DMA(())],
    )
    return pl.pallas_call(
        _moe_kernel,
        grid_spec=grid_spec,
        out_shape=jax.ShapeDtypeStruct((n_rows, D), F32),
        compiler_params=pltpu.CompilerParams(dimension_semantics=("arbitrary",),
                                             vmem_limit_bytes=VMEM_LIMIT),
        name="moe",
    )(item_e, item_start, item_nblk, tail_blk, xs, w_gate_up, w_down,
      b_gate_up.reshape(E, 2 * nj, tf), b_down.reshape(E, 1, D))


def _combine_kernel(pos_ref, x1_ref, g_ref, ys_hbm, o_ref, buf, sem):
    tm = CMB_TILE
    i = pl.program_id(0)

    def start_gather(t):
        base = t * (tm * TOP_K)
        b = t % 2

        def issue(g, c):
            r0 = pl.multiple_of(g * SUBLANES, SUBLANES)
            for s in range(SUBLANES):
                for kk in range(TOP_K):
                    p = pos_ref[base + (r0 + s) * TOP_K + kk]
                    pltpu.make_async_copy(ys_hbm.at[pl.ds(p, 1), :],
                                          buf.at[b, kk, pl.ds(r0, SUBLANES), :].at[pl.ds(s, 1), :],
                                          sem.at[b]).start()
            return c
        lax.fori_loop(0, tm // SUBLANES, issue, 0)

    @pl.when(i == 0)
    def _():
        start_gather(0)

    @pl.when(i + 1 < pl.num_programs(0))
    def _():
        start_gather(i + 1)

    b = i % 2
    for kk in range(TOP_K):
        pltpu.make_async_copy(ys_hbm.at[pl.ds(0, tm), :], buf.at[b, kk], sem.at[b]).wait()
    out = x1_ref[...]
    for kk in range(TOP_K):
        out = out + g_ref[:, kk:kk + 1] * buf[b, kk]
    o_ref[...] = out


def _combine(x1, gates, ys, dest):
    N, D = x1.shape
    tm = CMB_TILE
    grid_spec = pltpu.PrefetchScalarGridSpec(
        num_scalar_prefetch=1,
        grid=(N // tm,),
        in_specs=[pl.BlockSpec((tm, D), lambda i, pos: (i, 0)),
                  pl.BlockSpec((tm, LANES), lambda i, pos: (i, 0)),
                  pl.BlockSpec(memory_space=pl.ANY)],
        out_specs=pl.BlockSpec((tm, D), lambda i, pos: (i, 0)),
        scratch_shapes=[pltpu.VMEM((2, TOP_K, tm, D), F32), pltpu.SemaphoreType.DMA((2,))],
    )
    return pl.pallas_call(
        _combine_kernel,
        grid_spec=grid_spec,
        out_shape=jax.ShapeDtypeStruct((N, D), F32),
        compiler_params=pltpu.CompilerParams(dimension_semantics=("arbitrary",),
                                             vmem_limit_bytes=VMEM_LIMIT),
        name="combine",
    )(dest, x1, gates, ys)


def _routing(top_idx, n_tok):
    tm, bpi = MOE_TM, MOE_BPI
    nk = n_tok * TOP_K
    experts = jnp.arange(N_EXPERTS, dtype=jnp.int32)
    e_flat = top_idx.reshape(-1)
    onehot = (e_flat[:, None] == experts[None, :]).astype(jnp.int32)
    csum = jnp.cumsum(onehot, axis=0)
    rank = jnp.sum(csum * onehot, axis=1) - 1
    counts = csum[-1]
    nb = (counts + tm - 1) // tm
    bend = jnp.cumsum(nb)
    bstart = bend - nb
    dest = (jnp.sum(onehot * bstart[None, :], axis=1) * tm + rank).astype(jnp.int32)
    n_blocks = -(-(nk + N_EXPERTS * (tm - 1)) // tm)
    n_rows = n_blocks * tm
    tail_blk = bend[-1:].astype(jnp.int32)
    zb_e = jnp.where(nb > 0, bend - 1, -1)
    zb_t = jnp.arange(n_blocks, dtype=jnp.int32)
    zb_all = jnp.concatenate([zb_e, jnp.where(zb_t >= bend[-1], zb_t, -1)]).astype(jnp.int32)
    order = jnp.argsort(zb_all < 0, stable=True)
    zero_blocks = zb_all[order]
    n_zero = jnp.sum(zb_all >= 0).astype(jnp.int32).reshape(1)
    n_items = n_blocks // bpi + N_EXPERTS
    items_e = (nb + bpi - 1) // bpi
    iend = jnp.cumsum(items_e)
    istart = iend - items_e
    slot = jnp.arange(n_items, dtype=jnp.int32)
    valid = slot < iend[-1]
    exp_of = jnp.minimum(jnp.searchsorted(iend, slot, side="right"), N_EXPERTS - 1).astype(jnp.int32)
    last_e = jnp.max(jnp.where(nb > 0, experts, 0))
    local = slot - istart[exp_of]
    n_it = jnp.maximum(items_e[exp_of], 1)
    base, rem = nb[exp_of] // n_it, nb[exp_of] % n_it
    first_blk = bstart[exp_of] + local * base + jnp.minimum(local, rem)
    item_e = jnp.where(valid, exp_of, last_e).astype(jnp.int32)
    item_start = jnp.where(valid, first_blk * tm, 0).astype(jnp.int32)
    item_nblk = jnp.where(valid, base + (local < rem), 0).astype(jnp.int32)
    return dest, zero_blocks, n_zero, item_e, item_start, item_nblk, tail_blk, n_rows


def _rope_lane_tile(v):
    half = MLA_ROPE // 2
    z = jnp.zeros((half,), v.dtype)
    return jnp.concatenate([v[:half], z, v[half:], z]).reshape(1, LANES)


def _rope_cols(w):
    half = MLA_ROPE // 2
    z = jnp.zeros((w.shape[0], half), w.dtype)
    return jnp.concatenate([w[:, :half], z, w[:, half:], z], axis=1)


def _layer(x, mem, positions, attn_norm, w_in, q_a_norm, w_q_b, kv_a_norm, w_kv_b,
           q_nope_norm, q_rope_norm, k_nope_norm, k_rope_norm,
           gmlp_v_norm, w_spatial, b_spatial,
           mem_norm, w_mem_kv, mem_q_norm, mem_k_norm,
           mla_out_norm, gmlp_out_norm, mem_out_norm, w_o,
           ffn_norm, w_router, b_router, w_gate_up, b_gate_up, w_down, b_down):
    B, S, D = x.shape
    N = B * S
    x2 = x.reshape(N, D)

    half = MLA_ROPE // 2
    r1 = Q_LORA + KV_LORA + half
    zc = jnp.zeros((D, half), BF16)
    w_in_b = w_in.astype(BF16)
    w_in_r = jnp.concatenate([w_in_b[:, :r1], zc, w_in_b[:, r1:r1 + half], zc, w_in_b[:, r1 + half:]],
                             axis=1)
    wq = w_q_b.reshape(Q_LORA, MLA_HEADS, MLA_NOPE + MLA_ROPE)
    wq_rope = jax.vmap(_rope_cols, in_axes=1, out_axes=1)(wq[:, :, MLA_NOPE:])
    wq_pad = jnp.concatenate([wq[:, :, :MLA_NOPE], wq_rope], axis=2).reshape(Q_LORA, MLA_HEADS * HEAD_PAD)
    wq_pad = wq_pad.astype(BF16)
    inv_freq = ROPE_BASE ** (-jnp.arange(half, dtype=F32) / half)
    zf = jnp.zeros((half,), F32)
    invf_tile = jnp.concatenate([inv_freq, zf, inv_freq, zf]).reshape(1, LANES)
    sgn_tile = jnp.concatenate([-jnp.ones((half,), F32), zf, jnp.ones((half,), F32), zf]).reshape(1, LANES)
    bs_tile = jnp.repeat(b_spatial.T, GMLP_CH, axis=1)

    mk, mv = _mem_kv(mem, mem_norm, w_mem_kv, mem_k_norm)
    cq, ckv, kpe, ogm, cs = _inproj(
        x2, positions.reshape(N, 1), (invf_tile, sgn_tile), attn_norm, w_in_r, q_a_norm, kv_a_norm,
        _rope_lane_tile(k_rope_norm), gmlp_v_norm, w_spatial, bs_tile, mk, mv, mem_q_norm,
        gmlp_out_norm, mem_out_norm, S)
    q, k, v = _qkv(cq, ckv, kpe, cs, wq_pad, w_kv_b.astype(BF16), q_nope_norm,
                   _rope_lane_tile(q_rope_norm), k_nope_norm)
    o_mla = _attention(q, k, v, B, S).reshape(N, MLA_HEADS * MLA_V)

    wr_pad = jnp.pad(w_router, ((0, 0), (0, LANES - N_EXPERTS)))
    br_pad = jnp.pad(b_router, (0, LANES - N_EXPERTS)).reshape(1, LANES)
    x1, h2, ti, tg = _outproj(x2, o_mla, ogm, mla_out_norm, w_o.astype(BF16), ffn_norm, wr_pad,
                              br_pad)

    dest, zero_blocks, n_zero, item_e, item_start, item_nblk, tail_blk, n_rows = _routing(
        ti[:, :TOP_K], N)
    xs = _dispatch(h2, dest, zero_blocks, n_zero, n_rows)
    ys = _moe(xs, item_e, item_start, item_nblk, tail_blk, w_gate_up, b_gate_up, w_down, b_down)
    out = _combine(x1, tg, ys, dest)
    return out.reshape(B, S, D)


def kernel(x, mem, positions, attn_norm, w_in, q_a_norm, w_q_b, kv_a_norm, w_kv_b, q_nope_norm, q_rope_norm, k_nope_norm, k_rope_norm, gmlp_v_norm, w_spatial, b_spatial, mem_norm, w_mem_kv, mem_q_norm, mem_k_norm, mla_out_norm, gmlp_out_norm, mem_out_norm, w_o, ffn_norm, w_router, b_router, w_gate_up, b_gate_up, w_down, b_down):
    depth = attn_norm.shape[0]
    for l in range(depth):
        x = _layer(x, mem, positions, attn_norm[l], w_in[l], q_a_norm[l], w_q_b[l], kv_a_norm[l],
                   w_kv_b[l], q_nope_norm[l], q_rope_norm[l], k_nope_norm[l], k_rope_norm[l],
                   gmlp_v_norm[l], w_spatial[l], b_spatial[l], mem_norm[l], w_mem_kv[l],
                   mem_q_norm[l], mem_k_norm[l], mla_out_norm[l], gmlp_out_norm[l],
                   mem_out_norm[l], w_o[l], ffn_norm[l], w_router[l], b_router[l], w_gate_up[l],
                   b_gate_up[l], w_down[l], b_down[l])
    return x
```

```python
import functools

import jax
import jax.numpy as jnp
import numpy as np
from jax import lax
from jax.experimental import pallas as pl
from jax.experimental.pallas import tpu as pltpu

F32 = jnp.float32
BF16 = jnp.bfloat16

EPS = 1e-6
LANES = 128
SUBLANES = 8
VMEM_LIMIT = 56 * 1024 * 1024

MLA_HEADS = 8
MLA_NOPE = 128
MLA_ROPE = 64
MLA_V = 128
Q_LORA = 512
KV_LORA = 512
GMLP_GROUPS = 4
GMLP_CH = 128
GMLP_WIDTH = GMLP_GROUPS * GMLP_CH
CHUNK = 128
MEM_HEADS = 4
MEM_HEAD_DIM = 128
MEM_WIDTH = MEM_HEADS * MEM_HEAD_DIM
N_EXPERTS = 32
TOP_K = 4
SWIGLU_ALPHA = 1.702
SWIGLU_LIMIT = 7.0
ROPE_BASE = 10000.0
HEAD_PAD = 2 * LANES

TOK_TILE = 512
ATT_TQ = 512
ATT_TK = 512
ATT_HPS = 4
MOE_TM = 128
MOE_BPI = 10
MOE_TF = 256
MOE_WSLOTS = 3
DSP_TILE = 1024
CMB_TILE = 256


def _rms(x, g, n=None):
    n = x.shape[-1] if n is None else n
    ms = jnp.sum(x * x, axis=-1, keepdims=True) * (1.0 / n)
    return x * lax.rsqrt(ms + EPS) * g


def _gelu(x):
    return 0.5 * x * (1.0 + lax.erf(x * (2.0 ** -0.5)))


def _dot(a, b):
    return jnp.dot(a, b, preferred_element_type=F32)


def _dot_nt(a, b):
    return lax.dot_general(a, b, (((1,), (1,)), ((), ())), preferred_element_type=F32)


def _const_spec(shape):
    nd = len(shape)
    return pl.BlockSpec(shape, lambda *_: (0,) * nd)


def _mem_kv_kernel(mem_ref, g_ref, w_ref, kg_ref, mk_ref, mv_ref):
    m = mem_ref[0]
    hn = _rms(m, g_ref[...]).astype(BF16)
    kv = _dot(hn, w_ref[...])
    for h in range(MEM_HEADS):
        k = kv[:, h * MEM_HEAD_DIM:(h + 1) * MEM_HEAD_DIM]
        mk_ref[0, :, h * MEM_HEAD_DIM:(h + 1) * MEM_HEAD_DIM] = _rms(k, kg_ref[...]).astype(BF16)
    mv_ref[0] = kv[:, MEM_WIDTH:].astype(BF16)


def _mem_kv(mem, mem_norm, w_mem_kv, mem_k_norm):
    B, M, D = mem.shape
    return pl.pallas_call(
        _mem_kv_kernel,
        grid=(B,),
        in_specs=[pl.BlockSpec((1, M, D), lambda b: (b, 0, 0)),
                  _const_spec((1, D)),
                  _const_spec((D, 2 * MEM_WIDTH)),
                  _const_spec((1, MEM_HEAD_DIM))],
        out_specs=[pl.BlockSpec((1, M, MEM_WIDTH), lambda b: (b, 0, 0)),
                   pl.BlockSpec((1, M, MEM_WIDTH), lambda b: (b, 0, 0))],
        out_shape=[jax.ShapeDtypeStruct((B, M, MEM_WIDTH), BF16),
                   jax.ShapeDtypeStruct((B, M, MEM_WIDTH), BF16)],
        compiler_params=pltpu.CompilerParams(dimension_semantics=("arbitrary",),
                                             vmem_limit_bytes=VMEM_LIMIT),
        name="mem_kv",
    )(mem, mem_norm.reshape(1, D), w_mem_kv.astype(BF16), mem_k_norm.reshape(1, MEM_HEAD_DIM))


_O_CQ = 0
_O_CKV = _O_CQ + Q_LORA
_O_KR = _O_CKV + KV_LORA
_O_U = _O_KR + LANES
_O_VG = _O_U + GMLP_WIDTH
_O_QM = _O_VG + GMLP_WIDTH
_IN_COLS_PAD = _O_QM + MEM_WIDTH


def _rope_tile(t, cos, sin_signed):
    return t * cos + pltpu.roll(t, LANES // 2, axis=1) * sin_signed


def _inproj_kernel(x_ref, pos_ref, invf_ref, sgn_ref, an_ref, w_ref, qan_ref, kvan_ref, krn_ref,
                   gvn_ref, ws_ref, bs_ref, mk_ref, mv_ref, mqn_ref, gon_ref, mon_ref,
                   cq_ref, ckv_ref, kpe_ref, ogm_ref, cs_ref):
    def project(sl):
        h = _rms(x_ref[sl, :], an_ref[...]).astype(BF16)
        return _dot(h, w_ref[...])

    def finish(sl, n, z):
        cq_ref[sl, :] = _rms(z[:, _O_CQ:_O_CQ + Q_LORA], qan_ref[...]).astype(BF16)
        ckv_ref[sl, :] = _rms(z[:, _O_CKV:_O_CKV + KV_LORA], kvan_ref[...]).astype(BF16)

        ang = pos_ref[sl, :].astype(F32) * invf_ref[...]
        cos = jnp.cos(ang)
        sin_signed = jnp.sin(ang) * sgn_ref[...]
        cs_ref[sl, :LANES] = cos
        cs_ref[sl, LANES:] = sin_signed
        kr = _rms(z[:, _O_KR:_O_KR + LANES], krn_ref[...], MLA_ROPE)
        kpe_ref[sl, :] = _rope_tile(kr, cos, sin_signed).astype(BF16)

        u = _gelu(z[:, _O_U:_O_U + GMLP_WIDTH])
        vg = _gelu(z[:, _O_VG:_O_VG + GMLP_WIDTH])
        vg = _rms(vg, gvn_ref[...]).astype(BF16)
        row = lax.broadcasted_iota(jnp.int32, (CHUNK, CHUNK), 0)
        col = lax.broadcasted_iota(jnp.int32, (CHUNK, CHUNK), 1)
        sp_cols = []
        for g in range(GMLP_GROUPS):
            wsg = jnp.where(col <= row, ws_ref[g], 0.0).astype(BF16)
            sp_rows = [_dot(wsg, vg[c * CHUNK:(c + 1) * CHUNK, g * GMLP_CH:(g + 1) * GMLP_CH])
                       for c in range(n // CHUNK)]
            sp_cols.append(jnp.concatenate(sp_rows, axis=0))
        sp = jnp.concatenate(sp_cols, axis=1) + jnp.concatenate([bs_ref[...]] * (n // CHUNK), axis=0)
        ogm_ref[sl, :GMLP_WIDTH] = _rms(u * sp, gon_ref[...]).astype(BF16)

        o_heads = []
        for hd in range(MEM_HEADS):
            hs = slice(hd * MEM_HEAD_DIM, (hd + 1) * MEM_HEAD_DIM)
            q = z[:, _O_QM + hd * MEM_HEAD_DIM:_O_QM + (hd + 1) * MEM_HEAD_DIM]
            qn = (_rms(q, mqn_ref[...]) * (MEM_HEAD_DIM ** -0.5)).astype(BF16)
            s = _dot_nt(qn, mk_ref[0, :, hs])
            s = s - jnp.max(s, axis=-1, keepdims=True)
            p = jnp.exp(s)
            p = p / jnp.sum(p, axis=-1, keepdims=True)
            o_heads.append(_dot(p.astype(BF16), mv_ref[0, :, hs]))
        o_mem = jnp.concatenate(o_heads, axis=1)
        ogm_ref[sl, GMLP_WIDTH:] = _rms(o_mem, mon_ref[...]).astype(BF16)

    finish(slice(None), x_ref.shape[0], project(slice(None)))


def _inproj(x2, pos, tables, attn_norm, w_in_r, q_a_norm, kv_a_norm, krn_tile, gmlp_v_norm,
            w_spatial, bs_tile, mk, mv, mem_q_norm, gmlp_out_norm, mem_out_norm, seq):
    N, D = x2.shape
    tm = TOK_TILE
    tiles_per_seq = seq // tm
    invf_tile, sgn_tile = tables
    M = mk.shape[1]
    row = lambda w: pl.BlockSpec((tm, w), lambda i: (i, 0))
    batch_blk = pl.BlockSpec((1, M, MEM_WIDTH), lambda i: (i // tiles_per_seq, 0, 0))
    return pl.pallas_call(
        _inproj_kernel,
        grid=(N // tm,),
        in_specs=[row(D), row(1), _const_spec((1, LANES)), _const_spec((1, LANES)),
                  _const_spec((1, D)), _const_spec((D, _IN_COLS_PAD)),
                  _const_spec((1, Q_LORA)), _const_spec((1, KV_LORA)), _const_spec((1, LANES)),
                  _const_spec((1, GMLP_WIDTH)), _const_spec((GMLP_GROUPS, CHUNK, CHUNK)),
                  _const_spec((CHUNK, GMLP_WIDTH)), batch_blk, batch_blk,
                  _const_spec((1, MEM_HEAD_DIM)), _const_spec((1, GMLP_WIDTH)),
                  _const_spec((1, MEM_WIDTH))],
        out_specs=[row(Q_LORA), row(KV_LORA), row(LANES), row(GMLP_WIDTH + MEM_WIDTH),
                   row(2 * LANES)],
        out_shape=[jax.ShapeDtypeStruct((N, Q_LORA), BF16),
                   jax.ShapeDtypeStruct((N, KV_LORA), BF16),
                   jax.ShapeDtypeStruct((N, LANES), BF16),
                   jax.ShapeDtypeStruct((N, GMLP_WIDTH + MEM_WIDTH), BF16),
                   jax.ShapeDtypeStruct((N, 2 * LANES), F32)],
        compiler_params=pltpu.CompilerParams(dimension_semantics=("arbitrary",),
                                             vmem_limit_bytes=VMEM_LIMIT),
        name="inproj",
    )(x2, pos, invf_tile, sgn_tile, attn_norm.reshape(1, D), w_in_r,
      q_a_norm.reshape(1, -1), kv_a_norm.reshape(1, -1), krn_tile, gmlp_v_norm.reshape(1, -1),
      w_spatial, bs_tile, mk, mv, mem_q_norm.reshape(1, -1), gmlp_out_norm.reshape(1, -1),
      mem_out_norm.reshape(1, -1))


def _qkv_kernel(cq_ref, ckv_ref, kpe_ref, cs_ref, wq_ref, wkv_ref, qnn_ref, qrn_ref, knn_ref,
                q_ref, k_ref, v_ref):
    cos = cs_ref[:, :LANES]
    sin_signed = cs_ref[:, LANES:]
    scale = (MLA_NOPE + MLA_ROPE) ** -0.5 * np.log2(np.e)
    qr = _dot(cq_ref[...], wq_ref[...])
    kvr = _dot(ckv_ref[...], wkv_ref[...])
    kpe = kpe_ref[...]
    for h in range(MLA_HEADS):
        o = h * HEAD_PAD
        qn = _rms(qr[:, o:o + MLA_NOPE], qnn_ref[...]) * scale
        qt = _rms(qr[:, o + MLA_NOPE:o + HEAD_PAD], qrn_ref[...], MLA_ROPE)
        qt = _rope_tile(qt, cos, sin_signed) * scale
        q_ref[:, o:o + MLA_NOPE] = qn.astype(BF16)
        q_ref[:, o + MLA_NOPE:o + HEAD_PAD] = qt.astype(BF16)
        ko = h * (MLA_NOPE + MLA_V)
        k_ref[:, o:o + MLA_NOPE] = _rms(kvr[:, ko:ko + MLA_NOPE], knn_ref[...]).astype(BF16)
        k_ref[:, o + MLA_NOPE:o + HEAD_PAD] = kpe
        v_ref[:, h * MLA_V:(h + 1) * MLA_V] = kvr[:, ko + MLA_NOPE:ko + MLA_NOPE + MLA_V].astype(BF16)


def _qkv(cq, ckv, kpe, cs, wq_pad, wkv, q_nope_norm, qrn_tile, k_nope_norm):
    N = cq.shape[0]
    tm = TOK_TILE
    row = lambda w: pl.BlockSpec((tm, w), lambda i: (i, 0))
    return pl.pallas_call(
        _qkv_kernel,
        grid=(N // tm,),
        in_specs=[row(Q_LORA), row(KV_LORA), row(LANES), row(2 * LANES),
                  _const_spec(wq_pad.shape), _const_spec(wkv.shape),
                  _const_spec((1, MLA_NOPE)), _const_spec((1, LANES)), _const_spec((1, MLA_NOPE))],
        out_specs=[row(MLA_HEADS * HEAD_PAD), row(MLA_HEADS * HEAD_PAD), row(MLA_HEADS * MLA_V)],
        out_shape=[jax.ShapeDtypeStruct((N, MLA_HEADS * HEAD_PAD), BF16),
                   jax.ShapeDtypeStruct((N, MLA_HEADS * HEAD_PAD), BF16),
                   jax.ShapeDtypeStruct((N, MLA_HEADS * MLA_V), BF16)],
        compiler_params=pltpu.CompilerParams(dimension_semantics=("arbitrary",),
                                             vmem_limit_bytes=VMEM_LIMIT),
        name="qkv",
    )(cq, ckv, kpe, cs, wq_pad, wkv, q_nope_norm.reshape(1, -1), qrn_tile,
      k_nope_norm.reshape(1, -1))


def _attn_kernel(q_ref, k_ref, v_ref, o_ref):
    tq, tk = ATT_TQ, ATT_TK
    qi = pl.program_id(2)
    qs = [q_ref[0, :, h * HEAD_PAD:(h + 1) * HEAD_PAD] for h in range(ATT_HPS)]

    def scores(h, kb):
        k0 = pl.multiple_of(kb * tk, tk)
        return _dot_nt(k_ref[0, pl.ds(k0, tk), h * HEAD_PAD:(h + 1) * HEAD_PAD], qs[h])

    def head_step(h, kb, s, carry, masked):
        m, l, acc = carry
        k0 = pl.multiple_of(kb * tk, tk)
        if masked:
            kpos = kb * tk + lax.broadcasted_iota(jnp.int32, (tk, tq), 0)
            qpos = qi * tq + lax.broadcasted_iota(jnp.int32, (tk, tq), 1)
            s = jnp.where(kpos <= qpos, s, -jnp.inf)
        m_new = jnp.maximum(m, jnp.max(s, axis=0, keepdims=True))
        alpha = jnp.exp2(m - m_new)
        p = jnp.exp2(s - m_new)
        l = alpha * l + jnp.sum(p, axis=0, keepdims=True)
        v = v_ref[0, pl.ds(k0, tk), h * MLA_V:(h + 1) * MLA_V]
        pv = lax.dot_general(v, p.astype(BF16), (((0,), (0,)), ((), ())),
                             preferred_element_type=F32)
        return m_new, l, alpha * acc + pv

    def step(kb, carries, masked):
        ss = [scores(h, kb) for h in range(ATT_HPS)]
        return tuple(head_step(h, kb, ss[h], carries[h], masked) for h in range(ATT_HPS))

    init = tuple((jnp.full((1, tq), -jnp.inf, F32), jnp.zeros((1, tq), F32),
                  jnp.zeros((MLA_V, tq), F32)) for _ in range(ATT_HPS))
    n_full = (qi * tq) // tk
    carries = lax.fori_loop(0, n_full, lambda kb, c: step(kb, c, False), init)
    carries = step(n_full, carries, True)
    for h in range(ATT_HPS):
        _, l, acc = carries[h]
        o_ref[0, :, h * MLA_V:(h + 1) * MLA_V] = jnp.transpose(acc / l).astype(BF16)


def _attention(q, k, v, batch, seq):
    q3 = q.reshape(batch, seq, MLA_HEADS * HEAD_PAD)
    k3 = k.reshape(batch, seq, MLA_HEADS * HEAD_PAD)
    v3 = v.reshape(batch, seq, MLA_HEADS * MLA_V)
    hp, hv = ATT_HPS * HEAD_PAD, ATT_HPS * MLA_V
    assert ATT_TK % ATT_TQ == 0 and seq % ATT_TK == 0
    return pl.pallas_call(
        _attn_kernel,
        grid=(batch, MLA_HEADS // ATT_HPS, seq // ATT_TQ),
        in_specs=[pl.BlockSpec((1, ATT_TQ, hp), lambda b, h, i: (b, i, h)),
                  pl.BlockSpec((1, seq, hp), lambda b, h, i: (b, 0, h)),
                  pl.BlockSpec((1, seq, hv), lambda b, h, i: (b, 0, h))],
        out_specs=pl.BlockSpec((1, ATT_TQ, hv), lambda b, h, i: (b, i, h)),
        out_shape=jax.ShapeDtypeStruct((batch, seq, MLA_HEADS * MLA_V), BF16),
        compiler_params=pltpu.CompilerParams(
            dimension_semantics=("arbitrary", "arbitrary", "arbitrary"),
            vmem_limit_bytes=VMEM_LIMIT),
        name="attn",
    )(q3, k3, v3)


def _outproj_kernel(x_ref, oa_ref, ogm_ref, aon_ref, woa_ref, wob_ref, fn_ref, wr_ref, br_ref,
                    x1_ref, h2_ref, ti_ref, tg_ref):
    wr = wr_ref[...]
    w_hi = wr.astype(BF16)
    w_lo = (wr - w_hi.astype(F32)).astype(BF16)
    w_cat = jnp.concatenate([w_hi, w_lo], axis=1)

    def rows(sl, n):
        oa = _rms(oa_ref[sl, :].astype(F32), aon_ref[...]).astype(BF16)
        x1 = x_ref[sl, :] + _dot(oa, woa_ref[...]) + _dot(ogm_ref[sl, :], wob_ref[...])
        x1_ref[sl, :] = x1
        h2 = _rms(x1, fn_ref[...])
        h2_ref[sl, :] = h2
        h_hi = h2.astype(BF16)
        h_lo = (h2 - h_hi.astype(F32)).astype(BF16)
        hh = _dot(h_hi, w_cat)
        logits = hh[:, :LANES] + (_dot(h_lo, w_hi) + hh[:, LANES:]) + br_ref[...]
        lane = lax.broadcasted_iota(jnp.int32, (n, LANES), 1)
        lg = jnp.where(lane < N_EXPERTS, logits, -jnp.inf)
        vals, idxs = [], []
        for _ in range(TOP_K):
            m = jnp.max(lg, axis=-1, keepdims=True)
            am = jnp.min(jnp.where(lg == m, lane, LANES), axis=-1, keepdims=True)
            vals.append(m)
            idxs.append(am)
            lg = jnp.where(lane == am, -jnp.inf, lg)
        es = [jnp.exp(v - vals[0]) for v in vals]
        denom = es[0] + es[1] + es[2] + es[3]
        ti = jnp.zeros((n, LANES), jnp.int32)
        tg = jnp.zeros((n, LANES), F32)
        for kk in range(TOP_K):
            ti = jnp.where(lane == kk, idxs[kk], ti)
            tg = jnp.where(lane == kk, es[kk] / denom, tg)
        ti_ref[sl, :] = ti
        tg_ref[sl, :] = tg

    rows(slice(None), x_ref.shape[0])


def _outproj(x2, o_mla, ogm, mla_out_norm, wo, ffn_norm, wr_pad, br_pad):
    N, D = x2.shape
    tm = TOK_TILE
    row = lambda w: pl.BlockSpec((tm, w), lambda i: (i, 0))
    wa = o_mla.shape[1]
    wb = ogm.shape[1]
    assert wa == wb and wo.shape[0] == wa + wb
    return pl.pallas_call(
        _outproj_kernel,
        grid=(N // tm,),
        in_specs=[row(D), row(wa), row(wb), _const_spec((1, wa)),
                  pl.BlockSpec((wa, D), lambda i: (0, 0)), pl.BlockSpec((wb, D), lambda i: (1, 0)),
                  _const_spec((1, D)), _const_spec((D, LANES)), _const_spec((1, LANES))],
        out_specs=[row(D), row(D), row(LANES), row(LANES)],
        out_shape=[jax.ShapeDtypeStruct((N, D), F32), jax.ShapeDtypeStruct((N, D), F32),
                   jax.ShapeDtypeStruct((N, LANES), jnp.int32),
                   jax.ShapeDtypeStruct((N, LANES), F32)],
        compiler_params=pltpu.CompilerParams(dimension_semantics=("arbitrary",),
                                             vmem_limit_bytes=VMEM_LIMIT),
        name="outproj",
    )(x2, o_mla, ogm, mla_out_norm.reshape(1, -1), wo, wo, ffn_norm.reshape(1, -1), wr_pad,
      br_pad)


def _pack_bf16_pairs(x):
    k = x.shape[1] // 2
    lo = pltpu.bitcast(x[:, :k].astype(BF16).astype(F32), jnp.uint32)
    hi = pltpu.bitcast(x[:, k:].astype(BF16).astype(F32), jnp.uint32)
    return hi | (lo >> 16)


def _unpack_bf16_pairs(w):
    lo = pltpu.bitcast(w << 16, F32).astype(BF16)
    hi = pltpu.bitcast(w & jnp.uint32(0xFFFF0000), F32).astype(BF16)
    return jnp.concatenate([lo, hi], axis=1)


def _dispatch_kernel(dest_ref, zb_ref, nzb_ref, h2_ref, xs_hbm, pk, zblk, sem, sem_z):
    tm = h2_ref.shape[0]
    bm = MOE_TM
    i = pl.program_id(0)
    base = i * (tm * TOP_K)
    b = i % 2
    pk[b] = _pack_bf16_pairs(h2_ref[...])

    def wait_tile(bb):
        for kk in range(TOP_K):
            pltpu.make_async_copy(pk.at[bb], xs_hbm.at[pl.ds(0, tm), :], sem.at[bb]).wait()

    @pl.when(i == 0)
    def _zero_blocks():
        zblk[...] = jnp.zeros(zblk.shape, zblk.dtype)

        def zcopy(n):
            d0 = pl.multiple_of(zb_ref[n] * bm, bm)
            return pltpu.make_async_copy(zblk, xs_hbm.at[pl.ds(d0, bm), :], sem_z)

        def issue(n, c):
            zcopy(n).start()
            return c
        lax.fori_loop(0, nzb_ref[0], issue, 0)

        def finish(n, c):
            zcopy(n).wait()
            return c
        lax.fori_loop(0, nzb_ref[0], finish, 0)

    def issue(g, c):
        r0 = pl.multiple_of(g * SUBLANES, SUBLANES)
        rows = pk.at[b, pl.ds(r0, SUBLANES), :]
        for s in range(SUBLANES):
            for kk in range(TOP_K):
                d = dest_ref[base + (r0 + s) * TOP_K + kk]
                pltpu.make_async_copy(rows.at[pl.ds(s, 1), :], xs_hbm.at[pl.ds(d, 1), :],
                                      sem.at[b]).start()
        return c
    lax.fori_loop(0, tm // SUBLANES, issue, 0)

    @pl.when(i >= 1)
    def _():
        wait_tile(1 - b)

    @pl.when(i == pl.num_programs(0) - 1)
    def _():
        wait_tile(b)


def _dispatch(h2, dest, zero_blocks, n_zero, n_rows):
    N, D = h2.shape
    tm = DSP_TILE
    grid_spec = pltpu.PrefetchScalarGridSpec(
        num_scalar_prefetch=3,
        grid=(N // tm,),
        in_specs=[pl.BlockSpec((tm, D), lambda i, *_: (i, 0))],
        out_specs=pl.BlockSpec(memory_space=pl.ANY),
        scratch_shapes=[pltpu.VMEM((2, tm, D // 2), jnp.uint32),
                        pltpu.VMEM((MOE_TM, D // 2), jnp.uint32),
                        pltpu.SemaphoreType.DMA((2,)), pltpu.SemaphoreType.DMA(())],
    )
    return pl.pallas_call(
        _dispatch_kernel,
        grid_spec=grid_spec,
        out_shape=jax.ShapeDtypeStruct((n_rows, D // 2), jnp.uint32),
        compiler_params=pltpu.CompilerParams(dimension_semantics=("arbitrary",),
                                             vmem_limit_bytes=VMEM_LIMIT),
        name="dispatch",
    )(dest, zero_blocks, n_zero, h2)


def _moe_kernel(ie_ref, ist_ref, inb_ref, tail_ref,
                xs_hbm, wgu_hbm, wd_hbm, bgu_ref, bd_ref,
                ys_hbm,
                xbuf, acc, wg_buf, wu_buf, wd_buf, zblk, sem_x, sem_o, sem_w, sem_z):
    tm = MOE_TM
    i = pl.program_id(0)
    n_items = pl.num_programs(0)
    nblk = inb_ref[i]
    slot = i % 2
    n_blocks = ys_hbm.shape[0] // tm
    tf = wg_buf.shape[2]
    nj = wd_hbm.shape[1] // tf

    def x_copy(it, m):
        s0 = pl.multiple_of(ist_ref[it] + m * tm, tm)
        r0 = pl.multiple_of(m * tm, tm)
        return pltpu.make_async_copy(xs_hbm.at[pl.ds(s0, tm), :],
                                     xbuf.at[it % 2, pl.ds(r0, tm), :], sem_x.at[it % 2])

    def y_copy(it, m):
        d0 = pl.multiple_of(ist_ref[it] + m * tm, tm)
        r0 = pl.multiple_of(m * tm, tm)
        return pltpu.make_async_copy(acc.at[it % 2, pl.ds(r0, tm), :],
                                     ys_hbm.at[pl.ds(d0, tm), :], sem_o.at[it % 2])

    def for_blocks(it, fn):
        def body(m, c):
            fn(it, m)
            return c
        lax.fori_loop(0, inb_ref[it], body, 0)

    start_x = lambda it: for_blocks(it, lambda a, m: x_copy(a, m).start())
    wait_x = lambda it: for_blocks(it, lambda a, m: x_copy(a, m).wait())
    start_y = lambda it: for_blocks(it, lambda a, m: y_copy(a, m).start())
    wait_y = lambda it: for_blocks(it, lambda a, m: y_copy(a, m).wait())

    def zero_copy(b):
        d0 = pl.multiple_of(b * tm, tm)
        return pltpu.make_async_copy(zblk, ys_hbm.at[pl.ds(d0, tm), :], sem_z)

    n_ws = wg_buf.shape[0]

    def w_slot(it, j):
        return (it * nj + j) % n_ws

    def w_copies(it, j):
        e = ie_ref[it]
        ws = w_slot(it, j)
        c0 = pl.multiple_of(j * tf, tf)
        c1 = pl.multiple_of(nj * tf + j * tf, tf)
        return (pltpu.make_async_copy(wgu_hbm.at[e, :, pl.ds(c0, tf)], wg_buf.at[ws], sem_w.at[ws]),
                pltpu.make_async_copy(wgu_hbm.at[e, :, pl.ds(c1, tf)], wu_buf.at[ws], sem_w.at[ws]),
                pltpu.make_async_copy(wd_hbm.at[e, pl.ds(c0, tf), :], wd_buf.at[ws], sem_w.at[ws]))

    def start_w(it, j):
        for cp in w_copies(it, j):
            cp.start()

    def wait_w(it, j):
        for cp in w_copies(it, j):
            cp.wait()

    @pl.when(i == 0)
    def _first_step():
        zblk[...] = jnp.zeros(zblk.shape, zblk.dtype)

        def issue(b, c):
            zero_copy(b).start()
            return c
        lax.fori_loop(tail_ref[0], n_blocks, issue, 0)
        start_x(0)

        @pl.when(nblk > 0)
        def _():
            for a in range(n_ws - 1):
                start_w(0, a)

    wait_x(i)

    @pl.when(i + 1 < n_items)
    def _():
        start_x(i + 1)

    @pl.when(i >= 2)
    def _():
        wait_y(i - 2)

    def init(m, c):
        r0 = pl.multiple_of(m * tm, tm)
        acc[slot, pl.ds(r0, tm), :] = jnp.broadcast_to(bd_ref[0], (tm, acc.shape[2]))
        return c
    lax.fori_loop(0, nblk, init, 0)

    def ffn_rows(j, n):
        ws = w_slot(i, j)
        x = _unpack_bf16_pairs(xbuf[slot, pl.ds(0, n), :])
        g = jnp.minimum(_dot(x, wg_buf[ws].astype(BF16)) + bgu_ref[0, pl.ds(j, 1), :], SWIGLU_LIMIT)
        u = jnp.clip(_dot(x, wu_buf[ws].astype(BF16)) + bgu_ref[0, pl.ds(nj + j, 1), :],
                     -SWIGLU_LIMIT, SWIGLU_LIMIT)
        a = (u + 1.0) * (g * jax.nn.sigmoid(SWIGLU_ALPHA * g))
        acc[slot, pl.ds(0, n), :] += _dot(a.astype(BF16), wd_buf[ws].astype(BF16))

    def chunk(j, c):
        wait_w(i, j)
        ja = j + (n_ws - 1)

        @pl.when(ja < nj)
        def _():
            start_w(i, ja)

        @pl.when(jnp.logical_and(ja >= nj, i + 1 < n_items))
        def _():
            @pl.when(inb_ref[i + 1] > 0)
            def _():
                start_w(i + 1, ja - nj)

        for nb in range(1, MOE_BPI + 1):
            pl.when(nblk == nb)(functools.partial(ffn_rows, j, nb * tm))
        return c

    @pl.when(nblk > 0)
    def _chunks():
        lax.fori_loop(0, nj, chunk, 0)

    start_y(i)

    @pl.when(i == n_items - 1)
    def _last_step():
        @pl.when(i >= 1)
        def _():
            wait_y(i - 1)
        wait_y(i)

        def finish(b, c):
            zero_copy(b).wait()
            return c
        lax.fori_loop(tail_ref[0], n_blocks, finish, 0)


def _moe(xs, item_e, item_start, item_nblk, tail_blk, w_gate_up, b_gate_up, w_down, b_down):
    n_rows = xs.shape[0]
    E, D, F2 = w_gate_up.shape
    F = F2 // 2
    tf = MOE_TF
    nj = F // tf
    nw = MOE_WSLOTS
    assert 2 <= nw <= nj + 1
    n_items = item_e.shape[0]
    rows = MOE_BPI * MOE_TM
    any_spec = pl.BlockSpec(memory_space=pl.ANY)
    grid_spec = pltpu.PrefetchScalarGridSpec(
        num_scalar_prefetch=4,
        grid=(n_items,),
        in_specs=[any_spec, any_spec, any_spec,
                  pl.BlockSpec((1, 2 * nj, tf), lambda i, ie, ist, inb, tail: (ie[i], 0, 0)),
                  pl.BlockSpec((1, 1, D), lambda i, ie, ist, inb, tail: (ie[i], 0, 0))],
        out_specs=any_spec,
        scratch_shapes=[pltpu.VMEM((2, rows, D // 2), jnp.uint32), pltpu.VMEM((2, rows, D), F32),
                        pltpu.VMEM((nw, D, tf), F32), pltpu.VMEM((nw, D, tf), F32),
                        pltpu.VMEM((nw, tf, D), F32), pltpu.VMEM((MOE_TM, D), F32),
                        pltpu.SemaphoreType.DMA((2,)), pltpu.SemaphoreType.DMA((2,)),
                        pltpu.SemaphoreType.DMA((nw,)), pltpu.SemaphoreType.DMA(())],
    )
    return pl.pallas_call(
        _moe_kernel,
        grid_spec=grid_spec,
        out_shape=jax.ShapeDtypeStruct((n_rows, D), F32),
        compiler_params=pltpu.CompilerParams(dimension_semantics=("arbitrary",),
                                             vmem_limit_bytes=VMEM_LIMIT),
        name="moe",
    )(item_e, item_start, item_nblk, tail_blk, xs, w_gate_up, w_down,
      b_gate_up.reshape(E, 2 * nj, tf), b_down.reshape(E, 1, D))


def _combine_kernel(pos_ref, x1_ref, g_ref, ys_hbm, o_ref, buf, sem):
    tm = CMB_TILE
    i = pl.program_id(0)

    def start_gather(t):
        base = t * (tm * TOP_K)
        b = t % 2

        def issue(g, c):
            r0 = pl.multiple_of(g * SUBLANES, SUBLANES)
            for s in range(SUBLANES):
                for kk in range(TOP_K):
                    p = pos_ref[base + (r0 + s) * TOP_K + kk]
                    pltpu.make_async_copy(ys_hbm.at[pl.ds(p, 1), :],
                                          buf.at[b, kk, pl.ds(r0, SUBLANES), :].at[pl.ds(s, 1), :],
                                          sem.at[b]).start()
            return c
        lax.fori_loop(0, tm // SUBLANES, issue, 0)

    @pl.when(i == 0)
    def _():
        start_gather(0)

    @pl.when(i + 1 < pl.num_programs(0))
    def _():
        start_gather(i + 1)

    b = i % 2
    for kk in range(TOP_K):
        pltpu.make_async_copy(ys_hbm.at[pl.ds(0, tm), :], buf.at[b, kk], sem.at[b]).wait()
    out = x1_ref[...]
    for kk in range(TOP_K):
        out = out + g_ref[:, kk:kk + 1] * buf[b, kk]
    o_ref[...] = out


def _combine(x1, gates, ys, dest):
    N, D = x1.shape
    tm = CMB_TILE
    grid_spec = pltpu.PrefetchScalarGridSpec(
        num_scalar_prefetch=1,
        grid=(N // tm,),
        in_specs=[pl.BlockSpec((tm, D), lambda i, pos: (i, 0)),
                  pl.BlockSpec((tm, LANES), lambda i, pos: (i, 0)),
                  pl.BlockSpec(memory_space=pl.ANY)],
        out_specs=pl.BlockSpec((tm, D), lambda i, pos: (i, 0)),
        scratch_shapes=[pltpu.VMEM((2, TOP_K, tm, D), F32), pltpu.SemaphoreType.DMA((2,))],
    )
    return pl.pallas_call(
        _combine_kernel,
        grid_spec=grid_spec,
        out_shape=jax.ShapeDtypeStruct((N, D), F32),
        compiler_params=pltpu.CompilerParams(dimension_semantics=("arbitrary",),
                                             vmem_limit_bytes=VMEM_LIMIT),
        name="combine",
    )(dest, x1, gates, ys)


def _routing(top_idx, n_tok):
    tm, bpi = MOE_TM, MOE_BPI
    nk = n_tok * TOP_K
    experts = jnp.arange(N_EXPERTS, dtype=jnp.int32)
    e_flat = top_idx.reshape(-1)
    onehot = (e_flat[:, None] == experts[None, :]).astype(jnp.int32)
    csum = jnp.cumsum(onehot, axis=0)
    rank = jnp.sum(csum * onehot, axis=1) - 1
    counts = csum[-1]
    nb = (counts + tm - 1) // tm
    bend = jnp.cumsum(nb)
    bstart = bend - nb
    dest = (jnp.sum(onehot * bstart[None, :], axis=1) * tm + rank).astype(jnp.int32)
    n_blocks = -(-(nk + N_EXPERTS * (tm - 1)) // tm)
    n_rows = n_blocks * tm
    tail_blk = bend[-1:].astype(jnp.int32)
    zb_e = jnp.where(nb > 0, bend - 1, -1)
    zb_t = jnp.arange(n_blocks, dtype=jnp.int32)
    zb_all = jnp.concatenate([zb_e, jnp.where(zb_t >= bend[-1], zb_t, -1)]).astype(jnp.int32)
    order = jnp.argsort(zb_all < 0, stable=True)
    zero_blocks = zb_all[order]
    n_zero = jnp.sum(zb_all >= 0).astype(jnp.int32).reshape(1)
    n_items = n_blocks // bpi + N_EXPERTS
    items_e = (nb + bpi - 1) // bpi
    iend = jnp.cumsum(items_e)
    istart = iend - items_e
    slot = jnp.arange(n_items, dtype=jnp.int32)
    valid = slot < iend[-1]
    exp_of = jnp.minimum(jnp.searchsorted(iend, slot, side="right"), N_EXPERTS - 1).astype(jnp.int32)
    last_e = jnp.max(jnp.where(nb > 0, experts, 0))
    local = slot - istart[exp_of]
    n_it = jnp.maximum(items_e[exp_of], 1)
    base, rem = nb[exp_of] // n_it, nb[exp_of] % n_it
    first_blk = bstart[exp_of] + local * base + jnp.minimum(local, rem)
    item_e = jnp.where(valid, exp_of, last_e).astype(jnp.int32)
    item_start = jnp.where(valid, first_blk * tm, 0).astype(jnp.int32)
    item_nblk = jnp.where(valid, base + (local < rem), 0).astype(jnp.int32)
    return dest, zero_blocks, n_zero, item_e, item_start, item_nblk, tail_blk, n_rows


def _rope_lane_tile(v):
    half = MLA_ROPE // 2
    z = jnp.zeros((half,), v.dtype)
    return jnp.concatenate([v[:half], z, v[half:], z]).reshape(1, LANES)


def _rope_cols(w):
    half = MLA_ROPE // 2
    z = jnp.zeros((w.shape[0], half), w.dtype)
    return jnp.concatenate([w[:, :half], z, w[:, half:], z], axis=1)


def _layer(x, mem, positions, attn_norm, w_in, q_a_norm, w_q_b, kv_a_norm, w_kv_b,
           q_nope_norm, q_rope_norm, k_nope_norm, k_rope_norm,
           gmlp_v_norm, w_spatial, b_spatial,
           mem_norm, w_mem_kv, mem_q_norm, mem_k_norm,
           mla_out_norm, gmlp_out_norm, mem_out_norm, w_o,
           ffn_norm, w_router, b_router, w_gate_up, b_gate_up, w_down, b_down):
    B, S, D = x.shape
    N = B * S
    x2 = x.reshape(N, D)

    half = MLA_ROPE // 2
    r1 = Q_LORA + KV_LORA + half
    zc = jnp.zeros((D, half), BF16)
    w_in_b = w_in.astype(BF16)
    w_in_r = jnp.concatenate([w_in_b[:, :r1], zc, w_in_b[:, r1:r1 + half], zc, w_in_b[:, r1 + half:]],
                             axis=1)
    wq = w_q_b.reshape(Q_LORA, MLA_HEADS, MLA_NOPE + MLA_ROPE)
    wq_rope = jax.vmap(_rope_cols, in_axes=1, out_axes=1)(wq[:, :, MLA_NOPE:])
    wq_pad = jnp.concatenate([wq[:, :, :MLA_NOPE], wq_rope], axis=2).reshape(Q_LORA, MLA_HEADS * HEAD_PAD)
    wq_pad = wq_pad.astype(BF16)
    inv_freq = ROPE_BASE ** (-jnp.arange(half, dtype=F32) / half)
    zf = jnp.zeros((half,), F32)
    invf_tile = jnp.concatenate([inv_freq, zf, inv_freq, zf]).reshape(1, LANES)
    sgn_tile = jnp.concatenate([-jnp.ones((half,), F32), zf, jnp.ones((half,), F32), zf]).reshape(1, LANES)
    bs_tile = jnp.repeat(b_spatial.T, GMLP_CH, axis=1)

    mk, mv = _mem_kv(mem, mem_norm, w_mem_kv, mem_k_norm)
    cq, ckv, kpe, ogm, cs = _inproj(
        x2, positions.reshape(N, 1), (invf_tile, sgn_tile), attn_norm, w_in_r, q_a_norm, kv_a_norm,
        _rope_lane_tile(k_rope_norm), gmlp_v_norm, w_spatial, bs_tile, mk, mv, mem_q_norm,
        gmlp_out_norm, mem_out_norm, S)
    q, k, v = _qkv(cq, ckv, kpe, cs, wq_pad, w_kv_b.astype(BF16), q_nope_norm,
                   _rope_lane_tile(q_rope_norm), k_nope_norm)
    o_mla = _attention(q, k, v, B, S).reshape(N, MLA_HEADS * MLA_V)

    wr_pad = jnp.pad(w_router, ((0, 0), (0, LANES - N_EXPERTS)))
    br_pad = jnp.pad(b_router, (0, LANES - N_EXPERTS)).reshape(1, LANES)
    x1, h2, ti, tg = _outproj(x2, o_mla, ogm, mla_out_norm, w_o.astype(BF16), ffn_norm, wr_pad,
                              br_pad)

    dest, zero_blocks, n_zero, item_e, item_start, item_nblk, tail_blk, n_rows = _routing(
        ti[:, :TOP_K], N)
    xs = _dispatch(h2, dest, zero_blocks, n_zero, n_rows)
    ys = _moe(xs, item_e, item_start, item_nblk, tail_blk, w_gate_up, b_gate_up, w_down, b_down)
    out = _combine(x1, tg, ys, dest)
    return out.reshape(B, S, D)


def kernel(x, mem, positions, attn_norm, w_in, q_a_norm, w_q_b, kv_a_norm, w_kv_b, q_nope_norm, q_rope_norm, k_nope_norm, k_rope_norm, gmlp_v_norm, w_spatial, b_spatial, mem_norm, w_mem_kv, mem_q_norm, mem_k_norm, mla_out_norm, gmlp_out_norm, mem_out_norm, w_o, ffn_norm, w_router, b_router, w_gate_up, b_gate_up, w_down, b_down):
    depth = attn_norm.shape[0]
    for l in range(depth):
        x = _layer(x, mem, positions, attn_norm[l], w_in[l], q_a_norm[l], w_q_b[l], kv_a_norm[l],
                   w_kv_b[l], q_nope_norm[l], q_rope_norm[l], k_nope_norm[l], k_rope_norm[l],
                   gmlp_v_norm[l], w_spatial[l], b_spatial[l], mem_norm[l], w_mem_kv[l],
                   mem_q_norm[l], mem_k_norm[l], mla_out_norm[l], gmlp_out_norm[l],
                   mem_out_norm[l], w_o[l], ffn_norm[l], w_router[l], b_router[l], w_gate_up[l],
                   b_gate_up[l], w_down[l], b_down[l])
    return x
```

```python
import functools

import jax
import jax.numpy as jnp
import numpy as np
from jax import lax
from jax.experimental import pallas as pl
from jax.experimental.pallas import tpu as pltpu

F32 = jnp.float32
BF16 = jnp.bfloat16

EPS = 1e-6
LANES = 128
SUBLANES = 8
VMEM_LIMIT = 56 * 1024 * 1024

MLA_HEADS = 8
MLA_NOPE = 128
MLA_ROPE = 64
MLA_V = 128
Q_LORA = 512
KV_LORA = 512
GMLP_GROUPS = 4
GMLP_CH = 128
GMLP_WIDTH = GMLP_GROUPS * GMLP_CH
CHUNK = 128
MEM_HEADS = 4
MEM_HEAD_DIM = 128
MEM_WIDTH = MEM_HEADS * MEM_HEAD_DIM
N_EXPERTS = 32
TOP_K = 4
SWIGLU_ALPHA = 1.702
SWIGLU_LIMIT = 7.0
ROPE_BASE = 10000.0
HEAD_PAD = 2 * LANES

TOK_TILE = 512
ATT_TQ = 512
ATT_TK = 512
ATT_HPS = 4
MOE_TM = 128
MOE_BPI = 10
MOE_TF = 256
MOE_WSLOTS = 3
DSP_TILE = 1024
CMB_TILE = 256


def _rms(x, g, n=None):
    n = x.shape[-1] if n is None else n
    ms = jnp.sum(x * x, axis=-1, keepdims=True) * (1.0 / n)
    return x * lax.rsqrt(ms + EPS) * g


def _gelu(x):
    return 0.5 * x * (1.0 + lax.erf(x * (2.0 ** -0.5)))


def _dot(a, b):
    return jnp.dot(a, b, preferred_element_type=F32)


def _dot_nt(a, b):
    return lax.dot_general(a, b, (((1,), (1,)), ((), ())), preferred_element_type=F32)


def _const_spec(shape):
    nd = len(shape)
    return pl.BlockSpec(shape, lambda *_: (0,) * nd)


def _mem_kv_kernel(mem_ref, g_ref, w_ref, kg_ref, mk_ref, mv_ref):
    m = mem_ref[0]
    hn = _rms(m, g_ref[...]).astype(BF16)
    kv = _dot(hn, w_ref[...])
    for h in range(MEM_HEADS):
        k = kv[:, h * MEM_HEAD_DIM:(h + 1) * MEM_HEAD_DIM]
        mk_ref[0, :, h * MEM_HEAD_DIM:(h + 1) * MEM_HEAD_DIM] = _rms(k, kg_ref[...]).astype(BF16)
    mv_ref[0] = kv[:, MEM_WIDTH:].astype(BF16)


def _mem_kv(mem, mem_norm, w_mem_kv, mem_k_norm):
    B, M, D = mem.shape
    return pl.pallas_call(
        _mem_kv_kernel,
        grid=(B,),
        in_specs=[pl.BlockSpec((1, M, D), lambda b: (b, 0, 0)),
                  _const_spec((1, D)),
                  _const_spec((D, 2 * MEM_WIDTH)),
                  _const_spec((1, MEM_HEAD_DIM))],
        out_specs=[pl.BlockSpec((1, M, MEM_WIDTH), lambda b: (b, 0, 0)),
                   pl.BlockSpec((1, M, MEM_WIDTH), lambda b: (b, 0, 0))],
        out_shape=[jax.ShapeDtypeStruct((B, M, MEM_WIDTH), BF16),
                   jax.ShapeDtypeStruct((B, M, MEM_WIDTH), BF16)],
        compiler_params=pltpu.CompilerParams(dimension_semantics=("arbitrary",),
                                             vmem_limit_bytes=VMEM_LIMIT),
        name="mem_kv",
    )(mem, mem_norm.reshape(1, D), w_mem_kv.astype(BF16), mem_k_norm.reshape(1, MEM_HEAD_DIM))


_O_CQ = 0
_O_CKV = _O_CQ + Q_LORA
_O_KR = _O_CKV + KV_LORA
_O_U = _O_KR + LANES
_O_VG = _O_U + GMLP_WIDTH
_O_QM = _O_VG + GMLP_WIDTH
_IN_COLS_PAD = _O_QM + MEM_WIDTH


def _rope_tile(t, cos, sin_signed):
    return t * cos + pltpu.roll(t, LANES // 2, axis=1) * sin_signed


def _inproj_kernel(x_ref, pos_ref, invf_ref, sgn_ref, an_ref, w_ref, qan_ref, kvan_ref, krn_ref,
                   gvn_ref, ws_ref, bs_ref, mk_ref, mv_ref, mqn_ref, gon_ref, mon_ref,
                   cq_ref, ckv_ref, kpe_ref, ogm_ref, cs_ref):
    def project(sl):
        h = _rms(x_ref[sl, :], an_ref[...]).astype(BF16)
        return _dot(h, w_ref[...])

    def finish(sl, n, z):
        cq_ref[sl, :] = _rms(z[:, _O_CQ:_O_CQ + Q_LORA], qan_ref[...]).astype(BF16)
        ckv_ref[sl, :] = _rms(z[:, _O_CKV:_O_CKV + KV_LORA], kvan_ref[...]).astype(BF16)

        ang = pos_ref[sl, :].astype(F32) * invf_ref[...]
        cos = jnp.cos(ang)
        sin_signed = jnp.sin(ang) * sgn_ref[...]
        cs_ref[sl, :LANES] = cos
        cs_ref[sl, LANES:] = sin_signed
        kr = _rms(z[:, _O_KR:_O_KR + LANES], krn_ref[...], MLA_ROPE)
        kpe_ref[sl, :] = _rope_tile(kr, cos, sin_signed).astype(BF16)

        u = _gelu(z[:, _O_U:_O_U + GMLP_WIDTH])
        vg = _gelu(z[:, _O_VG:_O_VG + GMLP_WIDTH])
        vg = _rms(vg, gvn_ref[...]).astype(BF16)
        row = lax.broadcasted_iota(jnp.int32, (CHUNK, CHUNK), 0)
        col = lax.broadcasted_iota(jnp.int32, (CHUNK, CHUNK), 1)
        sp_cols = []
        for g in range(GMLP_GROUPS):
            wsg = jnp.where(col <= row, ws_ref[g], 0.0).astype(BF16)
            sp_rows = [_dot(wsg, vg[c * CHUNK:(c + 1) * CHUNK, g * GMLP_CH:(g + 1) * GMLP_CH])
                       for c in range(n // CHUNK)]
            sp_cols.append(jnp.concatenate(sp_rows, axis=0))
        sp = jnp.concatenate(sp_cols, axis=1) + jnp.concatenate([bs_ref[...]] * (n // CHUNK), axis=0)
        ogm_ref[sl, :GMLP_WIDTH] = _rms(u * sp, gon_ref[...]).astype(BF16)

        o_heads = []
        for hd in range(MEM_HEADS):
            hs = slice(hd * MEM_HEAD_DIM, (hd + 1) * MEM_HEAD_DIM)
            q = z[:, _O_QM + hd * MEM_HEAD_DIM:_O_QM + (hd + 1) * MEM_HEAD_DIM]
            qn = (_rms(q, mqn_ref[...]) * (MEM_HEAD_DIM ** -0.5)).astype(BF16)
            s = _dot_nt(qn, mk_ref[0, :, hs])
            s = s - jnp.max(s, axis=-1, keepdims=True)
            p = jnp.exp(s)
            p = p / jnp.sum(p, axis=-1, keepdims=True)
            o_heads.append(_dot(p.astype(BF16), mv_ref[0, :, hs]))
        o_mem = jnp.concatenate(o_heads, axis=1)
        ogm_ref[sl, GMLP_WIDTH:] = _rms(o_mem, mon_ref[...]).astype(BF16)

    finish(slice(None), x_ref.shape[0], project(slice(None)))


def _inproj(x2, pos, tables, attn_norm, w_in_r, q_a_norm, kv_a_norm, krn_tile, gmlp_v_norm,
            w_spatial, bs_tile, mk, mv, mem_q_norm, gmlp_out_norm, mem_out_norm, seq):
    N, D = x2.shape
    tm = TOK_TILE
    tiles_per_seq = seq // tm
    invf_tile, sgn_tile = tables
    M = mk.shape[1]
    row = lambda w: pl.BlockSpec((tm, w), lambda i: (i, 0))
    batch_blk = pl.BlockSpec((1, M, MEM_WIDTH), lambda i: (i // tiles_per_seq, 0, 0))
    return pl.pallas_call(
        _inproj_kernel,
        grid=(N // tm,),
        in_specs=[row(D), row(1), _const_spec((1, LANES)), _const_spec((1, LANES)),
                  _const_spec((1, D)), _const_spec((D, _IN_COLS_PAD)),
                  _const_spec((1, Q_LORA)), _const_spec((1, KV_LORA)), _const_spec((1, LANES)),
                  _const_spec((1, GMLP_WIDTH)), _const_spec((GMLP_GROUPS, CHUNK, CHUNK)),
                  _const_spec((CHUNK, GMLP_WIDTH)), batch_blk, batch_blk,
                  _const_spec((1, MEM_HEAD_DIM)), _const_spec((1, GMLP_WIDTH)),
                  _const_spec((1, MEM_WIDTH))],
        out_specs=[row(Q_LORA), row(KV_LORA), row(LANES), row(GMLP_WIDTH + MEM_WIDTH),
                   row(2 * LANES)],
        out_shape=[jax.ShapeDtypeStruct((N, Q_LORA), BF16),
                   jax.ShapeDtypeStruct((N, KV_LORA), BF16),
                   jax.ShapeDtypeStruct((N, LANES), BF16),
                   jax.ShapeDtypeStruct((N, GMLP_WIDTH + MEM_WIDTH), BF16),
                   jax.ShapeDtypeStruct((N, 2 * LANES), F32)],
        compiler_params=pltpu.CompilerParams(dimension_semantics=("arbitrary",),
                                             vmem_limit_bytes=VMEM_LIMIT),
        name="inproj",
    )(x2, pos, invf_tile, sgn_tile, attn_norm.reshape(1, D), w_in_r,
      q_a_norm.reshape(1, -1), kv_a_norm.reshape(1, -1), krn_tile, gmlp_v_norm.reshape(1, -1),
      w_spatial, bs_tile, mk, mv, mem_q_norm.reshape(1, -1), gmlp_out_norm.reshape(1, -1),
      mem_out_norm.reshape(1, -1))


def _qkv_kernel(cq_ref, ckv_ref, kpe_ref, cs_ref, wq_ref, wkv_ref, qnn_ref, qrn_ref, knn_ref,
                q_ref, k_ref, v_ref):
    cos = cs_ref[:, :LANES]
    sin_signed = cs_ref[:, LANES:]
    scale = (MLA_NOPE + MLA_ROPE) ** -0.5 * np.log2(np.e)
    qr = _dot(cq_ref[...], wq_ref[...])
    kvr = _dot(ckv_ref[...], wkv_ref[...])
    kpe = kpe_ref[...]
    for h in range(MLA_HEADS):
        o = h * HEAD_PAD
        qn = _rms(qr[:, o:o + MLA_NOPE], qnn_ref[...]) * scale
        qt = _rms(qr[:, o + MLA_NOPE:o + HEAD_PAD], qrn_ref[...], MLA_ROPE)
        qt = _rope_tile(qt, cos, sin_signed) * scale
        q_ref[:, o:o + MLA_NOPE] = qn.astype(BF16)
        q_ref[:, o + MLA_NOPE:o + HEAD_PAD] = qt.astype(BF16)
        ko = h * (MLA_NOPE + MLA_V)
        k_ref[:, o:o + MLA_NOPE] = _rms(kvr[:, ko:ko + MLA_NOPE], knn_ref[...]).astype(BF16)
        k_ref[:, o + MLA_NOPE:o + HEAD_PAD] = kpe
        v_ref[:, h * MLA_V:(h + 1) * MLA_V] = kvr[:, ko + MLA_NOPE:ko + MLA_NOPE + MLA_V].astype(BF16)


def _qkv(cq, ckv, kpe, cs, wq_pad, wkv, q_nope_norm, qrn_tile, k_nope_norm):
    N = cq.shape[0]
    tm = TOK_TILE
    row = lambda w: pl.BlockSpec((tm, w), lambda i: (i, 0))
    return pl.pallas_call(
        _qkv_kernel,
        grid=(N // tm,),
        in_specs=[row(Q_LORA), row(KV_LORA), row(LANES), row(2 * LANES),
                  _const_spec(wq_pad.shape), _const_spec(wkv.shape),
                  _const_spec((1, MLA_NOPE)), _const_spec((1, LANES)), _const_spec((1, MLA_NOPE))],
        out_specs=[row(MLA_HEADS * HEAD_PAD), row(MLA_HEADS * HEAD_PAD), row(MLA_HEADS * MLA_V)],
        out_shape=[jax.ShapeDtypeStruct((N, MLA_HEADS * HEAD_PAD), BF16),
                   jax.ShapeDtypeStruct((N, MLA_HEADS * HEAD_PAD), BF16),
                   jax.ShapeDtypeStruct((N, MLA_HEADS * MLA_V), BF16)],
        compiler_params=pltpu.CompilerParams(dimension_semantics=("arbitrary",),
                                             vmem_limit_bytes=VMEM_LIMIT),
        name="qkv",
    )(cq, ckv, kpe, cs, wq_pad, wkv, q_nope_norm.reshape(1, -1), qrn_tile,
      k_nope_norm.reshape(1, -1))


def _attn_kernel(q_ref, k_ref, v_ref, o_ref):
    tq, tk = ATT_TQ, ATT_TK
    qi = pl.program_id(2)
    qs = [q_ref[0, :, h * HEAD_PAD:(h + 1) * HEAD_PAD] for h in range(ATT_HPS)]

    def scores(h, kb):
        k0 = pl.multiple_of(kb * tk, tk)
        return _dot_nt(k_ref[0, pl.ds(k0, tk), h * HEAD_PAD:(h + 1) * HEAD_PAD], qs[h])

    def head_step(h, kb, s, carry, masked):
        m, l, acc = carry
        k0 = pl.multiple_of(kb * tk, tk)
        if masked:
            kpos = kb * tk + lax.broadcasted_iota(jnp.int32, (tk, tq), 0)
            qpos = qi * tq + lax.broadcasted_iota(jnp.int32, (tk, tq), 1)
            s = jnp.where(kpos <= qpos, s, -jnp.inf)
        m_new = jnp.maximum(m, jnp.max(s, axis=0, keepdims=True))
        alpha = jnp.exp2(m - m_new)
        p = jnp.exp2(s - m_new)
        l = alpha * l + jnp.sum(p, axis=0, keepdims=True)
        v = v_ref[0, pl.ds(k0, tk), h * MLA_V:(h + 1) * MLA_V]
        pv = lax.dot_general(v, p.astype(BF16), (((0,), (0,)), ((), ())),
                             preferred_element_type=F32)
        return m_new, l, alpha * acc + pv

    def step(kb, carries, masked):
        ss = [scores(h, kb) for h in range(ATT_HPS)]
        return tuple(head_step(h, kb, ss[h], carries[h], masked) for h in range(ATT_HPS))

    init = tuple((jnp.full((1, tq), -jnp.inf, F32), jnp.zeros((1, tq), F32),
                  jnp.zeros((MLA_V, tq), F32)) for _ in range(ATT_HPS))
    n_full = (qi * tq) // tk
    carries = lax.fori_loop(0, n_full, lambda kb, c: step(kb, c, False), init)
    carries = step(n_full, carries, True)
    for h in range(ATT_HPS):
        _, l, acc = carries[h]
        o_ref[0, :, h * MLA_V:(h + 1) * MLA_V] = jnp.transpose(acc / l).astype(BF16)


def _attention(q, k, v, batch, seq):
    q3 = q.reshape(batch, seq, MLA_HEADS * HEAD_PAD)
    k3 = k.reshape(batch, seq, MLA_HEADS * HEAD_PAD)
    v3 = v.reshape(batch, seq, MLA_HEADS * MLA_V)
    hp, hv = ATT_HPS * HEAD_PAD, ATT_HPS * MLA_V
    assert ATT_TK % ATT_TQ == 0 and seq % ATT_TK == 0
    return pl.pallas_call(
        _attn_kernel,
        grid=(batch, MLA_HEADS // ATT_HPS, seq // ATT_TQ),
        in_specs=[pl.BlockSpec((1, ATT_TQ, hp), lambda b, h, i: (b, i, h)),
                  pl.BlockSpec((1, seq, hp), lambda b, h, i: (b, 0, h)),
                  pl.BlockSpec((1, seq, hv), lambda b, h, i: (b, 0, h))],
        out_specs=pl.BlockSpec((1, ATT_TQ, hv), lambda b, h, i: (b, i, h)),
        out_shape=jax.ShapeDtypeStruct((batch, seq, MLA_HEADS * MLA_V), BF16),
        compiler_params=pltpu.CompilerParams(
            dimension_semantics=("arbitrary", "arbitrary", "arbitrary"),
            vmem_limit_bytes=VMEM_LIMIT),
        name="attn",
    )(q3, k3, v3)


def _outproj_kernel(x_ref, oa_ref, ogm_ref, aon_ref, woa_ref, wob_ref, fn_ref, wr_ref, br_ref,
                    x1_ref, h2_ref, ti_ref, tg_ref):
    wr = wr_ref[...]
    w_hi = wr.astype(BF16)
    w_lo = (wr - w_hi.astype(F32)).astype(BF16)
    w_cat = jnp.concatenate([w_hi, w_lo], axis=1)

    def rows(sl, n):
        oa = _rms(oa_ref[sl, :].astype(F32), aon_ref[...]).astype(BF16)
        x1 = x_ref[sl, :] + _dot(oa, woa_ref[...]) + _dot(ogm_ref[sl, :], wob_ref[...])
        x1_ref[sl, :] = x1
        h2 = _rms(x1, fn_ref[...])
        h2_ref[sl, :] = _pack_bf16_pairs(h2)
        h_hi = h2.astype(BF16)
        h_lo = (h2 - h_hi.astype(F32)).astype(BF16)
        hh = _dot(h_hi, w_cat)
        logits = hh[:, :LANES] + (_dot(h_lo, w_hi) + hh[:, LANES:]) + br_ref[...]
        lane = lax.broadcasted_iota(jnp.int32, (n, LANES), 1)
        lg = jnp.where(lane < N_EXPERTS, logits, -jnp.inf)
        vals, idxs = [], []
        for _ in range(TOP_K):
            m = jnp.max(lg, axis=-1, keepdims=True)
            am = jnp.min(jnp.where(lg == m, lane, LANES), axis=-1, keepdims=True)
            vals.append(m)
            idxs.append(am)
            lg = jnp.where(lane == am, -jnp.inf, lg)
        es = [jnp.exp(v - vals[0]) for v in vals]
        denom = es[0] + es[1] + es[2] + es[3]
        ti = jnp.zeros((n, LANES), jnp.int32)
        tg = jnp.zeros((n, LANES), F32)
        for kk in range(TOP_K):
            ti = jnp.where(lane == kk, idxs[kk], ti)
            tg = jnp.where(lane == kk, es[kk] / denom, tg)
        ti_ref[sl, :] = ti
        tg_ref[sl, :] = tg

    rows(slice(None), x_ref.shape[0])


def _outproj(x2, o_mla, ogm, mla_out_norm, wo, ffn_norm, wr_pad, br_pad):
    N, D = x2.shape
    tm = TOK_TILE
    row = lambda w: pl.BlockSpec((tm, w), lambda i: (i, 0))
    wa = o_mla.shape[1]
    wb = ogm.shape[1]
    assert wa == wb and wo.shape[0] == wa + wb
    return pl.pallas_call(
        _outproj_kernel,
        grid=(N // tm,),
        in_specs=[row(D), row(wa), row(wb), _const_spec((1, wa)),
                  pl.BlockSpec((wa, D), lambda i: (0, 0)), pl.BlockSpec((wb, D), lambda i: (1, 0)),
                  _const_spec((1, D)), _const_spec((D, LANES)), _const_spec((1, LANES))],
        out_specs=[row(D), row(D // 2), row(LANES), row(LANES)],
        out_shape=[jax.ShapeDtypeStruct((N, D), F32), jax.ShapeDtypeStruct((N, D // 2), jnp.uint32),
                   jax.ShapeDtypeStruct((N, LANES), jnp.int32),
                   jax.ShapeDtypeStruct((N, LANES), F32)],
        compiler_params=pltpu.CompilerParams(dimension_semantics=("arbitrary",),
                                             vmem_limit_bytes=VMEM_LIMIT),
        name="outproj",
    )(x2, o_mla, ogm, mla_out_norm.reshape(1, -1), wo, wo, ffn_norm.reshape(1, -1), wr_pad,
      br_pad)


def _pack_bf16_pairs(x):
    k = x.shape[1] // 2
    lo = pltpu.bitcast(x[:, :k].astype(BF16).astype(F32), jnp.uint32)
    hi = pltpu.bitcast(x[:, k:].astype(BF16).astype(F32), jnp.uint32)
    return hi | (lo >> 16)


def _unpack_bf16_pairs(w):
    lo = pltpu.bitcast(w << 16, F32).astype(BF16)
    hi = pltpu.bitcast(w & jnp.uint32(0xFFFF0000), F32).astype(BF16)
    return jnp.concatenate([lo, hi], axis=1)


def _dispatch_kernel(dest_ref, zb_ref, nzb_ref, h2_ref, xs_hbm, pk, zblk, sem, sem_z):
    tm = h2_ref.shape[0]
    bm = MOE_TM
    i = pl.program_id(0)
    base = i * (tm * TOP_K)
    b = i % 2
    pk[b] = h2_ref[...]

    def wait_tile(bb):
        for kk in range(TOP_K):
            pltpu.make_async_copy(pk.at[bb], xs_hbm.at[pl.ds(0, tm), :], sem.at[bb]).wait()

    @pl.when(i == 0)
    def _zero_blocks():
        zblk[...] = jnp.zeros(zblk.shape, zblk.dtype)

        def zcopy(n):
            d0 = pl.multiple_of(zb_ref[n] * bm, bm)
            return pltpu.make_async_copy(zblk, xs_hbm.at[pl.ds(d0, bm), :], sem_z)

        def issue(n, c):
            zcopy(n).start()
            return c
        lax.fori_loop(0, nzb_ref[0], issue, 0)

        def finish(n, c):
            zcopy(n).wait()
            return c
        lax.fori_loop(0, nzb_ref[0], finish, 0)

    def issue(g, c):
        r0 = pl.multiple_of(g * SUBLANES, SUBLANES)
        rows = pk.at[b, pl.ds(r0, SUBLANES), :]
        for s in range(SUBLANES):
            for kk in range(TOP_K):
                d = dest_ref[base + (r0 + s) * TOP_K + kk]
                pltpu.make_async_copy(rows.at[pl.ds(s, 1), :], xs_hbm.at[pl.ds(d, 1), :],
                                      sem.at[b]).start()
        return c
    lax.fori_loop(0, tm // SUBLANES, issue, 0)

    @pl.when(i >= 1)
    def _():
        wait_tile(1 - b)

    @pl.when(i == pl.num_programs(0) - 1)
    def _():
        wait_tile(b)


def _dispatch(h2, dest, zero_blocks, n_zero, n_rows):
    N, W = h2.shape
    tm = DSP_TILE
    grid_spec = pltpu.PrefetchScalarGridSpec(
        num_scalar_prefetch=3,
        grid=(N // tm,),
        in_specs=[pl.BlockSpec((tm, W), lambda i, *_: (i, 0))],
        out_specs=pl.BlockSpec(memory_space=pl.ANY),
        scratch_shapes=[pltpu.VMEM((2, tm, W), jnp.uint32),
                        pltpu.VMEM((MOE_TM, W), jnp.uint32),
                        pltpu.SemaphoreType.DMA((2,)), pltpu.SemaphoreType.DMA(())],
    )
    return pl.pallas_call(
        _dispatch_kernel,
        grid_spec=grid_spec,
        out_shape=jax.ShapeDtypeStruct((n_rows, W), jnp.uint32),
        compiler_params=pltpu.CompilerParams(dimension_semantics=("arbitrary",),
                                             vmem_limit_bytes=VMEM_LIMIT),
        name="dispatch",
    )(dest, zero_blocks, n_zero, h2)


def _moe_kernel(ie_ref, ist_ref, inb_ref, tail_ref,
                xs_hbm, wgu_hbm, wd_hbm, bgu_ref, bd_ref,
                ys_hbm,
                xbuf, acc, wg_buf, wu_buf, wd_buf, zblk, sem_x, sem_o, sem_w, sem_z):
    tm = MOE_TM
    i = pl.program_id(0)
    n_items = pl.num_programs(0)
    nblk = inb_ref[i]
    slot = i % 2
    n_blocks = ys_hbm.shape[0] // tm
    tf = wg_buf.shape[2]
    nj = wd_hbm.shape[1] // tf

    def x_copy(it, m):
        s0 = pl.multiple_of(ist_ref[it] + m * tm, tm)
        r0 = pl.multiple_of(m * tm, tm)
        return pltpu.make_async_copy(xs_hbm.at[pl.ds(s0, tm), :],
                                     xbuf.at[it % 2, pl.ds(r0, tm), :], sem_x.at[it % 2])

    def y_copy(it, m):
        d0 = pl.multiple_of(ist_ref[it] + m * tm, tm)
        r0 = pl.multiple_of(m * tm, tm)
        return pltpu.make_async_copy(acc.at[it % 2, pl.ds(r0, tm), :],
                                     ys_hbm.at[pl.ds(d0, tm), :], sem_o.at[it % 2])

    def for_blocks(it, fn):
        def body(m, c):
            fn(it, m)
            return c
        lax.fori_loop(0, inb_ref[it], body, 0)

    start_x = lambda it: for_blocks(it, lambda a, m: x_copy(a, m).start())
    wait_x = lambda it: for_blocks(it, lambda a, m: x_copy(a, m).wait())
    start_y = lambda it: for_blocks(it, lambda a, m: y_copy(a, m).start())
    wait_y = lambda it: for_blocks(it, lambda a, m: y_copy(a, m).wait())

    def zero_copy(b):
        d0 = pl.multiple_of(b * tm, tm)
        return pltpu.make_async_copy(zblk, ys_hbm.at[pl.ds(d0, tm), :], sem_z)

    n_ws = wg_buf.shape[0]

    def w_slot(it, j):
        return (it * nj + j) % n_ws

    def w_copies(it, j):
        e = ie_ref[it]
        ws = w_slot(it, j)
        c0 = pl.multiple_of(j * tf, tf)
        c1 = pl.multiple_of(nj * tf + j * tf, tf)
        return (pltpu.make_async_copy(wgu_hbm.at[e, :, pl.ds(c0, tf)], wg_buf.at[ws], sem_w.at[ws]),
                pltpu.make_async_copy(wgu_hbm.at[e, :, pl.ds(c1, tf)], wu_buf.at[ws], sem_w.at[ws]),
                pltpu.make_async_copy(wd_hbm.at[e, pl.ds(c0, tf), :], wd_buf.at[ws], sem_w.at[ws]))

    def start_w(it, j):
        for cp in w_copies(it, j):
            cp.start()

    def wait_w(it, j):
        for cp in w_copies(it, j):
            cp.wait()

    @pl.when(i == 0)
    def _first_step():
        zblk[...] = jnp.zeros(zblk.shape, zblk.dtype)

        def issue(b, c):
            zero_copy(b).start()
            return c
        lax.fori_loop(tail_ref[0], n_blocks, issue, 0)
        start_x(0)

        @pl.when(nblk > 0)
        def _():
            for a in range(n_ws - 1):
                start_w(0, a)

    wait_x(i)

    @pl.when(i + 1 < n_items)
    def _():
        start_x(i + 1)

    @pl.when(i >= 2)
    def _():
        wait_y(i - 2)

    def init(m, c):
        r0 = pl.multiple_of(m * tm, tm)
        acc[slot, pl.ds(r0, tm), :] = jnp.broadcast_to(bd_ref[0], (tm, acc.shape[2]))
        return c
    lax.fori_loop(0, nblk, init, 0)

    def ffn_rows(j, n):
        ws = w_slot(i, j)
        x = _unpack_bf16_pairs(xbuf[slot, pl.ds(0, n), :])
        g = jnp.minimum(_dot(x, wg_buf[ws].astype(BF16)) + bgu_ref[0, pl.ds(j, 1), :], SWIGLU_LIMIT)
        u = jnp.clip(_dot(x, wu_buf[ws].astype(BF16)) + bgu_ref[0, pl.ds(nj + j, 1), :],
                     -SWIGLU_LIMIT, SWIGLU_LIMIT)
        a = (u + 1.0) * (g * jax.nn.sigmoid(SWIGLU_ALPHA * g))
        acc[slot, pl.ds(0, n), :] += _dot(a.astype(BF16), wd_buf[ws].astype(BF16))

    def chunk(j, c):
        wait_w(i, j)
        ja = j + (n_ws - 1)

        @pl.when(ja < nj)
        def _():
            start_w(i, ja)

        @pl.when(jnp.logical_and(ja >= nj, i + 1 < n_items))
        def _():
            @pl.when(inb_ref[i + 1] > 0)
            def _():
                start_w(i + 1, ja - nj)

        for nb in range(1, MOE_BPI + 1):
            pl.when(nblk == nb)(functools.partial(ffn_rows, j, nb * tm))
        return c

    @pl.when(nblk > 0)
    def _chunks():
        lax.fori_loop(0, nj, chunk, 0)

    start_y(i)

    @pl.when(i == n_items - 1)
    def _last_step():
        @pl.when(i >= 1)
        def _():
            wait_y(i - 1)
        wait_y(i)

        def finish(b, c):
            zero_copy(b).wait()
            return c
        lax.fori_loop(tail_ref[0], n_blocks, finish, 0)


def _moe(xs, item_e, item_start, item_nblk, tail_blk, w_gate_up, b_gate_up, w_down, b_down):
    n_rows = xs.shape[0]
    E, D, F2 = w_gate_up.shape
    F = F2 // 2
    tf = MOE_TF
    nj = F // tf
    nw = MOE_WSLOTS
    assert 2 <= nw <= nj + 1
    n_items = item_e.shape[0]
    rows = MOE_BPI * MOE_TM
    any_spec = pl.BlockSpec(memory_space=pl.ANY)
    grid_spec = pltpu.PrefetchScalarGridSpec(
        num_scalar_prefetch=4,
        grid=(n_items,),
        in_specs=[any_spec, any_spec, any_spec,
                  pl.BlockSpec((1, 2 * nj, tf), lambda i, ie, ist, inb, tail: (ie[i], 0, 0)),
                  pl.BlockSpec((1, 1, D), lambda i, ie, ist, inb, tail: (ie[i], 0, 0))],
        out_specs=any_spec,
        scratch_shapes=[pltpu.VMEM((2, rows, D // 2), jnp.uint32), pltpu.VMEM((2, rows, D), F32),
                        pltpu.VMEM((nw, D, tf), F32), pltpu.VMEM((nw, D, tf), F32),
                        pltpu.VMEM((nw, tf, D), F32), pltpu.VMEM((MOE_TM, D), F32),
                        pltpu.SemaphoreType.DMA((2,)), pltpu.SemaphoreType.DMA((2,)),
                        pltpu.SemaphoreType.DMA((nw,)), pltpu.SemaphoreType.DMA(())],
    )
    return pl.pallas_call(
        _moe_kernel,
        grid_spec=grid_spec,
        out_shape=jax.ShapeDtypeStruct((n_rows, D), F32),
        compiler_params=pltpu.CompilerParams(dimension_semantics=("arbitrary",),
                                             vmem_limit_bytes=VMEM_LIMIT),
        name="moe",
    )(item_e, item_start, item_nblk, tail_blk, xs, w_gate_up, w_down,
      b_gate_up.reshape(E, 2 * nj, tf), b_down.reshape(E, 1, D))


def _combine_kernel(pos_ref, x1_ref, g_ref, ys_hbm, o_ref, buf, sem):
    tm = CMB_TILE
    i = pl.program_id(0)

    def start_gather(t):
        base = t * (tm * TOP_K)
        b = t % 2

        def issue(g, c):
            r0 = pl.multiple_of(g * SUBLANES, SUBLANES)
            for s in range(SUBLANES):
                for kk in range(TOP_K):
                    p = pos_ref[base + (r0 + s) * TOP_K + kk]
                    pltpu.make_async_copy(ys_hbm.at[pl.ds(p, 1), :],
                                          buf.at[b, kk, pl.ds(r0, SUBLANES), :].at[pl.ds(s, 1), :],
                                          sem.at[b]).start()
            return c
        lax.fori_loop(0, tm // SUBLANES, issue, 0)

    @pl.when(i == 0)
    def _():
        start_gather(0)

    @pl.when(i + 1 < pl.num_programs(0))
    def _():
        start_gather(i + 1)

    b = i % 2
    for kk in range(TOP_K):
        pltpu.make_async_copy(ys_hbm.at[pl.ds(0, tm), :], buf.at[b, kk], sem.at[b]).wait()
    out = x1_ref[...]
    for kk in range(TOP_K):
        out = out + g_ref[:, kk:kk + 1] * buf[b, kk]
    o_ref[...] = out


def _combine(x1, gates, ys, dest):
    N, D = x1.shape
    tm = CMB_TILE
    grid_spec = pltpu.PrefetchScalarGridSpec(
        num_scalar_prefetch=1,
        grid=(N // tm,),
        in_specs=[pl.BlockSpec((tm, D), lambda i, pos: (i, 0)),
                  pl.BlockSpec((tm, LANES), lambda i, pos: (i, 0)),
                  pl.BlockSpec(memory_space=pl.ANY)],
        out_specs=pl.BlockSpec((tm, D), lambda i, pos: (i, 0)),
        scratch_shapes=[pltpu.VMEM((2, TOP_K, tm, D), F32), pltpu.SemaphoreType.DMA((2,))],
    )
    return pl.pallas_call(
        _combine_kernel,
        grid_spec=grid_spec,
        out_shape=jax.ShapeDtypeStruct((N, D), F32),
        compiler_params=pltpu.CompilerParams(dimension_semantics=("arbitrary",),
                                             vmem_limit_bytes=VMEM_LIMIT),
        name="combine",
    )(dest, x1, gates, ys)


def _routing(top_idx, n_tok):
    tm, bpi = MOE_TM, MOE_BPI
    nk = n_tok * TOP_K
    experts = jnp.arange(N_EXPERTS, dtype=jnp.int32)
    e_flat = top_idx.reshape(-1)
    onehot = (e_flat[:, None] == experts[None, :]).astype(jnp.int32)
    csum = jnp.cumsum(onehot, axis=0)
    rank = jnp.sum(csum * onehot, axis=1) - 1
    counts = csum[-1]
    nb = (counts + tm - 1) // tm
    bend = jnp.cumsum(nb)
    bstart = bend - nb
    dest = (jnp.sum(onehot * bstart[None, :], axis=1) * tm + rank).astype(jnp.int32)
    n_blocks = -(-(nk + N_EXPERTS * (tm - 1)) // tm)
    n_rows = n_blocks * tm
    tail_blk = bend[-1:].astype(jnp.int32)
    zb_e = jnp.where(nb > 0, bend - 1, -1)
    zb_t = jnp.arange(n_blocks, dtype=jnp.int32)
    zb_all = jnp.concatenate([zb_e, jnp.where(zb_t >= bend[-1], zb_t, -1)]).astype(jnp.int32)
    order = jnp.argsort(zb_all < 0, stable=True)
    zero_blocks = zb_all[order]
    n_zero = jnp.sum(zb_all >= 0).astype(jnp.int32).reshape(1)
    n_items = n_blocks // bpi + N_EXPERTS
    items_e = (nb + bpi - 1) // bpi
    iend = jnp.cumsum(items_e)
    istart = iend - items_e
    slot = jnp.arange(n_items, dtype=jnp.int32)
    valid = slot < iend[-1]
    exp_of = jnp.minimum(jnp.searchsorted(iend, slot, side="right"), N_EXPERTS - 1).astype(jnp.int32)
    last_e = jnp.max(jnp.where(nb > 0, experts, 0))
    local = slot - istart[exp_of]
    n_it = jnp.maximum(items_e[exp_of], 1)
    base, rem = nb[exp_of] // n_it, nb[exp_of] % n_it
    first_blk = bstart[exp_of] + local * base + jnp.minimum(local, rem)
    item_e = jnp.where(valid, exp_of, last_e).astype(jnp.int32)
    item_start = jnp.where(valid, first_blk * tm, 0).astype(jnp.int32)
    item_nblk = jnp.where(valid, base + (local < rem), 0).astype(jnp.int32)
    return dest, zero_blocks, n_zero, item_e, item_start, item_nblk, tail_blk, n_rows


def _rope_lane_tile(v):
    half = MLA_ROPE // 2
    z = jnp.zeros((half,), v.dtype)
    return jnp.concatenate([v[:half], z, v[half:], z]).reshape(1, LANES)


def _rope_cols(w):
    half = MLA_ROPE // 2
    z = jnp.zeros((w.shape[0], half), w.dtype)
    return jnp.concatenate([w[:, :half], z, w[:, half:], z], axis=1)


def _layer(x, mem, positions, attn_norm, w_in, q_a_norm, w_q_b, kv_a_norm, w_kv_b,
           q_nope_norm, q_rope_norm, k_nope_norm, k_rope_norm,
           gmlp_v_norm, w_spatial, b_spatial,
           mem_norm, w_mem_kv, mem_q_norm, mem_k_norm,
           mla_out_norm, gmlp_out_norm, mem_out_norm, w_o,
           ffn_norm, w_router, b_router, w_gate_up, b_gate_up, w_down, b_down):
    B, S, D = x.shape
    N = B * S
    x2 = x.reshape(N, D)

    half = MLA_ROPE // 2
    r1 = Q_LORA + KV_LORA + half
    zc = jnp.zeros((D, half), BF16)
    w_in_b = w_in.astype(BF16)
    w_in_r = jnp.concatenate([w_in_b[:, :r1], zc, w_in_b[:, r1:r1 + half], zc, w_in_b[:, r1 + half:]],
                             axis=1)
    wq = w_q_b.reshape(Q_LORA, MLA_HEADS, MLA_NOPE + MLA_ROPE)
    wq_rope = jax.vmap(_rope_cols, in_axes=1, out_axes=1)(wq[:, :, MLA_NOPE:])
    wq_pad = jnp.concatenate([wq[:, :, :MLA_NOPE], wq_rope], axis=2).reshape(Q_LORA, MLA_HEADS * HEAD_PAD)
    wq_pad = wq_pad.astype(BF16)
    inv_freq = ROPE_BASE ** (-jnp.arange(half, dtype=F32) / half)
    zf = jnp.zeros((half,), F32)
    invf_tile = jnp.concatenate([inv_freq, zf, inv_freq, zf]).reshape(1, LANES)
    sgn_tile = jnp.concatenate([-jnp.ones((half,), F32), zf, jnp.ones((half,), F32), zf]).reshape(1, LANES)
    bs_tile = jnp.repeat(b_spatial.T, GMLP_CH, axis=1)

    mk, mv = _mem_kv(mem, mem_norm, w_mem_kv, mem_k_norm)
    cq, ckv, kpe, ogm, cs = _inproj(
        x2, positions.reshape(N, 1), (invf_tile, sgn_tile), attn_norm, w_in_r, q_a_norm, kv_a_norm,
        _rope_lane_tile(k_rope_norm), gmlp_v_norm, w_spatial, bs_tile, mk, mv, mem_q_norm,
        gmlp_out_norm, mem_out_norm, S)
    q, k, v = _qkv(cq, ckv, kpe, cs, wq_pad, w_kv_b.astype(BF16), q_nope_norm,
                   _rope_lane_tile(q_rope_norm), k_nope_norm)
    o_mla = _attention(q, k, v, B, S).reshape(N, MLA_HEADS * MLA_V)

    wr_pad = jnp.pad(w_router, ((0, 0), (0, LANES - N_EXPERTS)))
    br_pad = jnp.pad(b_router, (0, LANES - N_EXPERTS)).reshape(1, LANES)
    x1, h2, ti, tg = _outproj(x2, o_mla, ogm, mla_out_norm, w_o.astype(BF16), ffn_norm, wr_pad,
                              br_pad)

    dest, zero_blocks, n_zero, item_e, item_start, item_nblk, tail_blk, n_rows = _routing(
        ti[:, :TOP_K], N)
    xs = _dispatch(h2, dest, zero_blocks, n_zero, n_rows)
    ys = _moe(xs, item_e, item_start, item_nblk, tail_blk, w_gate_up, b_gate_up, w_down, b_down)
    out = _combine(x1, tg, ys, dest)
    return out.reshape(B, S, D)


def kernel(x, mem, positions, attn_norm, w_in, q_a_norm, w_q_b, kv_a_norm, w_kv_b, q_nope_norm, q_rope_norm, k_nope_norm, k_rope_norm, gmlp_v_norm, w_spatial, b_spatial, mem_norm, w_mem_kv, mem_q_norm, mem_k_norm, mla_out_norm, gmlp_out_norm, mem_out_norm, w_o, ffn_norm, w_router, b_router, w_gate_up, b_gate_up, w_down, b_down):
    depth = attn_norm.shape[0]
    for l in range(depth):
        x = _layer(x, mem, positions, attn_norm[l], w_in[l], q_a_norm[l], w_q_b[l], kv_a_norm[l],
                   w_kv_b[l], q_nope_norm[l], q_rope_norm[l], k_nope_norm[l], k_rope_norm[l],
                   gmlp_v_norm[l], w_spatial[l], b_spatial[l], mem_norm[l], w_mem_kv[l],
                   mem_q_norm[l], mem_k_norm[l], mla_out_norm[l], gmlp_out_norm[l],
                   mem_out_norm[l], w_o[l], ffn_norm[l], w_router[l], b_router[l], w_gate_up[l],
                   b_gate_up[l], w_down[l], b_down[l])
    return x
```

```python
import functools

import jax
import jax.numpy as jnp
import numpy as np
from jax import lax
from jax.experimental import pallas as pl
from jax.experimental.pallas import tpu as pltpu

F32 = jnp.float32
BF16 = jnp.bfloat16

EPS = 1e-6
LANES = 128
SUBLANES = 8
VMEM_LIMIT = 60 * 1024 * 1024

MLA_HEADS = 8
MLA_NOPE = 128
MLA_ROPE = 64
MLA_V = 128
Q_LORA = 512
KV_LORA = 512
GMLP_GROUPS = 4
GMLP_CH = 128
GMLP_WIDTH = GMLP_GROUPS * GMLP_CH
CHUNK = 128
MEM_HEADS = 4
MEM_HEAD_DIM = 128
MEM_WIDTH = MEM_HEADS * MEM_HEAD_DIM
N_EXPERTS = 32
TOP_K = 4
SWIGLU_ALPHA = 1.702
SWIGLU_LIMIT = 7.0
ROPE_BASE = 10000.0
HEAD_PAD = 2 * LANES

TOK_TILE = 512
ATT_TQ = 512
ATT_TK = 512
ATT_HPS = 4
MOE_TM = 128
MOE_BPI = 10
MOE_TF = 256
MOE_WSLOTS = 4
DSP_TILE = 1024
CMB_TILE = 256


def _rms(x, g, n=None):
    n = x.shape[-1] if n is None else n
    ms = jnp.sum(x * x, axis=-1, keepdims=True) * (1.0 / n)
    return x * lax.rsqrt(ms + EPS) * g


def _gelu(x):
    return 0.5 * x * (1.0 + lax.erf(x * (2.0 ** -0.5)))


def _dot(a, b):
    return jnp.dot(a, b, preferred_element_type=F32)


def _dot_nt(a, b):
    return lax.dot_general(a, b, (((1,), (1,)), ((), ())), preferred_element_type=F32)


def _const_spec(shape):
    nd = len(shape)
    return pl.BlockSpec(shape, lambda *_: (0,) * nd)


def _mem_kv_kernel(mem_ref, g_ref, w_ref, kg_ref, mk_ref, mv_ref):
    m = mem_ref[0]
    hn = _rms(m, g_ref[...]).astype(BF16)
    kv = _dot(hn, w_ref[...])
    for h in range(MEM_HEADS):
        k = kv[:, h * MEM_HEAD_DIM:(h + 1) * MEM_HEAD_DIM]
        mk_ref[0, :, h * MEM_HEAD_DIM:(h + 1) * MEM_HEAD_DIM] = _rms(k, kg_ref[...]).astype(BF16)
    mv_ref[0] = kv[:, MEM_WIDTH:].astype(BF16)


def _mem_kv(mem, mem_norm, w_mem_kv, mem_k_norm):
    B, M, D = mem.shape
    return pl.pallas_call(
        _mem_kv_kernel,
        grid=(B,),
        in_specs=[pl.BlockSpec((1, M, D), lambda b: (b, 0, 0)),
                  _const_spec((1, D)),
                  _const_spec((D, 2 * MEM_WIDTH)),
                  _const_spec((1, MEM_HEAD_DIM))],
        out_specs=[pl.BlockSpec((1, M, MEM_WIDTH), lambda b: (b, 0, 0)),
                   pl.BlockSpec((1, M, MEM_WIDTH), lambda b: (b, 0, 0))],
        out_shape=[jax.ShapeDtypeStruct((B, M, MEM_WIDTH), BF16),
                   jax.ShapeDtypeStruct((B, M, MEM_WIDTH), BF16)],
        compiler_params=pltpu.CompilerParams(dimension_semantics=("arbitrary",),
                                             vmem_limit_bytes=VMEM_LIMIT),
        name="mem_kv",
    )(mem, mem_norm.reshape(1, D), w_mem_kv.astype(BF16), mem_k_norm.reshape(1, MEM_HEAD_DIM))


_O_CQ = 0
_O_CKV = _O_CQ + Q_LORA
_O_KR = _O_CKV + KV_LORA
_O_U = _O_KR + LANES
_O_VG = _O_U + GMLP_WIDTH
_O_QM = _O_VG + GMLP_WIDTH
_IN_COLS_PAD = _O_QM + MEM_WIDTH


def _rope_tile(t, cos, sin_signed):
    return t * cos + pltpu.roll(t, LANES // 2, axis=1) * sin_signed


def _inproj_kernel(x_ref, pos_ref, invf_ref, sgn_ref, an_ref, w_ref, qan_ref, kvan_ref, krn_ref,
                   gvn_ref, ws_ref, bs_ref, mk_ref, mv_ref, mqn_ref, gon_ref, mon_ref,
                   cq_ref, ckv_ref, kpe_ref, ogm_ref, cs_ref):
    def project(sl):
        h = _rms(x_ref[sl, :], an_ref[...]).astype(BF16)
        return _dot(h, w_ref[...])

    def finish(sl, n, z):
        cq_ref[sl, :] = _rms(z[:, _O_CQ:_O_CQ + Q_LORA], qan_ref[...]).astype(BF16)
        ckv_ref[sl, :] = _rms(z[:, _O_CKV:_O_CKV + KV_LORA], kvan_ref[...]).astype(BF16)

        ang = pos_ref[sl, :].astype(F32) * invf_ref[...]
        cos = jnp.cos(ang)
        sin_signed = jnp.sin(ang) * sgn_ref[...]
        cs_ref[sl, :LANES] = cos
        cs_ref[sl, LANES:] = sin_signed
        kr = _rms(z[:, _O_KR:_O_KR + LANES], krn_ref[...], MLA_ROPE)
        kpe_ref[sl, :] = _rope_tile(kr, cos, sin_signed).astype(BF16)

        u = _gelu(z[:, _O_U:_O_U + GMLP_WIDTH])
        vg = _gelu(z[:, _O_VG:_O_VG + GMLP_WIDTH])
        vg = _rms(vg, gvn_ref[...]).astype(BF16)
        row = lax.broadcasted_iota(jnp.int32, (CHUNK, CHUNK), 0)
        col = lax.broadcasted_iota(jnp.int32, (CHUNK, CHUNK), 1)
        sp_cols = []
        for g in range(GMLP_GROUPS):
            wsg = jnp.where(col <= row, ws_ref[g], 0.0).astype(BF16)
            sp_rows = [_dot(wsg, vg[c * CHUNK:(c + 1) * CHUNK, g * GMLP_CH:(g + 1) * GMLP_CH])
                       for c in range(n // CHUNK)]
            sp_cols.append(jnp.concatenate(sp_rows, axis=0))
        sp = jnp.concatenate(sp_cols, axis=1) + jnp.concatenate([bs_ref[...]] * (n // CHUNK), axis=0)
        ogm_ref[sl, :GMLP_WIDTH] = _rms(u * sp, gon_ref[...]).astype(BF16)

        o_heads = []
        for hd in range(MEM_HEADS):
            hs = slice(hd * MEM_HEAD_DIM, (hd + 1) * MEM_HEAD_DIM)
            q = z[:, _O_QM + hd * MEM_HEAD_DIM:_O_QM + (hd + 1) * MEM_HEAD_DIM]
            qn = (_rms(q, mqn_ref[...]) * (MEM_HEAD_DIM ** -0.5)).astype(BF16)
            s = _dot_nt(qn, mk_ref[0, :, hs])
            s = s - jnp.max(s, axis=-1, keepdims=True)
            p = jnp.exp(s)
            p = p / jnp.sum(p, axis=-1, keepdims=True)
            o_heads.append(_dot(p.astype(BF16), mv_ref[0, :, hs]))
        o_mem = jnp.concatenate(o_heads, axis=1)
        ogm_ref[sl, GMLP_WIDTH:] = _rms(o_mem, mon_ref[...]).astype(BF16)

    finish(slice(None), x_ref.shape[0], project(slice(None)))


def _inproj(x2, pos, tables, attn_norm, w_in_r, q_a_norm, kv_a_norm, krn_tile, gmlp_v_norm,
            w_spatial, bs_tile, mk, mv, mem_q_norm, gmlp_out_norm, mem_out_norm, seq):
    N, D = x2.shape
    tm = TOK_TILE
    tiles_per_seq = seq // tm
    invf_tile, sgn_tile = tables
    M = mk.shape[1]
    row = lambda w: pl.BlockSpec((tm, w), lambda i: (i, 0))
    batch_blk = pl.BlockSpec((1, M, MEM_WIDTH), lambda i: (i // tiles_per_seq, 0, 0))
    return pl.pallas_call(
        _inproj_kernel,
        grid=(N // tm,),
        in_specs=[row(D), row(1), _const_spec((1, LANES)), _const_spec((1, LANES)),
                  _const_spec((1, D)), _const_spec((D, _IN_COLS_PAD)),
                  _const_spec((1, Q_LORA)), _const_spec((1, KV_LORA)), _const_spec((1, LANES)),
                  _const_spec((1, GMLP_WIDTH)), _const_spec((GMLP_GROUPS, CHUNK, CHUNK)),
                  _const_spec((CHUNK, GMLP_WIDTH)), batch_blk, batch_blk,
                  _const_spec((1, MEM_HEAD_DIM)), _const_spec((1, GMLP_WIDTH)),
                  _const_spec((1, MEM_WIDTH))],
        out_specs=[row(Q_LORA), row(KV_LORA), row(LANES), row(GMLP_WIDTH + MEM_WIDTH),
                   row(2 * LANES)],
        out_shape=[jax.ShapeDtypeStruct((N, Q_LORA), BF16),
                   jax.ShapeDtypeStruct((N, KV_LORA), BF16),
                   jax.ShapeDtypeStruct((N, LANES), BF16),
                   jax.ShapeDtypeStruct((N, GMLP_WIDTH + MEM_WIDTH), BF16),
                   jax.ShapeDtypeStruct((N, 2 * LANES), F32)],
        compiler_params=pltpu.CompilerParams(dimension_semantics=("arbitrary",),
                                             vmem_limit_bytes=VMEM_LIMIT),
        name="inproj",
    )(x2, pos, invf_tile, sgn_tile, attn_norm.reshape(1, D), w_in_r,
      q_a_norm.reshape(1, -1), kv_a_norm.reshape(1, -1), krn_tile, gmlp_v_norm.reshape(1, -1),
      w_spatial, bs_tile, mk, mv, mem_q_norm.reshape(1, -1), gmlp_out_norm.reshape(1, -1),
      mem_out_norm.reshape(1, -1))


def _qkv_kernel(cq_ref, ckv_ref, kpe_ref, cs_ref, wq_ref, wkv_ref, qnn_ref, qrn_ref, knn_ref,
                q_ref, k_ref, v_ref):
    cos = cs_ref[:, :LANES]
    sin_signed = cs_ref[:, LANES:]
    scale = (MLA_NOPE + MLA_ROPE) ** -0.5 * np.log2(np.e)
    qr = _dot(cq_ref[...], wq_ref[...])
    kvr = _dot(ckv_ref[...], wkv_ref[...])
    kpe = kpe_ref[...]
    for h in range(MLA_HEADS):
        o = h * HEAD_PAD
        qn = _rms(qr[:, o:o + MLA_NOPE], qnn_ref[...]) * scale
        qt = _rms(qr[:, o + MLA_NOPE:o + HEAD_PAD], qrn_ref[...], MLA_ROPE)
        qt = _rope_tile(qt, cos, sin_signed) * scale
        q_ref[:, o:o + MLA_NOPE] = qn.astype(BF16)
        q_ref[:, o + MLA_NOPE:o + HEAD_PAD] = qt.astype(BF16)
        ko = h * (MLA_NOPE + MLA_V)
        k_ref[:, o:o + MLA_NOPE] = _rms(kvr[:, ko:ko + MLA_NOPE], knn_ref[...]).astype(BF16)
        k_ref[:, o + MLA_NOPE:o + HEAD_PAD] = kpe
        v_ref[:, h * MLA_V:(h + 1) * MLA_V] = kvr[:, ko + MLA_NOPE:ko + MLA_NOPE + MLA_V].astype(BF16)


def _qkv(cq, ckv, kpe, cs, wq_pad, wkv, q_nope_norm, qrn_tile, k_nope_norm):
    N = cq.shape[0]
    tm = TOK_TILE
    row = lambda w: pl.BlockSpec((tm, w), lambda i: (i, 0))
    return pl.pallas_call(
        _qkv_kernel,
        grid=(N // tm,),
        in_specs=[row(Q_LORA), row(KV_LORA), row(LANES), row(2 * LANES),
                  _const_spec(wq_pad.shape), _const_spec(wkv.shape),
                  _const_spec((1, MLA_NOPE)), _const_spec((1, LANES)), _const_spec((1, MLA_NOPE))],
        out_specs=[row(MLA_HEADS * HEAD_PAD), row(MLA_HEADS * HEAD_PAD), row(MLA_HEADS * MLA_V)],
        out_shape=[jax.ShapeDtypeStruct((N, MLA_HEADS * HEAD_PAD), BF16),
                   jax.ShapeDtypeStruct((N, MLA_HEADS * HEAD_PAD), BF16),
                   jax.ShapeDtypeStruct((N, MLA_HEADS * MLA_V), BF16)],
        compiler_params=pltpu.CompilerParams(dimension_semantics=("arbitrary",),
                                             vmem_limit_bytes=VMEM_LIMIT),
        name="qkv",
    )(cq, ckv, kpe, cs, wq_pad, wkv, q_nope_norm.reshape(1, -1), qrn_tile,
      k_nope_norm.reshape(1, -1))


def _attn_kernel(q_ref, k_ref, v_ref, o_ref):
    tq, tk = ATT_TQ, ATT_TK
    qi = pl.program_id(2)
    qs = [q_ref[0, :, h * HEAD_PAD:(h + 1) * HEAD_PAD] for h in range(ATT_HPS)]

    def scores(h, kb):
        k0 = pl.multiple_of(kb * tk, tk)
        return _dot_nt(k_ref[0, pl.ds(k0, tk), h * HEAD_PAD:(h + 1) * HEAD_PAD], qs[h])

    def head_step(h, kb, s, carry, masked):
        m, l, acc = carry
        k0 = pl.multiple_of(kb * tk, tk)
        if masked:
            kpos = kb * tk + lax.broadcasted_iota(jnp.int32, (tk, tq), 0)
            qpos = qi * tq + lax.broadcasted_iota(jnp.int32, (tk, tq), 1)
            s = jnp.where(kpos <= qpos, s, -jnp.inf)
        m_new = jnp.maximum(m, jnp.max(s, axis=0, keepdims=True))
        alpha = jnp.exp2(m - m_new)
        p = jnp.exp2(s - m_new)
        l = alpha * l + jnp.sum(p, axis=0, keepdims=True)
        v = v_ref[0, pl.ds(k0, tk), h * MLA_V:(h + 1) * MLA_V]
        pv = lax.dot_general(v, p.astype(BF16), (((0,), (0,)), ((), ())),
                             preferred_element_type=F32)
        return m_new, l, alpha * acc + pv

    def step(kb, carries, masked):
        ss = [scores(h, kb) for h in range(ATT_HPS)]
        return tuple(head_step(h, kb, ss[h], carries[h], masked) for h in range(ATT_HPS))

    init = tuple((jnp.full((1, tq), -jnp.inf, F32), jnp.zeros((1, tq), F32),
                  jnp.zeros((MLA_V, tq), F32)) for _ in range(ATT_HPS))
    n_full = (qi * tq) // tk
    carries = lax.fori_loop(0, n_full, lambda kb, c: step(kb, c, False), init)
    carries = step(n_full, carries, True)
    for h in range(ATT_HPS):
        _, l, acc = carries[h]
        o_ref[0, :, h * MLA_V:(h + 1) * MLA_V] = jnp.transpose(acc / l).astype(BF16)


def _attention(q, k, v, batch, seq):
    q3 = q.reshape(batch, seq, MLA_HEADS * HEAD_PAD)
    k3 = k.reshape(batch, seq, MLA_HEADS * HEAD_PAD)
    v3 = v.reshape(batch, seq, MLA_HEADS * MLA_V)
    hp, hv = ATT_HPS * HEAD_PAD, ATT_HPS * MLA_V
    assert ATT_TK % ATT_TQ == 0 and seq % ATT_TK == 0
    return pl.pallas_call(
        _attn_kernel,
        grid=(batch, MLA_HEADS // ATT_HPS, seq // ATT_TQ),
        in_specs=[pl.BlockSpec((1, ATT_TQ, hp), lambda b, h, i: (b, i, h)),
                  pl.BlockSpec((1, seq, hp), lambda b, h, i: (b, 0, h)),
                  pl.BlockSpec((1, seq, hv), lambda b, h, i: (b, 0, h))],
        out_specs=pl.BlockSpec((1, ATT_TQ, hv), lambda b, h, i: (b, i, h)),
        out_shape=jax.ShapeDtypeStruct((batch, seq, MLA_HEADS * MLA_V), BF16),
        compiler_params=pltpu.CompilerParams(
            dimension_semantics=("arbitrary", "arbitrary", "arbitrary"),
            vmem_limit_bytes=VMEM_LIMIT),
        name="attn",
    )(q3, k3, v3)


def _outproj_kernel(x_ref, oa_ref, ogm_ref, aon_ref, woa_ref, wob_ref, fn_ref, wr_ref, br_ref,
                    x1_ref, h2_ref, ti_ref, tg_ref):
    wr = wr_ref[...]
    w_hi = wr.astype(BF16)
    w_lo = (wr - w_hi.astype(F32)).astype(BF16)
    w_cat = jnp.concatenate([w_hi, w_lo], axis=1)

    def rows(sl, n):
        oa = _rms(oa_ref[sl, :].astype(F32), aon_ref[...]).astype(BF16)
        x1 = x_ref[sl, :] + _dot(oa, woa_ref[...]) + _dot(ogm_ref[sl, :], wob_ref[...])
        x1_ref[sl, :] = x1
        h2 = _rms(x1, fn_ref[...])
        h2_ref[sl, :] = h2
        h_hi = h2.astype(BF16)
        h_lo = (h2 - h_hi.astype(F32)).astype(BF16)
        hh = _dot(h_hi, w_cat)
        logits = hh[:, :LANES] + (_dot(h_lo, w_hi) + hh[:, LANES:]) + br_ref[...]
        lane = lax.broadcasted_iota(jnp.int32, (n, LANES), 1)
        lg = jnp.where(lane < N_EXPERTS, logits, -jnp.inf)
        vals, idxs = [], []
        for _ in range(TOP_K):
            m = jnp.max(lg, axis=-1, keepdims=True)
            am = jnp.min(jnp.where(lg == m, lane, LANES), axis=-1, keepdims=True)
            vals.append(m)
            idxs.append(am)
            lg = jnp.where(lane == am, -jnp.inf, lg)
        es = [jnp.exp(v - vals[0]) for v in vals]
        denom = es[0] + es[1] + es[2] + es[3]
        ti = jnp.zeros((n, LANES), jnp.int32)
        tg = jnp.zeros((n, LANES), F32)
        for kk in range(TOP_K):
            ti = jnp.where(lane == kk, idxs[kk], ti)
            tg = jnp.where(lane == kk, es[kk] / denom, tg)
        ti_ref[sl, :] = ti
        tg_ref[sl, :] = tg

    rows(slice(None), x_ref.shape[0])


def _outproj(x2, o_mla, ogm, mla_out_norm, wo, ffn_norm, wr_pad, br_pad):
    N, D = x2.shape
    tm = TOK_TILE
    row = lambda w: pl.BlockSpec((tm, w), lambda i: (i, 0))
    wa = o_mla.shape[1]
    wb = ogm.shape[1]
    assert wa == wb and wo.shape[0] == wa + wb
    return pl.pallas_call(
        _outproj_kernel,
        grid=(N // tm,),
        in_specs=[row(D), row(wa), row(wb), _const_spec((1, wa)),
                  pl.BlockSpec((wa, D), lambda i: (0, 0)), pl.BlockSpec((wb, D), lambda i: (1, 0)),
                  _const_spec((1, D)), _const_spec((D, LANES)), _const_spec((1, LANES))],
        out_specs=[row(D), row(D), row(LANES), row(LANES)],
        out_shape=[jax.ShapeDtypeStruct((N, D), F32), jax.ShapeDtypeStruct((N, D), F32),
                   jax.ShapeDtypeStruct((N, LANES), jnp.int32),
                   jax.ShapeDtypeStruct((N, LANES), F32)],
        compiler_params=pltpu.CompilerParams(dimension_semantics=("arbitrary",),
                                             vmem_limit_bytes=VMEM_LIMIT),
        name="outproj",
    )(x2, o_mla, ogm, mla_out_norm.reshape(1, -1), wo, wo, ffn_norm.reshape(1, -1), wr_pad,
      br_pad)


def _pack_bf16_pairs(x):
    k = x.shape[1] // 2
    lo = pltpu.bitcast(x[:, :k].astype(BF16).astype(F32), jnp.uint32)
    hi = pltpu.bitcast(x[:, k:].astype(BF16).astype(F32), jnp.uint32)
    return hi | (lo >> 16)


def _unpack_bf16_pairs(w):
    lo = pltpu.bitcast(w << 16, F32).astype(BF16)
    hi = pltpu.bitcast(w & jnp.uint32(0xFFFF0000), F32).astype(BF16)
    return jnp.concatenate([lo, hi], axis=1)


def _dispatch_kernel(dest_ref, zb_ref, nzb_ref, h2_ref, xs_hbm, pk, zblk, sem, sem_z):
    tm = h2_ref.shape[0]
    bm = MOE_TM
    i = pl.program_id(0)
    base = i * (tm * TOP_K)
    b = i % 2
    pk[b] = _pack_bf16_pairs(h2_ref[...])

    def wait_tile(bb):
        for kk in range(TOP_K):
            pltpu.make_async_copy(pk.at[bb], xs_hbm.at[pl.ds(0, tm), :], sem.at[bb]).wait()

    @pl.when(i == 0)
    def _zero_blocks():
        zblk[...] = jnp.zeros(zblk.shape, zblk.dtype)

        def zcopy(n):
            d0 = pl.multiple_of(zb_ref[n] * bm, bm)
            return pltpu.make_async_copy(zblk, xs_hbm.at[pl.ds(d0, bm), :], sem_z)

        def issue(n, c):
            zcopy(n).start()
            return c
        lax.fori_loop(0, nzb_ref[0], issue, 0)

        def finish(n, c):
            zcopy(n).wait()
            return c
        lax.fori_loop(0, nzb_ref[0], finish, 0)

    def issue(g, c):
        r0 = pl.multiple_of(g * SUBLANES, SUBLANES)
        rows = pk.at[b, pl.ds(r0, SUBLANES), :]
        for s in range(SUBLANES):
            for kk in range(TOP_K):
                d = dest_ref[base + (r0 + s) * TOP_K + kk]
                pltpu.make_async_copy(rows.at[pl.ds(s, 1), :], xs_hbm.at[pl.ds(d, 1), :],
                                      sem.at[b]).start()
        return c
    lax.fori_loop(0, tm // SUBLANES, issue, 0)

    @pl.when(i >= 1)
    def _():
        wait_tile(1 - b)

    @pl.when(i == pl.num_programs(0) - 1)
    def _():
        wait_tile(b)


def _dispatch(h2, dest, zero_blocks, n_zero, n_rows):
    N, D = h2.shape
    tm = DSP_TILE
    grid_spec = pltpu.PrefetchScalarGridSpec(
        num_scalar_prefetch=3,
        grid=(N // tm,),
        in_specs=[pl.BlockSpec((tm, D), lambda i, *_: (i, 0))],
        out_specs=pl.BlockSpec(memory_space=pl.ANY),
        scratch_shapes=[pltpu.VMEM((2, tm, D // 2), jnp.uint32),
                        pltpu.VMEM((MOE_TM, D // 2), jnp.uint32),
                        pltpu.SemaphoreType.DMA((2,)), pltpu.SemaphoreType.DMA(())],
    )
    return pl.pallas_call(
        _dispatch_kernel,
        grid_spec=grid_spec,
        out_shape=jax.ShapeDtypeStruct((n_rows, D // 2), jnp.uint32),
        compiler_params=pltpu.CompilerParams(dimension_semantics=("arbitrary",),
                                             vmem_limit_bytes=VMEM_LIMIT),
        name="dispatch",
    )(dest, zero_blocks, n_zero, h2)


def _moe_kernel(ie_ref, ist_ref, inb_ref, tail_ref,
                xs_hbm, wgu_hbm, wd_hbm, bgu_ref, bd_ref,
                ys_hbm,
                xbuf, acc, wg_buf, wu_buf, wd_buf, zblk, sem_x, sem_o, sem_w, sem_z):
    tm = MOE_TM
    i = pl.program_id(0)
    n_items = pl.num_programs(0)
    nblk = inb_ref[i]
    slot = i % 2
    n_blocks = ys_hbm.shape[0] // tm
    tf = wg_buf.shape[2]
    nj = wd_hbm.shape[1] // tf

    def x_copy(it, m):
        s0 = pl.multiple_of(ist_ref[it] + m * tm, tm)
        r0 = pl.multiple_of(m * tm, tm)
        return pltpu.make_async_copy(xs_hbm.at[pl.ds(s0, tm), :],
                                     xbuf.at[it % 2, pl.ds(r0, tm), :], sem_x.at[it % 2])

    def y_copy(it, m):
        d0 = pl.multiple_of(ist_ref[it] + m * tm, tm)
        r0 = pl.multiple_of(m * tm, tm)
        return pltpu.make_async_copy(acc.at[it % 2, pl.ds(r0, tm), :],
                                     ys_hbm.at[pl.ds(d0, tm), :], sem_o.at[it % 2])

    def for_blocks(it, fn):
        def body(m, c):
            fn(it, m)
            return c
        lax.fori_loop(0, inb_ref[it], body, 0)

    start_x = lambda it: for_blocks(it, lambda a, m: x_copy(a, m).start())
    wait_x = lambda it: for_blocks(it, lambda a, m: x_copy(a, m).wait())
    start_y = lambda it: for_blocks(it, lambda a, m: y_copy(a, m).start())
    wait_y = lambda it: for_blocks(it, lambda a, m: y_copy(a, m).wait())

    def zero_copy(b):
        d0 = pl.multiple_of(b * tm, tm)
        return pltpu.make_async_copy(zblk, ys_hbm.at[pl.ds(d0, tm), :], sem_z)

    n_ws = wg_buf.shape[0]

    def w_slot(it, j):
        return (it * nj + j) % n_ws

    def w_copies(it, j):
        e = ie_ref[it]
        ws = w_slot(it, j)
        c0 = pl.multiple_of(j * tf, tf)
        c1 = pl.multiple_of(nj * tf + j * tf, tf)
        return (pltpu.make_async_copy(wgu_hbm.at[e, :, pl.ds(c0, tf)], wg_buf.at[ws], sem_w.at[ws]),
                pltpu.make_async_copy(wgu_hbm.at[e, :, pl.ds(c1, tf)], wu_buf.at[ws], sem_w.at[ws]),
                pltpu.make_async_copy(wd_hbm.at[e, pl.ds(c0, tf), :], wd_buf.at[ws], sem_w.at[ws]))

    def start_w(it, j):
        for cp in w_copies(it, j):
            cp.start()

    def wait_w(it, j):
        for cp in w_copies(it, j):
            cp.wait()

    @pl.when(i == 0)
    def _first_step():
        zblk[...] = jnp.zeros(zblk.shape, zblk.dtype)

        def issue(b, c):
            zero_copy(b).start()
            return c
        lax.fori_loop(tail_ref[0], n_blocks, issue, 0)
        start_x(0)

        @pl.when(nblk > 0)
        def _():
            for a in range(n_ws - 1):
                start_w(0, a)

    wait_x(i)

    @pl.when(i + 1 < n_items)
    def _():
        start_x(i + 1)

    @pl.when(i >= 2)
    def _():
        wait_y(i - 2)

    def init(m, c):
        r0 = pl.multiple_of(m * tm, tm)
        acc[slot, pl.ds(r0, tm), :] = jnp.broadcast_to(bd_ref[0], (tm, acc.shape[2]))
        return c
    lax.fori_loop(0, nblk, init, 0)

    def ffn_rows(j, n):
        ws = w_slot(i, j)
        x = _unpack_bf16_pairs(xbuf[slot, pl.ds(0, n), :])
        g = jnp.minimum(_dot(x, wg_buf[ws].astype(BF16)) + bgu_ref[0, pl.ds(j, 1), :], SWIGLU_LIMIT)
        u = jnp.clip(_dot(x, wu_buf[ws].astype(BF16)) + bgu_ref[0, pl.ds(nj + j, 1), :],
                     -SWIGLU_LIMIT, SWIGLU_LIMIT)
        a = (u + 1.0) * (g * jax.nn.sigmoid(SWIGLU_ALPHA * g))
        acc[slot, pl.ds(0, n), :] += _dot(a.astype(BF16), wd_buf[ws].astype(BF16))

    def chunk(j, c):
        wait_w(i, j)
        ja = j + (n_ws - 1)

        @pl.when(ja < nj)
        def _():
            start_w(i, ja)

        @pl.when(jnp.logical_and(ja >= nj, i + 1 < n_items))
        def _():
            @pl.when(inb_ref[i + 1] > 0)
            def _():
                start_w(i + 1, ja - nj)

        for nb in range(1, MOE_BPI + 1):
            pl.when(nblk == nb)(functools.partial(ffn_rows, j, nb * tm))
        return c

    @pl.when(nblk > 0)
    def _chunks():
        lax.fori_loop(0, nj, chunk, 0)

    start_y(i)

    @pl.when(i == n_items - 1)
    def _last_step():
        @pl.when(i >= 1)
        def _():
            wait_y(i - 1)
        wait_y(i)

        def finish(b, c):
            zero_copy(b).wait()
            return c
        lax.fori_loop(tail_ref[0], n_blocks, finish, 0)


def _moe(xs, item_e, item_start, item_nblk, tail_blk, w_gate_up, b_gate_up, w_down, b_down):
    n_rows = xs.shape[0]
    E, D, F2 = w_gate_up.shape
    F = F2 // 2
    tf = MOE_TF
    nj = F // tf
    nw = MOE_WSLOTS
    assert 2 <= nw <= nj + 1
    n_items = item_e.shape[0]
    rows = MOE_BPI * MOE_TM
    any_spec = pl.BlockSpec(memory_space=pl.ANY)
    grid_spec = pltpu.PrefetchScalarGridSpec(
        num_scalar_prefetch=4,
        grid=(n_items,),
        in_specs=[any_spec, any_spec, any_spec,
                  pl.BlockSpec((1, 2 * nj, tf), lambda i, ie, ist, inb, tail: (ie[i], 0, 0)),
                  pl.BlockSpec((1, 1, D), lambda i, ie, ist, inb, tail: (ie[i], 0, 0))],
        out_specs=any_spec,
        scratch_shapes=[pltpu.VMEM((2, rows, D // 2), jnp.uint32), pltpu.VMEM((2, rows, D), F32),
                        pltpu.VMEM((nw, D, tf), F32), pltpu.VMEM((nw, D, tf), F32),
                        pltpu.VMEM((nw, tf, D), F32), pltpu.VMEM((MOE_TM, D), F32),
                        pltpu.SemaphoreType.DMA((2,)), pltpu.SemaphoreType.DMA((2,)),
                        pltpu.SemaphoreType.DMA((nw,)), pltpu.SemaphoreType.DMA(())],
    )
    return pl.pallas_call(
        _moe_kernel,
        grid_spec=grid_spec,
        out_shape=jax.ShapeDtypeStruct((n_rows, D), F32),
        compiler_params=pltpu.CompilerParams(dimension_semantics=("arbitrary",),
                                             vmem_limit_bytes=VMEM_LIMIT),
        name="moe",
    )(item_e, item_start, item_nblk, tail_blk, xs, w_gate_up, w_down,
      b_gate_up.reshape(E, 2 * nj, tf), b_down.reshape(E, 1, D))


def _combine_kernel(pos_ref, x1_ref, g_ref, ys_hbm, o_ref, buf, sem):
    tm = CMB_TILE
    i = pl.program_id(0)

    def start_gather(t):
        base = t * (tm * TOP_K)
        b = t % 2

        def issue(g, c):
            r0 = pl.multiple_of(g * SUBLANES, SUBLANES)
            for s in range(SUBLANES):
                for kk in range(TOP_K):
                    p = pos_ref[base + (r0 + s) * TOP_K + kk]
                    pltpu.make_async_copy(ys_hbm.at[pl.ds(p, 1), :],
                                          buf.at[b, kk, pl.ds(r0, SUBLANES), :].at[pl.ds(s, 1), :],
                                          sem.at[b]).start()
            return c
        lax.fori_loop(0, tm // SUBLANES, issue, 0)

    @pl.when(i == 0)
    def _():
        start_gather(0)

    @pl.when(i + 1 < pl.num_programs(0))
    def _():
        start_gather(i + 1)

    b = i % 2
    for kk in range(TOP_K):
        pltpu.make_async_copy(ys_hbm.at[pl.ds(0, tm), :], buf.at[b, kk], sem.at[b]).wait()
    out = x1_ref[...]
    for kk in range(TOP_K):
        out = out + g_ref[:, kk:kk + 1] * buf[b, kk]
    o_ref[...] = out


def _combine(x1, gates, ys, dest):
    N, D = x1.shape
    tm = CMB_TILE
    grid_spec = pltpu.PrefetchScalarGridSpec(
        num_scalar_prefetch=1,
        grid=(N // tm,),
        in_specs=[pl.BlockSpec((tm, D), lambda i, pos: (i, 0)),
                  pl.BlockSpec((tm, LANES), lambda i, pos: (i, 0)),
                  pl.BlockSpec(memory_space=pl.ANY)],
        out_specs=pl.BlockSpec((tm, D), lambda i, pos: (i, 0)),
        scratch_shapes=[pltpu.VMEM((2, TOP_K, tm, D), F32), pltpu.SemaphoreType.DMA((2,))],
    )
    return pl.pallas_call(
        _combine_kernel,
        grid_spec=grid_spec,
        out_shape=jax.ShapeDtypeStruct((N, D), F32),
        compiler_params=pltpu.CompilerParams(dimension_semantics=("arbitrary",),
                                             vmem_limit_bytes=VMEM_LIMIT),
        name="combine",
    )(dest, x1, gates, ys)


def _routing(top_idx, n_tok):
    tm, bpi = MOE_TM, MOE_BPI
    nk = n_tok * TOP_K
    experts = jnp.arange(N_EXPERTS, dtype=jnp.int32)
    e_flat = top_idx.reshape(-1)
    onehot = (e_flat[:, None] == experts[None, :]).astype(jnp.int32)
    csum = jnp.cumsum(onehot, axis=0)
    rank = jnp.sum(csum * onehot, axis=1) - 1
    counts = csum[-1]
    nb = (counts + tm - 1) // tm
    bend = jnp.cumsum(nb)
    bstart = bend - nb
    dest = (jnp.sum(onehot * bstart[None, :], axis=1) * tm + rank).astype(jnp.int32)
    n_blocks = -(-(nk + N_EXPERTS * (tm - 1)) // tm)
    n_rows = n_blocks * tm
    tail_blk = bend[-1:].astype(jnp.int32)
    zb_e = jnp.where(nb > 0, bend - 1, -1)
    zb_t = jnp.arange(n_blocks, dtype=jnp.int32)
    zb_all = jnp.concatenate([zb_e, jnp.where(zb_t >= bend[-1], zb_t, -1)]).astype(jnp.int32)
    order = jnp.argsort(zb_all < 0, stable=True)
    zero_blocks = zb_all[order]
    n_zero = jnp.sum(zb_all >= 0).astype(jnp.int32).reshape(1)
    n_items = n_blocks // bpi + N_EXPERTS
    items_e = (nb + bpi - 1) // bpi
    iend = jnp.cumsum(items_e)
    istart = iend - items_e
    slot = jnp.arange(n_items, dtype=jnp.int32)
    valid = slot < iend[-1]
    exp_of = jnp.minimum(jnp.searchsorted(iend, slot, side="right"), N_EXPERTS - 1).astype(jnp.int32)
    last_e = jnp.max(jnp.where(nb > 0, experts, 0))
    local = slot - istart[exp_of]
    n_it = jnp.maximum(items_e[exp_of], 1)
    base, rem = nb[exp_of] // n_it, nb[exp_of] % n_it
    first_blk = bstart[exp_of] + local * base + jnp.minimum(local, rem)
    item_e = jnp.where(valid, exp_of, last_e).astype(jnp.int32)
    item_start = jnp.where(valid, first_blk * tm, 0).astype(jnp.int32)
    item_nblk = jnp.where(valid, base + (local < rem), 0).astype(jnp.int32)
    return dest, zero_blocks, n_zero, item_e, item_start, item_nblk, tail_blk, n_rows


def _rope_lane_tile(v):
    half = MLA_ROPE // 2
    z = jnp.zeros((half,), v.dtype)
    return jnp.concatenate([v[:half], z, v[half:], z]).reshape(1, LANES)


def _rope_cols(w):
    half = MLA_ROPE // 2
    z = jnp.zeros((w.shape[0], half), w.dtype)
    return jnp.concatenate([w[:, :half], z, w[:, half:], z], axis=1)


def _layer(x, mem, positions, attn_norm, w_in, q_a_norm, w_q_b, kv_a_norm, w_kv_b,
           q_nope_norm, q_rope_norm, k_nope_norm, k_rope_norm,
           gmlp_v_norm, w_spatial, b_spatial,
           mem_norm, w_mem_kv, mem_q_norm, mem_k_norm,
           mla_out_norm, gmlp_out_norm, mem_out_norm, w_o,
           ffn_norm, w_router, b_router, w_gate_up, b_gate_up, w_down, b_down):
    B, S, D = x.shape
    N = B * S
    x2 = x.reshape(N, D)

    half = MLA_ROPE // 2
    r1 = Q_LORA + KV_LORA + half
    zc = jnp.zeros((D, half), BF16)
    w_in_b = w_in.astype(BF16)
    w_in_r = jnp.concatenate([w_in_b[:, :r1], zc, w_in_b[:, r1:r1 + half], zc, w_in_b[:, r1 + half:]],
                             axis=1)
    wq = w_q_b.reshape(Q_LORA, MLA_HEADS, MLA_NOPE + MLA_ROPE)
    wq_rope = jax.vmap(_rope_cols, in_axes=1, out_axes=1)(wq[:, :, MLA_NOPE:])
    wq_pad = jnp.concatenate([wq[:, :, :MLA_NOPE], wq_rope], axis=2).reshape(Q_LORA, MLA_HEADS * HEAD_PAD)
    wq_pad = wq_pad.astype(BF16)
    inv_freq = ROPE_BASE ** (-jnp.arange(half, dtype=F32) / half)
    zf = jnp.zeros((half,), F32)
    invf_tile = jnp.concatenate([inv_freq, zf, inv_freq, zf]).reshape(1, LANES)
    sgn_tile = jnp.concatenate([-jnp.ones((half,), F32), zf, jnp.ones((half,), F32), zf]).reshape(1, LANES)
    bs_tile = jnp.repeat(b_spatial.T, GMLP_CH, axis=1)

    mk, mv = _mem_kv(mem, mem_norm, w_mem_kv, mem_k_norm)
    cq, ckv, kpe, ogm, cs = _inproj(
        x2, positions.reshape(N, 1), (invf_tile, sgn_tile), attn_norm, w_in_r, q_a_norm, kv_a_norm,
        _rope_lane_tile(k_rope_norm), gmlp_v_norm, w_spatial, bs_tile, mk, mv, mem_q_norm,
        gmlp_out_norm, mem_out_norm, S)
    q, k, v = _qkv(cq, ckv, kpe, cs, wq_pad, w_kv_b.astype(BF16), q_nope_norm,
                   _rope_lane_tile(q_rope_norm), k_nope_norm)
    o_mla = _attention(q, k, v, B, S).reshape(N, MLA_HEADS * MLA_V)

    wr_pad = jnp.pad(w_router, ((0, 0), (0, LANES - N_EXPERTS)))
    br_pad = jnp.pad(b_router, (0, LANES - N_EXPERTS)).reshape(1, LANES)
    x1, h2, ti, tg = _outproj(x2, o_mla, ogm, mla_out_norm, w_o.astype(BF16), ffn_norm, wr_pad,
                              br_pad)

    dest, zero_blocks, n_zero, item_e, item_start, item_nblk, tail_blk, n_rows = _routing(
        ti[:, :TOP_K], N)
    xs = _dispatch(h2, dest, zero_blocks, n_zero, n_rows)
    ys = _moe(xs, item_e, item_start, item_nblk, tail_blk, w_gate_up, b_gate_up, w_down, b_down)
    out = _combine(x1, tg, ys, dest)
    return out.reshape(B, S, D)


def kernel(x, mem, positions, attn_norm, w_in, q_a_norm, w_q_b, kv_a_norm, w_kv_b, q_nope_norm, q_rope_norm, k_nope_norm, k_rope_norm, gmlp_v_norm, w_spatial, b_spatial, mem_norm, w_mem_kv, mem_q_norm, mem_k_norm, mla_out_norm, gmlp_out_norm, mem_out_norm, w_o, ffn_norm, w_router, b_router, w_gate_up, b_gate_up, w_down, b_down):
    depth = attn_norm.shape[0]
    for l in range(depth):
        x = _layer(x, mem, positions, attn_norm[l], w_in[l], q_a_norm[l], w_q_b[l], kv_a_norm[l],
                   w_kv_b[l], q_nope_norm[l], q_rope_norm[l], k_nope_norm[l], k_rope_norm[l],
                   gmlp_v_norm[l], w_spatial[l], b_spatial[l], mem_norm[l], w_mem_kv[l],
                   mem_q_norm[l], mem_k_norm[l], mla_out_norm[l], gmlp_out_norm[l],
                   mem_out_norm[l], w_o[l], ffn_norm[l], w_router[l], b_router[l], w_gate_up[l],
                   b_gate_up[l], w_down[l], b_down[l])
    return x
```

```python
import functools

import jax
import jax.numpy as jnp
import numpy as np
from jax import lax
from jax.experimental import pallas as pl
from jax.experimental.pallas import tpu as pltpu

F32 = jnp.float32
BF16 = jnp.bfloat16

EPS = 1e-6
LANES = 128
SUBLANES = 8
VMEM_LIMIT = 56 * 1024 * 1024

MLA_HEADS = 8
MLA_NOPE = 128
MLA_ROPE = 64
MLA_V = 128
Q_LORA = 512
KV_LORA = 512
GMLP_GROUPS = 4
GMLP_CH = 128
GMLP_WIDTH = GMLP_GROUPS * GMLP_CH
CHUNK = 128
MEM_HEADS = 4
MEM_HEAD_DIM = 128
MEM_WIDTH = MEM_HEADS * MEM_HEAD_DIM
N_EXPERTS = 32
TOP_K = 4
SWIGLU_ALPHA = 1.702
SWIGLU_LIMIT = 7.0
ROPE_BASE = 10000.0
HEAD_PAD = 2 * LANES

TOK_TILE = 512
ATT_TQ = 512
ATT_TK = 512
ATT_HPS = 4
MOE_TM = 128
MOE_BPI = 10
MOE_TF = 256
MOE_WSLOTS = 3
DSP_TILE = 1024
CMB_TILE = 256


def _rms(x, g, n=None):
    n = x.shape[-1] if n is None else n
    ms = jnp.sum(x * x, axis=-1, keepdims=True) * (1.0 / n)
    return x * lax.rsqrt(ms + EPS) * g


def _gelu(x):
    return 0.5 * x * (1.0 + lax.erf(x * (2.0 ** -0.5)))


def _dot(a, b):
    return jnp.dot(a, b, preferred_element_type=F32)


def _dot_nt(a, b):
    return lax.dot_general(a, b, (((1,), (1,)), ((), ())), preferred_element_type=F32)


def _const_spec(shape):
    nd = len(shape)
    return pl.BlockSpec(shape, lambda *_: (0,) * nd)


def _mem_kv_kernel(mem_ref, g_ref, w_ref, kg_ref, mk_ref, mv_ref):
    m = mem_ref[0]
    hn = _rms(m, g_ref[...]).astype(BF16)
    kv = _dot(hn, w_ref[...])
    for h in range(MEM_HEADS):
        k = kv[:, h * MEM_HEAD_DIM:(h + 1) * MEM_HEAD_DIM]
        mk_ref[0, :, h * MEM_HEAD_DIM:(h + 1) * MEM_HEAD_DIM] = _rms(k, kg_ref[...]).astype(BF16)
    mv_ref[0] = kv[:, MEM_WIDTH:].astype(BF16)


def _mem_kv(mem, mem_norm, w_mem_kv, mem_k_norm):
    B, M, D = mem.shape
    return pl.pallas_call(
        _mem_kv_kernel,
        grid=(B,),
        in_specs=[pl.BlockSpec((1, M, D), lambda b: (b, 0, 0)),
                  _const_spec((1, D)),
                  _const_spec((D, 2 * MEM_WIDTH)),
                  _const_spec((1, MEM_HEAD_DIM))],
        out_specs=[pl.BlockSpec((1, M, MEM_WIDTH), lambda b: (b, 0, 0)),
                   pl.BlockSpec((1, M, MEM_WIDTH), lambda b: (b, 0, 0))],
        out_shape=[jax.ShapeDtypeStruct((B, M, MEM_WIDTH), BF16),
                   jax.ShapeDtypeStruct((B, M, MEM_WIDTH), BF16)],
        compiler_params=pltpu.CompilerParams(dimension_semantics=("arbitrary",),
                                             vmem_limit_bytes=VMEM_LIMIT),
        name="mem_kv",
    )(mem, mem_norm.reshape(1, D), w_mem_kv.astype(BF16), mem_k_norm.reshape(1, MEM_HEAD_DIM))


_O_CQ = 0
_O_CKV = _O_CQ + Q_LORA
_O_KR = _O_CKV + KV_LORA
_O_U = _O_KR + LANES
_O_VG = _O_U + GMLP_WIDTH
_O_QM = _O_VG + GMLP_WIDTH
_IN_COLS_PAD = _O_QM + MEM_WIDTH


def _rope_tile(t, cos, sin_signed):
    return t * cos + pltpu.roll(t, LANES // 2, axis=1) * sin_signed


def _inproj_kernel(x_ref, pos_ref, invf_ref, sgn_ref, an_ref, w_ref, qan_ref, kvan_ref, krn_ref,
                   gvn_ref, ws_ref, bs_ref, mk_ref, mv_ref, mqn_ref, gon_ref, mon_ref,
                   cq_ref, ckv_ref, kpe_ref, ogm_ref, cs_ref):
    def project(sl):
        h = _rms(x_ref[sl, :], an_ref[...]).astype(BF16)
        return _dot(h, w_ref[...])

    def finish(sl, n, z):
        cq_ref[sl, :] = _rms(z[:, _O_CQ:_O_CQ + Q_LORA], qan_ref[...]).astype(BF16)
        ckv_ref[sl, :] = _rms(z[:, _O_CKV:_O_CKV + KV_LORA], kvan_ref[...]).astype(BF16)

        ang = pos_ref[sl, :].astype(F32) * invf_ref[...]
        cos = jnp.cos(ang)
        sin_signed = jnp.sin(ang) * sgn_ref[...]
        cs_ref[sl, :LANES] = cos
        cs_ref[sl, LANES:] = sin_signed
        kr = _rms(z[:, _O_KR:_O_KR + LANES], krn_ref[...], MLA_ROPE)
        kpe_ref[sl, :] = _rope_tile(kr, cos, sin_signed).astype(BF16)

        u = _gelu(z[:, _O_U:_O_U + GMLP_WIDTH])
        vg = _gelu(z[:, _O_VG:_O_VG + GMLP_WIDTH])
        vg = _rms(vg, gvn_ref[...]).astype(BF16)
        row = lax.broadcasted_iota(jnp.int32, (CHUNK, CHUNK), 0)
        col = lax.broadcasted_iota(jnp.int32, (CHUNK, CHUNK), 1)
        sp_cols = []
        for g in range(GMLP_GROUPS):
            wsg = jnp.where(col <= row, ws_ref[g], 0.0).astype(BF16)
            sp_rows = [_dot(wsg, vg[c * CHUNK:(c + 1) * CHUNK, g * GMLP_CH:(g + 1) * GMLP_CH])
                       for c in range(n // CHUNK)]
            sp_cols.append(jnp.concatenate(sp_rows, axis=0))
        sp = jnp.concatenate(sp_cols, axis=1) + jnp.concatenate([bs_ref[...]] * (n // CHUNK), axis=0)
        ogm_ref[sl, :GMLP_WIDTH] = _rms(u * sp, gon_ref[...]).astype(BF16)

        o_heads = []
        for hd in range(MEM_HEADS):
            hs = slice(hd * MEM_HEAD_DIM, (hd + 1) * MEM_HEAD_DIM)
            q = z[:, _O_QM + hd * MEM_HEAD_DIM:_O_QM + (hd + 1) * MEM_HEAD_DIM]
            qn = (_rms(q, mqn_ref[...]) * (MEM_HEAD_DIM ** -0.5)).astype(BF16)
            s = _dot_nt(qn, mk_ref[0, :, hs])
            s = s - jnp.max(s, axis=-1, keepdims=True)
            p = jnp.exp(s)
            p = p / jnp.sum(p, axis=-1, keepdims=True)
            o_heads.append(_dot(p.astype(BF16), mv_ref[0, :, hs]))
        o_mem = jnp.concatenate(o_heads, axis=1)
        ogm_ref[sl, GMLP_WIDTH:] = _rms(o_mem, mon_ref[...]).astype(BF16)

    finish(slice(None), x_ref.shape[0], project(slice(None)))


def _inproj(x2, pos, tables, attn_norm, w_in_r, q_a_norm, kv_a_norm, krn_tile, gmlp_v_norm,
            w_spatial, bs_tile, mk, mv, mem_q_norm, gmlp_out_norm, mem_out_norm, seq):
    N, D = x2.shape
    tm = TOK_TILE
    tiles_per_seq = seq // tm
    invf_tile, sgn_tile = tables
    M = mk.shape[1]
    row = lambda w: pl.BlockSpec((tm, w), lambda i: (i, 0))
    batch_blk = pl.BlockSpec((1, M, MEM_WIDTH), lambda i: (i // tiles_per_seq, 0, 0))
    return pl.pallas_call(
        _inproj_kernel,
        grid=(N // tm,),
        in_specs=[row(D), row(1), _const_spec((1, LANES)), _const_spec((1, LANES)),
                  _const_spec((1, D)), _const_spec((D, _IN_COLS_PAD)),
                  _const_spec((1, Q_LORA)), _const_spec((1, KV_LORA)), _const_spec((1, LANES)),
                  _const_spec((1, GMLP_WIDTH)), _const_spec((GMLP_GROUPS, CHUNK, CHUNK)),
                  _const_spec((CHUNK, GMLP_WIDTH)), batch_blk, batch_blk,
                  _const_spec((1, MEM_HEAD_DIM)), _const_spec((1, GMLP_WIDTH)),
                  _const_spec((1, MEM_WIDTH))],
        out_specs=[row(Q_LORA), row(KV_LORA), row(LANES), row(GMLP_WIDTH + MEM_WIDTH),
                   row(2 * LANES)],
        out_shape=[jax.ShapeDtypeStruct((N, Q_LORA), BF16),
                   jax.ShapeDtypeStruct((N, KV_LORA), BF16),
                   jax.ShapeDtypeStruct((N, LANES), BF16),
                   jax.ShapeDtypeStruct((N, GMLP_WIDTH + MEM_WIDTH), BF16),
                   jax.ShapeDtypeStruct((N, 2 * LANES), F32)],
        compiler_params=pltpu.CompilerParams(dimension_semantics=("arbitrary",),
                                             vmem_limit_bytes=VMEM_LIMIT),
        name="inproj",
    )(x2, pos, invf_tile, sgn_tile, attn_norm.reshape(1, D), w_in_r,
      q_a_norm.reshape(1, -1), kv_a_norm.reshape(1, -1), krn_tile, gmlp_v_norm.reshape(1, -1),
      w_spatial, bs_tile, mk, mv, mem_q_norm.reshape(1, -1), gmlp_out_norm.reshape(1, -1),
      mem_out_norm.reshape(1, -1))


def _qkv_kernel(cq_ref, ckv_ref, kpe_ref, cs_ref, wq_ref, wkv_ref, qnn_ref, qrn_ref, knn_ref,
                q_ref, k_ref, v_ref):
    cos = cs_ref[:, :LANES]
    sin_signed = cs_ref[:, LANES:]
    scale = (MLA_NOPE + MLA_ROPE) ** -0.5 * np.log2(np.e)
    qr = _dot(cq_ref[...], wq_ref[...])
    kvr = _dot(ckv_ref[...], wkv_ref[...])
    kpe = kpe_ref[...]
    for h in range(MLA_HEADS):
        o = h * HEAD_PAD
        qn = _rms(qr[:, o:o + MLA_NOPE], qnn_ref[...]) * scale
        qt = _rms(qr[:, o + MLA_NOPE:o + HEAD_PAD], qrn_ref[...], MLA_ROPE)
        qt = _rope_tile(qt, cos, sin_signed) * scale
        q_ref[:, o:o + MLA_NOPE] = qn.astype(BF16)
        q_ref[:, o + MLA_NOPE:o + HEAD_PAD] = qt.astype(BF16)
        ko = h * (MLA_NOPE + MLA_V)
        k_ref[:, o:o + MLA_NOPE] = _rms(kvr[:, ko:ko + MLA_NOPE], knn_ref[...]).astype(BF16)
        k_ref[:, o + MLA_NOPE:o + HEAD_PAD] = kpe
        v_ref[:, h * MLA_V:(h + 1) * MLA_V] = kvr[:, ko + MLA_NOPE:ko + MLA_NOPE + MLA_V].astype(BF16)


def _qkv(cq, ckv, kpe, cs, wq_pad, wkv, q_nope_norm, qrn_tile, k_nope_norm):
    N = cq.shape[0]
    tm = TOK_TILE
    row = lambda w: pl.BlockSpec((tm, w), lambda i: (i, 0))
    return pl.pallas_call(
        _qkv_kernel,
        grid=(N // tm,),
        in_specs=[row(Q_LORA), row(KV_LORA), row(LANES), row(2 * LANES),
                  _const_spec(wq_pad.shape), _const_spec(wkv.shape),
                  _const_spec((1, MLA_NOPE)), _const_spec((1, LANES)), _const_spec((1, MLA_NOPE))],
        out_specs=[row(MLA_HEADS * HEAD_PAD), row(MLA_HEADS * HEAD_PAD), row(MLA_HEADS * MLA_V)],
        out_shape=[jax.ShapeDtypeStruct((N, MLA_HEADS * HEAD_PAD), BF16),
                   jax.ShapeDtypeStruct((N, MLA_HEADS * HEAD_PAD), BF16),
                   jax.ShapeDtypeStruct((N, MLA_HEADS * MLA_V), BF16)],
        compiler_params=pltpu.CompilerParams(dimension_semantics=("arbitrary",),
                                             vmem_limit_bytes=VMEM_LIMIT),
        name="qkv",
    )(cq, ckv, kpe, cs, wq_pad, wkv, q_nope_norm.reshape(1, -1), qrn_tile,
      k_nope_norm.reshape(1, -1))


def _attn_kernel(q_ref, k_ref, v_ref, o_ref):
    tq, tk = ATT_TQ, ATT_TK
    qi = pl.program_id(2)
    qs = [q_ref[0, :, h * HEAD_PAD:(h + 1) * HEAD_PAD] for h in range(ATT_HPS)]

    def scores(h, kb):
        k0 = pl.multiple_of(kb * tk, tk)
        return _dot_nt(k_ref[0, pl.ds(k0, tk), h * HEAD_PAD:(h + 1) * HEAD_PAD], qs[h])

    def head_step(h, kb, s, carry, masked):
        m, l, acc = carry
        k0 = pl.multiple_of(kb * tk, tk)
        if masked:
            kpos = kb * tk + lax.broadcasted_iota(jnp.int32, (tk, tq), 0)
            qpos = qi * tq + lax.broadcasted_iota(jnp.int32, (tk, tq), 1)
            s = jnp.where(kpos <= qpos, s, -jnp.inf)
        m_new = jnp.maximum(m, jnp.max(s, axis=0, keepdims=True))
        alpha = jnp.exp2(m - m_new)
        p = jnp.exp2(s - m_new)
        l = alpha * l + jnp.sum(p, axis=0, keepdims=True)
        v = v_ref[0, pl.ds(k0, tk), h * MLA_V:(h + 1) * MLA_V]
        pv = lax.dot_general(v, p.astype(BF16), (((0,), (0,)), ((), ())),
                             preferred_element_type=F32)
        return m_new, l, alpha * acc + pv

    def step(kb, carries, masked):
        ss = [scores(h, kb) for h in range(ATT_HPS)]
        return tuple(head_step(h, kb, ss[h], carries[h], masked) for h in range(ATT_HPS))

    init = tuple((jnp.full((1, tq), -jnp.inf, F32), jnp.zeros((1, tq), F32),
                  jnp.zeros((MLA_V, tq), F32)) for _ in range(ATT_HPS))
    n_full = (qi * tq) // tk
    carries = lax.fori_loop(0, n_full, lambda kb, c: step(kb, c, False), init)
    carries = step(n_full, carries, True)
    for h in range(ATT_HPS):
        _, l, acc = carries[h]
        o_ref[0, :, h * MLA_V:(h + 1) * MLA_V] = jnp.transpose(acc / l).astype(BF16)


def _attention(q, k, v, batch, seq):
    q3 = q.reshape(batch, seq, MLA_HEADS * HEAD_PAD)
    k3 = k.reshape(batch, seq, MLA_HEADS * HEAD_PAD)
    v3 = v.reshape(batch, seq, MLA_HEADS * MLA_V)
    hp, hv = ATT_HPS * HEAD_PAD, ATT_HPS * MLA_V
    assert ATT_TK % ATT_TQ == 0 and seq % ATT_TK == 0
    return pl.pallas_call(
        _attn_kernel,
        grid=(batch, MLA_HEADS // ATT_HPS, seq // ATT_TQ),
        in_specs=[pl.BlockSpec((1, ATT_TQ, hp), lambda b, h, i: (b, i, h)),
                  pl.BlockSpec((1, seq, hp), lambda b, h, i: (b, 0, h)),
                  pl.BlockSpec((1, seq, hv), lambda b, h, i: (b, 0, h))],
        out_specs=pl.BlockSpec((1, ATT_TQ, hv), lambda b, h, i: (b, i, h)),
        out_shape=jax.ShapeDtypeStruct((batch, seq, MLA_HEADS * MLA_V), BF16),
        compiler_params=pltpu.CompilerParams(
            dimension_semantics=("arbitrary", "arbitrary", "arbitrary"),
            vmem_limit_bytes=VMEM_LIMIT),
        name="attn",
    )(q3, k3, v3)


def _outproj_kernel(x_ref, oa_ref, ogm_ref, aon_ref, woa_ref, wob_ref, fn_ref, wr_ref, br_ref,
                    x1_ref, h2_ref, ti_ref, tg_ref):
    wr = wr_ref[...]
    w_hi = wr.astype(BF16)
    w_lo = (wr - w_hi.astype(F32)).astype(BF16)
    w_cat = jnp.concatenate([w_hi, w_lo], axis=1)

    def rows(sl, n):
        oa = _rms(oa_ref[sl, :].astype(F32), aon_ref[...]).astype(BF16)
        x1 = x_ref[sl, :] + _dot(oa, woa_ref[...]) + _dot(ogm_ref[sl, :], wob_ref[...])
        x1_ref[sl, :] = x1
        h2 = _rms(x1, fn_ref[...])
        h2_ref[sl, :] = h2
        h_hi = h2.astype(BF16)
        h_lo = (h2 - h_hi.astype(F32)).astype(BF16)
        hh = _dot(h_hi, w_cat)
        logits = hh[:, :LANES] + (_dot(h_lo, w_hi) + hh[:, LANES:]) + br_ref[...]
        lane = lax.broadcasted_iota(jnp.int32, (n, LANES), 1)
        lg = jnp.where(lane < N_EXPERTS, logits, -jnp.inf)
        vals, idxs = [], []
        for _ in range(TOP_K):
            m = jnp.max(lg, axis=-1, keepdims=True)
            am = jnp.min(jnp.where(lg == m, lane, LANES), axis=-1, keepdims=True)
            vals.append(m)
            idxs.append(am)
            lg = jnp.where(lane == am, -jnp.inf, lg)
        es = [jnp.exp(v - vals[0]) for v in vals]
        denom = es[0] + es[1] + es[2] + es[3]
        ti = jnp.zeros((n, LANES), jnp.int32)
        tg = jnp.zeros((n, LANES), F32)
        for kk in range(TOP_K):
            ti = jnp.where(lane == kk, idxs[kk], ti)
            tg = jnp.where(lane == kk, es[kk] / denom, tg)
        ti_ref[sl, :] = ti
        tg_ref[sl, :] = tg

    rows(slice(None), x_ref.shape[0])


def _outproj(x2, o_mla, ogm, mla_out_norm, wo, ffn_norm, wr_pad, br_pad):
    N, D = x2.shape
    tm = TOK_TILE
    row = lambda w: pl.BlockSpec((tm, w), lambda i: (i, 0))
    wa = o_mla.shape[1]
    wb = ogm.shape[1]
    assert wa == wb and wo.shape[0] == wa + wb
    return pl.pallas_call(
        _outproj_kernel,
        grid=(N // tm,),
        in_specs=[row(D), row(wa), row(wb), _const_spec((1, wa)),
                  pl.BlockSpec((wa, D), lambda i: (0, 0)), pl.BlockSpec((wb, D), lambda i: (1, 0)),
                  _const_spec((1, D)), _const_spec((D, LANES)), _const_spec((1, LANES))],
        out_specs=[row(D), row(D), row(LANES), row(LANES)],
        out_shape=[jax.ShapeDtypeStruct((N, D), F32), jax.ShapeDtypeStruct((N, D), F32),
                   jax.ShapeDtypeStruct((N, LANES), jnp.int32),
                   jax.ShapeDtypeStruct((N, LANES), F32)],
        compiler_params=pltpu.CompilerParams(dimension_semantics=("arbitrary",),
                                             vmem_limit_bytes=VMEM_LIMIT),
        name="outproj",
    )(x2, o_mla, ogm, mla_out_norm.reshape(1, -1), wo, wo, ffn_norm.reshape(1, -1), wr_pad,
      br_pad)


def _pack_bf16_pairs(x):
    k = x.shape[1] // 2
    lo = pltpu.bitcast(x[:, :k].astype(BF16).astype(F32), jnp.uint32)
    hi = pltpu.bitcast(x[:, k:].astype(BF16).astype(F32), jnp.uint32)
    return hi | (lo >> 16)


def _unpack_bf16_pairs(w):
    lo = pltpu.bitcast(w << 16, F32).astype(BF16)
    hi = pltpu.bitcast(w & jnp.uint32(0xFFFF0000), F32).astype(BF16)
    return jnp.concatenate([lo, hi], axis=1)


def _dispatch_kernel(dest_ref, zb_ref, nzb_ref, h2_ref, xs_hbm, pk, zblk, sem, sem_z):
    tm = h2_ref.shape[0]
    bm = MOE_TM
    i = pl.program_id(0)
    base = i * (tm * TOP_K)
    b = i % 2
    pk[b] = _pack_bf16_pairs(h2_ref[...])

    def wait_tile(bb):
        for kk in range(TOP_K):
            pltpu.make_async_copy(pk.at[bb], xs_hbm.at[pl.ds(0, tm), :], sem.at[bb]).wait()

    @pl.when(i == 0)
    def _zero_blocks():
        zblk[...] = jnp.zeros(zblk.shape, zblk.dtype)

        def zcopy(n):
            d0 = pl.multiple_of(zb_ref[n] * bm, bm)
            return pltpu.make_async_copy(zblk, xs_hbm.at[pl.ds(d0, bm), :], sem_z)

        def issue(n, c):
            zcopy(n).start()
            return c
        lax.fori_loop(0, nzb_ref[0], issue, 0)

        def finish(n, c):
            zcopy(n).wait()
            return c
        lax.fori_loop(0, nzb_ref[0], finish, 0)

    def issue(g, c):
        r0 = pl.multiple_of(g * SUBLANES, SUBLANES)
        rows = pk.at[b, pl.ds(r0, SUBLANES), :]
        for s in range(SUBLANES):
            for kk in range(TOP_K):
                d = dest_ref[base + (r0 + s) * TOP_K + kk]
                pltpu.make_async_copy(rows.at[pl.ds(s, 1), :], xs_hbm.at[pl.ds(d, 1), :],
                                      sem.at[b]).start(priority=kk % 2)
        return c
    lax.fori_loop(0, tm // SUBLANES, issue, 0)

    @pl.when(i >= 1)
    def _():
        wait_tile(1 - b)

    @pl.when(i == pl.num_programs(0) - 1)
    def _():
        wait_tile(b)


def _dispatch(h2, dest, zero_blocks, n_zero, n_rows):
    N, D = h2.shape
    tm = DSP_TILE
    grid_spec = pltpu.PrefetchScalarGridSpec(
        num_scalar_prefetch=3,
        grid=(N // tm,),
        in_specs=[pl.BlockSpec((tm, D), lambda i, *_: (i, 0))],
        out_specs=pl.BlockSpec(memory_space=pl.ANY),
        scratch_shapes=[pltpu.VMEM((2, tm, D // 2), jnp.uint32),
                        pltpu.VMEM((MOE_TM, D // 2), jnp.uint32),
                        pltpu.SemaphoreType.DMA((2,)), pltpu.SemaphoreType.DMA(())],
    )
    return pl.pallas_call(
        _dispatch_kernel,
        grid_spec=grid_spec,
        out_shape=jax.ShapeDtypeStruct((n_rows, D // 2), jnp.uint32),
        compiler_params=pltpu.CompilerParams(dimension_semantics=("arbitrary",),
                                             vmem_limit_bytes=VMEM_LIMIT),
        name="dispatch",
    )(dest, zero_blocks, n_zero, h2)


def _moe_kernel(ie_ref, ist_ref, inb_ref, tail_ref,
                xs_hbm, wgu_hbm, wd_hbm, bgu_ref, bd_ref,
                ys_hbm,
                xbuf, acc, wg_buf, wu_buf, wd_buf, zblk, sem_x, sem_o, sem_w, sem_z):
    tm = MOE_TM
    i = pl.program_id(0)
    n_items = pl.num_programs(0)
    nblk = inb_ref[i]
    slot = i % 2
    n_blocks = ys_hbm.shape[0] // tm
    tf = wg_buf.shape[2]
    nj = wd_hbm.shape[1] // tf

    def x_copy(it, m):
        s0 = pl.multiple_of(ist_ref[it] + m * tm, tm)
        r0 = pl.multiple_of(m * tm, tm)
        return pltpu.make_async_copy(xs_hbm.at[pl.ds(s0, tm), :],
                                     xbuf.at[it % 2, pl.ds(r0, tm), :], sem_x.at[it % 2])

    def y_copy(it, m):
        d0 = pl.multiple_of(ist_ref[it] + m * tm, tm)
        r0 = pl.multiple_of(m * tm, tm)
        return pltpu.make_async_copy(acc.at[it % 2, pl.ds(r0, tm), :],
                                     ys_hbm.at[pl.ds(d0, tm), :], sem_o.at[it % 2])

    def for_blocks(it, fn):
        def body(m, c):
            fn(it, m)
            return c
        lax.fori_loop(0, inb_ref[it], body, 0)

    start_x = lambda it: for_blocks(it, lambda a, m: x_copy(a, m).start())
    wait_x = lambda it: for_blocks(it, lambda a, m: x_copy(a, m).wait())
    start_y = lambda it: for_blocks(it, lambda a, m: y_copy(a, m).start())
    wait_y = lambda it: for_blocks(it, lambda a, m: y_copy(a, m).wait())

    def zero_copy(b):
        d0 = pl.multiple_of(b * tm, tm)
        return pltpu.make_async_copy(zblk, ys_hbm.at[pl.ds(d0, tm), :], sem_z)

    n_ws = wg_buf.shape[0]

    def w_slot(it, j):
        return (it * nj + j) % n_ws

    def w_copies(it, j):
        e = ie_ref[it]
        ws = w_slot(it, j)
        c0 = pl.multiple_of(j * tf, tf)
        c1 = pl.multiple_of(nj * tf + j * tf, tf)
        return (pltpu.make_async_copy(wgu_hbm.at[e, :, pl.ds(c0, tf)], wg_buf.at[ws], sem_w.at[ws]),
                pltpu.make_async_copy(wgu_hbm.at[e, :, pl.ds(c1, tf)], wu_buf.at[ws], sem_w.at[ws]),
                pltpu.make_async_copy(wd_hbm.at[e, pl.ds(c0, tf), :], wd_buf.at[ws], sem_w.at[ws]))

    def start_w(it, j):
        for cp in w_copies(it, j):
            cp.start()

    def wait_w(it, j):
        for cp in w_copies(it, j):
            cp.wait()

    @pl.when(i == 0)
    def _first_step():
        zblk[...] = jnp.zeros(zblk.shape, zblk.dtype)

        def issue(b, c):
            zero_copy(b).start()
            return c
        lax.fori_loop(tail_ref[0], n_blocks, issue, 0)
        start_x(0)

        @pl.when(nblk > 0)
        def _():
            for a in range(n_ws - 1):
                start_w(0, a)

    wait_x(i)

    @pl.when(i + 1 < n_items)
    def _():
        start_x(i + 1)

    @pl.when(i >= 2)
    def _():
        wait_y(i - 2)

    def init(m, c):
        r0 = pl.multiple_of(m * tm, tm)
        acc[slot, pl.ds(r0, tm), :] = jnp.broadcast_to(bd_ref[0], (tm, acc.shape[2]))
        return c
    lax.fori_loop(0, nblk, init, 0)

    def ffn_rows(j, n):
        ws = w_slot(i, j)
        x = _unpack_bf16_pairs(xbuf[slot, pl.ds(0, n), :])
        g = jnp.minimum(_dot(x, wg_buf[ws].astype(BF16)) + bgu_ref[0, pl.ds(j, 1), :], SWIGLU_LIMIT)
        u = jnp.clip(_dot(x, wu_buf[ws].astype(BF16)) + bgu_ref[0, pl.ds(nj + j, 1), :],
                     -SWIGLU_LIMIT, SWIGLU_LIMIT)
        a = (u + 1.0) * (g * jax.nn.sigmoid(SWIGLU_ALPHA * g))
        acc[slot, pl.ds(0, n), :] += _dot(a.astype(BF16), wd_buf[ws].astype(BF16))

    def chunk(j, c):
        wait_w(i, j)
        ja = j + (n_ws - 1)

        @pl.when(ja < nj)
        def _():
            start_w(i, ja)

        @pl.when(jnp.logical_and(ja >= nj, i + 1 < n_items))
        def _():
            @pl.when(inb_ref[i + 1] > 0)
            def _():
                start_w(i + 1, ja - nj)

        for nb in range(1, MOE_BPI + 1):
            pl.when(nblk == nb)(functools.partial(ffn_rows, j, nb * tm))
        return c

    @pl.when(nblk > 0)
    def _chunks():
        lax.fori_loop(0, nj, chunk, 0)

    start_y(i)

    @pl.when(i == n_items - 1)
    def _last_step():
        @pl.when(i >= 1)
        def _():
            wait_y(i - 1)
        wait_y(i)

        def finish(b, c):
            zero_copy(b).wait()
            return c
        lax.fori_loop(tail_ref[0], n_blocks, finish, 0)


def _moe(xs, item_e, item_start, item_nblk, tail_blk, w_gate_up, b_gate_up, w_down, b_down):
    n_rows = xs.shape[0]
    E, D, F2 = w_gate_up.shape
    F = F2 // 2
    tf = MOE_TF
    nj = F // tf
    nw = MOE_WSLOTS
    assert 2 <= nw <= nj + 1
    n_items = item_e.shape[0]
    rows = MOE_BPI * MOE_TM
    any_spec = pl.BlockSpec(memory_space=pl.ANY)
    grid_spec = pltpu.PrefetchScalarGridSpec(
        num_scalar_prefetch=4,
        grid=(n_items,),
        in_specs=[any_spec, any_spec, any_spec,
                  pl.BlockSpec((1, 2 * nj, tf), lambda i, ie, ist, inb, tail: (ie[i], 0, 0)),
                  pl.BlockSpec((1, 1, D), lambda i, ie, ist, inb, tail: (ie[i], 0, 0))],
        out_specs=any_spec,
        scratch_shapes=[pltpu.VMEM((2, rows, D // 2), jnp.uint32), pltpu.VMEM((2, rows, D), F32),
                        pltpu.VMEM((nw, D, tf), F32), pltpu.VMEM((nw, D, tf), F32),
                        pltpu.VMEM((nw, tf, D), F32), pltpu.VMEM((MOE_TM, D), F32),
                        pltpu.SemaphoreType.DMA((2,)), pltpu.SemaphoreType.DMA((2,)),
                        pltpu.SemaphoreType.DMA((nw,)), pltpu.SemaphoreType.DMA(())],
    )
    return pl.pallas_call(
        _moe_kernel,
        grid_spec=grid_spec,
        out_shape=jax.ShapeDtypeStruct((n_rows, D), F32),
        compiler_params=pltpu.CompilerParams(dimension_semantics=("arbitrary",),
                                             vmem_limit_bytes=VMEM_LIMIT),
        name="moe",
    )(item_e, item_start, item_nblk, tail_blk, xs, w_gate_up, w_down,
      b_gate_up.reshape(E, 2 * nj, tf), b_down.reshape(E, 1, D))


def _combine_kernel(pos_ref, x1_ref, g_ref, ys_hbm, o_ref, buf, sem):
    tm = CMB_TILE
    i = pl.program_id(0)

    def start_gather(t):
        base = t * (tm * TOP_K)
        b = t % 2

        def issue(g, c):
            r0 = pl.multiple_of(g * SUBLANES, SUBLANES)
            for s in range(SUBLANES):
                for kk in range(TOP_K):
                    p = pos_ref[base + (r0 + s) * TOP_K + kk]
                    pltpu.make_async_copy(ys_hbm.at[pl.ds(p, 1), :],
                                          buf.at[b, kk, pl.ds(r0, SUBLANES), :].at[pl.ds(s, 1), :],
                                          sem.at[b]).start(priority=kk % 2)
            return c
        lax.fori_loop(0, tm // SUBLANES, issue, 0)

    @pl.when(i == 0)
    def _():
        start_gather(0)

    @pl.when(i + 1 < pl.num_programs(0))
    def _():
        start_gather(i + 1)

    b = i % 2
    for kk in range(TOP_K):
        pltpu.make_async_copy(ys_hbm.at[pl.ds(0, tm), :], buf.at[b, kk], sem.at[b]).wait()
    out = x1_ref[...]
    for kk in range(TOP_K):
        out = out + g_ref[:, kk:kk + 1] * buf[b, kk]
    o_ref[...] = out


def _combine(x1, gates, ys, dest):
    N, D = x1.shape
    tm = CMB_TILE
    grid_spec = pltpu.PrefetchScalarGridSpec(
        num_scalar_prefetch=1,
        grid=(N // tm,),
        in_specs=[pl.BlockSpec((tm, D), lambda i, pos: (i, 0)),
                  pl.BlockSpec((tm, LANES), lambda i, pos: (i, 0)),
                  pl.BlockSpec(memory_space=pl.ANY)],
        out_specs=pl.BlockSpec((tm, D), lambda i, pos: (i, 0)),
        scratch_shapes=[pltpu.VMEM((2, TOP_K, tm, D), F32), pltpu.SemaphoreType.DMA((2,))],
    )
    return pl.pallas_call(
        _combine_kernel,
        grid_spec=grid_spec,
        out_shape=jax.ShapeDtypeStruct((N, D), F32),
        compiler_params=pltpu.CompilerParams(dimension_semantics=("arbitrary",),
                                             vmem_limit_bytes=VMEM_LIMIT),
        name="combine",
    )(dest, x1, gates, ys)


def _routing(top_idx, n_tok):
    tm, bpi = MOE_TM, MOE_BPI
    nk = n_tok * TOP_K
    experts = jnp.arange(N_EXPERTS, dtype=jnp.int32)
    e_flat = top_idx.reshape(-1)
    onehot = (e_flat[:, None] == experts[None, :]).astype(jnp.int32)
    csum = jnp.cumsum(onehot, axis=0)
    rank = jnp.sum(csum * onehot, axis=1) - 1
    counts = csum[-1]
    nb = (counts + tm - 1) // tm
    bend = jnp.cumsum(nb)
    bstart = bend - nb
    dest = (jnp.sum(onehot * bstart[None, :], axis=1) * tm + rank).astype(jnp.int32)
    n_blocks = -(-(nk + N_EXPERTS * (tm - 1)) // tm)
    n_rows = n_blocks * tm
    tail_blk = bend[-1:].astype(jnp.int32)
    zb_e = jnp.where(nb > 0, bend - 1, -1)
    zb_t = jnp.arange(n_blocks, dtype=jnp.int32)
    zb_all = jnp.concatenate([zb_e, jnp.where(zb_t >= bend[-1], zb_t, -1)]).astype(jnp.int32)
    order = jnp.argsort(zb_all < 0, stable=True)
    zero_blocks = zb_all[order]
    n_zero = jnp.sum(zb_all >= 0).astype(jnp.int32).reshape(1)
    n_items = n_blocks // bpi + N_EXPERTS
    items_e = (nb + bpi - 1) // bpi
    iend = jnp.cumsum(items_e)
    istart = iend - items_e
    slot = jnp.arange(n_items, dtype=jnp.int32)
    valid = slot < iend[-1]
    exp_of = jnp.minimum(jnp.searchsorted(iend, slot, side="right"), N_EXPERTS - 1).astype(jnp.int32)
    last_e = jnp.max(jnp.where(nb > 0, experts, 0))
    local = slot - istart[exp_of]
    n_it = jnp.maximum(items_e[exp_of], 1)
    base, rem = nb[exp_of] // n_it, nb[exp_of] % n_it
    first_blk = bstart[exp_of] + local * base + jnp.minimum(local, rem)
    item_e = jnp.where(valid, exp_of, last_e).astype(jnp.int32)
    item_start = jnp.where(valid, first_blk * tm, 0).astype(jnp.int32)
    item_nblk = jnp.where(valid, base + (local < rem), 0).astype(jnp.int32)
    return dest, zero_blocks, n_zero, item_e, item_start, item_nblk, tail_blk, n_rows


def _rope_lane_tile(v):
    half = MLA_ROPE // 2
    z = jnp.zeros((half,), v.dtype)
    return jnp.concatenate([v[:half], z, v[half:], z]).reshape(1, LANES)


def _rope_cols(w):
    half = MLA_ROPE // 2
    z = jnp.zeros((w.shape[0], half), w.dtype)
    return jnp.concatenate([w[:, :half], z, w[:, half:], z], axis=1)


def _layer(x, mem, positions, attn_norm, w_in, q_a_norm, w_q_b, kv_a_norm, w_kv_b,
           q_nope_norm, q_rope_norm, k_nope_norm, k_rope_norm,
           gmlp_v_norm, w_spatial, b_spatial,
           mem_norm, w_mem_kv, mem_q_norm, mem_k_norm,
           mla_out_norm, gmlp_out_norm, mem_out_norm, w_o,
           ffn_norm, w_router, b_router, w_gate_up, b_gate_up, w_down, b_down):
    B, S, D = x.shape
    N = B * S
    x2 = x.reshape(N, D)

    half = MLA_ROPE // 2
    r1 = Q_LORA + KV_LORA + half
    zc = jnp.zeros((D, half), BF16)
    w_in_b = w_in.astype(BF16)
    w_in_r = jnp.concatenate([w_in_b[:, :r1], zc, w_in_b[:, r1:r1 + half], zc, w_in_b[:, r1 + half:]],
                             axis=1)
    wq = w_q_b.reshape(Q_LORA, MLA_HEADS, MLA_NOPE + MLA_ROPE)
    wq_rope = jax.vmap(_rope_cols, in_axes=1, out_axes=1)(wq[:, :, MLA_NOPE:])
    wq_pad = jnp.concatenate([wq[:, :, :MLA_NOPE], wq_rope], axis=2).reshape(Q_LORA, MLA_HEADS * HEAD_PAD)
    wq_pad = wq_pad.astype(BF16)
    inv_freq = ROPE_BASE ** (-jnp.arange(half, dtype=F32) / half)
    zf = jnp.zeros((half,), F32)
    invf_tile = jnp.concatenate([inv_freq, zf, inv_freq, zf]).reshape(1, LANES)
    sgn_tile = jnp.concatenate([-jnp.ones((half,), F32), zf, jnp.ones((half,), F32), zf]).reshape(1, LANES)
    bs_tile = jnp.repeat(b_spatial.T, GMLP_CH, axis=1)

    mk, mv = _mem_kv(mem, mem_norm, w_mem_kv, mem_k_norm)
    cq, ckv, kpe, ogm, cs = _inproj(
        x2, positions.reshape(N, 1), (invf_tile, sgn_tile), attn_norm, w_in_r, q_a_norm, kv_a_norm,
        _rope_lane_tile(k_rope_norm), gmlp_v_norm, w_spatial, bs_tile, mk, mv, mem_q_norm,
        gmlp_out_norm, mem_out_norm, S)
    q, k, v = _qkv(cq, ckv, kpe, cs, wq_pad, w_kv_b.astype(BF16), q_nope_norm,
                   _rope_lane_tile(q_rope_norm), k_nope_norm)
    o_mla = _attention(q, k, v, B, S).reshape(N, MLA_HEADS * MLA_V)

    wr_pad = jnp.pad(w_router, ((0, 0), (0, LANES - N_EXPERTS)))
    br_pad = jnp.pad(b_router, (0, LANES - N_EXPERTS)).reshape(1, LANES)
    x1, h2, ti, tg = _outproj(x2, o_mla, ogm, mla_out_norm, w_o.astype(BF16), ffn_norm, wr_pad,
                              br_pad)

    dest, zero_blocks, n_zero, item_e, item_start, item_nblk, tail_blk, n_rows = _routing(
        ti[:, :TOP_K], N)
    xs = _dispatch(h2, dest, zero_blocks, n_zero, n_rows)
    ys = _moe(xs, item_e, item_start, item_nblk, tail_blk, w_gate_up, b_gate_up, w_down, b_down)
    out = _combine(x1, tg, ys, dest)
    return out.reshape(B, S, D)


def kernel(x, mem, positions, attn_norm, w_in, q_a_norm, w_q_b, kv_a_norm, w_kv_b, q_nope_norm, q_rope_norm, k_nope_norm, k_rope_norm, gmlp_v_norm, w_spatial, b_spatial, mem_norm, w_mem_kv, mem_q_norm, mem_k_norm, mla_out_norm, gmlp_out_norm, mem_out_norm, w_o, ffn_norm, w_router, b_router, w_gate_up, b_gate_up, w_down, b_down):
    depth = attn_norm.shape[0]
    for l in range(depth):
        x = _layer(x, mem, positions, attn_norm[l], w_in[l], q_a_norm[l], w_q_b[l], kv_a_norm[l],
                   w_kv_b[l], q_nope_norm[l], q_rope_norm[l], k_nope_norm[l], k_rope_norm[l],
                   gmlp_v_norm[l], w_spatial[l], b_spatial[l], mem_norm[l], w_mem_kv[l],
                   mem_q_norm[l], mem_k_norm[l], mla_out_norm[l], gmlp_out_norm[l],
                   mem_out_norm[l], w_o[l], ffn_norm[l], w_router[l], b_router[l], w_gate_up[l],
                   b_gate_up[l], w_down[l], b_down[l])
    return x
```
